```python
import math
import jax
import jax.numpy as jnp
from jax import lax
import numpy as np

D_MODEL = 1024
BATCH = 4
SEQ = 4096
DEPTH = 2

GRID_W = 64
CTX_LEN = 256
N_MIXERS = 4
GROUP_WIDTH = D_MODEL // N_MIXERS
HEAD_DIM = 64
N_HEADS = GROUP_WIDTH // HEAD_DIM
MIX_WIDTH = N_MIXERS * GROUP_WIDTH
D_FF = 2816
CHUNK = 64
ATTN_BLOCK = 128
DIFF_QK_DIM = HEAD_DIM // 2
ROPE_BASE = 10000.0
RWKV_DECAY_LORA = 64
RWKV_ICLR_LORA = 64
RWKV_GATE_LORA = 160
RWKV_LNX_EPS = 64e-5
LN_EPS = 1e-5
MASK_NEG = -1e30
LB_FLOOR = 1e-30
MAX_NEG_LOG_STAB = 60.0
DEEPNORM_ALPHA = (2.0 * DEPTH) ** 0.25
DEEPNORM_BETA = (8.0 * DEPTH) ** -0.25

RWKV_COLS = (GROUP_WIDTH,) * 3 + (RWKV_DECAY_LORA,) * 2 + (RWKV_ICLR_LORA,) * 2 + (RWKV_GATE_LORA,)
DIFF_COLS = (GROUP_WIDTH,) * 3
HGRN_COLS = (GROUP_WIDTH,) * 5
MLSTM_COLS = (GROUP_WIDTH,) * 4 + (N_HEADS,) * 4
GROUP_COLS = (sum(RWKV_COLS), sum(DIFF_COLS), sum(HGRN_COLS), sum(MLSTM_COLS))
IN_COLS = sum(GROUP_COLS)

kernel_name = 'hybrid_rwkv7_diffattn_hgrn2_mlstm_dit_block'


def _split(x, sizes):
    offs = [int(o) for o in np.cumsum(sizes)[:-1]]
    return jnp.split(x, offs, axis=-1)


def _heads(x):
    return x.reshape(x.shape[:-1] + (N_HEADS, x.shape[-1] // N_HEADS))


def _norm(y, gain, bias=None, eps=LN_EPS, rms=False):
    yf = y.astype(jnp.float32)
    if not rms:
        yf = yf - jnp.mean(yf, axis=-1, keepdims=True)
    yf = yf * lax.rsqrt(jnp.mean(jnp.square(yf), axis=-1, keepdims=True) + eps) * gain
    if bias is not None:
        yf = yf + bias
    return yf.astype(y.dtype)


def _l2_normalize(x, eps=1e-12):
    xf = x.astype(jnp.float32)
    return (xf / jnp.maximum(jnp.linalg.norm(xf, axis=-1, keepdims=True), eps)).astype(x.dtype)


def _neighbours(z):
    zero = jnp.zeros_like(z[:, :1])
    return jnp.concatenate([zero, z[:, :-1]], axis=1), jnp.concatenate([z[:, 1:], zero], axis=1)


def _dwconv3(z, w, b):
    prev, nxt = _neighbours(z)
    return prev * w[0] + z * w[1] + nxt * w[2] + b


def _modulate(h, shift, scale):
    return h * (1.0 + scale) + shift


def _adaln(cond, w, b):
    mod = (jax.nn.silu(cond) @ w + b)[:, None, :]
    return jnp.split(mod, 6, axis=-1)


def _to_chunks(x):
    B, T, H = x.shape[:3]
    x = x.reshape((B, T // CHUNK, CHUNK, H) + x.shape[3:])
    return jnp.moveaxis(jnp.moveaxis(x, 1, 0), 3, 2)


def _from_chunks(y):
    nc, B, H, C = y.shape[:4]
    y = jnp.moveaxis(jnp.moveaxis(y, 2, 3), 0, 1)
    return y.reshape((B, nc * C, H) + y.shape[4:])


def _bidir_scan(scan_fn, ctx_dirs, lat_dirs, init):
    flip = lambda xs: tuple(jnp.flip(t, axis=1) for t in xs)
    dtype = lat_dirs[0][0].dtype
    y_ctx, y_lat = [], []
    for d in range(2):
        cx, lx = ctx_dirs[d], lat_dirs[d]
        if d == 1:
            cx, lx = flip(cx), flip(lx)
        s_ctx, yc = scan_fn(cx, init)
        _, yl = scan_fn(lx, s_ctx)
        if d == 1:
            yc, yl = flip((yc, yl))
        y_ctx.append(yc)
        y_lat.append(yl)
    return (y_ctx[0] + y_ctx[1]).astype(dtype), (y_lat[0] + y_lat[1]).astype(dtype)


def _rwkv7_scan(inputs, state):
    r, w, k, v, a, b = (jnp.moveaxis(t.astype(jnp.float32), 1, 0) for t in inputs)

    def step(S, xt):
        r_t, w_t, k_t, v_t, a_t, b_t = xt
        sa = jnp.einsum('bhvk,bhk->bhv', S, a_t)
        S = S * w_t[:, :, None, :] + sa[..., None] * b_t[:, :, None, :] + v_t[..., None] * k_t[:, :, None, :]
        return S, jnp.einsum('bhvk,bhk->bhv', S, r_t)

    S, y = lax.scan(step, state, (r, w, k, v, a, b))
    return S, jnp.moveaxis(y, 0, 1)


def _rwkv7_features(p, mu, w0, w2, a0, a2, g2, k_k, k_a):
    prev, nxt = _neighbours(p)
    p = p + (0.5 * (prev + nxt) - p) * mu
    r, k, v, wd_f, wd_b, ad_f, ad_b, gd = _split(p, RWKV_COLS)
    rh, kh, vh = _heads(r), _heads(k), _heads(v)
    kk = _l2_normalize(kh * k_k.reshape(N_HEADS, HEAD_DIM))
    ka = k_a.reshape(N_HEADS, HEAD_DIM)
    dirs = []
    for d, (wd, ad) in enumerate(((wd_f, ad_f), (wd_b, ad_b))):
        w_log = -jax.nn.softplus(-(w0[d] + jnp.tanh(wd) @ w2[d]).astype(jnp.float32)) - 0.5
        decay = jnp.exp(-jnp.exp(w_log))
        a = _heads(jax.nn.sigmoid(a0[d] + ad @ a2[d]))
        kd = kh * (1.0 + (a - 1.0) * ka)
        dirs.append((rh, _heads(decay), kd, vh, -kk, kk * a))
    g = jax.nn.sigmoid(gd) @ g2
    return rh, vh, g, dirs


def _rwkv7_mixer(p_ctx, p_lat, mu, w0, w2, a0, a2, g2, k_k, k_a, r_k, lnx_g, lnx_b):
    f_ctx = _rwkv7_features(p_ctx, mu, w0, w2, a0, a2, g2, k_k, k_a)
    f_lat = _rwkv7_features(p_lat, mu, w0, w2, a0, a2, g2, k_k, k_a)
    init = jnp.zeros((p_lat.shape[0], N_HEADS, HEAD_DIM, HEAD_DIM), jnp.float32)
    y_ctx, y_lat = _bidir_scan(_rwkv7_scan, f_ctx[3], f_lat[3], init)
    gn_g, gn_b = lnx_g.reshape(N_HEADS, HEAD_DIM), lnx_b.reshape(N_HEADS, HEAD_DIM)
    rk = r_k.reshape(N_HEADS, HEAD_DIM)

    def readout(feats, y):
        rh, vh, g, dirs = feats
        y = _norm(y, gn_g, gn_b, eps=RWKV_LNX_EPS)
        bonus = jnp.sum(rh * (dirs[0][2] + dirs[1][2]) * rk, axis=-1, keepdims=True) * vh
        return (y + bonus).reshape(g.shape) * g

    return readout(f_ctx, y_ctx), readout(f_lat, y_lat)


def _axial_rope(n_tokens):
    rows = n_tokens // GRID_W
    row = jnp.repeat(jnp.arange(rows, dtype=jnp.float32), GRID_W)
    col = jnp.tile(jnp.arange(GRID_W, dtype=jnp.float32), rows)
    n_freq = DIFF_QK_DIM // 4
    inv_freq = ROPE_BASE ** (-jnp.arange(n_freq, dtype=jnp.float32) / n_freq)
    ang = jnp.concatenate([row[:, None] * inv_freq, col[:, None] * inv_freq], axis=-1)
    return jnp.cos(ang), jnp.sin(ang)


def _apply_rope(x, cos, sin):
    half = x.shape[-1] // 2
    x1, x2 = x[..., :half], x[..., half:]
    cos = cos[:, None, None, :].astype(x.dtype)
    sin = sin[:, None, None, :].astype(x.dtype)
    return jnp.concatenate([x1 * cos - x2 * sin, x2 * cos + x1 * sin], axis=-1)


def _diff_attention(p_ctx, p_lat, lam_vecs, norm_g, layer):
    lam_init = 0.8 - 0.6 * math.exp(-0.3 * layer)
    lv = lam_vecs.astype(jnp.float32)
    lam = jnp.exp(jnp.sum(lv[0] * lv[1])) - jnp.exp(jnp.sum(lv[2] * lv[3])) + lam_init

    def qkv(p):
        B, T = p.shape[:2]
        q, k, v = _split(p, DIFF_COLS)
        return (q.reshape(B, T, N_HEADS, 2, DIFF_QK_DIM), k.reshape(B, T, N_HEADS, 2, DIFF_QK_DIM),
                v.reshape(B, T, N_HEADS, HEAD_DIM))

    qc, kc, vc = qkv(p_ctx)
    ql, kl, vl = qkv(p_lat)
    B, N = p_lat.shape[:2]
    cos, sin = _axial_rope(N)
    ql, kl = _apply_rope(ql, cos, sin), _apply_rope(kl, cos, sin)
    scale = DIFF_QK_DIM ** -0.5

    def attend(q, k, v):
        s = jnp.einsum('bqhcd,bkhcd->bhcqk', q, k).astype(jnp.float32) * scale
        pr = jax.nn.softmax(s, axis=-1)
        pr = pr[:, :, 0] - lam * pr[:, :, 1]
        return jnp.einsum('bhqk,bkhd->bqhd', pr.astype(v.dtype), v)

    y_ctx = attend(qc, kc, vc)
    k_all = jnp.concatenate([kc, kl], axis=1)
    v_all = jnp.concatenate([vc, vl], axis=1)
    nblk = N // ATTN_BLOCK
    qb = jnp.moveaxis(ql.reshape(B, nblk, ATTN_BLOCK, N_HEADS, 2, DIFF_QK_DIM), 1, 0)
    y_lat = lax.map(lambda q: attend(q, k_all, v_all), qb)
    y_lat = jnp.moveaxis(y_lat, 0, 1).reshape(B, N, N_HEADS, HEAD_DIM)

    def finish(y):
        return (_norm(y, norm_g, rms=True) * (1.0 - lam_init)).reshape(y.shape[0], y.shape[1], GROUP_WIDTH)

    return finish(y_ctx), finish(y_lat)


def _gla_chunk_scan(inputs, state):
    q, k, v, logf = (t.astype(jnp.float32) for t in inputs)
    causal = jnp.tril(jnp.ones((CHUNK, CHUNK), dtype=bool))

    def step(S, blk):
        qc, kc, vc, gc = blk
        b = jnp.cumsum(gc, axis=2)
        diff = b[:, :, :, None, :] - b[:, :, None, :, :]
        pair_decay = jnp.where(causal[:, :, None], jnp.exp(jnp.minimum(diff, 0.0)), 0.0)
        att = jnp.einsum('bhtd,bhsd,bhtsd->bhts', qc, kc, pair_decay)
        o = jnp.einsum('bhtd,bhdv->bhtv', qc * jnp.exp(b), S) + jnp.einsum('bhts,bhsv->bhtv', att, vc)
        b_end = b[:, :, -1:, :]
        S = jnp.exp(b_end[:, :, 0, :, None]) * S + jnp.einsum('bhsd,bhsv->bhdv', kc * jnp.exp(b_end - b), vc)
        return S, o

    S, o = lax.scan(step, state, tuple(_to_chunks(t) for t in (q, k, v, logf)))
    return S, _from_chunks(o)


def _hgrn2_features(p, lb):
    q, f_f, f_b, i, g = _split(p, HGRN_COLS)
    qh, ih = _heads(jax.nn.silu(q)), _heads(i)
    log_lb, log_1mlb = jnp.log(jnp.maximum(lb, LB_FLOOR)), jnp.log1p(-lb)
    dirs = []
    for f in (f_f, f_b):
        f = f.astype(jnp.float32)
        logf = jnp.logaddexp(log_lb, log_1mlb + jax.nn.log_sigmoid(f))
        k = (1.0 - lb) * jax.nn.sigmoid(-f)
        dirs.append((qh, _heads(k), ih, _heads(logf)))
    return g, dirs


def _hgrn2_mixer(p_ctx, p_lat, lb, norm_g):
    g_c, dirs_c = _hgrn2_features(p_ctx, lb)
    g_l, dirs_l = _hgrn2_features(p_lat, lb)
    init = jnp.zeros((p_lat.shape[0], N_HEADS, HEAD_DIM, HEAD_DIM), jnp.float32)
    o_c, o_l = _bidir_scan(_gla_chunk_scan, dirs_c, dirs_l, init)
    out = lambda o, g: _norm(o, norm_g, rms=True).reshape(g.shape) * jax.nn.silu(g)
    return out(o_c, g_c), out(o_l, g_l)


def _mlstm_chunk_scan(inputs, state):
    q, k, v, ig, logf = (t.astype(jnp.float32) for t in inputs)
    causal = jnp.tril(jnp.ones((CHUNK, CHUNK), dtype=bool))

    def step(carry, blk):
        C, n, m = carry
        qc, kc, vc, ic, fc = blk
        b = jnp.cumsum(fc, axis=-1)
        log_w = jnp.where(causal, b[..., :, None] - b[..., None, :] + ic[..., None, :], MASK_NEG)
        log_inter = b + m[..., None]
        m_t = jnp.maximum(log_inter, jnp.max(log_w, axis=-1))
        w = jnp.exp(log_w - m_t[..., None]) * jnp.einsum('bhtd,bhsd->bhts', qc, kc)
        w_inter = jnp.exp(log_inter - m_t)
        num = w_inter[..., None] * jnp.einsum('bhtd,bhdv->bhtv', qc, C) + jnp.einsum('bhts,bhsv->bhtv', w, vc)
        den = w_inter * jnp.einsum('bhtd,bhd->bht', qc, n) + jnp.sum(w, axis=-1)
        floor = jnp.exp(jnp.minimum(-m_t, MAX_NEG_LOG_STAB))
        h = num / jnp.maximum(jnp.abs(den), floor)[..., None]
        m_new = m_t[..., -1]
        kw = kc * jnp.exp(b[..., -1:] - b + ic - m_new[..., None])[..., None]
        decay = jnp.exp(b[..., -1] + m - m_new)
        C = decay[..., None, None] * C + jnp.einsum('bhsd,bhsv->bhdv', kw, vc)
        n = decay[..., None] * n + jnp.sum(kw, axis=2)
        return (C, n, m_new), h

    state, h = lax.scan(step, state, tuple(_to_chunks(t) for t in (q, k, v, ig, logf)))
    return state, _from_chunks(h)


def _mlstm_features(p, conv_w, conv_b, gate_b):
    q, k, v, o, i_f, i_b, f_f, f_b = _split(p, MLSTM_COLS)
    qk = jax.nn.silu(_dwconv3(jnp.concatenate([q, k], axis=-1), conv_w, conv_b))
    q, k = jnp.split(qk, 2, axis=-1)
    qh, kh, vh = _heads(q), _heads(k) * HEAD_DIM ** -0.5, _heads(v)
    dirs = ((qh, kh, vh, i_f + gate_b[0], jax.nn.log_sigmoid((f_f + gate_b[2]).astype(jnp.float32))),
            (qh, kh, vh, i_b + gate_b[1], jax.nn.log_sigmoid((f_b + gate_b[3]).astype(jnp.float32))))
    return o, dirs


def _mlstm_mixer(p_ctx, p_lat, conv_w, conv_b, gate_b, norm_g):
    o_c, dirs_c = _mlstm_features(p_ctx, conv_w, conv_b, gate_b)
    o_l, dirs_l = _mlstm_features(p_lat, conv_w, conv_b, gate_b)
    B = p_lat.shape[0]
    init = (jnp.zeros((B, N_HEADS, HEAD_DIM, HEAD_DIM), jnp.float32),
            jnp.zeros((B, N_HEADS, HEAD_DIM), jnp.float32),
            jnp.zeros((B, N_HEADS), jnp.float32))
    h_c, h_l = _bidir_scan(_mlstm_chunk_scan, dirs_c, dirs_l, init)
    gn = norm_g.reshape(N_HEADS, HEAD_DIM)
    out = lambda h, o: _norm(h, gn).reshape(o.shape) * jax.nn.sigmoid(o)
    return out(h_c, o_c), out(h_l, o_l)


def _conv_ffn(h, w_up, conv_w, conv_b, w_down):
    gate, val = jnp.split(h @ w_up, 2, axis=-1)
    return (jax.nn.gelu(_dwconv3(gate, conv_w, conv_b)) * val) @ w_down


def setup_inputs(seed: int = 0) -> dict:
    key = jax.random.key(seed)
    keys = iter(jax.random.split(key, 48))
    f32 = jnp.float32

    def nrm(shape, scale):
        return jax.random.normal(next(keys), shape, f32) * scale

    def unif(shape, lo, hi):
        return jax.random.uniform(next(keys), shape, f32, lo, hi)

    L, D, GW = DEPTH, D_MODEL, GROUP_WIDTH
    centre_tap = jnp.array([0.0, 1.0, 0.0], f32)[:, None]
    gate_slot = jnp.array([0.0, 0.0, 1.0, 0.0, 0.0, 1.0], f32)[None, :, None]
    return {
        'x': nrm((BATCH, SEQ, D), 1.0),
        'c': nrm((BATCH, D), 1.0),
        'ctx': nrm((BATCH, CTX_LEN, D), 1.0),
        'c_ctx': nrm((D,), 1.0),
        'ada_w': nrm((L, D, 6 * D), 0.2 * D ** -0.5),
        'ada_b': (nrm((L, 6, D), 0.02) + gate_slot).reshape(L, 6 * D),
        'w_in': nrm((L, D, IN_COLS), D ** -0.5),
        'rwkv_mu': unif((L, sum(RWKV_COLS)), 0.0, 1.0),
        'rwkv_w0': unif((L, 2, GW), -6.0, -1.0),
        'rwkv_w2': nrm((L, 2, RWKV_DECAY_LORA, GW), 0.5 * RWKV_DECAY_LORA ** -0.5),
        'rwkv_a0': nrm((L, 2, GW), 0.3),
        'rwkv_a2': nrm((L, 2, RWKV_ICLR_LORA, GW), 0.5 * RWKV_ICLR_LORA ** -0.5),
        'rwkv_g2': nrm((L, RWKV_GATE_LORA, GW), RWKV_GATE_LORA ** -0.5),
        'rwkv_k_k': 0.85 + nrm((L, GW), 0.02),
        'rwkv_k_a': 1.0 + nrm((L, GW), 0.02),
        'rwkv_r_k': nrm((L, GW), 0.1),
        'rwkv_lnx_g': 1.0 + nrm((L, GW), 0.02),
        'rwkv_lnx_b': nrm((L, GW), 0.02),
        'diff_lambda': nrm((L, 4, DIFF_QK_DIM), 0.1),
        'diff_norm_g': 1.0 + nrm((L, HEAD_DIM), 0.02),
        'hgrn_lb_logits': nrm((L, GW), 0.1),
        'hgrn_norm_g': 1.0 + nrm((L, HEAD_DIM), 0.02),
        'mlstm_conv_w': nrm((L, 3, 2 * GW), 0.3) + centre_tap,
        'mlstm_conv_b': nrm((L, 2 * GW), 0.02),
        'mlstm_gate_b': jnp.concatenate([nrm((L, 2, N_HEADS), 0.1), unif((L, 2, N_HEADS), 3.0, 6.0)], axis=1),
        'mlstm_norm_g': 1.0 + nrm((L, GW), 0.02),
        'w_out': nrm((L, MIX_WIDTH, D), DEEPNORM_BETA * MIX_WIDTH ** -0.5),
        'ffn_w_up': nrm((L, D, 2 * D_FF), D ** -0.5),
        'ffn_conv_w': nrm((L, 3, D_FF), 0.3) + centre_tap,
        'ffn_conv_b': nrm((L, D_FF), 0.02),
        'ffn_w_down': nrm((L, D_FF, D), DEEPNORM_BETA * D_FF ** -0.5),
        'ln_g': 1.0 + nrm((L, 2, D), 0.02),
        'ln_b': nrm((L, 2, D), 0.02),
    }


def reference(x, c, ctx, c_ctx, ada_w, ada_b, w_in, rwkv_mu, rwkv_w0, rwkv_w2, rwkv_a0, rwkv_a2, rwkv_g2,
              rwkv_k_k, rwkv_k_a, rwkv_r_k, rwkv_lnx_g, rwkv_lnx_b, diff_lambda, diff_norm_g, hgrn_lb_logits,
              hgrn_norm_g, mlstm_conv_w, mlstm_conv_b, mlstm_gate_b, mlstm_norm_g, w_out, ffn_w_up, ffn_conv_w,
              ffn_conv_b, ffn_w_down, ln_g, ln_b):
    lb_w = jax.nn.softmax(hgrn_lb_logits.astype(jnp.float32), axis=0)
    lower_bounds = jnp.cumsum(lb_w, axis=0) - lb_w[0]
    h_x, h_c = x, ctx
    for layer in range(DEPTH):
        mod_lat = _adaln(c, ada_w[layer], ada_b[layer])
        mod_ctx = _adaln(c_ctx[None, :], ada_w[layer], ada_b[layer])
        p_lat = _split(_modulate(h_x, mod_lat[0], mod_lat[1]) @ w_in[layer], GROUP_COLS)
        p_ctx = _split(_modulate(h_c, mod_ctx[0], mod_ctx[1]) @ w_in[layer], GROUP_COLS)
        y_a = _rwkv7_mixer(p_ctx[0], p_lat[0], rwkv_mu[layer], rwkv_w0[layer], rwkv_w2[layer], rwkv_a0[layer],
                           rwkv_a2[layer], rwkv_g2[layer], rwkv_k_k[layer], rwkv_k_a[layer], rwkv_r_k[layer],
                           rwkv_lnx_g[layer], rwkv_lnx_b[layer])
        y_b = _diff_attention(p_ctx[1], p_lat[1], diff_lambda[layer], diff_norm_g[layer], layer)
        y_c = _hgrn2_mixer(p_ctx[2], p_lat[2], lower_bounds[layer], hgrn_norm_g[layer])
        y_d = _mlstm_mixer(p_ctx[3], p_lat[3], mlstm_conv_w[layer], mlstm_conv_b[layer], mlstm_gate_b[layer],
                           mlstm_norm_g[layer])

        def sublayers(h, mod, side):
            y = jnp.concatenate([y_a[side], y_b[side], y_c[side], y_d[side]], axis=-1) @ w_out[layer]
            h = _norm(DEEPNORM_ALPHA * h + mod[2] * y, ln_g[layer, 0], ln_b[layer, 0])
            f = _conv_ffn(_modulate(h, mod[3], mod[4]), ffn_w_up[layer], ffn_conv_w[layer], ffn_conv_b[layer],
                          ffn_w_down[layer])
            return _norm(DEEPNORM_ALPHA * h + mod[5] * f, ln_g[layer, 1], ln_b[layer, 1])

        if layer < DEPTH - 1:
            h_c = sublayers(h_c, mod_ctx, 0)
        h_x = sublayers(h_x, mod_lat, 1)
    return h_x
```

```python
import functools
import math

import jax
import jax.numpy as jnp
import numpy as np
from jax import lax
from jax.experimental import pallas as pl
from jax.experimental.pallas import tpu as pltpu

F32 = jnp.float32
BF16 = jnp.bfloat16

N_MIXERS = 4
HEAD_DIM = 64
N_HEADS = 4
GROUP_WIDTH = N_HEADS * HEAD_DIM
DIFF_QK_DIM = HEAD_DIM // 2
GRID_W = 64
ROPE_BASE = 10000.0
RWKV_DECAY_LORA = 64
RWKV_ICLR_LORA = 64
RWKV_GATE_LORA = 160
RWKV_LNX_EPS = 64e-5
LN_EPS = 1e-5
LB_FLOOR = 1e-30
MAX_NEG_LOG_STAB = 60.0

LANES = 128
ROW_TILE = 256
SCAN_TB = 32
VMEM_LIMIT = 56 * 1024 * 1024


def _cparams(sem):
    return pltpu.CompilerParams(dimension_semantics=sem, vmem_limit_bytes=VMEM_LIMIT)


def _adaln_body(c_ref, w_ref, b_ref, o_ref):
    x = c_ref[...]
    x = (x * jax.nn.sigmoid(x)).astype(BF16)
    o_ref[...] = jnp.dot(x, w_ref[...].astype(BF16), preferred_element_type=F32) + b_ref[...]


def _adaln(cond, w, b):
    m, d = cond.shape
    n = w.shape[1]
    tn = 1536
    return pl.pallas_call(
        _adaln_body,
        grid=(n // tn,),
        in_specs=[pl.BlockSpec((m, d), lambda j: (0, 0)),
                  pl.BlockSpec((d, tn), lambda j: (0, j)),
                  pl.BlockSpec((1, tn), lambda j: (0, j))],
        out_specs=pl.BlockSpec((m, tn), lambda j: (0, j)),
        out_shape=jax.ShapeDtypeStruct((m, n), F32),
        compiler_params=_cparams(("arbitrary",)),
        name="adaln",
    )(cond, w, b.reshape(1, n))


def _inproj_body(x_ref, sl_ref, cl_ref, sc_ref, cc_ref, w_ref, o_ref, *, ctx_tiles):
    is_ctx = pl.program_id(1) < ctx_tiles
    shift = jnp.where(is_ctx, sc_ref[0], sl_ref[0])
    scale = jnp.where(is_ctx, cc_ref[0], cl_ref[0])
    xm = (x_ref[0] * (1.0 + scale) + shift).astype(BF16)
    o_ref[0] = jnp.dot(xm, w_ref[...], preferred_element_type=F32)


def _inproj(h, shift_l, scale_l, shift_c, scale_c, w, ctx_tiles):
    b, t, d = h.shape
    n = w.shape[1]
    row = pl.BlockSpec((1, ROW_TILE, d), lambda bi, i: (bi, i, 0))
    lat = pl.BlockSpec((1, 1, d), lambda bi, i: (bi, 0, 0))
    ctx = pl.BlockSpec((1, 1, d), lambda bi, i: (0, 0, 0))
    return pl.pallas_call(
        functools.partial(_inproj_body, ctx_tiles=ctx_tiles),
        grid=(b, t // ROW_TILE),
        in_specs=[row, lat, lat, ctx, ctx, pl.BlockSpec((d, n), lambda bi, i: (0, 0))],
        out_specs=pl.BlockSpec((1, ROW_TILE, n), lambda bi, i: (bi, i, 0)),
        out_shape=jax.ShapeDtypeStruct((b, t, n), F32),
        compiler_params=_cparams(("parallel", "parallel")),
        name="inproj",
    )(h, shift_l, scale_l, shift_c, scale_c, w)


def _mm_body(x_ref, w_ref, o_ref):
    o_ref[...] = jnp.dot(x_ref[...].astype(BF16), w_ref[...], preferred_element_type=F32)


def _mm(x, w):
    m, k = x.shape
    n = w.shape[1]
    tm = 2176 if m % 2176 == 0 else m
    return pl.pallas_call(
        _mm_body,
        grid=(m // tm,),
        in_specs=[pl.BlockSpec((tm, k), lambda i: (i, 0)), pl.BlockSpec((k, n), lambda i: (0, 0))],
        out_specs=pl.BlockSpec((tm, n), lambda i: (i, 0)),
        out_shape=jax.ShapeDtypeStruct((m, n), F32),
        compiler_params=_cparams(("parallel",)),
        name="lora_mm",
    )(x, w.astype(BF16))


def _scan_body(*refs, mode, tb, n_vh):
    if mode == "rwkv":
        r_ref, w_ref, k_ref, a_ref, b_ref, v_ref, y_ref, s_ref = refs
    elif mode == "gla":
        r_ref, w_ref, k_ref, v_ref, y_ref, s_ref = refs
    else:
        r_ref, k_ref, v_ref, ig_ref, fg_ref, y_ref, s_ref, n_ref, m_ref = refs

    @pl.when(pl.program_id(0) == 0)
    def _():
        s_ref[...] = jnp.zeros_like(s_ref)
        if mode == "mlstm":
            n_ref[...] = jnp.zeros_like(n_ref)
            m_ref[...] = jnp.zeros_like(m_ref)

    def step(t, carry):
        r = r_ref[t]
        k = k_ref[t]
        v = v_ref[t]
        ys = []
        if mode == "mlstm":
            i_t = ig_ref[pl.ds(t, 1), :]
            f_t = fg_ref[pl.ds(t, 1), :]
            m_prev = m_ref[0:1, :]
            m_t = jnp.maximum(f_t + m_prev, i_t)
            alpha = jnp.exp(f_t + m_prev - m_t)
            beta = jnp.exp(i_t - m_t)
            m_ref[0:1, :] = m_t
            bk = beta * k
            n = alpha * n_ref[...] + bk
            n_ref[...] = n
            den = jnp.sum(n * r, axis=0, keepdims=True)
            floor = jnp.exp(jnp.minimum(-m_t, MAX_NEG_LOG_STAB))
            inv = 1.0 / jnp.maximum(jnp.abs(den), floor)
            for vh in range(n_vh):
                c = alpha * s_ref[vh] + v[vh:vh + 1, :] * bk
                s_ref[vh] = c
                ys.append(jnp.sum(c * r, axis=0, keepdims=True) * inv)
        else:
            w = w_ref[t]
            if mode == "rwkv":
                a = a_ref[t]
                b = b_ref[t]
            for vh in range(n_vh):
                s = s_ref[vh]
                if mode == "rwkv":
                    sa = jnp.sum(s * a, axis=0, keepdims=True)
                    s = s * w + sa * b + v[vh:vh + 1, :] * k
                else:
                    s = s * w + v[vh:vh + 1, :] * k
                s_ref[vh] = s
                ys.append(jnp.sum(s * r, axis=0, keepdims=True))
        y_ref[t] = jnp.concatenate(ys, axis=0)
        return carry

    lax.fori_loop(0, tb, step, 0)


def _scan(mode, k_inputs, v_in, gates=()):
    t, n_vh, _ = v_in.shape
    tb = SCAN_TB
    big = pl.BlockSpec((tb, HEAD_DIM, LANES), lambda i: (i, 0, 0))
    small = pl.BlockSpec((tb, n_vh, LANES), lambda i: (i, 0, 0))
    gate = pl.BlockSpec((tb, LANES), lambda i: (i, 0))
    scratch = [pltpu.VMEM((n_vh, HEAD_DIM, LANES), F32)]
    if mode == "mlstm":
        scratch += [pltpu.VMEM((HEAD_DIM, LANES), F32), pltpu.VMEM((8, LANES), F32)]
    return pl.pallas_call(
        functools.partial(_scan_body, mode=mode, tb=tb, n_vh=n_vh),
        grid=(t // tb,),
        in_specs=[big] * len(k_inputs) + [small] + [gate] * len(gates),
        out_specs=small,
        out_shape=jax.ShapeDtypeStruct((t, n_vh, LANES), F32),
        scratch_shapes=scratch,
        compiler_params=_cparams(("arbitrary",)),
        name="scan_" + mode,
    )(*k_inputs, v_in, *gates)


def _attn_body(q_ref, k_ref, v_ref, lam_ref, g_ref, o_ref, *, out_scale):
    q = q_ref[0, 0] * (DIFF_QK_DIM ** -0.5)
    lane = lax.broadcasted_iota(jnp.int32, q.shape, 1)
    first = lane < DIFF_QK_DIM
    q0 = jnp.where(first, q, 0.0).astype(BF16)
    q1 = jnp.where(first, 0.0, q).astype(BF16)
    k = k_ref[0, 0].astype(BF16)
    nt = (((1,), (1,)), ((), ()))
    s0 = lax.dot_general(q0, k, nt, preferred_element_type=F32)
    s1 = lax.dot_general(q1, k, nt, preferred_element_type=F32)
    e0 = jnp.exp(s0 - jnp.max(s0, axis=-1, keepdims=True))
    e1 = jnp.exp(s1 - jnp.max(s1, axis=-1, keepdims=True))
    inv0 = 1.0 / jnp.sum(e0, axis=-1, keepdims=True)
    inv1 = lam_ref[0:1, 0:1] / jnp.sum(e1, axis=-1, keepdims=True)
    pr = (e0 * inv0 - e1 * inv1).astype(BF16)
    y = jnp.dot(pr, v_ref[0, 0].astype(BF16), preferred_element_type=F32)
    y = y * lax.rsqrt(jnp.mean(y * y, axis=-1, keepdims=True) + LN_EPS) * g_ref[...]
    o_ref[0, 0] = y * out_scale


def _attention(q, k, v, lam, norm_g, out_scale):
    b, h, tq, hd = q.shape
    tk = k.shape[2]
    tile = ROW_TILE
    qspec = pl.BlockSpec((1, 1, tile, hd), lambda bi, hi, i: (bi, hi, i, 0))
    kspec = pl.BlockSpec((1, 1, tk, hd), lambda bi, hi, i: (bi, hi, 0, 0))
    return pl.pallas_call(
        functools.partial(_attn_body, out_scale=out_scale),
        grid=(b, h, tq // tile),
        in_specs=[qspec, kspec, kspec,
                  pl.BlockSpec((1, LANES), lambda bi, hi, i: (0, 0)),
                  pl.BlockSpec((1, hd), lambda bi, hi, i: (0, 0))],
        out_specs=qspec,
        out_shape=jax.ShapeDtypeStruct((b, h, tq, hd), F32),
        compiler_params=_cparams(("parallel", "parallel", "parallel")),
        name="diff_attn",
    )(q, k, v, jnp.full((1, LANES), lam, F32), norm_g.reshape(1, hd))


def _layernorm(z, g, b):
    z = z - jnp.mean(z, axis=-1, keepdims=True)
    return z * lax.rsqrt(jnp.mean(z * z, axis=-1, keepdims=True) + LN_EPS) * g + b


def _outproj_body(y_ref, h_ref, gl_ref, gc_ref, w_ref, g_ref, b_ref, o_ref, *, ctx_tiles, alpha):
    is_ctx = pl.program_id(1) < ctx_tiles
    gate = jnp.where(is_ctx, gc_ref[0], gl_ref[0])
    y = jnp.dot(y_ref[0].astype(BF16), w_ref[...], preferred_element_type=F32)
    o_ref[0] = _layernorm(alpha * h_ref[0] + gate * y, g_ref[...], b_ref[...])


def _outproj(y, h, gate_l, gate_c, w, ln_g, ln_b, ctx_tiles, alpha):
    b, t, d = h.shape
    row = pl.BlockSpec((1, ROW_TILE, d), lambda bi, i: (bi, i, 0))
    lat = pl.BlockSpec((1, 1, d), lambda bi, i: (bi, 0, 0))
    ctx = pl.BlockSpec((1, 1, d), lambda bi, i: (0, 0, 0))
    vec = pl.BlockSpec((1, d), lambda bi, i: (0, 0))
    return pl.pallas_call(
        functools.partial(_outproj_body, ctx_tiles=ctx_tiles, alpha=alpha),
        grid=(b, t // ROW_TILE),
        in_specs=[row, row, lat, ctx, pl.BlockSpec(w.shape, lambda bi, i: (0, 0)), vec, vec],
        out_specs=row,
        out_shape=jax.ShapeDtypeStruct((b, t, d), F32),
        compiler_params=_cparams(("parallel", "parallel")),
        name="outproj_ln",
    )(y, h, gate_l, gate_c, w, ln_g.reshape(1, d), ln_b.reshape(1, d))


FFN_CHUNKS = 2


def _ffn_body(x_ref, xp_ref, xn_ref, sl_ref, cl_ref, gl_ref, sc_ref, cc_ref, gc_ref,
              wu_ref, cw_ref, cb_ref, wd_ref, g_ref, b_ref, o_ref, *, ctx_tiles, n_tiles, alpha, d_ff):
    i = pl.program_id(1)
    is_ctx = i < ctx_tiles
    shift = jnp.where(is_ctx, sc_ref[0], sl_ref[0])
    scale = jnp.where(is_ctx, cc_ref[0], cl_ref[0])
    gate_mod = jnp.where(is_ctx, gc_ref[0], gl_ref[0])
    seg_start = jnp.logical_or(i == 0, i == ctx_tiles)
    seg_end = jnp.logical_or(i == ctx_tiles - 1, i == n_tiles - 1)

    x = x_ref[0]
    xm = (x * (1.0 + scale) + shift).astype(BF16)
    xp = (xp_ref[0] * (1.0 + scale) + shift).astype(BF16)
    xn = (xn_ref[0] * (1.0 + scale) + shift).astype(BF16)
    rows = x.shape[0]
    row_id = lax.broadcasted_iota(jnp.int32, (rows, 1), 0)
    ch = d_ff // FFN_CHUNKS
    f = jnp.zeros(x.shape, F32)
    for c in range(FFN_CHUNKS):
        wg = wu_ref[:, c * ch:(c + 1) * ch]
        wv = wu_ref[:, d_ff + c * ch:d_ff + (c + 1) * ch]
        gate = jnp.dot(xm, wg, preferred_element_type=F32)
        val = jnp.dot(xm, wv, preferred_element_type=F32)
        gp = jnp.dot(xp, wg, preferred_element_type=F32)[7:8, :]
        gn = jnp.dot(xn, wg, preferred_element_type=F32)[0:1, :]
        gp = jnp.where(seg_start, 0.0, gp)
        gn = jnp.where(seg_end, 0.0, gn)
        prev = jnp.where(row_id == 0, gp, pltpu.roll(gate, 1, 0))
        nxt = jnp.where(row_id == rows - 1, gn, pltpu.roll(gate, rows - 1, 0))
        cw = cw_ref[:, c * ch:(c + 1) * ch]
        conv = prev * cw[0:1, :] + gate * cw[1:2, :] + nxt * cw[2:3, :] + cb_ref[:, c * ch:(c + 1) * ch]
        act = (jax.nn.gelu(conv) * val).astype(BF16)
        f = f + jnp.dot(act, wd_ref[c * ch:(c + 1) * ch, :], preferred_element_type=F32)
    o_ref[0] = _layernorm(alpha * x + gate_mod * f, g_ref[...], b_ref[...])


def _ffn(h, mods_l, mods_c, w_up, conv_w, conv_b, w_down, ln_g, ln_b, ctx_tiles, alpha):
    b, t, d = h.shape
    d_ff = w_down.shape[0]
    n_tiles = t // ROW_TILE
    sub = ROW_TILE // 8
    n_sub = t // 8
    row = pl.BlockSpec((1, ROW_TILE, d), lambda bi, i: (bi, i, 0))
    prev = pl.BlockSpec((1, 8, d), lambda bi, i: (bi, jnp.maximum(i * sub - 1, 0), 0))
    nxt = pl.BlockSpec((1, 8, d), lambda bi, i: (bi, jnp.minimum((i + 1) * sub, n_sub - 1), 0))
    lat = pl.BlockSpec((1, 1, d), lambda bi, i: (bi, 0, 0))
    ctx = pl.BlockSpec((1, 1, d), lambda bi, i: (0, 0, 0))
    vec = pl.BlockSpec((1, d), lambda bi, i: (0, 0))
    full = lambda a: pl.BlockSpec(a.shape, lambda bi, i: (0,) * a.ndim)
    cb = conv_b.reshape(1, d_ff)
    return pl.pallas_call(
        functools.partial(_ffn_body, ctx_tiles=ctx_tiles, n_tiles=n_tiles, alpha=alpha, d_ff=d_ff),
        grid=(b, n_tiles),
        in_specs=[row, prev, nxt, lat, lat, lat, ctx, ctx, ctx,
                  full(w_up), full(conv_w), full(cb), full(w_down), vec, vec],
        out_specs=row,
        out_shape=jax.ShapeDtypeStruct((b, t, d), F32),
        compiler_params=_cparams(("parallel", "parallel")),
        name="ffn",
    )(h, h, h, *mods_l, *mods_c, w_up, conv_w, cb, w_down, ln_g.reshape(1, d), ln_b.reshape(1, d))


def _seg_neighbours(z, tc):
    zero = jnp.zeros_like(z[:, :1])
    prev = jnp.concatenate([zero, z[:, :-1]], axis=1)
    nxt = jnp.concatenate([z[:, 1:], zero], axis=1)
    if 0 < tc < z.shape[1]:
        t = jnp.arange(z.shape[1])[None, :, None]
        prev = jnp.where(t == tc, 0.0, prev)
        nxt = jnp.where(t == tc - 1, 0.0, nxt)
    return prev, nxt


def _seg_flip(z, tc):
    return jnp.concatenate([jnp.flip(z[:, :tc], axis=1), jnp.flip(z[:, tc:], axis=1)], axis=1)


def _dirs(fwd, bwd, tc):
    return jnp.stack([fwd, _seg_flip(bwd, tc)], axis=0)


def _k_layout(x):
    d, b, t, _ = x.shape
    n_scan = d * b * N_HEADS
    x = x.reshape(d, b, t, N_HEADS, HEAD_DIM)
    x = jnp.transpose(x, (2, 4, 0, 1, 3)).reshape(t, HEAD_DIM, n_scan)
    return jnp.tile(x, (1, 1, LANES // n_scan))


def _v_layout(x):
    d, b, t, _ = x.shape
    n_scan = d * b * N_HEADS
    v_lo = LANES // n_scan
    x = x.reshape(d, b, t, N_HEADS, HEAD_DIM // v_lo, v_lo)
    x = jnp.transpose(x, (2, 4, 5, 0, 1, 3))
    return x.reshape(t, HEAD_DIM // v_lo, LANES)


def _gate_layout(x):
    d, b, t, h = x.shape
    n_scan = d * b * h
    x = jnp.transpose(x, (2, 0, 1, 3)).reshape(t, n_scan)
    return jnp.tile(x, (1, LANES // n_scan))


def _from_v_layout(y, b, tc):
    t = y.shape[0]
    n_scan = 2 * b * N_HEADS
    v_lo = LANES // n_scan
    y = y.reshape(t, HEAD_DIM // v_lo, v_lo, 2, b, N_HEADS)
    y = jnp.transpose(y, (3, 4, 0, 5, 1, 2)).reshape(2, b, t, GROUP_WIDTH)
    return y[0] + _seg_flip(y[1], tc)


def _head_norm(y, gain, bias, eps, rms):
    b, t, _ = y.shape
    yh = y.reshape(b, t, N_HEADS, HEAD_DIM)
    if not rms:
        yh = yh - jnp.mean(yh, axis=-1, keepdims=True)
    yh = yh * lax.rsqrt(jnp.mean(jnp.square(yh), axis=-1, keepdims=True) + eps) * gain
    if bias is not None:
        yh = yh + bias
    return yh.reshape(b, t, GROUP_WIDTH)


def _rwkv7(p, tc, mu, w0, w2, a0, a2, g2, k_k, k_a, r_k, lnx_g, lnx_b):
    b, t, _ = p.shape
    gw = GROUP_WIDTH
    prev, nxt = _seg_neighbours(p, tc)
    p = p + (0.5 * (prev + nxt) - p) * mu
    r, k, v = p[..., :gw], p[..., gw:2 * gw], p[..., 2 * gw:3 * gw]
    o = 3 * gw
    wd = (p[..., o:o + 64], p[..., o + 64:o + 128])
    ad = (p[..., o + 128:o + 192], p[..., o + 192:o + 256])
    gd = p[..., o + 256:o + 256 + RWKV_GATE_LORA]
    kh = k.reshape(b, t, N_HEADS, HEAD_DIM)
    kk = kh * k_k.reshape(N_HEADS, HEAD_DIM)
    kk = kk / jnp.maximum(jnp.linalg.norm(kk, axis=-1, keepdims=True), 1e-12)
    kk = kk.reshape(b, t, gw)
    ws, ks, bs = [], [], []
    for d in range(2):
        lora_w = _mm(jnp.tanh(wd[d]).reshape(b * t, -1), w2[d]).reshape(b, t, gw)
        w_log = -jax.nn.softplus(-(w0[d] + lora_w)) - 0.5
        ws.append(jnp.exp(-jnp.exp(w_log)))
        a = jax.nn.sigmoid(a0[d] + _mm(ad[d].reshape(b * t, -1), a2[d]).reshape(b, t, gw))
        ks.append(k * (1.0 + (a - 1.0) * k_a))
        bs.append(kk * a)
    g = _mm(jax.nn.sigmoid(gd).reshape(b * t, -1), g2).reshape(b, t, gw)
    y = _scan("rwkv",
              [_k_layout(_dirs(r, r, tc)), _k_layout(_dirs(ws[0], ws[1], tc)),
               _k_layout(_dirs(ks[0], ks[1], tc)), _k_layout(_dirs(-kk, -kk, tc)),
               _k_layout(_dirs(bs[0], bs[1], tc))],
              _v_layout(_dirs(v, v, tc)))
    y = _from_v_layout(y, b, tc)
    y = _head_norm(y, lnx_g.reshape(N_HEADS, HEAD_DIM), lnx_b.reshape(N_HEADS, HEAD_DIM), RWKV_LNX_EPS, False)
    bonus = jnp.sum((r * (ks[0] + ks[1]) * r_k).reshape(b, t, N_HEADS, HEAD_DIM), axis=-1, keepdims=True)
    bonus = (bonus * v.reshape(b, t, N_HEADS, HEAD_DIM)).reshape(b, t, gw)
    return (y + bonus) * g


def _rope_tables(n):
    rows = n // GRID_W
    row = jnp.repeat(jnp.arange(rows, dtype=F32), GRID_W)
    col = jnp.tile(jnp.arange(GRID_W, dtype=F32), rows)
    n_freq = DIFF_QK_DIM // 4
    inv_freq = ROPE_BASE ** (-jnp.arange(n_freq, dtype=F32) / n_freq)
    ang = jnp.concatenate([row[:, None] * inv_freq, col[:, None] * inv_freq], axis=-1)
    return jnp.cos(ang), jnp.sin(ang)


def _rope(x, cos, sin):
    half = x.shape[-1] // 2
    x1, x2 = x[..., :half], x[..., half:]
    cos = cos[:, None, None, :]
    sin = sin[:, None, None, :]
    return jnp.concatenate([x1 * cos - x2 * sin, x2 * cos + x1 * sin], axis=-1)


def _diff_attn(p, tc, lam_vecs, norm_g, layer, need_ctx):
    b, t, _ = p.shape
    gw = GROUP_WIDTH
    lam_init = 0.8 - 0.6 * math.exp(-0.3 * layer)
    lv = lam_vecs.astype(F32)
    lam = jnp.exp(jnp.sum(lv[0] * lv[1])) - jnp.exp(jnp.sum(lv[2] * lv[3])) + lam_init
    q = p[..., :gw].reshape(b, t, N_HEADS, 2, DIFF_QK_DIM)
    k = p[..., gw:2 * gw].reshape(b, t, N_HEADS, 2, DIFF_QK_DIM)
    v = p[..., 2 * gw:3 * gw].reshape(b, t, N_HEADS, HEAD_DIM)
    cos, sin = _rope_tables(t - tc)
    q = jnp.concatenate([q[:, :tc], _rope(q[:, tc:], cos, sin)], axis=1)
    k = jnp.concatenate([k[:, :tc], _rope(k[:, tc:], cos, sin)], axis=1)
    heads = lambda z: jnp.transpose(z.reshape(b, t, N_HEADS, HEAD_DIM), (0, 2, 1, 3))
    qh, kh, vh = heads(q), heads(k), heads(v)
    out_scale = 1.0 - lam_init
    y_lat = _attention(qh[:, :, tc:], kh, vh, lam, norm_g, out_scale)
    if need_ctx:
        y_ctx = _attention(qh[:, :, :tc], kh[:, :, :tc], vh[:, :, :tc], lam, norm_g, out_scale)
    else:
        y_ctx = jnp.zeros((b, N_HEADS, tc, HEAD_DIM), F32)
    y = jnp.concatenate([y_ctx, y_lat], axis=2)
    return jnp.transpose(y, (0, 2, 1, 3)).reshape(b, t, gw)


def _hgrn2(p, tc, lb, norm_g):
    b, t, _ = p.shape
    gw = GROUP_WIDTH
    q, f_f, f_b, i, g = (p[..., j * gw:(j + 1) * gw] for j in range(5))
    q = jax.nn.silu(q)
    log_lb, log_1mlb = jnp.log(jnp.maximum(lb, LB_FLOOR)), jnp.log1p(-lb)
    decays, keys = [], []
    for f in (f_f, f_b):
        decays.append(jnp.exp(jnp.logaddexp(log_lb, log_1mlb + jax.nn.log_sigmoid(f))))
        keys.append((1.0 - lb) * jax.nn.sigmoid(-f))
    o = _scan("gla",
              [_k_layout(_dirs(q, q, tc)), _k_layout(_dirs(decays[0], decays[1], tc)),
               _k_layout(_dirs(keys[0], keys[1], tc))],
              _v_layout(_dirs(i, i, tc)))
    o = _from_v_layout(o, b, tc)
    return _head_norm(o, norm_g, None, LN_EPS, True) * jax.nn.silu(g)


def _mlstm(p, tc, conv_w, conv_b, gate_b, norm_g):
    b, t, _ = p.shape
    gw = GROUP_WIDTH
    qk, v, o = p[..., :2 * gw], p[..., 2 * gw:3 * gw], p[..., 3 * gw:4 * gw]
    gates = p[..., 4 * gw:4 * gw + 4 * N_HEADS]
    i_f, i_b, f_f, f_b = (gates[..., j * N_HEADS:(j + 1) * N_HEADS] for j in range(4))
    prev, nxt = _seg_neighbours(qk, tc)
    qk = jax.nn.silu(prev * conv_w[0] + qk * conv_w[1] + nxt * conv_w[2] + conv_b)
    q, k = qk[..., :gw], qk[..., gw:] * HEAD_DIM ** -0.5
    ig = jnp.stack([i_f + gate_b[0], _seg_flip(i_b + gate_b[1], tc)], axis=0)
    fg = jnp.stack([jax.nn.log_sigmoid(f_f + gate_b[2]), _seg_flip(jax.nn.log_sigmoid(f_b + gate_b[3]), tc)], axis=0)
    h = _scan("mlstm", [_k_layout(_dirs(q, q, tc)), _k_layout(_dirs(k, k, tc))],
              _v_layout(_dirs(v, v, tc)), gates=(_gate_layout(ig), _gate_layout(fg)))
    h = _from_v_layout(h, b, tc)
    return _head_norm(h, norm_g.reshape(N_HEADS, HEAD_DIM), None, LN_EPS, False) * jax.nn.sigmoid(o)


RWKV_COLS = 3 * GROUP_WIDTH + 2 * RWKV_DECAY_LORA + 2 * RWKV_ICLR_LORA + RWKV_GATE_LORA
DIFF_COLS = 3 * GROUP_WIDTH
HGRN_COLS = 5 * GROUP_WIDTH
MLSTM_COLS = 4 * GROUP_WIDTH + 4 * N_HEADS
_pad = lambda n: -(-n // LANES) * LANES
SLAB_OFFSETS = np.cumsum([0, _pad(RWKV_COLS), _pad(DIFF_COLS), _pad(HGRN_COLS), _pad(MLSTM_COLS)])
SLAB_COLS = (RWKV_COLS, DIFF_COLS, HGRN_COLS, MLSTM_COLS)


def _pad_cols(w):
    parts, off = [], 0
    for n in SLAB_COLS:
        parts.append(jnp.pad(w[:, off:off + n], ((0, 0), (0, _pad(n) - n))))
        off += n
    return jnp.concatenate(parts, axis=1)


def kernel(x, c, ctx, c_ctx, ada_w, ada_b, w_in, rwkv_mu, rwkv_w0, rwkv_w2, rwkv_a0, rwkv_a2, rwkv_g2, rwkv_k_k, rwkv_k_a, rwkv_r_k, rwkv_lnx_g, rwkv_lnx_b, diff_lambda, diff_norm_g, hgrn_lb_logits, hgrn_norm_g, mlstm_conv_w, mlstm_conv_b, mlstm_gate_b, mlstm_norm_g, w_out, ffn_w_up, ffn_conv_w, ffn_conv_b, ffn_w_down, ln_g, ln_b):
    depth = w_in.shape[0]
    b, seq, d = x.shape
    tc = ctx.shape[1]
    assert tc == ROW_TILE and seq % ROW_TILE == 0 and 2 * b * N_HEADS <= LANES
    alpha = (2.0 * depth) ** 0.25

    lb_w = jax.nn.softmax(hgrn_lb_logits.astype(F32), axis=0)
    lower_bounds = jnp.cumsum(lb_w, axis=0) - lb_w[0]

    cond = jnp.zeros((8, d), F32).at[:b].set(c).at[b].set(c_ctx)
    h = jnp.concatenate([ctx, x], axis=1)
    for layer in range(depth):
        last = layer == depth - 1
        mod = _adaln(cond, ada_w[layer], ada_b[layer]).reshape(8, 6, d)
        mods_l = [mod[:b, j][:, None, :] for j in range(6)]
        mods_c = [mod[b:b + 1, j][:, None, :] for j in range(6)]

        p = _inproj(h, mods_l[0], mods_l[1], mods_c[0], mods_c[1], _pad_cols(w_in[layer]).astype(BF16), 1)
        slab = lambda j: p[..., SLAB_OFFSETS[j]:SLAB_OFFSETS[j] + SLAB_COLS[j]]
        y_a = _rwkv7(slab(0), tc, rwkv_mu[layer], rwkv_w0[layer], rwkv_w2[layer], rwkv_a0[layer], rwkv_a2[layer],
                     rwkv_g2[layer], rwkv_k_k[layer], rwkv_k_a[layer], rwkv_r_k[layer], rwkv_lnx_g[layer],
                     rwkv_lnx_b[layer])
        y_b = _diff_attn(slab(1), tc, diff_lambda[layer], diff_norm_g[layer], layer, not last)
        y_c = _hgrn2(slab(2), tc, lower_bounds[layer], hgrn_norm_g[layer])
        y_d = _mlstm(slab(3), tc, mlstm_conv_w[layer], mlstm_conv_b[layer], mlstm_gate_b[layer], mlstm_norm_g[layer])
        y = jnp.concatenate([y_a, y_b, y_c, y_d], axis=-1)

        if last:
            y, h, ctx_tiles = y[:, tc:], h[:, tc:], 0
        else:
            ctx_tiles = 1
        h = _outproj(y, h, mods_l[2], mods_c[2], w_out[layer].astype(BF16), ln_g[layer, 0], ln_b[layer, 0],
                     ctx_tiles, alpha)
        h = _ffn(h, mods_l[3:6], mods_c[3:6], ffn_w_up[layer].astype(BF16), ffn_conv_w[layer], ffn_conv_b[layer],
                 ffn_w_down[layer].astype(BF16), ln_g[layer, 1], ln_b[layer, 1], ctx_tiles, alpha)
    return h
```

```python
import functools
import math

import jax
import jax.numpy as jnp
import numpy as np
from jax import lax
from jax.experimental import pallas as pl
from jax.experimental.pallas import tpu as pltpu

F32 = jnp.float32
BF16 = jnp.bfloat16
HIGHEST = lax.Precision.HIGHEST

HEAD_DIM = 64
N_HEADS = 4
GROUP_WIDTH = N_HEADS * HEAD_DIM
DIFF_QK_DIM = HEAD_DIM // 2
GRID_W = 64
ROPE_BASE = 10000.0
RWKV_DECAY_LORA = 64
RWKV_ICLR_LORA = 64
RWKV_GATE_LORA = 160
RWKV_LNX_EPS = 64e-5
LN_EPS = 1e-5
LB_FLOOR = 1e-30
MAX_NEG_LOG_STAB = 60.0

LANES = 128
ROW_TILE = 256
ATTN_TILE = 128
TIME_BLOCK = 128
SCAN_TB = 64
VMEM_LIMIT = 56 * 1024 * 1024


def _cparams(sem):
    return pltpu.CompilerParams(dimension_semantics=sem, vmem_limit_bytes=VMEM_LIMIT)


def _rev_block(i, n_ctx, n_all):
    return jnp.where(i < n_ctx, n_ctx - 1 - i, n_ctx + n_all - 1 - i)


def _flip_rows(x):
    n = x.shape[0]
    r = lax.broadcasted_iota(jnp.int32, (n, n), 0)
    c = lax.broadcasted_iota(jnp.int32, (n, n), 1)
    perm = jnp.where(r + c == n - 1, 1.0, 0.0).astype(F32)
    return jnp.dot(perm, x, precision=HIGHEST, preferred_element_type=F32)


def _adaln_body(c_ref, w_ref, b_ref, o_ref):
    x = c_ref[...]
    x = (x * jax.nn.sigmoid(x)).astype(BF16)
    o_ref[...] = jnp.dot(x, w_ref[...].astype(BF16), preferred_element_type=F32) + b_ref[...]


def _adaln(cond, w, b):
    m, d = cond.shape
    n = w.shape[1]
    tn = 1536
    return pl.pallas_call(
        _adaln_body,
        grid=(n // tn,),
        in_specs=[pl.BlockSpec((m, d), lambda j: (0, 0)),
                  pl.BlockSpec((d, tn), lambda j: (0, j)),
                  pl.BlockSpec((1, tn), lambda j: (0, j))],
        out_specs=pl.BlockSpec((m, tn), lambda j: (0, j)),
        out_shape=jax.ShapeDtypeStruct((m, n), F32),
        compiler_params=_cparams(("arbitrary",)),
        name="adaln",
    )(cond, w, b.reshape(1, n))


def _inproj_body(x_ref, sl_ref, cl_ref, sc_ref, cc_ref, w_ref, o_ref, *, ctx_tiles):
    is_ctx = pl.program_id(1) < ctx_tiles
    shift = jnp.where(is_ctx, sc_ref[0], sl_ref[0])
    scale = jnp.where(is_ctx, cc_ref[0], cl_ref[0])
    xm = (x_ref[0] * (1.0 + scale) + shift).astype(BF16)
    o_ref[0] = jnp.dot(xm, w_ref[...], preferred_element_type=F32)


def _inproj(h, shift_l, scale_l, shift_c, scale_c, w, ctx_tiles):
    b, t, d = h.shape
    n = w.shape[1]
    row = pl.BlockSpec((1, ROW_TILE, d), lambda bi, i: (bi, i, 0))
    lat = pl.BlockSpec((1, 1, d), lambda bi, i: (bi, 0, 0))
    ctx = pl.BlockSpec((1, 1, d), lambda bi, i: (0, 0, 0))
    return pl.pallas_call(
        functools.partial(_inproj_body, ctx_tiles=ctx_tiles),
        grid=(b, t // ROW_TILE),
        in_specs=[row, lat, lat, ctx, ctx, pl.BlockSpec((d, n), lambda bi, i: (0, 0))],
        out_specs=pl.BlockSpec((1, ROW_TILE, n), lambda bi, i: (bi, i, 0)),
        out_shape=jax.ShapeDtypeStruct((b, t, n), F32),
        compiler_params=_cparams(("parallel", "parallel")),
        name="inproj",
    )(h, shift_l, scale_l, shift_c, scale_c, w)


def _mm_body(x_ref, w_ref, o_ref):
    o_ref[...] = jnp.dot(x_ref[...].astype(BF16), w_ref[...], preferred_element_type=F32)


def _mm(x, w):
    m, k = x.shape
    n = w.shape[1]
    tm = 2176 if m % 2176 == 0 else m
    return pl.pallas_call(
        _mm_body,
        grid=(m // tm,),
        in_specs=[pl.BlockSpec((tm, k), lambda i: (i, 0)), pl.BlockSpec((k, n), lambda i: (0, 0))],
        out_specs=pl.BlockSpec((tm, n), lambda i: (i, 0)),
        out_shape=jax.ShapeDtypeStruct((m, n), F32),
        compiler_params=_cparams(("parallel",)),
        name="lora_mm",
    )(x, w.astype(BF16))


def _to_scan_body(x0_ref, x1_ref, o_ref, r_ref, *, kind, nb):
    n_scan = 2 * nb * N_HEADS
    rep = LANES // n_scan
    for b in range(nb):
        r_ref[pl.ds(b * GROUP_WIDTH, GROUP_WIDTH), :] = x0_ref[b].T
        r_ref[pl.ds((nb + b) * GROUP_WIDTH, GROUP_WIDTH), :] = _flip_rows(x1_ref[b]).T
    if kind == "k":
        for k in range(HEAD_DIM):
            rows = r_ref[pl.ds(k, n_scan, stride=HEAD_DIM), :]
            o_ref[k] = jnp.concatenate([rows] * rep, axis=0).T
    else:
        for vh in range(HEAD_DIM // rep):
            rows = [r_ref[pl.ds(vh * rep + vl, n_scan, stride=HEAD_DIM), :] for vl in range(rep)]
            o_ref[:, vh, :] = jnp.concatenate(rows, axis=0).T


def _to_scan(x0, x1, kind, tc):
    nb, t, gw = x0.shape
    n_all, n_ctx = t // TIME_BLOCK, tc // TIME_BLOCK
    n_scan = 2 * nb * N_HEADS
    rep = LANES // n_scan
    fwd = pl.BlockSpec((nb, TIME_BLOCK, gw), lambda i: (0, i, 0))
    bwd = pl.BlockSpec((nb, TIME_BLOCK, gw), lambda i: (0, _rev_block(i, n_ctx, n_all), 0))
    if kind == "k":
        out_shape = (HEAD_DIM, t, LANES)
        out_spec = pl.BlockSpec((HEAD_DIM, TIME_BLOCK, LANES), lambda i: (0, i, 0))
    else:
        out_shape = (t, HEAD_DIM // rep, LANES)
        out_spec = pl.BlockSpec((TIME_BLOCK, HEAD_DIM // rep, LANES), lambda i: (i, 0, 0))
    return pl.pallas_call(
        functools.partial(_to_scan_body, kind=kind, nb=nb),
        grid=(n_all,),
        in_specs=[fwd, bwd],
        out_specs=out_spec,
        out_shape=jax.ShapeDtypeStruct(out_shape, F32),
        scratch_shapes=[pltpu.VMEM((n_scan * HEAD_DIM, TIME_BLOCK), F32)],
        compiler_params=_cparams(("parallel",)),
        name="to_scan_" + kind,
    )(x0, x1)


def _from_scan_body(yf_ref, yb_ref, o_ref, r_ref, *, nb):
    n_scan = 2 * nb * N_HEADS
    rep = LANES // n_scan
    for vh in range(HEAD_DIM // rep):
        r_ref[0, pl.ds(vh * LANES, LANES), :] = yf_ref[:, vh, :].T
        r_ref[1, pl.ds(vh * LANES, LANES), :] = yb_ref[:, vh, :].T
    for b in range(nb):
        slab = lambda d: jnp.concatenate(
            [r_ref[d, pl.ds((d * nb + b) * N_HEADS + h, HEAD_DIM, stride=n_scan), :] for h in range(N_HEADS)],
            axis=0).T
        o_ref[b] = slab(0) + _flip_rows(slab(1))


def _from_scan(y, nb, tc):
    t, n_vh, _ = y.shape
    n_all, n_ctx = t // TIME_BLOCK, tc // TIME_BLOCK
    fwd = pl.BlockSpec((TIME_BLOCK, n_vh, LANES), lambda i: (i, 0, 0))
    bwd = pl.BlockSpec((TIME_BLOCK, n_vh, LANES), lambda i: (_rev_block(i, n_ctx, n_all), 0, 0))
    return pl.pallas_call(
        functools.partial(_from_scan_body, nb=nb),
        grid=(n_all,),
        in_specs=[fwd, bwd],
        out_specs=pl.BlockSpec((nb, TIME_BLOCK, GROUP_WIDTH), lambda i: (0, i, 0)),
        out_shape=jax.ShapeDtypeStruct((nb, t, GROUP_WIDTH), F32),
        scratch_shapes=[pltpu.VMEM((2, n_vh * LANES, TIME_BLOCK), F32)],
        compiler_params=_cparams(("parallel",)),
        name="from_scan",
    )(y, y)


def _gate_layout(fwd, bwd, tc):
    flip = lambda z: jnp.concatenate([jnp.flip(z[:, :tc], axis=1), jnp.flip(z[:, tc:], axis=1)], axis=1)
    x = jnp.stack([fwd, flip(bwd)], axis=0)
    d, b, t, h = x.shape
    x = jnp.transpose(x, (2, 0, 1, 3)).reshape(t, d * b * h)
    return jnp.tile(x, (1, LANES // (d * b * h)))


N_ACC = 4


def _acc_add(acc, i, x):
    acc[i % N_ACC] = x if acc[i % N_ACC] is None else acc[i % N_ACC] + x


def _acc_total(acc):
    return (acc[0] + acc[1]) + (acc[2] + acc[3])


def _scan_body(*refs, mode, tb, n_vh):
    if mode == "rwkv":
        r_ref, w_ref, k_ref, a_ref, b_ref, v_ref, y_ref, s_ref = refs
    elif mode == "gla":
        r_ref, w_ref, k_ref, v_ref, y_ref, s_ref = refs
    else:
        r_ref, k_ref, v_ref, ig_ref, fg_ref, y_ref, s_ref, n_ref, m_ref = refs

    @pl.when(pl.program_id(0) == 0)
    def _():
        s_ref[...] = jnp.zeros_like(s_ref)
        if mode == "mlstm":
            n_ref[...] = jnp.zeros_like(n_ref)
            m_ref[...] = jnp.zeros_like(m_ref)

    row = lambda ref, k, j: ref[k, pl.ds(j, 1), :]
    wide = lambda x: jnp.broadcast_to(x, (n_vh, LANES))

    def rwkv_step(j, sa):
        v = v_ref[j]
        j_next = jnp.minimum(j + 1, tb - 1)
        ys, sas = [None] * N_ACC, [None] * N_ACC
        for k in range(HEAD_DIM):
            s = s_ref[k] * wide(row(w_ref, k, j)) + sa * wide(row(b_ref, k, j)) + v * wide(row(k_ref, k, j))
            s_ref[k] = s
            _acc_add(ys, k, s * wide(row(r_ref, k, j)))
            _acc_add(sas, k, s * wide(row(a_ref, k, j_next)))
        y_ref[j] = _acc_total(ys)
        return _acc_total(sas)

    def gla_step(j, carry):
        v = v_ref[j]
        ys = [None] * N_ACC
        for k in range(HEAD_DIM):
            s = s_ref[k] * wide(row(w_ref, k, j)) + v * wide(row(k_ref, k, j))
            s_ref[k] = s
            _acc_add(ys, k, s * wide(row(r_ref, k, j)))
        y_ref[j] = _acc_total(ys)
        return carry

    def mlstm_step(j, carry):
        i_t = ig_ref[pl.ds(j, 1), :]
        f_t = fg_ref[pl.ds(j, 1), :]
        m_prev = m_ref[0:1, :]
        m_t = jnp.maximum(f_t + m_prev, i_t)
        alpha = jnp.exp(f_t + m_prev - m_t)
        beta = jnp.exp(i_t - m_t)
        m_ref[0:1, :] = m_t
        alpha_w = wide(alpha)
        v = v_ref[j]
        nums, dens = [None] * N_ACC, [None] * N_ACC
        for k in range(HEAD_DIM):
            q1 = row(r_ref, k, j)
            bk = beta * row(k_ref, k, j)
            n1 = alpha * n_ref[k:k + 1, :] + bk
            n_ref[k:k + 1, :] = n1
            _acc_add(dens, k, n1 * q1)
            c = alpha_w * s_ref[k] + v * wide(bk)
            s_ref[k] = c
            _acc_add(nums, k, c * wide(q1))
        floor = jnp.exp(jnp.minimum(-m_t, MAX_NEG_LOG_STAB))
        inv = 1.0 / jnp.maximum(jnp.abs(_acc_total(dens)), floor)
        y_ref[j] = _acc_total(nums) * wide(inv)
        return carry

    if mode == "rwkv":
        sa0 = [None] * N_ACC
        for k in range(HEAD_DIM):
            _acc_add(sa0, k, s_ref[k] * wide(row(a_ref, k, 0)))
        lax.fori_loop(0, tb, rwkv_step, _acc_total(sa0))
    elif mode == "gla":
        lax.fori_loop(0, tb, gla_step, 0)
    else:
        lax.fori_loop(0, tb, mlstm_step, 0)


def _scan(mode, k_inputs, v_in, gates=()):
    t, n_vh, _ = v_in.shape
    tb = SCAN_TB
    big = pl.BlockSpec((HEAD_DIM, tb, LANES), lambda i: (0, i, 0))
    small = pl.BlockSpec((tb, n_vh, LANES), lambda i: (i, 0, 0))
    gate = pl.BlockSpec((tb, LANES), lambda i: (i, 0))
    scratch = [pltpu.VMEM((HEAD_DIM, n_vh, LANES), F32)]
    if mode == "mlstm":
        scratch += [pltpu.VMEM((HEAD_DIM, LANES), F32), pltpu.VMEM((8, LANES), F32)]
    return pl.pallas_call(
        functools.partial(_scan_body, mode=mode, tb=tb, n_vh=n_vh),
        grid=(t // tb,),
        in_specs=[big] * len(k_inputs) + [small] + [gate] * len(gates),
        out_specs=small,
        out_shape=jax.ShapeDtypeStruct((t, n_vh, LANES), F32),
        scratch_shapes=scratch,
        compiler_params=_cparams(("arbitrary",)),
        name="scan_" + mode,
    )(*k_inputs, v_in, *gates)


def _attn_body(q_ref, k_ref, v_ref, lam_ref, g_ref, o_ref, *, out_scale):
    q = q_ref[0] * (DIFF_QK_DIM ** -0.5)
    k = k_ref[0].astype(BF16)
    v = v_ref[0].astype(BF16)
    lane = lax.broadcasted_iota(jnp.int32, q.shape, 1)
    comp = lane // DIFF_QK_DIM
    nt = (((1,), (1,)), ((), ()))

    def softmax_map(c):
        s = lax.dot_general(jnp.where(comp == c, q, 0.0).astype(BF16), k, nt, preferred_element_type=F32)
        e = jnp.exp(s - jnp.max(s, axis=-1, keepdims=True))
        return e, jnp.sum(e, axis=-1, keepdims=True)

    ys = []
    for hh in range(LANES // HEAD_DIM):
        e0, l0 = softmax_map(2 * hh)
        e1, l1 = softmax_map(2 * hh + 1)
        pr = (e0 * (1.0 / l0) - e1 * (lam_ref[0:1, 0:1] / l1)).astype(BF16)
        ys.append(jnp.dot(pr, v, preferred_element_type=F32))
    first = lane < HEAD_DIM
    y = jnp.where(first, ys[0], ys[1])
    ysq = y * y
    ms = jnp.where(first, jnp.sum(jnp.where(first, ysq, 0.0), axis=-1, keepdims=True),
                   jnp.sum(jnp.where(first, 0.0, ysq), axis=-1, keepdims=True)) * (1.0 / HEAD_DIM)
    o_ref[0] = y * lax.rsqrt(ms + LN_EPS) * g_ref[...] * out_scale


def _attention(q, k, v, lam, norm_g, out_scale, q_block0, n_q_blocks, n_keys, prev=None):
    b, t, gw = q.shape
    qspec = pl.BlockSpec((1, ATTN_TILE, LANES), lambda bi, pi, i: (bi, q_block0 + i, pi))
    kspec = pl.BlockSpec((1, n_keys, LANES), lambda bi, pi, i: (bi, 0, pi))
    in_specs = [qspec, kspec, kspec,
                pl.BlockSpec((1, LANES), lambda bi, pi, i: (0, 0)),
                pl.BlockSpec((1, LANES), lambda bi, pi, i: (0, 0))]
    args = [q, k, v, jnp.full((1, LANES), lam, F32), jnp.tile(norm_g.reshape(1, HEAD_DIM), (1, LANES // HEAD_DIM))]
    aliases = {}
    body = functools.partial(_attn_body, out_scale=out_scale)
    if prev is not None:
        in_specs.append(pl.BlockSpec(memory_space=pl.ANY))
        args.append(prev)
        aliases = {5: 0}
        body = lambda q_ref, k_ref, v_ref, lam_ref, g_ref, prev_ref, o_ref: _attn_body(
            q_ref, k_ref, v_ref, lam_ref, g_ref, o_ref, out_scale=out_scale)
    return pl.pallas_call(
        body,
        grid=(b, gw // LANES, n_q_blocks),
        in_specs=in_specs,
        out_specs=qspec,
        out_shape=jax.ShapeDtypeStruct((b, t, gw), F32),
        input_output_aliases=aliases,
        compiler_params=_cparams(("parallel", "parallel", "parallel")),
        name="diff_attn",
    )(*args)


def _layernorm(z, g, b):
    z = z - jnp.mean(z, axis=-1, keepdims=True)
    return z * lax.rsqrt(jnp.mean(z * z, axis=-1, keepdims=True) + LN_EPS) * g + b


def _outproj_body(ya_ref, yb_ref, yc_ref, yd_ref, h_ref, gl_ref, gc_ref, w_ref, g_ref, b_ref, o_ref, *,
                  ctx_tiles, alpha):
    is_ctx = pl.program_id(1) < ctx_tiles
    gate = jnp.where(is_ctx, gc_ref[0], gl_ref[0])
    y = None
    for m, y_ref in enumerate((ya_ref, yb_ref, yc_ref, yd_ref)):
        part = jnp.dot(y_ref[0].astype(BF16), w_ref[m * GROUP_WIDTH:(m + 1) * GROUP_WIDTH, :],
                       preferred_element_type=F32)
        y = part if y is None else y + part
    o_ref[0] = _layernorm(alpha * h_ref[0] + gate * y, g_ref[...], b_ref[...])


def _outproj(ys, h, gate_l, gate_c, w, ln_g, ln_b, tile0, ctx_tiles, alpha):
    b, t, d = h.shape
    n_tiles = t // ROW_TILE - tile0
    yrow = pl.BlockSpec((1, ROW_TILE, GROUP_WIDTH), lambda bi, i: (bi, tile0 + i, 0))
    hrow = pl.BlockSpec((1, ROW_TILE, d), lambda bi, i: (bi, tile0 + i, 0))
    lat = pl.BlockSpec((1, 1, d), lambda bi, i: (bi, 0, 0))
    ctx = pl.BlockSpec((1, 1, d), lambda bi, i: (0, 0, 0))
    vec = pl.BlockSpec((1, d), lambda bi, i: (0, 0))
    return pl.pallas_call(
        functools.partial(_outproj_body, ctx_tiles=ctx_tiles, alpha=alpha),
        grid=(b, n_tiles),
        in_specs=[yrow] * 4 + [hrow, lat, ctx, pl.BlockSpec(w.shape, lambda bi, i: (0, 0)), vec, vec],
        out_specs=pl.BlockSpec((1, ROW_TILE, d), lambda bi, i: (bi, i, 0)),
        out_shape=jax.ShapeDtypeStruct((b, n_tiles * ROW_TILE, d), F32),
        compiler_params=_cparams(("parallel", "parallel")),
        name="outproj_ln",
    )(*ys, h, gate_l, gate_c, w, ln_g.reshape(1, d), ln_b.reshape(1, d))


FFN_CHUNKS = 2


def _ffn_body(x_ref, xp_ref, xn_ref, sl_ref, cl_ref, gl_ref, sc_ref, cc_ref, gc_ref,
              wu_ref, cw_ref, cb_ref, wd_ref, g_ref, b_ref, o_ref, *, ctx_tiles, n_tiles, alpha, d_ff):
    i = pl.program_id(1)
    is_ctx = i < ctx_tiles
    shift = jnp.where(is_ctx, sc_ref[0], sl_ref[0])
    scale = jnp.where(is_ctx, cc_ref[0], cl_ref[0])
    gate_mod = jnp.where(is_ctx, gc_ref[0], gl_ref[0])
    seg_start = jnp.logical_or(i == 0, i == ctx_tiles)
    seg_end = jnp.logical_or(i == ctx_tiles - 1, i == n_tiles - 1)

    x = x_ref[0]
    xm = (x * (1.0 + scale) + shift).astype(BF16)
    xp = (xp_ref[0] * (1.0 + scale) + shift).astype(BF16)
    xn = (xn_ref[0] * (1.0 + scale) + shift).astype(BF16)
    rows = x.shape[0]
    row_id = lax.broadcasted_iota(jnp.int32, (rows, 1), 0)
    ch = d_ff // FFN_CHUNKS
    f = jnp.zeros(x.shape, F32)
    for c in range(FFN_CHUNKS):
        wg = wu_ref[:, c * ch:(c + 1) * ch]
        wv = wu_ref[:, d_ff + c * ch:d_ff + (c + 1) * ch]
        gate = jnp.dot(xm, wg, preferred_element_type=F32)
        val = jnp.dot(xm, wv, preferred_element_type=F32)
        gp = jnp.dot(xp, wg, preferred_element_type=F32)[7:8, :]
        gn = jnp.dot(xn, wg, preferred_element_type=F32)[0:1, :]
        gp = jnp.where(seg_start, 0.0, gp)
        gn = jnp.where(seg_end, 0.0, gn)
        prev = jnp.where(row_id == 0, gp, pltpu.roll(gate, 1, 0))
        nxt = jnp.where(row_id == rows - 1, gn, pltpu.roll(gate, rows - 1, 0))
        cw = cw_ref[:, c * ch:(c + 1) * ch]
        conv = prev * cw[0:1, :] + gate * cw[1:2, :] + nxt * cw[2:3, :] + cb_ref[:, c * ch:(c + 1) * ch]
        act = (jax.nn.gelu(conv) * val).astype(BF16)
        f = f + jnp.dot(act, wd_ref[c * ch:(c + 1) * ch, :], preferred_element_type=F32)
    o_ref[0] = _layernorm(alpha * x + gate_mod * f, g_ref[...], b_ref[...])


def _ffn(h, mods_l, mods_c, w_up, conv_w, conv_b, w_down, ln_g, ln_b, ctx_tiles, alpha):
    b, t, d = h.shape
    d_ff = w_down.shape[0]
    n_tiles = t // ROW_TILE
    sub = ROW_TILE // 8
    n_sub = t // 8
    row = pl.BlockSpec((1, ROW_TILE, d), lambda bi, i: (bi, i, 0))
    prev = pl.BlockSpec((1, 8, d), lambda bi, i: (bi, jnp.maximum(i * sub - 1, 0), 0))
    nxt = pl.BlockSpec((1, 8, d), lambda bi, i: (bi, jnp.minimum((i + 1) * sub, n_sub - 1), 0))
    lat = pl.BlockSpec((1, 1, d), lambda bi, i: (bi, 0, 0))
    ctx = pl.BlockSpec((1, 1, d), lambda bi, i: (0, 0, 0))
    vec = pl.BlockSpec((1, d), lambda bi, i: (0, 0))
    full = lambda a: pl.BlockSpec(a.shape, lambda bi, i: (0,) * a.ndim)
    cb = conv_b.reshape(1, d_ff)
    return pl.pallas_call(
        functools.partial(_ffn_body, ctx_tiles=ctx_tiles, n_tiles=n_tiles, alpha=alpha, d_ff=d_ff),
        grid=(b, n_tiles),
        in_specs=[row, prev, nxt, lat, lat, lat, ctx, ctx, ctx,
                  full(w_up), full(conv_w), full(cb), full(w_down), vec, vec],
        out_specs=row,
        out_shape=jax.ShapeDtypeStruct((b, t, d), F32),
        compiler_params=_cparams(("parallel", "parallel")),
        name="ffn",
    )(h, h, h, *mods_l, *mods_c, w_up, conv_w, cb, w_down, ln_g.reshape(1, d), ln_b.reshape(1, d))


def _seg_neighbours(z, tc):
    zero = jnp.zeros_like(z[:, :1])
    prev = jnp.concatenate([zero, z[:, :-1]], axis=1)
    nxt = jnp.concatenate([z[:, 1:], zero], axis=1)
    if 0 < tc < z.shape[1]:
        t = jnp.arange(z.shape[1])[None, :, None]
        prev = jnp.where(t == tc, 0.0, prev)
        nxt = jnp.where(t == tc - 1, 0.0, nxt)
    return prev, nxt


def _head_norm(y, gain, bias, eps, rms):
    b, t, _ = y.shape
    yh = y.reshape(b, t, N_HEADS, HEAD_DIM)
    if not rms:
        yh = yh - jnp.mean(yh, axis=-1, keepdims=True)
    yh = yh * lax.rsqrt(jnp.mean(jnp.square(yh), axis=-1, keepdims=True) + eps) * gain
    if bias is not None:
        yh = yh + bias
    return yh.reshape(b, t, GROUP_WIDTH)


def _rwkv7(p, tc, mu, w0, w2, a0, a2, g2, k_k, k_a, r_k, lnx_g, lnx_b):
    b, t, _ = p.shape
    gw = GROUP_WIDTH
    prev, nxt = _seg_neighbours(p, tc)
    p = p + (0.5 * (prev + nxt) - p) * mu
    r, k, v = p[..., :gw], p[..., gw:2 * gw], p[..., 2 * gw:3 * gw]
    o = 3 * gw
    wd = (p[..., o:o + 64], p[..., o + 64:o + 128])
    ad = (p[..., o + 128:o + 192], p[..., o + 192:o + 256])
    gd = p[..., o + 256:o + 256 + RWKV_GATE_LORA]
    kh = k.reshape(b, t, N_HEADS, HEAD_DIM)
    kk = kh * k_k.reshape(N_HEADS, HEAD_DIM)
    kk = kk / jnp.maximum(jnp.linalg.norm(kk, axis=-1, keepdims=True), 1e-12)
    kk = kk.reshape(b, t, gw)
    ws, ks, bs = [], [], []
    for d in range(2):
        lora_w = _mm(jnp.tanh(wd[d]).reshape(b * t, -1), w2[d]).reshape(b, t, gw)
        w_log = -jax.nn.softplus(-(w0[d] + lora_w)) - 0.5
        ws.append(jnp.exp(-jnp.exp(w_log)))
        a = jax.nn.sigmoid(a0[d] + _mm(ad[d].reshape(b * t, -1), a2[d]).reshape(b, t, gw))
        ks.append(k * (1.0 + (a - 1.0) * k_a))
        bs.append(kk * a)
    g = _mm(jax.nn.sigmoid(gd).reshape(b * t, -1), g2).reshape(b, t, gw)
    neg_kk = -kk
    y = _scan("rwkv",
              [_to_scan(r, r, "k", tc), _to_scan(ws[0], ws[1], "k", tc), _to_scan(ks[0], ks[1], "k", tc),
               _to_scan(neg_kk, neg_kk, "k", tc), _to_scan(bs[0], bs[1], "k", tc)],
              _to_scan(v, v, "v", tc))
    y = _from_scan(y, b, tc)
    y = _head_norm(y, lnx_g.reshape(N_HEADS, HEAD_DIM), lnx_b.reshape(N_HEADS, HEAD_DIM), RWKV_LNX_EPS, False)
    bonus = jnp.sum((r * (ks[0] + ks[1]) * r_k).reshape(b, t, N_HEADS, HEAD_DIM), axis=-1, keepdims=True)
    bonus = (bonus * v.reshape(b, t, N_HEADS, HEAD_DIM)).reshape(b, t, gw)
    return (y + bonus) * g


def _rope_tables(n):
    rows = n // GRID_W
    row = jnp.repeat(jnp.arange(rows, dtype=F32), GRID_W)
    col = jnp.tile(jnp.arange(GRID_W, dtype=F32), rows)
    n_freq = DIFF_QK_DIM // 4
    inv_freq = ROPE_BASE ** (-jnp.arange(n_freq, dtype=F32) / n_freq)
    ang = jnp.concatenate([row[:, None] * inv_freq, col[:, None] * inv_freq], axis=-1)
    return jnp.cos(ang), jnp.sin(ang)


def _rope(x, cos, sin, tc):
    b, t, gw = x.shape
    half = DIFF_QK_DIM // 2
    cos = jnp.concatenate([jnp.ones((tc, half), F32), cos], axis=0)[:, None, :]
    sin = jnp.concatenate([jnp.zeros((tc, half), F32), sin], axis=0)[:, None, :]
    x = x.reshape(b, t, gw // DIFF_QK_DIM, DIFF_QK_DIM)
    x1, x2 = x[..., :half], x[..., half:]
    return jnp.concatenate([x1 * cos - x2 * sin, x2 * cos + x1 * sin], axis=-1).reshape(b, t, gw)


def _diff_attn(p, tc, lam_vecs, norm_g, layer, need_ctx):
    b, t, _ = p.shape
    gw = GROUP_WIDTH
    lam_init = 0.8 - 0.6 * math.exp(-0.3 * layer)
    lv = lam_vecs.astype(F32)
    lam = jnp.exp(jnp.sum(lv[0] * lv[1])) - jnp.exp(jnp.sum(lv[2] * lv[3])) + lam_init
    cos, sin = _rope_tables(t - tc)
    q = _rope(p[..., :gw], cos, sin, tc)
    k = _rope(p[..., gw:2 * gw], cos, sin, tc)
    v = p[..., 2 * gw:3 * gw]
    out_scale = 1.0 - lam_init
    n_ctx, n_all = tc // ATTN_TILE, t // ATTN_TILE
    y = _attention(q, k, v, lam, norm_g, out_scale, n_ctx, n_all - n_ctx, t)
    if need_ctx:
        y = _attention(q, k, v, lam, norm_g, out_scale, 0, n_ctx, tc, prev=y)
    return y


def _hgrn2(p, tc, lb, norm_g):
    b, t, _ = p.shape
    gw = GROUP_WIDTH
    q, f_f, f_b, i, g = (p[..., j * gw:(j + 1) * gw] for j in range(5))
    q = jax.nn.silu(q)
    log_lb, log_1mlb = jnp.log(jnp.maximum(lb, LB_FLOOR)), jnp.log1p(-lb)
    decays, keys = [], []
    for f in (f_f, f_b):
        decays.append(jnp.exp(jnp.logaddexp(log_lb, log_1mlb + jax.nn.log_sigmoid(f))))
        keys.append((1.0 - lb) * jax.nn.sigmoid(-f))
    o = _scan("gla",
              [_to_scan(q, q, "k", tc), _to_scan(decays[0], decays[1], "k", tc),
               _to_scan(keys[0], keys[1], "k", tc)],
              _to_scan(i, i, "v", tc))
    o = _from_scan(o, b, tc)
    return _head_norm(o, norm_g, None, LN_EPS, True) * jax.nn.silu(g)


def _mlstm(p, tc, conv_w, conv_b, gate_b, norm_g):
    b, t, _ = p.shape
    gw = GROUP_WIDTH
    qk, v, o = p[..., :2 * gw], p[..., 2 * gw:3 * gw], p[..., 3 * gw:4 * gw]
    gates = p[..., 4 * gw:4 * gw + 4 * N_HEADS]
    i_f, i_b, f_f, f_b = (gates[..., j * N_HEADS:(j + 1) * N_HEADS] for j in range(4))
    prev, nxt = _seg_neighbours(qk, tc)
    qk = jax.nn.silu(prev * conv_w[0] + qk * conv_w[1] + nxt * conv_w[2] + conv_b)
    q, k = qk[..., :gw], qk[..., gw:] * HEAD_DIM ** -0.5
    ig = _gate_layout(i_f + gate_b[0], i_b + gate_b[1], tc)
    fg = _gate_layout(jax.nn.log_sigmoid(f_f + gate_b[2]), jax.nn.log_sigmoid(f_b + gate_b[3]), tc)
    h = _scan("mlstm", [_to_scan(q, q, "k", tc), _to_scan(k, k, "k", tc)], _to_scan(v, v, "v", tc),
              gates=(ig, fg))
    h = _from_scan(h, b, tc)
    return _head_norm(h, norm_g.reshape(N_HEADS, HEAD_DIM), None, LN_EPS, False) * jax.nn.sigmoid(o)


RWKV_COLS = 3 * GROUP_WIDTH + 2 * RWKV_DECAY_LORA + 2 * RWKV_ICLR_LORA + RWKV_GATE_LORA
DIFF_COLS = 3 * GROUP_WIDTH
HGRN_COLS = 5 * GROUP_WIDTH
MLSTM_COLS = 4 * GROUP_WIDTH + 4 * N_HEADS
_pad = lambda n: -(-n // LANES) * LANES
SLAB_OFFSETS = np.cumsum([0, _pad(RWKV_COLS), _pad(DIFF_COLS), _pad(HGRN_COLS), _pad(MLSTM_COLS)])
SLAB_COLS = (RWKV_COLS, DIFF_COLS, HGRN_COLS, MLSTM_COLS)


def _pad_cols(w):
    parts, off = [], 0
    for n in SLAB_COLS:
        parts.append(jnp.pad(w[:, off:off + n], ((0, 0), (0, _pad(n) - n))))
        off += n
    return jnp.concatenate(parts, axis=1)


def kernel(x, c, ctx, c_ctx, ada_w, ada_b, w_in, rwkv_mu, rwkv_w0, rwkv_w2, rwkv_a0, rwkv_a2, rwkv_g2, rwkv_k_k, rwkv_k_a, rwkv_r_k, rwkv_lnx_g, rwkv_lnx_b, diff_lambda, diff_norm_g, hgrn_lb_logits, hgrn_norm_g, mlstm_conv_w, mlstm_conv_b, mlstm_gate_b, mlstm_norm_g, w_out, ffn_w_up, ffn_conv_w, ffn_conv_b, ffn_w_down, ln_g, ln_b):
    depth = w_in.shape[0]
    b, seq, d = x.shape
    tc = ctx.shape[1]
    assert tc == ROW_TILE and seq % ROW_TILE == 0 and 2 * b * N_HEADS <= LANES
    alpha = (2.0 * depth) ** 0.25

    lb_w = jax.nn.softmax(hgrn_lb_logits.astype(F32), axis=0)
    lower_bounds = jnp.cumsum(lb_w, axis=0) - lb_w[0]

    cond = jnp.zeros((8, d), F32).at[:b].set(c).at[b].set(c_ctx)
    h = jnp.concatenate([ctx, x], axis=1)
    for layer in range(depth):
        last = layer == depth - 1
        mod = _adaln(cond, ada_w[layer], ada_b[layer]).reshape(8, 6, d)
        mods_l = [mod[:b, j][:, None, :] for j in range(6)]
        mods_c = [mod[b:b + 1, j][:, None, :] for j in range(6)]

        p = _inproj(h, mods_l[0], mods_l[1], mods_c[0], mods_c[1], _pad_cols(w_in[layer]).astype(BF16), 1)
        slab = lambda j: p[..., SLAB_OFFSETS[j]:SLAB_OFFSETS[j] + SLAB_COLS[j]]
        y_a = _rwkv7(slab(0), tc, rwkv_mu[layer], rwkv_w0[layer], rwkv_w2[layer], rwkv_a0[layer], rwkv_a2[layer],
                     rwkv_g2[layer], rwkv_k_k[layer], rwkv_k_a[layer], rwkv_r_k[layer], rwkv_lnx_g[layer],
                     rwkv_lnx_b[layer])
        y_b = _diff_attn(slab(1), tc, diff_lambda[layer], diff_norm_g[layer], layer, not last)
        y_c = _hgrn2(slab(2), tc, lower_bounds[layer], hgrn_norm_g[layer])
        y_d = _mlstm(slab(3), tc, mlstm_conv_w[layer], mlstm_conv_b[layer], mlstm_gate_b[layer], mlstm_norm_g[layer])

        tile0, ctx_tiles = (tc // ROW_TILE, 0) if last else (0, tc // ROW_TILE)
        h = _outproj((y_a, y_b, y_c, y_d), h, mods_l[2], mods_c[2], w_out[layer].astype(BF16), ln_g[layer, 0],
                     ln_b[layer, 0], tile0, ctx_tiles, alpha)
        h = _ffn(h, mods_l[3:6], mods_c[3:6], ffn_w_up[layer].astype(BF16), ffn_conv_w[layer], ffn_conv_b[layer],
                 ffn_w_down[layer].astype(BF16), ln_g[layer, 1], ln_b[layer, 1], ctx_tiles, alpha)
    return h
```

```python
import functools
import math

import jax
import jax.numpy as jnp
import numpy as np
from jax import lax
from jax.experimental import pallas as pl
from jax.experimental.pallas import tpu as pltpu

F32 = jnp.float32
BF16 = jnp.bfloat16
HIGHEST = lax.Precision.HIGHEST

HEAD_DIM = 64
N_HEADS = 4
GROUP_WIDTH = N_HEADS * HEAD_DIM
DIFF_QK_DIM = HEAD_DIM // 2
GRID_W = 64
ROPE_BASE = 10000.0
RWKV_DECAY_LORA = 64
RWKV_ICLR_LORA = 64
RWKV_GATE_LORA = 160
RWKV_LNX_EPS = 64e-5
LN_EPS = 1e-5
LB_FLOOR = 1e-30
MAX_NEG_LOG_STAB = 60.0

LANES = 128
ROW_TILE = 256
ATTN_TILE = 128
TIME_BLOCK = 128
SCAN_TB = 64
VMEM_LIMIT = 56 * 1024 * 1024


def _cparams(sem):
    return pltpu.CompilerParams(dimension_semantics=sem, vmem_limit_bytes=VMEM_LIMIT)


def _rev_block(i, n_ctx, n_all):
    return jnp.where(i < n_ctx, n_ctx - 1 - i, n_ctx + n_all - 1 - i)


def _flip_rows(x):
    n = x.shape[0]
    r = lax.broadcasted_iota(jnp.int32, (n, n), 0)
    c = lax.broadcasted_iota(jnp.int32, (n, n), 1)
    perm = jnp.where(r + c == n - 1, 1.0, 0.0).astype(F32)
    return jnp.dot(perm, x, precision=HIGHEST, preferred_element_type=F32)


def _adaln_body(c_ref, w_ref, b_ref, o_ref):
    x = c_ref[...]
    x = (x * jax.nn.sigmoid(x)).astype(BF16)
    o_ref[...] = jnp.dot(x, w_ref[...].astype(BF16), preferred_element_type=F32) + b_ref[...]


def _adaln(cond, w, b):
    m, d = cond.shape
    n = w.shape[1]
    tn = 1536
    return pl.pallas_call(
        _adaln_body,
        grid=(n // tn,),
        in_specs=[pl.BlockSpec((m, d), lambda j: (0, 0)),
                  pl.BlockSpec((d, tn), lambda j: (0, j)),
                  pl.BlockSpec((1, tn), lambda j: (0, j))],
        out_specs=pl.BlockSpec((m, tn), lambda j: (0, j)),
        out_shape=jax.ShapeDtypeStruct((m, n), F32),
        compiler_params=_cparams(("arbitrary",)),
        name="adaln",
    )(cond, w, b.reshape(1, n))


def _inproj_body(x_ref, sl_ref, cl_ref, sc_ref, cc_ref, w_ref, o_ref, *, ctx_tiles):
    is_ctx = pl.program_id(1) < ctx_tiles
    shift = jnp.where(is_ctx, sc_ref[0], sl_ref[0])
    scale = jnp.where(is_ctx, cc_ref[0], cl_ref[0])
    xm = (x_ref[0] * (1.0 + scale) + shift).astype(BF16)
    o_ref[0] = jnp.dot(xm, w_ref[...], preferred_element_type=F32)


def _inproj(h, shift_l, scale_l, shift_c, scale_c, w, ctx_tiles):
    b, t, d = h.shape
    n = w.shape[1]
    row = pl.BlockSpec((1, ROW_TILE, d), lambda bi, i: (bi, i, 0))
    lat = pl.BlockSpec((1, 1, d), lambda bi, i: (bi, 0, 0))
    ctx = pl.BlockSpec((1, 1, d), lambda bi, i: (0, 0, 0))
    return pl.pallas_call(
        functools.partial(_inproj_body, ctx_tiles=ctx_tiles),
        grid=(b, t // ROW_TILE),
        in_specs=[row, lat, lat, ctx, ctx, pl.BlockSpec((d, n), lambda bi, i: (0, 0))],
        out_specs=pl.BlockSpec((1, ROW_TILE, n), lambda bi, i: (bi, i, 0)),
        out_shape=jax.ShapeDtypeStruct((b, t, n), F32),
        compiler_params=_cparams(("parallel", "parallel")),
        name="inproj",
    )(h, shift_l, scale_l, shift_c, scale_c, w)


def _mm_body(x_ref, w_ref, o_ref):
    o_ref[...] = jnp.dot(x_ref[...].astype(BF16), w_ref[...], preferred_element_type=F32)


def _mm(x, w):
    m, k = x.shape
    n = w.shape[1]
    tm = 2176 if m % 2176 == 0 else m
    return pl.pallas_call(
        _mm_body,
        grid=(m // tm,),
        in_specs=[pl.BlockSpec((tm, k), lambda i: (i, 0)), pl.BlockSpec((k, n), lambda i: (0, 0))],
        out_specs=pl.BlockSpec((tm, n), lambda i: (i, 0)),
        out_shape=jax.ShapeDtypeStruct((m, n), F32),
        compiler_params=_cparams(("parallel",)),
        name="lora_mm",
    )(x, w.astype(BF16))


def _to_scan_body(x0_ref, x1_ref, o_ref, r_ref, *, kind, nb):
    n_scan = 2 * nb * N_HEADS
    rep = LANES // n_scan
    for b in range(nb):
        r_ref[pl.ds(b * GROUP_WIDTH, GROUP_WIDTH), :] = x0_ref[b].T
        r_ref[pl.ds((nb + b) * GROUP_WIDTH, GROUP_WIDTH), :] = _flip_rows(x1_ref[b]).T
    if kind == "k":
        for k in range(HEAD_DIM):
            rows = r_ref[pl.ds(k, n_scan, stride=HEAD_DIM), :]
            o_ref[k] = jnp.concatenate([rows] * rep, axis=0).T
    else:
        for vh in range(HEAD_DIM // rep):
            rows = [r_ref[pl.ds(vh * rep + vl, n_scan, stride=HEAD_DIM), :] for vl in range(rep)]
            o_ref[:, vh, :] = jnp.concatenate(rows, axis=0).T


def _to_scan(x0, x1, kind, tc):
    nb, t, gw = x0.shape
    n_all, n_ctx = t // TIME_BLOCK, tc // TIME_BLOCK
    n_scan = 2 * nb * N_HEADS
    rep = LANES // n_scan
    fwd = pl.BlockSpec((nb, TIME_BLOCK, gw), lambda i: (0, i, 0))
    bwd = pl.BlockSpec((nb, TIME_BLOCK, gw), lambda i: (0, _rev_block(i, n_ctx, n_all), 0))
    if kind == "k":
        out_shape = (HEAD_DIM, t, LANES)
        out_spec = pl.BlockSpec((HEAD_DIM, TIME_BLOCK, LANES), lambda i: (0, i, 0))
    else:
        out_shape = (t, HEAD_DIM // rep, LANES)
        out_spec = pl.BlockSpec((TIME_BLOCK, HEAD_DIM // rep, LANES), lambda i: (i, 0, 0))
    return pl.pallas_call(
        functools.partial(_to_scan_body, kind=kind, nb=nb),
        grid=(n_all,),
        in_specs=[fwd, bwd],
        out_specs=out_spec,
        out_shape=jax.ShapeDtypeStruct(out_shape, F32),
        scratch_shapes=[pltpu.VMEM((n_scan * HEAD_DIM, TIME_BLOCK), F32)],
        compiler_params=_cparams(("parallel",)),
        name="to_scan_" + kind,
    )(x0, x1)


def _from_scan_body(yf_ref, yb_ref, o_ref, r_ref, *, nb):
    n_scan = 2 * nb * N_HEADS
    rep = LANES // n_scan
    for vh in range(HEAD_DIM // rep):
        r_ref[0, pl.ds(vh * LANES, LANES), :] = yf_ref[:, vh, :].T
        r_ref[1, pl.ds(vh * LANES, LANES), :] = yb_ref[:, vh, :].T
    for b in range(nb):
        slab = lambda d: jnp.concatenate(
            [r_ref[d, pl.ds((d * nb + b) * N_HEADS + h, HEAD_DIM, stride=n_scan), :] for h in range(N_HEADS)],
            axis=0).T
        o_ref[b] = slab(0) + _flip_rows(slab(1))


def _from_scan(y, nb, tc):
    t, n_vh, _ = y.shape
    n_all, n_ctx = t // TIME_BLOCK, tc // TIME_BLOCK
    fwd = pl.BlockSpec((TIME_BLOCK, n_vh, LANES), lambda i: (i, 0, 0))
    bwd = pl.BlockSpec((TIME_BLOCK, n_vh, LANES), lambda i: (_rev_block(i, n_ctx, n_all), 0, 0))
    return pl.pallas_call(
        functools.partial(_from_scan_body, nb=nb),
        grid=(n_all,),
        in_specs=[fwd, bwd],
        out_specs=pl.BlockSpec((nb, TIME_BLOCK, GROUP_WIDTH), lambda i: (0, i, 0)),
        out_shape=jax.ShapeDtypeStruct((nb, t, GROUP_WIDTH), F32),
        scratch_shapes=[pltpu.VMEM((2, n_vh * LANES, TIME_BLOCK), F32)],
        compiler_params=_cparams(("parallel",)),
        name="from_scan",
    )(y, y)


N_ACC = 4


def _acc_add(acc, i, x):
    acc[i % N_ACC] = x if acc[i % N_ACC] is None else acc[i % N_ACC] + x


def _acc_total(acc):
    return (acc[0] + acc[1]) + (acc[2] + acc[3])


def _scan_body(*refs, mode, tb, n_vh):
    if mode == "rwkv":
        r_ref, w_ref, k_ref, a_ref, b_ref, v_ref, y_ref, s_ref = refs
    else:
        r_ref, w_ref, k_ref, v_ref, y_ref, s_ref = refs

    @pl.when(pl.program_id(0) == 0)
    def _():
        s_ref[...] = jnp.zeros_like(s_ref)

    row = lambda ref, k, j: ref[k, pl.ds(j, 1), :]
    wide = lambda x: jnp.broadcast_to(x, (n_vh, LANES))

    def rwkv_step(j, sa):
        v = v_ref[j]
        j_next = jnp.minimum(j + 1, tb - 1)
        ys, sas = [None] * N_ACC, [None] * N_ACC
        for k in range(HEAD_DIM):
            s = s_ref[k] * wide(row(w_ref, k, j)) + sa * wide(row(b_ref, k, j)) + v * wide(row(k_ref, k, j))
            s_ref[k] = s
            _acc_add(ys, k, s * wide(row(r_ref, k, j)))
            _acc_add(sas, k, s * wide(row(a_ref, k, j_next)))
        y_ref[j] = _acc_total(ys)
        return _acc_total(sas)

    def gla_step(j, carry):
        v = v_ref[j]
        ys = [None] * N_ACC
        for k in range(HEAD_DIM):
            s = s_ref[k] * wide(row(w_ref, k, j)) + v * wide(row(k_ref, k, j))
            s_ref[k] = s
            _acc_add(ys, k, s * wide(row(r_ref, k, j)))
        y_ref[j] = _acc_total(ys)
        return carry

    if mode == "rwkv":
        sa0 = [None] * N_ACC
        for k in range(HEAD_DIM):
            _acc_add(sa0, k, s_ref[k] * wide(row(a_ref, k, 0)))
        lax.fori_loop(0, tb, rwkv_step, _acc_total(sa0), unroll=4)
    else:
        lax.fori_loop(0, tb, gla_step, 0, unroll=8)


def _scan(mode, k_inputs, v_in):
    t, n_vh, _ = v_in.shape
    tb = SCAN_TB
    big = pl.BlockSpec((HEAD_DIM, tb, LANES), lambda i: (0, i, 0))
    small = pl.BlockSpec((tb, n_vh, LANES), lambda i: (i, 0, 0))
    return pl.pallas_call(
        functools.partial(_scan_body, mode=mode, tb=tb, n_vh=n_vh),
        grid=(t // tb,),
        in_specs=[big] * len(k_inputs) + [small],
        out_specs=small,
        out_shape=jax.ShapeDtypeStruct((t, n_vh, LANES), F32),
        scratch_shapes=[pltpu.VMEM((HEAD_DIM, n_vh, LANES), F32)],
        compiler_params=_cparams(("arbitrary",)),
        name="scan_" + mode,
    )(*k_inputs, v_in)


MLSTM_CHUNK = 64


def _prefix_max(x, reverse):
    n = x.shape[0]
    row = lax.broadcasted_iota(jnp.int32, x.shape, 0)
    sh = 1
    while sh < n:
        if reverse:
            x = jnp.where(row < n - sh, jnp.maximum(x, pltpu.roll(x, n - sh, 0)), x)
        else:
            x = jnp.where(row >= sh, jnp.maximum(x, pltpu.roll(x, sh, 0)), x)
        sh *= 2
    return x


def _mlstm_body(q_ref, k_ref, v_ref, i_ref, f_ref, o_ref, st_ref, m_ref, *, reverse, n_sub):
    c, gw = MLSTM_CHUNK, GROUP_WIDTH

    @pl.when(pl.program_id(1) == 0)
    def _():
        st_ref[...] = jnp.zeros_like(st_ref)
        m_ref[...] = jnp.zeros_like(m_ref)

    tt = lax.broadcasted_iota(jnp.int32, (c, c), 0)
    ss = lax.broadcasted_iota(jnp.int32, (c, c), 1)
    tri = jnp.where((ss >= tt) if reverse else (ss <= tt), 1.0, 0.0).astype(F32)
    row = lax.broadcasted_iota(jnp.int32, (c, gw), 0)
    s_of_lane = lax.broadcasted_iota(jnp.int32, (c, gw), 1) % c
    causal = (s_of_lane >= row) if reverse else (s_of_lane <= row)
    diag = s_of_lane == row
    block = jnp.where(lax.broadcasted_iota(jnp.int32, (gw, gw), 0) // HEAD_DIM
                      == lax.broadcasted_iota(jnp.int32, (gw, gw), 1) // HEAD_DIM, 1.0, 0.0).astype(F32)
    block2 = jnp.concatenate([block, block], axis=1)
    ones8 = jnp.ones((8, c), F32)
    nt = (((1,), (1,)), ((), ()))
    last = 0 if reverse else c - 1
    for u in (range(n_sub - 1, -1, -1) if reverse else range(n_sub)):
        sl = pl.ds(u * c, c)
        q, k, v = q_ref[0, sl, :], k_ref[0, sl, :], v_ref[0, sl, :]
        ig, fg = i_ref[0, sl, :], f_ref[0, sl, :]
        b = jnp.dot(tri, fg, precision=HIGHEST, preferred_element_type=F32)
        g = ig - b
        m_prev = m_ref[0:1, :]
        m_t = b + jnp.maximum(m_prev, _prefix_max(g, reverse))
        w_inter = jnp.exp(b + m_prev - m_t)
        g_row = jnp.dot(ones8, jnp.where(diag, g, 0.0), precision=HIGHEST, preferred_element_type=F32)[0:1]
        qb = q.astype(BF16)
        k_bd = (jnp.concatenate([k] * N_HEADS, axis=0) * block).astype(BF16)
        scores = lax.dot_general(qb, k_bd, nt, preferred_element_type=F32)
        w = jnp.where(causal, jnp.exp((b - m_t) + g_row), 0.0) * scores
        v_bd = jnp.concatenate([jnp.concatenate([v] * N_HEADS, axis=0) * block, block], axis=1).astype(BF16)
        intra = jnp.dot(w.astype(BF16), v_bd, preferred_element_type=F32)
        inter = jnp.dot(qb, st_ref[...].astype(BF16), preferred_element_type=F32)
        num = w_inter * inter[:, :gw] + intra[:, :gw]
        den = w_inter * inter[:, gw:] + intra[:, gw:]
        floor = jnp.exp(jnp.minimum(-m_t, MAX_NEG_LOG_STAB))
        o_ref[0, sl, :] = num / jnp.maximum(jnp.abs(den), floor)
        m_new = m_t[last:last + 1]
        b_end = b[last:last + 1]
        kw = k * jnp.exp(b_end - b + ig - m_new)
        decay = jnp.exp(b_end + m_prev - m_new)
        v_one = jnp.concatenate([v, jnp.ones_like(v)], axis=1).astype(BF16)
        upd = jnp.dot(kw.T.astype(BF16), v_one, preferred_element_type=F32)
        st_ref[...] = st_ref[...] * jnp.concatenate([decay, decay], axis=1) + upd * block2
        m_ref[0:1, :] = m_new


def _mlstm_chunked(q, k, v, ig, fg, reverse, tc):
    b, t, gw = q.shape
    n_blocks, n_ctx = t // ROW_TILE, tc // ROW_TILE
    if reverse:
        imap = lambda bi, i: (bi, _rev_block(i, n_ctx, n_blocks), 0)
    else:
        imap = lambda bi, i: (bi, i, 0)
    spec = pl.BlockSpec((1, ROW_TILE, gw), imap)
    return pl.pallas_call(
        functools.partial(_mlstm_body, reverse=reverse, n_sub=ROW_TILE // MLSTM_CHUNK),
        grid=(b, n_blocks),
        in_specs=[spec] * 5,
        out_specs=spec,
        out_shape=jax.ShapeDtypeStruct((b, t, gw), F32),
        scratch_shapes=[pltpu.VMEM((gw, 2 * gw), F32), pltpu.VMEM((8, gw), F32)],
        compiler_params=_cparams(("parallel", "arbitrary")),
        name="mlstm_chunk",
    )(q, k, v, ig, fg)


def _attn_body(q_ref, k_ref, v_ref, lam_ref, g_ref, o_ref, *, out_scale):
    q = q_ref[0] * (DIFF_QK_DIM ** -0.5 * math.log2(math.e))
    k = k_ref[0].astype(BF16)
    v = v_ref[0].astype(BF16)
    lane = lax.broadcasted_iota(jnp.int32, q.shape, 1)
    comp = lane // DIFF_QK_DIM
    nt = (((1,), (1,)), ((), ()))

    n_maps = LANES // DIFF_QK_DIM
    ss = [lax.dot_general(jnp.where(comp == c, q, 0.0).astype(BF16), k, nt, preferred_element_type=F32)
          for c in range(n_maps)]
    es = [jnp.exp2(s - jnp.max(s, axis=-1, keepdims=True)) for s in ss]
    ls = [jnp.sum(e, axis=-1, keepdims=True) for e in es]
    ys = []
    for hh in range(LANES // HEAD_DIM):
        e0, e1, l0, l1 = es[2 * hh], es[2 * hh + 1], ls[2 * hh], ls[2 * hh + 1]
        pr = (e0 - e1 * (lam_ref[0:1, 0:1] * l0 / l1)).astype(BF16)
        ys.append(jnp.dot(pr, v, preferred_element_type=F32) * (1.0 / l0))
    first = lane < HEAD_DIM
    y = jnp.where(first, ys[0], ys[1])
    ysq = y * y
    ms = jnp.where(first, jnp.sum(jnp.where(first, ysq, 0.0), axis=-1, keepdims=True),
                   jnp.sum(jnp.where(first, 0.0, ysq), axis=-1, keepdims=True)) * (1.0 / HEAD_DIM)
    o_ref[0] = y * lax.rsqrt(ms + LN_EPS) * g_ref[...] * out_scale


def _attention(q, k, v, lam, norm_g, out_scale, q_block0, n_q_blocks, n_keys, prev=None):
    b, t, gw = q.shape
    qspec = pl.BlockSpec((1, ATTN_TILE, LANES), lambda bi, pi, i: (bi, q_block0 + i, pi))
    kspec = pl.BlockSpec((1, n_keys, LANES), lambda bi, pi, i: (bi, 0, pi))
    in_specs = [qspec, kspec, kspec,
                pl.BlockSpec((1, LANES), lambda bi, pi, i: (0, 0)),
                pl.BlockSpec((1, LANES), lambda bi, pi, i: (0, 0))]
    args = [q, k, v, jnp.full((1, LANES), lam, F32), jnp.tile(norm_g.reshape(1, HEAD_DIM), (1, LANES // HEAD_DIM))]
    aliases = {}
    body = functools.partial(_attn_body, out_scale=out_scale)
    if prev is not None:
        in_specs.append(pl.BlockSpec(memory_space=pl.ANY))
        args.append(prev)
        aliases = {5: 0}
        body = lambda q_ref, k_ref, v_ref, lam_ref, g_ref, prev_ref, o_ref: _attn_body(
            q_ref, k_ref, v_ref, lam_ref, g_ref, o_ref, out_scale=out_scale)
    return pl.pallas_call(
        body,
        grid=(b, gw // LANES, n_q_blocks),
        in_specs=in_specs,
        out_specs=qspec,
        out_shape=jax.ShapeDtypeStruct((b, t, gw), F32),
        input_output_aliases=aliases,
        compiler_params=_cparams(("parallel", "parallel", "parallel")),
        name="diff_attn",
    )(*args)


def _layernorm(z, g, b):
    z = z - jnp.mean(z, axis=-1, keepdims=True)
    return z * lax.rsqrt(jnp.mean(z * z, axis=-1, keepdims=True) + LN_EPS) * g + b


def _outproj_body(ya_ref, yb_ref, yc_ref, yd_ref, h_ref, gl_ref, gc_ref, w_ref, g_ref, b_ref, o_ref, *,
                  ctx_tiles, alpha):
    is_ctx = pl.program_id(1) < ctx_tiles
    gate = jnp.where(is_ctx, gc_ref[0], gl_ref[0])
    y = None
    for m, y_ref in enumerate((ya_ref, yb_ref, yc_ref, yd_ref)):
        part = jnp.dot(y_ref[0].astype(BF16), w_ref[m * GROUP_WIDTH:(m + 1) * GROUP_WIDTH, :],
                       preferred_element_type=F32)
        y = part if y is None else y + part
    o_ref[0] = _layernorm(alpha * h_ref[0] + gate * y, g_ref[...], b_ref[...])


def _outproj(ys, h, gate_l, gate_c, w, ln_g, ln_b, tile0, ctx_tiles, alpha):
    b, t, d = h.shape
    n_tiles = t // ROW_TILE - tile0
    yrow = pl.BlockSpec((1, ROW_TILE, GROUP_WIDTH), lambda bi, i: (bi, tile0 + i, 0))
    hrow = pl.BlockSpec((1, ROW_TILE, d), lambda bi, i: (bi, tile0 + i, 0))
    lat = pl.BlockSpec((1, 1, d), lambda bi, i: (bi, 0, 0))
    ctx = pl.BlockSpec((1, 1, d), lambda bi, i: (0, 0, 0))
    vec = pl.BlockSpec((1, d), lambda bi, i: (0, 0))
    return pl.pallas_call(
        functools.partial(_outproj_body, ctx_tiles=ctx_tiles, alpha=alpha),
        grid=(b, n_tiles),
        in_specs=[yrow] * 4 + [hrow, lat, ctx, pl.BlockSpec(w.shape, lambda bi, i: (0, 0)), vec, vec],
        out_specs=pl.BlockSpec((1, ROW_TILE, d), lambda bi, i: (bi, i, 0)),
        out_shape=jax.ShapeDtypeStruct((b, n_tiles * ROW_TILE, d), F32),
        compiler_params=_cparams(("parallel", "parallel")),
        name="outproj_ln",
    )(*ys, h, gate_l, gate_c, w, ln_g.reshape(1, d), ln_b.reshape(1, d))


FFN_CHUNKS = 2


def _ffn_body(x_ref, xp_ref, xn_ref, sl_ref, cl_ref, gl_ref, sc_ref, cc_ref, gc_ref,
              wu_ref, cw_ref, cb_ref, wd_ref, g_ref, b_ref, o_ref, *, ctx_tiles, n_tiles, alpha, d_ff):
    i = pl.program_id(1)
    is_ctx = i < ctx_tiles
    shift = jnp.where(is_ctx, sc_ref[0], sl_ref[0])
    scale = jnp.where(is_ctx, cc_ref[0], cl_ref[0])
    gate_mod = jnp.where(is_ctx, gc_ref[0], gl_ref[0])
    seg_start = jnp.logical_or(i == 0, i == ctx_tiles)
    seg_end = jnp.logical_or(i == ctx_tiles - 1, i == n_tiles - 1)

    x = x_ref[0]
    xm = (x * (1.0 + scale) + shift).astype(BF16)
    xp = (xp_ref[0] * (1.0 + scale) + shift).astype(BF16)
    xn = (xn_ref[0] * (1.0 + scale) + shift).astype(BF16)
    rows = x.shape[0]
    row_id = lax.broadcasted_iota(jnp.int32, (rows, 1), 0)
    ch = d_ff // FFN_CHUNKS
    f = jnp.zeros(x.shape, F32)
    for c in range(FFN_CHUNKS):
        wg = wu_ref[:, c * ch:(c + 1) * ch]
        wv = wu_ref[:, d_ff + c * ch:d_ff + (c + 1) * ch]
        gate = jnp.dot(xm, wg, preferred_element_type=F32)
        val = jnp.dot(xm, wv, preferred_element_type=F32)
        gp = jnp.dot(xp, wg, preferred_element_type=F32)[7:8, :]
        gn = jnp.dot(xn, wg, preferred_element_type=F32)[0:1, :]
        gp = jnp.where(seg_start, 0.0, gp)
        gn = jnp.where(seg_end, 0.0, gn)
        prev = jnp.where(row_id == 0, gp, pltpu.roll(gate, 1, 0))
        nxt = jnp.where(row_id == rows - 1, gn, pltpu.roll(gate, rows - 1, 0))
        cw = cw_ref[:, c * ch:(c + 1) * ch]
        conv = prev * cw[0:1, :] + gate * cw[1:2, :] + nxt * cw[2:3, :] + cb_ref[:, c * ch:(c + 1) * ch]
        act = (jax.nn.gelu(conv) * val).astype(BF16)
        f = f + jnp.dot(act, wd_ref[c * ch:(c + 1) * ch, :], preferred_element_type=F32)
    o_ref[0] = _layernorm(alpha * x + gate_mod * f, g_ref[...], b_ref[...])


def _ffn(h, mods_l, mods_c, w_up, conv_w, conv_b, w_down, ln_g, ln_b, ctx_tiles, alpha):
    b, t, d = h.shape
    d_ff = w_down.shape[0]
    n_tiles = t // ROW_TILE
    sub = ROW_TILE // 8
    n_sub = t // 8
    row = pl.BlockSpec((1, ROW_TILE, d), lambda bi, i: (bi, i, 0))
    prev = pl.BlockSpec((1, 8, d), lambda bi, i: (bi, jnp.maximum(i * sub - 1, 0), 0))
    nxt = pl.BlockSpec((1, 8, d), lambda bi, i: (bi, jnp.minimum((i + 1) * sub, n_sub - 1), 0))
    lat = pl.BlockSpec((1, 1, d), lambda bi, i: (bi, 0, 0))
    ctx = pl.BlockSpec((1, 1, d), lambda bi, i: (0, 0, 0))
    vec = pl.BlockSpec((1, d), lambda bi, i: (0, 0))
    full = lambda a: pl.BlockSpec(a.shape, lambda bi, i: (0,) * a.ndim)
    cb = conv_b.reshape(1, d_ff)
    return pl.pallas_call(
        functools.partial(_ffn_body, ctx_tiles=ctx_tiles, n_tiles=n_tiles, alpha=alpha, d_ff=d_ff),
        grid=(b, n_tiles),
        in_specs=[row, prev, nxt, lat, lat, lat, ctx, ctx, ctx,
                  full(w_up), full(conv_w), full(cb), full(w_down), vec, vec],
        out_specs=row,
        out_shape=jax.ShapeDtypeStruct((b, t, d), F32),
        compiler_params=_cparams(("parallel", "parallel")),
        name="ffn",
    )(h, h, h, *mods_l, *mods_c, w_up, conv_w, cb, w_down, ln_g.reshape(1, d), ln_b.reshape(1, d))


def _seg_neighbours(z, tc):
    zero = jnp.zeros_like(z[:, :1])
    prev = jnp.concatenate([zero, z[:, :-1]], axis=1)
    nxt = jnp.concatenate([z[:, 1:], zero], axis=1)
    if 0 < tc < z.shape[1]:
        t = jnp.arange(z.shape[1])[None, :, None]
        prev = jnp.where(t == tc, 0.0, prev)
        nxt = jnp.where(t == tc - 1, 0.0, nxt)
    return prev, nxt


def _head_norm(y, gain, bias, eps, rms):
    b, t, _ = y.shape
    yh = y.reshape(b, t, N_HEADS, HEAD_DIM)
    if not rms:
        yh = yh - jnp.mean(yh, axis=-1, keepdims=True)
    yh = yh * lax.rsqrt(jnp.mean(jnp.square(yh), axis=-1, keepdims=True) + eps) * gain
    if bias is not None:
        yh = yh + bias
    return yh.reshape(b, t, GROUP_WIDTH)


def _rwkv7(p, tc, mu, w0, w2, a0, a2, g2, k_k, k_a, r_k, lnx_g, lnx_b):
    b, t, _ = p.shape
    gw = GROUP_WIDTH
    prev, nxt = _seg_neighbours(p, tc)
    p = p + (0.5 * (prev + nxt) - p) * mu
    r, k, v = p[..., :gw], p[..., gw:2 * gw], p[..., 2 * gw:3 * gw]
    o = 3 * gw
    wd = (p[..., o:o + 64], p[..., o + 64:o + 128])
    ad = (p[..., o + 128:o + 192], p[..., o + 192:o + 256])
    gd = p[..., o + 256:o + 256 + RWKV_GATE_LORA]
    kh = k.reshape(b, t, N_HEADS, HEAD_DIM)
    kk = kh * k_k.reshape(N_HEADS, HEAD_DIM)
    kk = kk / jnp.maximum(jnp.linalg.norm(kk, axis=-1, keepdims=True), 1e-12)
    kk = kk.reshape(b, t, gw)
    ws, ks, bs = [], [], []
    for d in range(2):
        lora_w = _mm(jnp.tanh(wd[d]).reshape(b * t, -1), w2[d]).reshape(b, t, gw)
        w_log = -jax.nn.softplus(-(w0[d] + lora_w)) - 0.5
        ws.append(jnp.exp(-jnp.exp(w_log)))
        a = jax.nn.sigmoid(a0[d] + _mm(ad[d].reshape(b * t, -1), a2[d]).reshape(b, t, gw))
        ks.append(k * (1.0 + (a - 1.0) * k_a))
        bs.append(kk * a)
    g = _mm(jax.nn.sigmoid(gd).reshape(b * t, -1), g2).reshape(b, t, gw)
    neg_kk = -kk
    y = _scan("rwkv",
              [_to_scan(r, r, "k", tc), _to_scan(ws[0], ws[1], "k", tc), _to_scan(ks[0], ks[1], "k", tc),
               _to_scan(neg_kk, neg_kk, "k", tc), _to_scan(bs[0], bs[1], "k", tc)],
              _to_scan(v, v, "v", tc))
    y = _from_scan(y, b, tc)
    y = _head_norm(y, lnx_g.reshape(N_HEADS, HEAD_DIM), lnx_b.reshape(N_HEADS, HEAD_DIM), RWKV_LNX_EPS, False)
    bonus = jnp.sum((r * (ks[0] + ks[1]) * r_k).reshape(b, t, N_HEADS, HEAD_DIM), axis=-1, keepdims=True)
    bonus = (bonus * v.reshape(b, t, N_HEADS, HEAD_DIM)).reshape(b, t, gw)
    return (y + bonus) * g


def _rope_tables(n):
    rows = n // GRID_W
    row = jnp.repeat(jnp.arange(rows, dtype=F32), GRID_W)
    col = jnp.tile(jnp.arange(GRID_W, dtype=F32), rows)
    n_freq = DIFF_QK_DIM // 4
    inv_freq = ROPE_BASE ** (-jnp.arange(n_freq, dtype=F32) / n_freq)
    ang = jnp.concatenate([row[:, None] * inv_freq, col[:, None] * inv_freq], axis=-1)
    return jnp.cos(ang), jnp.sin(ang)


def _rope(x, cos, sin, tc):
    b, t, gw = x.shape
    half = DIFF_QK_DIM // 2
    cos = jnp.concatenate([jnp.ones((tc, half), F32), cos], axis=0)[:, None, :]
    sin = jnp.concatenate([jnp.zeros((tc, half), F32), sin], axis=0)[:, None, :]
    x = x.reshape(b, t, gw // DIFF_QK_DIM, DIFF_QK_DIM)
    x1, x2 = x[..., :half], x[..., half:]
    return jnp.concatenate([x1 * cos - x2 * sin, x2 * cos + x1 * sin], axis=-1).reshape(b, t, gw)


def _diff_attn(p, tc, lam_vecs, norm_g, layer, need_ctx):
    b, t, _ = p.shape
    gw = GROUP_WIDTH
    lam_init = 0.8 - 0.6 * math.exp(-0.3 * layer)
    lv = lam_vecs.astype(F32)
    lam = jnp.exp(jnp.sum(lv[0] * lv[1])) - jnp.exp(jnp.sum(lv[2] * lv[3])) + lam_init
    cos, sin = _rope_tables(t - tc)
    q = _rope(p[..., :gw], cos, sin, tc)
    k = _rope(p[..., gw:2 * gw], cos, sin, tc)
    v = p[..., 2 * gw:3 * gw]
    out_scale = 1.0 - lam_init
    n_ctx, n_all = tc // ATTN_TILE, t // ATTN_TILE
    y = _attention(q, k, v, lam, norm_g, out_scale, n_ctx, n_all - n_ctx, t)
    if need_ctx:
        y = _attention(q, k, v, lam, norm_g, out_scale, 0, n_ctx, tc, prev=y)
    return y


def _hgrn2(p, tc, lb, norm_g):
    b, t, _ = p.shape
    gw = GROUP_WIDTH
    q, f_f, f_b, i, g = (p[..., j * gw:(j + 1) * gw] for j in range(5))
    q = jax.nn.silu(q)
    log_lb, log_1mlb = jnp.log(jnp.maximum(lb, LB_FLOOR)), jnp.log1p(-lb)
    decays, keys = [], []
    for f in (f_f, f_b):
        decays.append(jnp.exp(jnp.logaddexp(log_lb, log_1mlb + jax.nn.log_sigmoid(f))))
        keys.append((1.0 - lb) * jax.nn.sigmoid(-f))
    o = _scan("gla",
              [_to_scan(q, q, "k", tc), _to_scan(decays[0], decays[1], "k", tc),
               _to_scan(keys[0], keys[1], "k", tc)],
              _to_scan(i, i, "v", tc))
    o = _from_scan(o, b, tc)
    return _head_norm(o, norm_g, None, LN_EPS, True) * jax.nn.silu(g)


def _mlstm(p, tc, conv_w, conv_b, gate_b, norm_g):
    b, t, _ = p.shape
    gw = GROUP_WIDTH
    qk, v, o = p[..., :2 * gw], p[..., 2 * gw:3 * gw], p[..., 3 * gw:4 * gw]
    gates = p[..., 4 * gw:4 * gw + 4 * N_HEADS]
    i_f, i_b, f_f, f_b = (gates[..., j * N_HEADS:(j + 1) * N_HEADS] for j in range(4))
    prev, nxt = _seg_neighbours(qk, tc)
    qk = jax.nn.silu(prev * conv_w[0] + qk * conv_w[1] + nxt * conv_w[2] + conv_b)
    q, k = qk[..., :gw], qk[..., gw:] * HEAD_DIM ** -0.5
    wide = lambda z: jnp.repeat(z, HEAD_DIM, axis=-1)
    h = (_mlstm_chunked(q, k, v, wide(i_f + gate_b[0]), wide(jax.nn.log_sigmoid(f_f + gate_b[2])), False, tc)
         + _mlstm_chunked(q, k, v, wide(i_b + gate_b[1]), wide(jax.nn.log_sigmoid(f_b + gate_b[3])), True, tc))
    return _head_norm(h, norm_g.reshape(N_HEADS, HEAD_DIM), None, LN_EPS, False) * jax.nn.sigmoid(o)


RWKV_COLS = 3 * GROUP_WIDTH + 2 * RWKV_DECAY_LORA + 2 * RWKV_ICLR_LORA + RWKV_GATE_LORA
DIFF_COLS = 3 * GROUP_WIDTH
HGRN_COLS = 5 * GROUP_WIDTH
MLSTM_COLS = 4 * GROUP_WIDTH + 4 * N_HEADS
_pad = lambda n: -(-n // LANES) * LANES
SLAB_OFFSETS = np.cumsum([0, _pad(RWKV_COLS), _pad(DIFF_COLS), _pad(HGRN_COLS), _pad(MLSTM_COLS)])
SLAB_COLS = (RWKV_COLS, DIFF_COLS, HGRN_COLS, MLSTM_COLS)


def _pad_cols(w):
    parts, off = [], 0
    for n in SLAB_COLS:
        parts.append(jnp.pad(w[:, off:off + n], ((0, 0), (0, _pad(n) - n))))
        off += n
    return jnp.concatenate(parts, axis=1)


def kernel(x, c, ctx, c_ctx, ada_w, ada_b, w_in, rwkv_mu, rwkv_w0, rwkv_w2, rwkv_a0, rwkv_a2, rwkv_g2, rwkv_k_k, rwkv_k_a, rwkv_r_k, rwkv_lnx_g, rwkv_lnx_b, diff_lambda, diff_norm_g, hgrn_lb_logits, hgrn_norm_g, mlstm_conv_w, mlstm_conv_b, mlstm_gate_b, mlstm_norm_g, w_out, ffn_w_up, ffn_conv_w, ffn_conv_b, ffn_w_down, ln_g, ln_b):
    depth = w_in.shape[0]
    b, seq, d = x.shape
    tc = ctx.shape[1]
    assert tc == ROW_TILE and seq % ROW_TILE == 0 and 2 * b * N_HEADS <= LANES
    alpha = (2.0 * depth) ** 0.25

    lb_w = jax.nn.softmax(hgrn_lb_logits.astype(F32), axis=0)
    lower_bounds = jnp.cumsum(lb_w, axis=0) - lb_w[0]

    cond = jnp.zeros((8, d), F32).at[:b].set(c).at[b].set(c_ctx)
    h = jnp.concatenate([ctx, x], axis=1)
    for layer in range(depth):
        last = layer == depth - 1
        mod = _adaln(cond, ada_w[layer], ada_b[layer]).reshape(8, 6, d)
        mods_l = [mod[:b, j][:, None, :] for j in range(6)]
        mods_c = [mod[b:b + 1, j][:, None, :] for j in range(6)]

        p = _inproj(h, mods_l[0], mods_l[1], mods_c[0], mods_c[1], _pad_cols(w_in[layer]).astype(BF16), 1)
        slab = lambda j: p[..., SLAB_OFFSETS[j]:SLAB_OFFSETS[j] + SLAB_COLS[j]]
        y_a = _rwkv7(slab(0), tc, rwkv_mu[layer], rwkv_w0[layer], rwkv_w2[layer], rwkv_a0[layer], rwkv_a2[layer],
                     rwkv_g2[layer], rwkv_k_k[layer], rwkv_k_a[layer], rwkv_r_k[layer], rwkv_lnx_g[layer],
                     rwkv_lnx_b[layer])
        y_b = _diff_attn(slab(1), tc, diff_lambda[layer], diff_norm_g[layer], layer, not last)
        y_c = _hgrn2(slab(2), tc, lower_bounds[layer], hgrn_norm_g[layer])
        y_d = _mlstm(slab(3), tc, mlstm_conv_w[layer], mlstm_conv_b[layer], mlstm_gate_b[layer], mlstm_norm_g[layer])

        tile0, ctx_tiles = (tc // ROW_TILE, 0) if last else (0, tc // ROW_TILE)
        h = _outproj((y_a, y_b, y_c, y_d), h, mods_l[2], mods_c[2], w_out[layer].astype(BF16), ln_g[layer, 0],
                     ln_b[layer, 0], tile0, ctx_tiles, alpha)
        h = _ffn(h, mods_l[3:6], mods_c[3:6], ffn_w_up[layer].astype(BF16), ffn_conv_w[layer], ffn_conv_b[layer],
                 ffn_w_down[layer].astype(BF16), ln_g[layer, 1], ln_b[layer, 1], ctx_tiles, alpha)
    return h
```

```python
import functools
import math

import jax
import jax.numpy as jnp
import numpy as np
from jax import lax
from jax.experimental import pallas as pl
from jax.experimental.pallas import tpu as pltpu

F32 = jnp.float32
BF16 = jnp.bfloat16
HIGHEST = lax.Precision.HIGHEST

HEAD_DIM = 64
N_HEADS = 4
GROUP_WIDTH = N_HEADS * HEAD_DIM
DIFF_QK_DIM = HEAD_DIM // 2
GRID_W = 64
ROPE_BASE = 10000.0
RWKV_DECAY_LORA = 64
RWKV_ICLR_LORA = 64
RWKV_GATE_LORA = 160
RWKV_LNX_EPS = 64e-5
LN_EPS = 1e-5
LB_FLOOR = 1e-30
MAX_NEG_LOG_STAB = 60.0

LANES = 128
ROW_TILE = 256
ATTN_TILE = 128
TIME_BLOCK = 128
SCAN_TB = 64
VMEM_LIMIT = 56 * 1024 * 1024


def _cparams(sem):
    return pltpu.CompilerParams(dimension_semantics=sem, vmem_limit_bytes=VMEM_LIMIT)


def _rev_block(i, n_ctx, n_all):
    return jnp.where(i < n_ctx, n_ctx - 1 - i, n_ctx + n_all - 1 - i)


def _flip_rows(x):
    n = x.shape[0]
    r = lax.broadcasted_iota(jnp.int32, (n, n), 0)
    c = lax.broadcasted_iota(jnp.int32, (n, n), 1)
    perm = jnp.where(r + c == n - 1, 1.0, 0.0).astype(F32)
    return jnp.dot(perm, x, precision=HIGHEST, preferred_element_type=F32)


def _adaln_body(c_ref, w_ref, b_ref, o_ref):
    x = c_ref[...]
    x = (x * jax.nn.sigmoid(x)).astype(BF16)
    o_ref[...] = jnp.dot(x, w_ref[...].astype(BF16), preferred_element_type=F32) + b_ref[...]


def _adaln(cond, w, b):
    m, d = cond.shape
    n = w.shape[1]
    tn = 1536
    return pl.pallas_call(
        _adaln_body,
        grid=(n // tn,),
        in_specs=[pl.BlockSpec((m, d), lambda j: (0, 0)),
                  pl.BlockSpec((d, tn), lambda j: (0, j)),
                  pl.BlockSpec((1, tn), lambda j: (0, j))],
        out_specs=pl.BlockSpec((m, tn), lambda j: (0, j)),
        out_shape=jax.ShapeDtypeStruct((m, n), F32),
        compiler_params=_cparams(("arbitrary",)),
        name="adaln",
    )(cond, w, b.reshape(1, n))


def _inproj_body(x_ref, sl_ref, cl_ref, sc_ref, cc_ref, *refs, ctx_tiles):
    n = len(refs) // 2
    is_ctx = pl.program_id(1) < ctx_tiles
    shift = jnp.where(is_ctx, sc_ref[0], sl_ref[0])
    scale = jnp.where(is_ctx, cc_ref[0], cl_ref[0])
    xm = (x_ref[0] * (1.0 + scale) + shift).astype(BF16)
    for w_ref, o_ref in zip(refs[:n], refs[n:]):
        o_ref[0] = jnp.dot(xm, w_ref[...], preferred_element_type=F32)


def _inproj(h, shift_l, scale_l, shift_c, scale_c, ws, ctx_tiles):
    b, t, d = h.shape
    row = lambda n: pl.BlockSpec((1, ROW_TILE, n), lambda bi, i: (bi, i, 0))
    lat = pl.BlockSpec((1, 1, d), lambda bi, i: (bi, 0, 0))
    ctx = pl.BlockSpec((1, 1, d), lambda bi, i: (0, 0, 0))
    return pl.pallas_call(
        functools.partial(_inproj_body, ctx_tiles=ctx_tiles),
        grid=(b, t // ROW_TILE),
        in_specs=[row(d), lat, lat, ctx, ctx] + [pl.BlockSpec(w.shape, lambda bi, i: (0, 0)) for w in ws],
        out_specs=[row(w.shape[1]) for w in ws],
        out_shape=[jax.ShapeDtypeStruct((b, t, w.shape[1]), F32) for w in ws],
        compiler_params=_cparams(("parallel", "parallel")),
        name="inproj",
    )(h, shift_l, scale_l, shift_c, scale_c, *ws)


def _softplus(x):
    return jnp.maximum(x, 0.0) + jnp.log(1.0 + jnp.exp(-jnp.abs(x)))


def _head_block(scale=1.0):
    r = lax.broadcasted_iota(jnp.int32, (GROUP_WIDTH, GROUP_WIDTH), 0) // HEAD_DIM
    c = lax.broadcasted_iota(jnp.int32, (GROUP_WIDTH, GROUP_WIDTH), 1) // HEAD_DIM
    return jnp.where(r == c, scale, 0.0).astype(F32)


def _head_sum(x, block_bf16):
    hi = x.astype(BF16)
    lo = (x - hi.astype(F32)).astype(BF16)
    return (jnp.dot(hi, block_bf16, preferred_element_type=F32)
            + jnp.dot(lo, block_bf16, preferred_element_type=F32))


def _shifted_rows(x, prev8, next8, seg_start, seg_end):
    rows = x.shape[0]
    row_id = lax.broadcasted_iota(jnp.int32, (rows, 1), 0)
    prev_row = jnp.where(seg_start, 0.0, prev8[7:8, :])
    next_row = jnp.where(seg_end, 0.0, next8[0:1, :])
    prev = jnp.where(row_id == 0, prev_row, pltpu.roll(x, 1, 0))
    nxt = jnp.where(row_id == rows - 1, next_row, pltpu.roll(x, rows - 1, 0))
    return prev, nxt


RWKV_FEATS = ("r", "v", "w0", "w1", "k0", "k1", "a", "b0", "b1", "g", "bonus")


def _rwkv_feat_body(p_ref, pp_ref, pn_ref, mu_ref, w0_ref, w2_ref, a0_ref, a2_ref, g2_ref, kk_ref, ka_ref,
                    rk_ref, o_ref, *, ctx_tiles, n_tiles):
    i = pl.program_id(1)
    seg_start = jnp.logical_or(i == 0, i == ctx_tiles)
    seg_end = jnp.logical_or(i == ctx_tiles - 1, i == n_tiles - 1)
    gw = GROUP_WIDTH
    p = p_ref[0]
    prev, nxt = _shifted_rows(p, pp_ref[0], pn_ref[0], seg_start, seg_end)
    p = p + (0.5 * (prev + nxt) - p) * mu_ref[...]
    r, k, v = p[:, :gw], p[:, gw:2 * gw], p[:, 2 * gw:3 * gw]
    wd, ad, gd = p[:, 3 * gw:3 * gw + LANES], p[:, 3 * gw + LANES:4 * gw], p[:, 4 * gw:5 * gw]
    lora_w = jnp.dot(jnp.tanh(wd).astype(BF16), w2_ref[...], preferred_element_type=F32)
    lora_a = jnp.dot(ad.astype(BF16), a2_ref[...], preferred_element_type=F32)
    decay = jnp.exp(-jnp.exp(-_softplus(-(w0_ref[...] + lora_w)) - 0.5))
    a = jax.nn.sigmoid(a0_ref[...] + lora_a)
    g = jnp.dot(jax.nn.sigmoid(gd).astype(BF16), g2_ref[...], preferred_element_type=F32)
    block = _head_block().astype(BF16)
    kk = k * kk_ref[...]
    kk = kk / jnp.maximum(jnp.sqrt(_head_sum(kk * kk, block)), 1e-12)
    k0 = k * (1.0 + (a[:, :gw] - 1.0) * ka_ref[...])
    k1 = k * (1.0 + (a[:, gw:] - 1.0) * ka_ref[...])
    bonus = _head_sum(r * (k0 + k1) * rk_ref[...], block) * v
    feats = dict(r=r, v=v, w0=decay[:, :gw], w1=decay[:, gw:], k0=k0, k1=k1, a=-kk, b0=kk * a[:, :gw],
                 b1=kk * a[:, gw:], g=g, bonus=bonus)
    for j, name in enumerate(RWKV_FEATS):
        o_ref[0, :, j * gw:(j + 1) * gw] = feats[name]


def _block_diag2(w):
    z = jnp.zeros_like(w[0])
    return jnp.concatenate([jnp.concatenate([w[0], z], axis=1), jnp.concatenate([z, w[1]], axis=1)], axis=0)


def _rwkv_features(p, tc, mu, w0, w2, a0, a2, g2, k_k, k_a, r_k):
    b, t, cols = p.shape
    gw = GROUP_WIDTH
    n_tiles, sub, n_sub = t // ROW_TILE, ROW_TILE // 8, t // 8
    row = lambda w: pl.BlockSpec((1, ROW_TILE, w), lambda bi, i: (bi, i, 0))
    prev = pl.BlockSpec((1, 8, cols), lambda bi, i: (bi, jnp.maximum(i * sub - 1, 0), 0))
    nxt = pl.BlockSpec((1, 8, cols), lambda bi, i: (bi, jnp.minimum((i + 1) * sub, n_sub - 1), 0))
    full = lambda a: pl.BlockSpec(a.shape, lambda bi, i: (0,) * a.ndim)
    vec = lambda a: a.reshape(1, -1).astype(F32)
    g2_pad = jnp.pad(g2, ((0, gw - g2.shape[0]), (0, 0))).astype(BF16)
    args = [jnp.pad(vec(mu), ((0, 0), (0, cols - mu.shape[0]))), vec(w0), _block_diag2(w2).astype(BF16), vec(a0),
            _block_diag2(a2).astype(BF16), g2_pad, vec(k_k), vec(k_a), vec(r_k)]
    return pl.pallas_call(
        functools.partial(_rwkv_feat_body, ctx_tiles=tc // ROW_TILE, n_tiles=n_tiles),
        grid=(b, n_tiles),
        in_specs=[row(cols), prev, nxt] + [full(a) for a in args],
        out_specs=row(len(RWKV_FEATS) * gw),
        out_shape=jax.ShapeDtypeStruct((b, t, len(RWKV_FEATS) * gw), F32),
        compiler_params=_cparams(("parallel", "parallel")),
        name="rwkv_feat",
    )(p, p, p, *args)


HGRN_FEATS = ("q", "w0", "w1", "k0", "k1", "gate")


def _hgrn_feat_body(p_ref, llb_ref, l1m_ref, oml_ref, o_ref):
    gw = GROUP_WIDTH
    p = p_ref[0]
    q, f_f, f_b, g = p[:, :gw], p[:, gw:2 * gw], p[:, 2 * gw:3 * gw], p[:, 4 * gw:5 * gw]
    feats = dict(q=q * jax.nn.sigmoid(q), gate=g * jax.nn.sigmoid(g))
    for d, f in enumerate((f_f, f_b)):
        x = llb_ref[...]
        y = l1m_ref[...] - _softplus(-f)
        log_f = jnp.maximum(x, y) + jnp.log(1.0 + jnp.exp(-jnp.abs(x - y)))
        feats["w%d" % d] = jnp.exp(log_f)
        feats["k%d" % d] = oml_ref[...] * jax.nn.sigmoid(-f)
    for j, name in enumerate(HGRN_FEATS):
        o_ref[0, :, j * gw:(j + 1) * gw] = feats[name]


def _hgrn_features(p, lb):
    b, t, cols = p.shape
    gw = GROUP_WIDTH
    row = lambda w: pl.BlockSpec((1, ROW_TILE, w), lambda bi, i: (bi, i, 0))
    vecs = [jnp.log(jnp.maximum(lb, LB_FLOOR)).reshape(1, gw), jnp.log1p(-lb).reshape(1, gw), (1.0 - lb).reshape(1, gw)]
    return pl.pallas_call(
        _hgrn_feat_body,
        grid=(b, t // ROW_TILE),
        in_specs=[row(cols)] + [pl.BlockSpec((1, gw), lambda bi, i: (0, 0))] * 3,
        out_specs=row(len(HGRN_FEATS) * gw),
        out_shape=jax.ShapeDtypeStruct((b, t, len(HGRN_FEATS) * gw), F32),
        compiler_params=_cparams(("parallel", "parallel")),
        name="hgrn_feat",
    )(p, *vecs)


def _to_scan_body(x0_ref, x1_ref, o_ref, r_ref, *, kind, nb):
    n_scan = 2 * nb * N_HEADS
    rep = LANES // n_scan
    for b in range(nb):
        r_ref[pl.ds(b * GROUP_WIDTH, GROUP_WIDTH), :] = x0_ref[b].T
        r_ref[pl.ds((nb + b) * GROUP_WIDTH, GROUP_WIDTH), :] = _flip_rows(x1_ref[b]).T
    if kind == "k":
        for k in range(HEAD_DIM):
            rows = r_ref[pl.ds(k, n_scan, stride=HEAD_DIM), :]
            o_ref[k] = jnp.concatenate([rows] * rep, axis=0).T
    else:
        for vh in range(HEAD_DIM // rep):
            rows = [r_ref[pl.ds(vh * rep + vl, n_scan, stride=HEAD_DIM), :] for vl in range(rep)]
            o_ref[:, vh, :] = jnp.concatenate(rows, axis=0).T


def _to_scan(x, col0, col1, kind, tc):
    nb, t, _ = x.shape
    gw = GROUP_WIDTH
    x0 = x1 = x
    n_all, n_ctx = t // TIME_BLOCK, tc // TIME_BLOCK
    n_scan = 2 * nb * N_HEADS
    rep = LANES // n_scan
    fwd = pl.BlockSpec((nb, TIME_BLOCK, gw), lambda i: (0, i, col0))
    bwd = pl.BlockSpec((nb, TIME_BLOCK, gw), lambda i: (0, _rev_block(i, n_ctx, n_all), col1))
    if kind == "k":
        out_shape = (HEAD_DIM, t, LANES)
        out_spec = pl.BlockSpec((HEAD_DIM, TIME_BLOCK, LANES), lambda i: (0, i, 0))
    else:
        out_shape = (t, HEAD_DIM // rep, LANES)
        out_spec = pl.BlockSpec((TIME_BLOCK, HEAD_DIM // rep, LANES), lambda i: (i, 0, 0))
    return pl.pallas_call(
        functools.partial(_to_scan_body, kind=kind, nb=nb),
        grid=(n_all,),
        in_specs=[fwd, bwd],
        out_specs=out_spec,
        out_shape=jax.ShapeDtypeStruct(out_shape, F32),
        scratch_shapes=[pltpu.VMEM((n_scan * HEAD_DIM, TIME_BLOCK), F32)],
        compiler_params=_cparams(("parallel",)),
        name="to_scan_" + kind,
    )(x0, x1)


def _from_scan_body(yf_ref, yb_ref, *rest, nb, readout):
    if readout == "rwkv":
        bonus_ref, g_ref, ng_ref, nb_ref, o_ref, r_ref = rest
    else:
        gate_ref, ng_ref, o_ref, r_ref = rest
    n_scan = 2 * nb * N_HEADS
    rep = LANES // n_scan
    for vh in range(HEAD_DIM // rep):
        r_ref[0, pl.ds(vh * LANES, LANES), :] = yf_ref[:, vh, :].T
        r_ref[1, pl.ds(vh * LANES, LANES), :] = yb_ref[:, vh, :].T
    mean_block = _head_block(1.0 / HEAD_DIM).astype(BF16)
    for b in range(nb):
        slab = lambda d: jnp.concatenate(
            [r_ref[d, pl.ds((d * nb + b) * N_HEADS + h, HEAD_DIM, stride=n_scan), :] for h in range(N_HEADS)],
            axis=0).T
        y = slab(0) + _flip_rows(slab(1))
        if readout == "rwkv":
            y = y - _head_sum(y, mean_block)
            y = y * lax.rsqrt(_head_sum(y * y, mean_block) + RWKV_LNX_EPS) * ng_ref[...] + nb_ref[...]
            o_ref[b] = (y + bonus_ref[b]) * g_ref[b]
        else:
            y = y * lax.rsqrt(_head_sum(y * y, mean_block) + LN_EPS) * ng_ref[...]
            o_ref[b] = y * gate_ref[b]


def _from_scan(y, nb, tc, readout, feats, cols, vecs):
    t, n_vh, _ = y.shape
    gw = GROUP_WIDTH
    n_all, n_ctx = t // TIME_BLOCK, tc // TIME_BLOCK
    fwd = pl.BlockSpec((TIME_BLOCK, n_vh, LANES), lambda i: (i, 0, 0))
    bwd = pl.BlockSpec((TIME_BLOCK, n_vh, LANES), lambda i: (_rev_block(i, n_ctx, n_all), 0, 0))
    tok = lambda c: pl.BlockSpec((nb, TIME_BLOCK, gw), lambda i: (0, i, c))
    vec = pl.BlockSpec((1, gw), lambda i: (0, 0))
    return pl.pallas_call(
        functools.partial(_from_scan_body, nb=nb, readout=readout),
        grid=(n_all,),
        in_specs=[fwd, bwd] + [tok(c) for c in cols] + [vec] * len(vecs),
        out_specs=tok(0),
        out_shape=jax.ShapeDtypeStruct((nb, t, gw), F32),
        scratch_shapes=[pltpu.VMEM((2, n_vh * LANES, TIME_BLOCK), F32)],
        compiler_params=_cparams(("parallel",)),
        name="from_scan_" + readout,
    )(y, y, *([feats] * len(cols)), *[v.reshape(1, gw) for v in vecs])


N_ACC = 4


def _acc_add(acc, i, x):
    acc[i % N_ACC] = x if acc[i % N_ACC] is None else acc[i % N_ACC] + x


def _acc_total(acc):
    return (acc[0] + acc[1]) + (acc[2] + acc[3])


def _scan_body(*refs, mode, tb, n_vh):
    if mode == "rwkv":
        r_ref, w_ref, k_ref, a_ref, b_ref, v_ref, y_ref, s_ref = refs
    else:
        r_ref, w_ref, k_ref, v_ref, y_ref, s_ref = refs

    @pl.when(pl.program_id(0) == 0)
    def _():
        s_ref[...] = jnp.zeros_like(s_ref)

    row = lambda ref, k, j: ref[k, pl.ds(j, 1), :]
    wide = lambda x: jnp.broadcast_to(x, (n_vh, LANES))

    def rwkv_step(j, sa):
        v = v_ref[j]
        j_next = jnp.minimum(j + 1, tb - 1)
        ys, sas = [None] * N_ACC, [None] * N_ACC
        for k in range(HEAD_DIM):
            s = s_ref[k] * wide(row(w_ref, k, j)) + sa * wide(row(b_ref, k, j)) + v * wide(row(k_ref, k, j))
            s_ref[k] = s
            _acc_add(ys, k, s * wide(row(r_ref, k, j)))
            _acc_add(sas, k, s * wide(row(a_ref, k, j_next)))
        y_ref[j] = _acc_total(ys)
        return _acc_total(sas)

    def gla_step(j, carry):
        v = v_ref[j]
        ys = [None] * N_ACC
        for k in range(HEAD_DIM):
            s = s_ref[k] * wide(row(w_ref, k, j)) + v * wide(row(k_ref, k, j))
            s_ref[k] = s
            _acc_add(ys, k, s * wide(row(r_ref, k, j)))
        y_ref[j] = _acc_total(ys)
        return carry

    if mode == "rwkv":
        sa0 = [None] * N_ACC
        for k in range(HEAD_DIM):
            _acc_add(sa0, k, s_ref[k] * wide(row(a_ref, k, 0)))
        lax.fori_loop(0, tb, rwkv_step, _acc_total(sa0), unroll=4)
    else:
        lax.fori_loop(0, tb, gla_step, 0, unroll=8)


def _scan(mode, k_inputs, v_in):
    t, n_vh, _ = v_in.shape
    tb = SCAN_TB
    big = pl.BlockSpec((HEAD_DIM, tb, LANES), lambda i: (0, i, 0))
    small = pl.BlockSpec((tb, n_vh, LANES), lambda i: (i, 0, 0))
    return pl.pallas_call(
        functools.partial(_scan_body, mode=mode, tb=tb, n_vh=n_vh),
        grid=(t // tb,),
        in_specs=[big] * len(k_inputs) + [small],
        out_specs=small,
        out_shape=jax.ShapeDtypeStruct((t, n_vh, LANES), F32),
        scratch_shapes=[pltpu.VMEM((HEAD_DIM, n_vh, LANES), F32)],
        compiler_params=_cparams(("arbitrary",)),
        name="scan_" + mode,
    )(*k_inputs, v_in)


MLSTM_CHUNK = 64


def _prefix_max(x, reverse):
    n = x.shape[0]
    row = lax.broadcasted_iota(jnp.int32, x.shape, 0)
    sh = 1
    while sh < n:
        if reverse:
            x = jnp.where(row < n - sh, jnp.maximum(x, pltpu.roll(x, n - sh, 0)), x)
        else:
            x = jnp.where(row >= sh, jnp.maximum(x, pltpu.roll(x, sh, 0)), x)
        sh *= 2
    return x


def _mlstm_body(qk_ref, qkp_ref, qkn_ref, v_ref, gt_ref, cw_ref, cb_ref, sel_ref, gb_ref, o_ref, st_ref, m_ref, *,
                reverse, n_sub, n_ctx, n_blocks):
    c, gw = MLSTM_CHUNK, GROUP_WIDTH
    i = pl.program_id(1)

    @pl.when(i == 0)
    def _():
        st_ref[...] = jnp.zeros_like(st_ref)
        m_ref[...] = jnp.zeros_like(m_ref)

    blk = _rev_block(i, n_ctx, n_blocks) if reverse else i
    seg_start = jnp.logical_or(blk == 0, blk == n_ctx)
    seg_end = jnp.logical_or(blk == n_ctx - 1, blk == n_blocks - 1)
    qk = qk_ref[0]
    prev, nxt = _shifted_rows(qk, qkp_ref[0], qkn_ref[0], seg_start, seg_end)
    qk = prev * cw_ref[0:1, :] + qk * cw_ref[1:2, :] + nxt * cw_ref[2:3, :] + cb_ref[...]
    qk = qk * jax.nn.sigmoid(qk)
    q_all, k_all = qk[:, :gw], qk[:, gw:] * HEAD_DIM ** -0.5
    v_all = v_ref[0]
    gt = gt_ref[0]
    ig_all = jnp.dot(gt, sel_ref[0], precision=HIGHEST, preferred_element_type=F32) + gb_ref[0:1, :]
    fg_all = -_softplus(-(jnp.dot(gt, sel_ref[1], precision=HIGHEST, preferred_element_type=F32) + gb_ref[1:2, :]))

    tt = lax.broadcasted_iota(jnp.int32, (c, c), 0)
    ss = lax.broadcasted_iota(jnp.int32, (c, c), 1)
    tri = jnp.where((ss >= tt) if reverse else (ss <= tt), 1.0, 0.0).astype(F32)
    row = lax.broadcasted_iota(jnp.int32, (c, gw), 0)
    s_of_lane = lax.broadcasted_iota(jnp.int32, (c, gw), 1) % c
    causal = (s_of_lane >= row) if reverse else (s_of_lane <= row)
    diag = s_of_lane == row
    block = _head_block()
    block2 = jnp.concatenate([block, block], axis=1)
    ones8 = jnp.ones((8, c), F32)
    nt = (((1,), (1,)), ((), ()))
    last = 0 if reverse else c - 1
    for u in (range(n_sub - 1, -1, -1) if reverse else range(n_sub)):
        sl = slice(u * c, (u + 1) * c)
        q, k, v, ig, fg = q_all[sl], k_all[sl], v_all[sl], ig_all[sl], fg_all[sl]
        b = jnp.dot(tri, fg, precision=HIGHEST, preferred_element_type=F32)
        g = ig - b
        m_prev = m_ref[0:1, :]
        m_t = b + jnp.maximum(m_prev, _prefix_max(g, reverse))
        w_inter = jnp.exp(b + m_prev - m_t)
        g_row = jnp.dot(ones8, jnp.where(diag, g, 0.0), precision=HIGHEST, preferred_element_type=F32)[0:1]
        qb = q.astype(BF16)
        k_bd = (jnp.concatenate([k] * N_HEADS, axis=0) * block).astype(BF16)
        scores = lax.dot_general(qb, k_bd, nt, preferred_element_type=F32)
        w = jnp.where(causal, jnp.exp((b - m_t) + g_row), 0.0) * scores
        v_bd = jnp.concatenate([jnp.concatenate([v] * N_HEADS, axis=0) * block, block], axis=1).astype(BF16)
        intra = jnp.dot(w.astype(BF16), v_bd, preferred_element_type=F32)
        inter = jnp.dot(qb, st_ref[...].astype(BF16), preferred_element_type=F32)
        num = w_inter * inter[:, :gw] + intra[:, :gw]
        den = w_inter * inter[:, gw:] + intra[:, gw:]
        floor = jnp.exp(jnp.minimum(-m_t, MAX_NEG_LOG_STAB))
        o_ref[0, pl.ds(u * c, c), :] = num / jnp.maximum(jnp.abs(den), floor)
        m_new = m_t[last:last + 1]
        b_end = b[last:last + 1]
        kw = k * jnp.exp(b_end - b + ig - m_new)
        decay = jnp.exp(b_end + m_prev - m_new)
        v_one = jnp.concatenate([v, jnp.ones_like(v)], axis=1).astype(BF16)
        upd = jnp.dot(kw.T.astype(BF16), v_one, preferred_element_type=F32)
        st_ref[...] = st_ref[...] * jnp.concatenate([decay, decay], axis=1) + upd * block2
        m_ref[0:1, :] = m_new


def _mlstm_chunked(p, conv_w, conv_b, gate_b, reverse, tc):
    b, t, _ = p.shape
    gw = GROUP_WIDTH
    n_blocks, n_ctx = t // ROW_TILE, tc // ROW_TILE
    sub, n_sub8 = ROW_TILE // 8, t // 8
    blk = (lambda i: _rev_block(i, n_ctx, n_blocks)) if reverse else (lambda i: i)
    d = 1 if reverse else 0
    lane_head = np.arange(gw) // HEAD_DIM
    sel = np.zeros((2, LANES, gw), np.float32)
    for j, base in enumerate((d * N_HEADS, (2 + d) * N_HEADS)):
        sel[j, base + lane_head, np.arange(gw)] = 1.0
    gate_bias = jnp.stack([jnp.repeat(gate_b[d], HEAD_DIM), jnp.repeat(gate_b[2 + d], HEAD_DIM)], axis=0)
    full = lambda a: pl.BlockSpec(a.shape, lambda bi, i: (0,) * a.ndim)
    cb = conv_b.reshape(1, 2 * gw)
    return pl.pallas_call(
        functools.partial(_mlstm_body, reverse=reverse, n_sub=ROW_TILE // MLSTM_CHUNK, n_ctx=n_ctx,
                          n_blocks=n_blocks),
        grid=(b, n_blocks),
        in_specs=[pl.BlockSpec((1, ROW_TILE, 2 * gw), lambda bi, i: (bi, blk(i), 0)),
                  pl.BlockSpec((1, 8, 2 * gw), lambda bi, i: (bi, jnp.maximum(blk(i) * sub - 1, 0), 0)),
                  pl.BlockSpec((1, 8, 2 * gw), lambda bi, i: (bi, jnp.minimum((blk(i) + 1) * sub, n_sub8 - 1), 0)),
                  pl.BlockSpec((1, ROW_TILE, gw), lambda bi, i: (bi, blk(i), 2)),
                  pl.BlockSpec((1, ROW_TILE, LANES), lambda bi, i: (bi, blk(i), 4 * gw // LANES)),
                  full(conv_w), full(cb), full(sel), full(gate_bias)],
        out_specs=pl.BlockSpec((1, ROW_TILE, gw), lambda bi, i: (bi, blk(i), 0)),
        out_shape=jax.ShapeDtypeStruct((b, t, gw), F32),
        scratch_shapes=[pltpu.VMEM((gw, 2 * gw), F32), pltpu.VMEM((8, gw), F32)],
        compiler_params=_cparams(("parallel", "arbitrary")),
        name="mlstm_chunk",
    )(p, p, p, p, p, conv_w, cb, jnp.asarray(sel), gate_bias)


def _attn_body(q_ref, k_ref, v_ref, lam_ref, g_ref, o_ref, *, out_scale):
    q = q_ref[0] * (DIFF_QK_DIM ** -0.5 * math.log2(math.e))
    k = k_ref[0].astype(BF16)
    v = v_ref[0].astype(BF16)
    lane = lax.broadcasted_iota(jnp.int32, q.shape, 1)
    comp = lane // DIFF_QK_DIM
    nt = (((1,), (1,)), ((), ()))

    n_maps = LANES // DIFF_QK_DIM
    ss = [lax.dot_general(jnp.where(comp == c, q, 0.0).astype(BF16), k, nt, preferred_element_type=F32)
          for c in range(n_maps)]
    es = [jnp.exp2(s - jnp.max(s, axis=-1, keepdims=True)) for s in ss]
    ls = [jnp.sum(e, axis=-1, keepdims=True) for e in es]
    ys = []
    for hh in range(LANES // HEAD_DIM):
        e0, e1, l0, l1 = es[2 * hh], es[2 * hh + 1], ls[2 * hh], ls[2 * hh + 1]
        pr = (e0 - e1 * (lam_ref[0:1, 0:1] * l0 / l1)).astype(BF16)
        ys.append(jnp.dot(pr, v, preferred_element_type=F32) * (1.0 / l0))
    first = lane < HEAD_DIM
    y = jnp.where(first, ys[0], ys[1])
    ysq = y * y
    ms = jnp.where(first, jnp.sum(jnp.where(first, ysq, 0.0), axis=-1, keepdims=True),
                   jnp.sum(jnp.where(first, 0.0, ysq), axis=-1, keepdims=True)) * (1.0 / HEAD_DIM)
    o_ref[0] = y * lax.rsqrt(ms + LN_EPS) * g_ref[...] * out_scale


def _attention(q, k, v, v_col0, lam, norm_g, out_scale, q_block0, n_q_blocks, n_keys, prev=None):
    b, t, gw = q.shape
    qspec = pl.BlockSpec((1, ATTN_TILE, LANES), lambda bi, pi, i: (bi, q_block0 + i, pi))
    kspec = pl.BlockSpec((1, n_keys, LANES), lambda bi, pi, i: (bi, 0, pi))
    vspec = pl.BlockSpec((1, n_keys, LANES), lambda bi, pi, i: (bi, 0, v_col0 + pi))
    in_specs = [qspec, kspec, vspec,
                pl.BlockSpec((1, LANES), lambda bi, pi, i: (0, 0)),
                pl.BlockSpec((1, LANES), lambda bi, pi, i: (0, 0))]
    args = [q, k, v, jnp.full((1, LANES), lam, F32), jnp.tile(norm_g.reshape(1, HEAD_DIM), (1, LANES // HEAD_DIM))]
    aliases = {}
    body = functools.partial(_attn_body, out_scale=out_scale)
    if prev is not None:
        in_specs.append(pl.BlockSpec(memory_space=pl.ANY))
        args.append(prev)
        aliases = {5: 0}
        body = lambda q_ref, k_ref, v_ref, lam_ref, g_ref, prev_ref, o_ref: _attn_body(
            q_ref, k_ref, v_ref, lam_ref, g_ref, o_ref, out_scale=out_scale)
    return pl.pallas_call(
        body,
        grid=(b, gw // LANES, n_q_blocks),
        in_specs=in_specs,
        out_specs=qspec,
        out_shape=jax.ShapeDtypeStruct((b, t, gw), F32),
        input_output_aliases=aliases,
        compiler_params=_cparams(("parallel", "parallel", "parallel")),
        name="diff_attn",
    )(*args)


def _layernorm(z, g, b):
    z = z - jnp.mean(z, axis=-1, keepdims=True)
    return z * lax.rsqrt(jnp.mean(z * z, axis=-1, keepdims=True) + LN_EPS) * g + b


def _outproj_body(ya_ref, yb_ref, yc_ref, hf_ref, hb_ref, og_ref, ng_ref, h_ref, gl_ref, gc_ref, w_ref, g_ref,
                  b_ref, o_ref, *, ctx_tiles, alpha):
    is_ctx = pl.program_id(1) < ctx_tiles
    gate = jnp.where(is_ctx, gc_ref[0], gl_ref[0])
    mean_block = _head_block(1.0 / HEAD_DIM).astype(BF16)
    yd = hf_ref[0] + hb_ref[0]
    yd = yd - _head_sum(yd, mean_block)
    yd = yd * lax.rsqrt(_head_sum(yd * yd, mean_block) + LN_EPS) * ng_ref[...] * jax.nn.sigmoid(og_ref[0])
    y = None
    for m, ym in enumerate((ya_ref[0], yb_ref[0], yc_ref[0], yd)):
        part = jnp.dot(ym.astype(BF16), w_ref[m * GROUP_WIDTH:(m + 1) * GROUP_WIDTH, :],
                       preferred_element_type=F32)
        y = part if y is None else y + part
    o_ref[0] = _layernorm(alpha * h_ref[0] + gate * y, g_ref[...], b_ref[...])


def _outproj(ya, yb, yc, hf, hb, p_mlstm, mlstm_norm_g, h, gate_l, gate_c, w, ln_g, ln_b, tile0, ctx_tiles, alpha):
    b, t, d = h.shape
    gw = GROUP_WIDTH
    n_tiles = t // ROW_TILE - tile0
    yrow = lambda c: pl.BlockSpec((1, ROW_TILE, gw), lambda bi, i: (bi, tile0 + i, c))
    hrow = pl.BlockSpec((1, ROW_TILE, d), lambda bi, i: (bi, tile0 + i, 0))
    lat = pl.BlockSpec((1, 1, d), lambda bi, i: (bi, 0, 0))
    ctx = pl.BlockSpec((1, 1, d), lambda bi, i: (0, 0, 0))
    vec = lambda n: pl.BlockSpec((1, n), lambda bi, i: (0, 0))
    return pl.pallas_call(
        functools.partial(_outproj_body, ctx_tiles=ctx_tiles, alpha=alpha),
        grid=(b, n_tiles),
        in_specs=[yrow(0)] * 5 + [yrow(3), vec(gw), hrow, lat, ctx, pl.BlockSpec(w.shape, lambda bi, i: (0, 0)),
                                   vec(d), vec(d)],
        out_specs=pl.BlockSpec((1, ROW_TILE, d), lambda bi, i: (bi, i, 0)),
        out_shape=jax.ShapeDtypeStruct((b, n_tiles * ROW_TILE, d), F32),
        compiler_params=_cparams(("parallel", "parallel")),
        name="outproj_ln",
    )(ya, yb, yc, hf, hb, p_mlstm, mlstm_norm_g.reshape(1, gw), h, gate_l, gate_c, w, ln_g.reshape(1, d),
      ln_b.reshape(1, d))


FFN_CHUNKS = 2


def _ffn_body(x_ref, xp_ref, xn_ref, sl_ref, cl_ref, gl_ref, sc_ref, cc_ref, gc_ref,
              wu_ref, cw_ref, cb_ref, wd_ref, g_ref, b_ref, o_ref, *, ctx_tiles, n_tiles, alpha, d_ff):
    i = pl.program_id(1)
    is_ctx = i < ctx_tiles
    shift = jnp.where(is_ctx, sc_ref[0], sl_ref[0])
    scale = jnp.where(is_ctx, cc_ref[0], cl_ref[0])
    gate_mod = jnp.where(is_ctx, gc_ref[0], gl_ref[0])
    seg_start = jnp.logical_or(i == 0, i == ctx_tiles)
    seg_end = jnp.logical_or(i == ctx_tiles - 1, i == n_tiles - 1)

    x = x_ref[0]
    xm = (x * (1.0 + scale) + shift).astype(BF16)
    xp = (xp_ref[0] * (1.0 + scale) + shift).astype(BF16)
    xn = (xn_ref[0] * (1.0 + scale) + shift).astype(BF16)
    rows = x.shape[0]
    row_id = lax.broadcasted_iota(jnp.int32, (rows, 1), 0)
    ch = d_ff // FFN_CHUNKS
    f = jnp.zeros(x.shape, F32)
    for c in range(FFN_CHUNKS):
        wg = wu_ref[:, c * ch:(c + 1) * ch]
        wv = wu_ref[:, d_ff + c * ch:d_ff + (c + 1) * ch]
        gate = jnp.dot(xm, wg, preferred_element_type=F32)
        val = jnp.dot(xm, wv, preferred_element_type=F32)
        gp = jnp.dot(xp, wg, preferred_element_type=F32)[7:8, :]
        gn = jnp.dot(xn, wg, preferred_element_type=F32)[0:1, :]
        gp = jnp.where(seg_start, 0.0, gp)
        gn = jnp.where(seg_end, 0.0, gn)
        prev = jnp.where(row_id == 0, gp, pltpu.roll(gate, 1, 0))
        nxt = jnp.where(row_id == rows - 1, gn, pltpu.roll(gate, rows - 1, 0))
        cw = cw_ref[:, c * ch:(c + 1) * ch]
        conv = prev * cw[0:1, :] + gate * cw[1:2, :] + nxt * cw[2:3, :] + cb_ref[:, c * ch:(c + 1) * ch]
        act = (jax.nn.gelu(conv) * val).astype(BF16)
        f = f + jnp.dot(act, wd_ref[c * ch:(c + 1) * ch, :], preferred_element_type=F32)
    o_ref[0] = _layernorm(alpha * x + gate_mod * f, g_ref[...], b_ref[...])


def _ffn(h, mods_l, mods_c, w_up, conv_w, conv_b, w_down, ln_g, ln_b, ctx_tiles, alpha):
    b, t, d = h.shape
    d_ff = w_down.shape[0]
    n_tiles = t // ROW_TILE
    sub = ROW_TILE // 8
    n_sub = t // 8
    row = pl.BlockSpec((1, ROW_TILE, d), lambda bi, i: (bi, i, 0))
    prev = pl.BlockSpec((1, 8, d), lambda bi, i: (bi, jnp.maximum(i * sub - 1, 0), 0))
    nxt = pl.BlockSpec((1, 8, d), lambda bi, i: (bi, jnp.minimum((i + 1) * sub, n_sub - 1), 0))
    lat = pl.BlockSpec((1, 1, d), lambda bi, i: (bi, 0, 0))
    ctx = pl.BlockSpec((1, 1, d), lambda bi, i: (0, 0, 0))
    vec = pl.BlockSpec((1, d), lambda bi, i: (0, 0))
    full = lambda a: pl.BlockSpec(a.shape, lambda bi, i: (0,) * a.ndim)
    cb = conv_b.reshape(1, d_ff)
    return pl.pallas_call(
        functools.partial(_ffn_body, ctx_tiles=ctx_tiles, n_tiles=n_tiles, alpha=alpha, d_ff=d_ff),
        grid=(b, n_tiles),
        in_specs=[row, prev, nxt, lat, lat, lat, ctx, ctx, ctx,
                  full(w_up), full(conv_w), full(cb), full(w_down), vec, vec],
        out_specs=row,
        out_shape=jax.ShapeDtypeStruct((b, t, d), F32),
        compiler_params=_cparams(("parallel", "parallel")),
        name="ffn",
    )(h, h, h, *mods_l, *mods_c, w_up, conv_w, cb, w_down, ln_g.reshape(1, d), ln_b.reshape(1, d))


def _rwkv7(p, tc, mu, w0, w2, a0, a2, g2, k_k, k_a, r_k, lnx_g, lnx_b):
    b = p.shape[0]
    f = _rwkv_features(p, tc, mu, w0, w2, a0, a2, g2, k_k, k_a, r_k)
    col = RWKV_FEATS.index
    y = _scan("rwkv",
              [_to_scan(f, col("r"), col("r"), "k", tc), _to_scan(f, col("w0"), col("w1"), "k", tc),
               _to_scan(f, col("k0"), col("k1"), "k", tc), _to_scan(f, col("a"), col("a"), "k", tc),
               _to_scan(f, col("b0"), col("b1"), "k", tc)],
              _to_scan(f, col("v"), col("v"), "v", tc))
    return _from_scan(y, b, tc, "rwkv", f, (col("bonus"), col("g")), (lnx_g, lnx_b))


def _rope_tables(n):
    rows = n // GRID_W
    row = jnp.repeat(jnp.arange(rows, dtype=F32), GRID_W)
    col = jnp.tile(jnp.arange(GRID_W, dtype=F32), rows)
    n_freq = DIFF_QK_DIM // 4
    inv_freq = ROPE_BASE ** (-jnp.arange(n_freq, dtype=F32) / n_freq)
    ang = jnp.concatenate([row[:, None] * inv_freq, col[:, None] * inv_freq], axis=-1)
    return jnp.cos(ang), jnp.sin(ang)


def _rope(x, cos, sin, tc):
    b, t, gw = x.shape
    half = DIFF_QK_DIM // 2
    cos = jnp.concatenate([jnp.ones((tc, half), F32), cos], axis=0)[:, None, :]
    sin = jnp.concatenate([jnp.zeros((tc, half), F32), sin], axis=0)[:, None, :]
    x = x.reshape(b, t, gw // DIFF_QK_DIM, DIFF_QK_DIM)
    x1, x2 = x[..., :half], x[..., half:]
    return jnp.concatenate([x1 * cos - x2 * sin, x2 * cos + x1 * sin], axis=-1).reshape(b, t, gw)


def _diff_attn(p, tc, lam_vecs, norm_g, layer, need_ctx):
    b, t, _ = p.shape
    gw = GROUP_WIDTH
    lam_init = 0.8 - 0.6 * math.exp(-0.3 * layer)
    lv = lam_vecs.astype(F32)
    lam = jnp.exp(jnp.sum(lv[0] * lv[1])) - jnp.exp(jnp.sum(lv[2] * lv[3])) + lam_init
    cos, sin = _rope_tables(t - tc)
    q = _rope(p[..., :gw], cos, sin, tc)
    k = _rope(p[..., gw:2 * gw], cos, sin, tc)
    v_col0 = 2 * gw // LANES
    out_scale = 1.0 - lam_init
    n_ctx, n_all = tc // ATTN_TILE, t // ATTN_TILE
    y = _attention(q, k, p, v_col0, lam, norm_g, out_scale, n_ctx, n_all - n_ctx, t)
    if need_ctx:
        y = _attention(q, k, p, v_col0, lam, norm_g, out_scale, 0, n_ctx, tc, prev=y)
    return y


def _hgrn2(p, tc, lb, norm_g):
    b = p.shape[0]
    f = _hgrn_features(p, lb)
    col = HGRN_FEATS.index
    o = _scan("gla",
              [_to_scan(f, col("q"), col("q"), "k", tc), _to_scan(f, col("w0"), col("w1"), "k", tc),
               _to_scan(f, col("k0"), col("k1"), "k", tc)],
              _to_scan(p, 3, 3, "v", tc))
    return _from_scan(o, b, tc, "gla", f, (col("gate"),), (jnp.tile(norm_g, N_HEADS),))


RWKV_COLS = 3 * GROUP_WIDTH + 2 * RWKV_DECAY_LORA + 2 * RWKV_ICLR_LORA + RWKV_GATE_LORA
DIFF_COLS = 3 * GROUP_WIDTH
HGRN_COLS = 5 * GROUP_WIDTH
MLSTM_COLS = 4 * GROUP_WIDTH + 4 * N_HEADS
SLAB_COLS = (RWKV_COLS, DIFF_COLS, HGRN_COLS, MLSTM_COLS)
SLAB_PAD = (5 * GROUP_WIDTH, DIFF_COLS, HGRN_COLS, 4 * GROUP_WIDTH + LANES)


def _split_cols(w):
    parts, off = [], 0
    for n, n_pad in zip(SLAB_COLS, SLAB_PAD):
        parts.append(jnp.pad(w[:, off:off + n], ((0, 0), (0, n_pad - n))).astype(BF16))
        off += n
    return parts


def kernel(x, c, ctx, c_ctx, ada_w, ada_b, w_in, rwkv_mu, rwkv_w0, rwkv_w2, rwkv_a0, rwkv_a2, rwkv_g2, rwkv_k_k, rwkv_k_a, rwkv_r_k, rwkv_lnx_g, rwkv_lnx_b, diff_lambda, diff_norm_g, hgrn_lb_logits, hgrn_norm_g, mlstm_conv_w, mlstm_conv_b, mlstm_gate_b, mlstm_norm_g, w_out, ffn_w_up, ffn_conv_w, ffn_conv_b, ffn_w_down, ln_g, ln_b):
    depth = w_in.shape[0]
    b, seq, d = x.shape
    tc = ctx.shape[1]
    assert tc == ROW_TILE and seq % ROW_TILE == 0 and 2 * b * N_HEADS <= LANES
    alpha = (2.0 * depth) ** 0.25

    lb_w = jax.nn.softmax(hgrn_lb_logits.astype(F32), axis=0)
    lower_bounds = jnp.cumsum(lb_w, axis=0) - lb_w[0]

    cond = jnp.zeros((8, d), F32).at[:b].set(c).at[b].set(c_ctx)
    h = jnp.concatenate([ctx, x], axis=1)
    for layer in range(depth):
        last = layer == depth - 1
        mod = _adaln(cond, ada_w[layer], ada_b[layer]).reshape(8, 6, d)
        mods_l = [mod[:b, j][:, None, :] for j in range(6)]
        mods_c = [mod[b:b + 1, j][:, None, :] for j in range(6)]

        p_rwkv, p_diff, p_hgrn, p_mlstm = _inproj(h, mods_l[0], mods_l[1], mods_c[0], mods_c[1],
                                                  _split_cols(w_in[layer]), tc // ROW_TILE)
        y_a = _rwkv7(p_rwkv, tc, rwkv_mu[layer], rwkv_w0[layer], rwkv_w2[layer], rwkv_a0[layer], rwkv_a2[layer],
                     rwkv_g2[layer], rwkv_k_k[layer], rwkv_k_a[layer], rwkv_r_k[layer], rwkv_lnx_g[layer],
                     rwkv_lnx_b[layer])
        y_b = _diff_attn(p_diff, tc, diff_lambda[layer], diff_norm_g[layer], layer, not last)
        y_c = _hgrn2(p_hgrn, tc, lower_bounds[layer], hgrn_norm_g[layer])
        h_f, h_b = (_mlstm_chunked(p_mlstm, mlstm_conv_w[layer], mlstm_conv_b[layer], mlstm_gate_b[layer], rev, tc)
                    for rev in (False, True))

        tile0, ctx_tiles = (tc // ROW_TILE, 0) if last else (0, tc // ROW_TILE)
        h = _outproj(y_a, y_b, y_c, h_f, h_b, p_mlstm, mlstm_norm_g[layer], h, mods_l[2], mods_c[2],
                     w_out[layer].astype(BF16), ln_g[layer, 0], ln_b[layer, 0], tile0, ctx_tiles, alpha)
        h = _ffn(h, mods_l[3:6], mods_c[3:6], ffn_w_up[layer].astype(BF16), ffn_conv_w[layer], ffn_conv_b[layer],
                 ffn_w_down[layer].astype(BF16), ln_g[layer, 1], ln_b[layer, 1], ctx_tiles, alpha)
    return h
```

```python
import functools
import math

import jax
import jax.numpy as jnp
import numpy as np
from jax import lax
from jax.experimental import pallas as pl
from jax.experimental.pallas import tpu as pltpu

F32 = jnp.float32
BF16 = jnp.bfloat16
HIGHEST = lax.Precision.HIGHEST

HEAD_DIM = 64
N_HEADS = 4
GROUP_WIDTH = N_HEADS * HEAD_DIM
DIFF_QK_DIM = HEAD_DIM // 2
GRID_W = 64
ROPE_BASE = 10000.0
RWKV_DECAY_LORA = 64
RWKV_ICLR_LORA = 64
RWKV_GATE_LORA = 160
RWKV_LNX_EPS = 64e-5
LN_EPS = 1e-5
LB_FLOOR = 1e-30
MAX_NEG_LOG_STAB = 60.0

LANES = 128
ROW_TILE = 256
ATTN_TILE = 256
TIME_BLOCK = 128
SCAN_TB = 64
VMEM_LIMIT = 56 * 1024 * 1024


def _cparams(sem):
    return pltpu.CompilerParams(dimension_semantics=sem, vmem_limit_bytes=VMEM_LIMIT)


def _rev_block(i, n_ctx, n_all):
    return jnp.where(i < n_ctx, n_ctx - 1 - i, n_ctx + n_all - 1 - i)


def _flip_rows(x):
    n = x.shape[0]
    r = lax.broadcasted_iota(jnp.int32, (n, n), 0)
    c = lax.broadcasted_iota(jnp.int32, (n, n), 1)
    perm = jnp.where(r + c == n - 1, 1.0, 0.0).astype(F32)
    return jnp.dot(perm, x, precision=HIGHEST, preferred_element_type=F32)


def _adaln_body(c_ref, w_ref, b_ref, o_ref):
    x = c_ref[...]
    x = (x * jax.nn.sigmoid(x)).astype(BF16)
    o_ref[...] = jnp.dot(x, w_ref[...].astype(BF16), preferred_element_type=F32) + b_ref[...]


def _adaln(cond, w, b):
    m, d = cond.shape
    n = w.shape[1]
    tn = 1536
    return pl.pallas_call(
        _adaln_body,
        grid=(n // tn,),
        in_specs=[pl.BlockSpec((m, d), lambda j: (0, 0)),
                  pl.BlockSpec((d, tn), lambda j: (0, j)),
                  pl.BlockSpec((1, tn), lambda j: (0, j))],
        out_specs=pl.BlockSpec((m, tn), lambda j: (0, j)),
        out_shape=jax.ShapeDtypeStruct((m, n), F32),
        compiler_params=_cparams(("arbitrary",)),
        name="adaln",
    )(cond, w, b.reshape(1, n))


def _inproj_body(x_ref, sl_ref, cl_ref, sc_ref, cc_ref, *refs, ctx_tiles):
    n = len(refs) // 2
    is_ctx = pl.program_id(1) < ctx_tiles
    shift = jnp.where(is_ctx, sc_ref[0], sl_ref[0])
    scale = jnp.where(is_ctx, cc_ref[0], cl_ref[0])
    xm = (x_ref[0] * (1.0 + scale) + shift).astype(BF16)
    for w_ref, o_ref in zip(refs[:n], refs[n:]):
        o_ref[0] = jnp.dot(xm, w_ref[...], preferred_element_type=F32)


def _inproj(h, shift_l, scale_l, shift_c, scale_c, ws, ctx_tiles):
    b, t, d = h.shape
    row = lambda n: pl.BlockSpec((1, ROW_TILE, n), lambda bi, i: (bi, i, 0))
    lat = pl.BlockSpec((1, 1, d), lambda bi, i: (bi, 0, 0))
    ctx = pl.BlockSpec((1, 1, d), lambda bi, i: (0, 0, 0))
    return pl.pallas_call(
        functools.partial(_inproj_body, ctx_tiles=ctx_tiles),
        grid=(b, t // ROW_TILE),
        in_specs=[row(d), lat, lat, ctx, ctx] + [pl.BlockSpec(w.shape, lambda bi, i: (0, 0)) for w in ws],
        out_specs=[row(w.shape[1]) for w in ws],
        out_shape=[jax.ShapeDtypeStruct((b, t, w.shape[1]), F32) for w in ws],
        compiler_params=_cparams(("parallel", "parallel")),
        name="inproj",
    )(h, shift_l, scale_l, shift_c, scale_c, *ws)


def _softplus(x):
    return jnp.maximum(x, 0.0) + jnp.log(1.0 + jnp.exp(-jnp.abs(x)))


def _head_block(scale=1.0):
    r = lax.broadcasted_iota(jnp.int32, (GROUP_WIDTH, GROUP_WIDTH), 0) // HEAD_DIM
    c = lax.broadcasted_iota(jnp.int32, (GROUP_WIDTH, GROUP_WIDTH), 1) // HEAD_DIM
    return jnp.where(r == c, scale, 0.0).astype(F32)


def _head_sum(x, block_bf16):
    hi = x.astype(BF16)
    lo = (x - hi.astype(F32)).astype(BF16)
    return (jnp.dot(hi, block_bf16, preferred_element_type=F32)
            + jnp.dot(lo, block_bf16, preferred_element_type=F32))


def _shifted_rows(x, prev8, next8, seg_start, seg_end):
    rows = x.shape[0]
    row_id = lax.broadcasted_iota(jnp.int32, (rows, 1), 0)
    prev_row = jnp.where(seg_start, 0.0, prev8[7:8, :])
    next_row = jnp.where(seg_end, 0.0, next8[0:1, :])
    prev = jnp.where(row_id == 0, prev_row, pltpu.roll(x, 1, 0))
    nxt = jnp.where(row_id == rows - 1, next_row, pltpu.roll(x, rows - 1, 0))
    return prev, nxt


RWKV_FEATS = ("r", "v", "w0", "w1", "k0", "k1", "a", "b0", "b1", "g", "bonus")


def _rwkv_feat_body(p_ref, pp_ref, pn_ref, mu_ref, w0_ref, w2_ref, a0_ref, a2_ref, g2_ref, kk_ref, ka_ref,
                    rk_ref, o_ref, *, ctx_tiles, n_tiles):
    i = pl.program_id(1)
    seg_start = jnp.logical_or(i == 0, i == ctx_tiles)
    seg_end = jnp.logical_or(i == ctx_tiles - 1, i == n_tiles - 1)
    gw = GROUP_WIDTH
    p = p_ref[0]
    prev, nxt = _shifted_rows(p, pp_ref[0], pn_ref[0], seg_start, seg_end)
    p = p + (0.5 * (prev + nxt) - p) * mu_ref[...]
    r, k, v = p[:, :gw], p[:, gw:2 * gw], p[:, 2 * gw:3 * gw]
    wd, ad, gd = p[:, 3 * gw:3 * gw + LANES], p[:, 3 * gw + LANES:4 * gw], p[:, 4 * gw:5 * gw]
    lora_w = jnp.dot(jnp.tanh(wd).astype(BF16), w2_ref[...], preferred_element_type=F32)
    lora_a = jnp.dot(ad.astype(BF16), a2_ref[...], preferred_element_type=F32)
    decay = jnp.exp(-jnp.exp(-_softplus(-(w0_ref[...] + lora_w)) - 0.5))
    a = jax.nn.sigmoid(a0_ref[...] + lora_a)
    g = jnp.dot(jax.nn.sigmoid(gd).astype(BF16), g2_ref[...], preferred_element_type=F32)
    block = _head_block().astype(BF16)
    kk = k * kk_ref[...]
    kk = kk / jnp.maximum(jnp.sqrt(_head_sum(kk * kk, block)), 1e-12)
    k0 = k * (1.0 + (a[:, :gw] - 1.0) * ka_ref[...])
    k1 = k * (1.0 + (a[:, gw:] - 1.0) * ka_ref[...])
    bonus = _head_sum(r * (k0 + k1) * rk_ref[...], block) * v
    feats = dict(r=r, v=v, w0=decay[:, :gw], w1=decay[:, gw:], k0=k0, k1=k1, a=-kk, b0=kk * a[:, :gw],
                 b1=kk * a[:, gw:], g=g, bonus=bonus)
    for j, name in enumerate(RWKV_FEATS):
        o_ref[0, :, j * gw:(j + 1) * gw] = feats[name]


def _block_diag2(w):
    z = jnp.zeros_like(w[0])
    return jnp.concatenate([jnp.concatenate([w[0], z], axis=1), jnp.concatenate([z, w[1]], axis=1)], axis=0)


def _rwkv_features(p, tc, mu, w0, w2, a0, a2, g2, k_k, k_a, r_k):
    b, t, cols = p.shape
    gw = GROUP_WIDTH
    n_tiles, sub, n_sub = t // ROW_TILE, ROW_TILE // 8, t // 8
    row = lambda w: pl.BlockSpec((1, ROW_TILE, w), lambda bi, i: (bi, i, 0))
    prev = pl.BlockSpec((1, 8, cols), lambda bi, i: (bi, jnp.maximum(i * sub - 1, 0), 0))
    nxt = pl.BlockSpec((1, 8, cols), lambda bi, i: (bi, jnp.minimum((i + 1) * sub, n_sub - 1), 0))
    full = lambda a: pl.BlockSpec(a.shape, lambda bi, i: (0,) * a.ndim)
    vec = lambda a: a.reshape(1, -1).astype(F32)
    g2_pad = jnp.pad(g2, ((0, gw - g2.shape[0]), (0, 0))).astype(BF16)
    args = [jnp.pad(vec(mu), ((0, 0), (0, cols - mu.shape[0]))), vec(w0), _block_diag2(w2).astype(BF16), vec(a0),
            _block_diag2(a2).astype(BF16), g2_pad, vec(k_k), vec(k_a), vec(r_k)]
    return pl.pallas_call(
        functools.partial(_rwkv_feat_body, ctx_tiles=tc // ROW_TILE, n_tiles=n_tiles),
        grid=(b, n_tiles),
        in_specs=[row(cols), prev, nxt] + [full(a) for a in args],
        out_specs=row(len(RWKV_FEATS) * gw),
        out_shape=jax.ShapeDtypeStruct((b, t, len(RWKV_FEATS) * gw), F32),
        compiler_params=_cparams(("parallel", "parallel")),
        name="rwkv_feat",
    )(p, p, p, *args)


HGRN_FEATS = ("q", "w0", "w1", "k0", "k1", "gate")


def _hgrn_feat_body(p_ref, llb_ref, l1m_ref, oml_ref, o_ref):
    gw = GROUP_WIDTH
    p = p_ref[0]
    q, f_f, f_b, g = p[:, :gw], p[:, gw:2 * gw], p[:, 2 * gw:3 * gw], p[:, 4 * gw:5 * gw]
    feats = dict(q=q * jax.nn.sigmoid(q), gate=g * jax.nn.sigmoid(g))
    for d, f in enumerate((f_f, f_b)):
        x = llb_ref[...]
        y = l1m_ref[...] - _softplus(-f)
        log_f = jnp.maximum(x, y) + jnp.log(1.0 + jnp.exp(-jnp.abs(x - y)))
        feats["w%d" % d] = jnp.exp(log_f)
        feats["k%d" % d] = oml_ref[...] * jax.nn.sigmoid(-f)
    for j, name in enumerate(HGRN_FEATS):
        o_ref[0, :, j * gw:(j + 1) * gw] = feats[name]


def _hgrn_features(p, lb):
    b, t, cols = p.shape
    gw = GROUP_WIDTH
    row = lambda w: pl.BlockSpec((1, ROW_TILE, w), lambda bi, i: (bi, i, 0))
    vecs = [jnp.log(jnp.maximum(lb, LB_FLOOR)).reshape(1, gw), jnp.log1p(-lb).reshape(1, gw), (1.0 - lb).reshape(1, gw)]
    return pl.pallas_call(
        _hgrn_feat_body,
        grid=(b, t // ROW_TILE),
        in_specs=[row(cols)] + [pl.BlockSpec((1, gw), lambda bi, i: (0, 0))] * 3,
        out_specs=row(len(HGRN_FEATS) * gw),
        out_shape=jax.ShapeDtypeStruct((b, t, len(HGRN_FEATS) * gw), F32),
        compiler_params=_cparams(("parallel", "parallel")),
        name="hgrn_feat",
    )(p, *vecs)


def _to_scan_body(x0_ref, x1_ref, o_ref, r_ref, *, kind, nb):
    n_scan = 2 * nb * N_HEADS
    rep = LANES // n_scan
    for b in range(nb):
        r_ref[pl.ds(b * GROUP_WIDTH, GROUP_WIDTH), :] = x0_ref[b].T
        r_ref[pl.ds((nb + b) * GROUP_WIDTH, GROUP_WIDTH), :] = _flip_rows(x1_ref[b]).T
    if kind == "k":
        for k in range(HEAD_DIM):
            rows = r_ref[pl.ds(k, n_scan, stride=HEAD_DIM), :]
            o_ref[k] = jnp.concatenate([rows] * rep, axis=0).T
    else:
        for vh in range(HEAD_DIM // rep):
            rows = [r_ref[pl.ds(vh * rep + vl, n_scan, stride=HEAD_DIM), :] for vl in range(rep)]
            o_ref[:, vh, :] = jnp.concatenate(rows, axis=0).T


def _to_scan(x, col0, col1, kind, tc):
    nb, t, _ = x.shape
    gw = GROUP_WIDTH
    x0 = x1 = x
    n_all, n_ctx = t // TIME_BLOCK, tc // TIME_BLOCK
    n_scan = 2 * nb * N_HEADS
    rep = LANES // n_scan
    fwd = pl.BlockSpec((nb, TIME_BLOCK, gw), lambda i: (0, i, col0))
    bwd = pl.BlockSpec((nb, TIME_BLOCK, gw), lambda i: (0, _rev_block(i, n_ctx, n_all), col1))
    if kind == "k":
        out_shape = (HEAD_DIM, t, LANES)
        out_spec = pl.BlockSpec((HEAD_DIM, TIME_BLOCK, LANES), lambda i: (0, i, 0))
    else:
        out_shape = (t, HEAD_DIM // rep, LANES)
        out_spec = pl.BlockSpec((TIME_BLOCK, HEAD_DIM // rep, LANES), lambda i: (i, 0, 0))
    return pl.pallas_call(
        functools.partial(_to_scan_body, kind=kind, nb=nb),
        grid=(n_all,),
        in_specs=[fwd, bwd],
        out_specs=out_spec,
        out_shape=jax.ShapeDtypeStruct(out_shape, F32),
        scratch_shapes=[pltpu.VMEM((n_scan * HEAD_DIM, TIME_BLOCK), F32)],
        compiler_params=_cparams(("parallel",)),
        name="to_scan_" + kind,
    )(x0, x1)


def _from_scan_body(yf_ref, yb_ref, *rest, nb, readout):
    if readout == "rwkv":
        bonus_ref, g_ref, ng_ref, nb_ref, o_ref, r_ref = rest
    else:
        gate_ref, ng_ref, o_ref, r_ref = rest
    n_scan = 2 * nb * N_HEADS
    rep = LANES // n_scan
    for vh in range(HEAD_DIM // rep):
        r_ref[0, pl.ds(vh * LANES, LANES), :] = yf_ref[:, vh, :].T
        r_ref[1, pl.ds(vh * LANES, LANES), :] = yb_ref[:, vh, :].T
    mean_block = _head_block(1.0 / HEAD_DIM).astype(BF16)
    for b in range(nb):
        slab = lambda d: jnp.concatenate(
            [r_ref[d, pl.ds((d * nb + b) * N_HEADS + h, HEAD_DIM, stride=n_scan), :] for h in range(N_HEADS)],
            axis=0).T
        y = slab(0) + _flip_rows(slab(1))
        if readout == "rwkv":
            y = y - _head_sum(y, mean_block)
            y = y * lax.rsqrt(_head_sum(y * y, mean_block) + RWKV_LNX_EPS) * ng_ref[...] + nb_ref[...]
            o_ref[b] = (y + bonus_ref[b]) * g_ref[b]
        else:
            y = y * lax.rsqrt(_head_sum(y * y, mean_block) + LN_EPS) * ng_ref[...]
            o_ref[b] = y * gate_ref[b]


def _from_scan(y, nb, tc, readout, feats, cols, vecs):
    t, n_vh, _ = y.shape
    gw = GROUP_WIDTH
    n_all, n_ctx = t // TIME_BLOCK, tc // TIME_BLOCK
    fwd = pl.BlockSpec((TIME_BLOCK, n_vh, LANES), lambda i: (i, 0, 0))
    bwd = pl.BlockSpec((TIME_BLOCK, n_vh, LANES), lambda i: (_rev_block(i, n_ctx, n_all), 0, 0))
    tok = lambda c: pl.BlockSpec((nb, TIME_BLOCK, gw), lambda i: (0, i, c))
    vec = pl.BlockSpec((1, gw), lambda i: (0, 0))
    return pl.pallas_call(
        functools.partial(_from_scan_body, nb=nb, readout=readout),
        grid=(n_all,),
        in_specs=[fwd, bwd] + [tok(c) for c in cols] + [vec] * len(vecs),
        out_specs=tok(0),
        out_shape=jax.ShapeDtypeStruct((nb, t, gw), F32),
        scratch_shapes=[pltpu.VMEM((2, n_vh * LANES, TIME_BLOCK), F32)],
        compiler_params=_cparams(("parallel",)),
        name="from_scan_" + readout,
    )(y, y, *([feats] * len(cols)), *[v.reshape(1, gw) for v in vecs])


N_ACC = 4


def _acc_add(acc, i, x):
    acc[i % N_ACC] = x if acc[i % N_ACC] is None else acc[i % N_ACC] + x


def _acc_total(acc):
    return (acc[0] + acc[1]) + (acc[2] + acc[3])


def _scan_body(*refs, mode, tb, n_vh):
    if mode == "rwkv":
        r_ref, w_ref, k_ref, a_ref, b_ref, v_ref, y_ref, s_ref = refs
    else:
        r_ref, w_ref, k_ref, v_ref, y_ref, s_ref = refs

    @pl.when(pl.program_id(0) == 0)
    def _():
        s_ref[...] = jnp.zeros_like(s_ref)

    row = lambda ref, k, j: ref[k, pl.ds(j, 1), :]
    wide = lambda x: jnp.broadcast_to(x, (n_vh, LANES))

    def rwkv_step(j, sa):
        v = v_ref[j]
        j_next = jnp.minimum(j + 1, tb - 1)
        ys, sas = [None] * N_ACC, [None] * N_ACC
        for k in range(HEAD_DIM):
            s = s_ref[k] * wide(row(w_ref, k, j)) + sa * wide(row(b_ref, k, j)) + v * wide(row(k_ref, k, j))
            s_ref[k] = s
            _acc_add(ys, k, s * wide(row(r_ref, k, j)))
            _acc_add(sas, k, s * wide(row(a_ref, k, j_next)))
        y_ref[j] = _acc_total(ys)
        return _acc_total(sas)

    def gla_step(j, carry):
        v = v_ref[j]
        ys = [None] * N_ACC
        for k in range(HEAD_DIM):
            s = s_ref[k] * wide(row(w_ref, k, j)) + v * wide(row(k_ref, k, j))
            s_ref[k] = s
            _acc_add(ys, k, s * wide(row(r_ref, k, j)))
        y_ref[j] = _acc_total(ys)
        return carry

    if mode == "rwkv":
        sa0 = [None] * N_ACC
        for k in range(HEAD_DIM):
            _acc_add(sa0, k, s_ref[k] * wide(row(a_ref, k, 0)))
        lax.fori_loop(0, tb, rwkv_step, _acc_total(sa0), unroll=4)
    else:
        lax.fori_loop(0, tb, gla_step, 0, unroll=8)


def _scan(mode, k_inputs, v_in):
    t, n_vh, _ = v_in.shape
    tb = SCAN_TB
    big = pl.BlockSpec((HEAD_DIM, tb, LANES), lambda i: (0, i, 0))
    small = pl.BlockSpec((tb, n_vh, LANES), lambda i: (i, 0, 0))
    return pl.pallas_call(
        functools.partial(_scan_body, mode=mode, tb=tb, n_vh=n_vh),
        grid=(t // tb,),
        in_specs=[big] * len(k_inputs) + [small],
        out_specs=small,
        out_shape=jax.ShapeDtypeStruct((t, n_vh, LANES), F32),
        scratch_shapes=[pltpu.VMEM((HEAD_DIM, n_vh, LANES), F32)],
        compiler_params=_cparams(("arbitrary",)),
        name="scan_" + mode,
    )(*k_inputs, v_in)


MLSTM_CHUNK = 64


def _prefix_max(x, reverse):
    n = x.shape[0]
    row = lax.broadcasted_iota(jnp.int32, x.shape, 0)
    sh = 1
    while sh < n:
        if reverse:
            x = jnp.where(row < n - sh, jnp.maximum(x, pltpu.roll(x, n - sh, 0)), x)
        else:
            x = jnp.where(row >= sh, jnp.maximum(x, pltpu.roll(x, sh, 0)), x)
        sh *= 2
    return x


def _mlstm_body(qk_ref, qkp_ref, qkn_ref, v_ref, gt_ref, cw_ref, cb_ref, sel_ref, gb_ref, o_ref, st_ref, m_ref, *,
                reverse, n_sub, n_ctx, n_blocks):
    c, gw = MLSTM_CHUNK, GROUP_WIDTH
    i = pl.program_id(1)

    @pl.when(i == 0)
    def _():
        st_ref[...] = jnp.zeros_like(st_ref)
        m_ref[...] = jnp.zeros_like(m_ref)

    blk = _rev_block(i, n_ctx, n_blocks) if reverse else i
    seg_start = jnp.logical_or(blk == 0, blk == n_ctx)
    seg_end = jnp.logical_or(blk == n_ctx - 1, blk == n_blocks - 1)
    qk = qk_ref[0]
    prev, nxt = _shifted_rows(qk, qkp_ref[0], qkn_ref[0], seg_start, seg_end)
    qk = prev * cw_ref[0:1, :] + qk * cw_ref[1:2, :] + nxt * cw_ref[2:3, :] + cb_ref[...]
    qk = qk * jax.nn.sigmoid(qk)
    q_all, k_all = qk[:, :gw], qk[:, gw:] * HEAD_DIM ** -0.5
    v_all = v_ref[0]
    gt = gt_ref[0]
    ig_all = jnp.dot(gt, sel_ref[0], precision=HIGHEST, preferred_element_type=F32) + gb_ref[0:1, :]
    fg_all = -_softplus(-(jnp.dot(gt, sel_ref[1], precision=HIGHEST, preferred_element_type=F32) + gb_ref[1:2, :]))

    tt = lax.broadcasted_iota(jnp.int32, (c, c), 0)
    ss = lax.broadcasted_iota(jnp.int32, (c, c), 1)
    tri = jnp.where((ss >= tt) if reverse else (ss <= tt), 1.0, 0.0).astype(F32)
    row = lax.broadcasted_iota(jnp.int32, (c, gw), 0)
    s_of_lane = lax.broadcasted_iota(jnp.int32, (c, gw), 1) % c
    causal = (s_of_lane >= row) if reverse else (s_of_lane <= row)
    diag = s_of_lane == row
    block = _head_block()
    block2 = jnp.concatenate([block, block], axis=1)
    ones8 = jnp.ones((8, c), F32)
    nt = (((1,), (1,)), ((), ()))
    last = 0 if reverse else c - 1
    for u in (range(n_sub - 1, -1, -1) if reverse else range(n_sub)):
        sl = slice(u * c, (u + 1) * c)
        q, k, v, ig, fg = q_all[sl], k_all[sl], v_all[sl], ig_all[sl], fg_all[sl]
        b = jnp.dot(tri, fg, precision=HIGHEST, preferred_element_type=F32)
        g = ig - b
        m_prev = m_ref[0:1, :]
        m_t = b + jnp.maximum(m_prev, _prefix_max(g, reverse))
        w_inter = jnp.exp(b + m_prev - m_t)
        g_row = jnp.dot(ones8, jnp.where(diag, g, 0.0), precision=HIGHEST, preferred_element_type=F32)[0:1]
        qb = q.astype(BF16)
        k_bd = (jnp.concatenate([k] * N_HEADS, axis=0) * block).astype(BF16)
        scores = lax.dot_general(qb, k_bd, nt, preferred_element_type=F32)
        w = jnp.where(causal, jnp.exp((b - m_t) + g_row), 0.0) * scores
        v_bd = jnp.concatenate([jnp.concatenate([v] * N_HEADS, axis=0) * block, block], axis=1).astype(BF16)
        intra = jnp.dot(w.astype(BF16), v_bd, preferred_element_type=F32)
        inter = jnp.dot(qb, st_ref[...].astype(BF16), preferred_element_type=F32)
        num = w_inter * inter[:, :gw] + intra[:, :gw]
        den = w_inter * inter[:, gw:] + intra[:, gw:]
        floor = jnp.exp(jnp.minimum(-m_t, MAX_NEG_LOG_STAB))
        o_ref[0, pl.ds(u * c, c), :] = num / jnp.maximum(jnp.abs(den), floor)
        m_new = m_t[last:last + 1]
        b_end = b[last:last + 1]
        kw = k * jnp.exp(b_end - b + ig - m_new)
        decay = jnp.exp(b_end + m_prev - m_new)
        v_one = jnp.concatenate([v, jnp.ones_like(v)], axis=1).astype(BF16)
        upd = jnp.dot(kw.T.astype(BF16), v_one, preferred_element_type=F32)
        st_ref[...] = st_ref[...] * jnp.concatenate([decay, decay], axis=1) + upd * block2
        m_ref[0:1, :] = m_new


def _mlstm_chunked(p, conv_w, conv_b, gate_b, reverse, tc):
    b, t, _ = p.shape
    gw = GROUP_WIDTH
    n_blocks, n_ctx = t // ROW_TILE, tc // ROW_TILE
    sub, n_sub8 = ROW_TILE // 8, t // 8
    blk = (lambda i: _rev_block(i, n_ctx, n_blocks)) if reverse else (lambda i: i)
    d = 1 if reverse else 0
    lane_head = np.arange(gw) // HEAD_DIM
    sel = np.zeros((2, LANES, gw), np.float32)
    for j, base in enumerate((d * N_HEADS, (2 + d) * N_HEADS)):
        sel[j, base + lane_head, np.arange(gw)] = 1.0
    gate_bias = jnp.stack([jnp.repeat(gate_b[d], HEAD_DIM), jnp.repeat(gate_b[2 + d], HEAD_DIM)], axis=0)
    full = lambda a: pl.BlockSpec(a.shape, lambda bi, i: (0,) * a.ndim)
    cb = conv_b.reshape(1, 2 * gw)
    return pl.pallas_call(
        functools.partial(_mlstm_body, reverse=reverse, n_sub=ROW_TILE // MLSTM_CHUNK, n_ctx=n_ctx,
                          n_blocks=n_blocks),
        grid=(b, n_blocks),
        in_specs=[pl.BlockSpec((1, ROW_TILE, 2 * gw), lambda bi, i: (bi, blk(i), 0)),
                  pl.BlockSpec((1, 8, 2 * gw), lambda bi, i: (bi, jnp.maximum(blk(i) * sub - 1, 0), 0)),
                  pl.BlockSpec((1, 8, 2 * gw), lambda bi, i: (bi, jnp.minimum((blk(i) + 1) * sub, n_sub8 - 1), 0)),
                  pl.BlockSpec((1, ROW_TILE, gw), lambda bi, i: (bi, blk(i), 2)),
                  pl.BlockSpec((1, ROW_TILE, LANES), lambda bi, i: (bi, blk(i), 4 * gw // LANES)),
                  full(conv_w), full(cb), full(sel), full(gate_bias)],
        out_specs=pl.BlockSpec((1, ROW_TILE, gw), lambda bi, i: (bi, blk(i), 0)),
        out_shape=jax.ShapeDtypeStruct((b, t, gw), F32),
        scratch_shapes=[pltpu.VMEM((gw, 2 * gw), F32), pltpu.VMEM((8, gw), F32)],
        compiler_params=_cparams(("parallel", "arbitrary")),
        name="mlstm_chunk",
    )(p, p, p, p, p, conv_w, cb, jnp.asarray(sel), gate_bias)


def _attn_body(q_ref, k_ref, v_ref, lam_ref, g_ref, o_ref, *, out_scale, ctx_tiles, n_ctx_keys):
    def attend(n_keys):
        q = q_ref[0] * (DIFF_QK_DIM ** -0.5 * math.log2(math.e))
        k = k_ref[0, :n_keys, :].astype(BF16)
        v = v_ref[0, :n_keys, :].astype(BF16)
        lane = lax.broadcasted_iota(jnp.int32, q.shape, 1)
        comp = lane // DIFF_QK_DIM
        nt = (((1,), (1,)), ((), ()))
        n_maps = LANES // DIFF_QK_DIM
        ss = [lax.dot_general(jnp.where(comp == c, q, 0.0).astype(BF16), k, nt, preferred_element_type=F32)
              for c in range(n_maps)]
        es = [jnp.exp2(s - jnp.max(s, axis=-1, keepdims=True)) for s in ss]
        ls = [jnp.sum(e, axis=-1, keepdims=True) for e in es]
        ys = []
        for hh in range(LANES // HEAD_DIM):
            e0, e1, l0, l1 = es[2 * hh], es[2 * hh + 1], ls[2 * hh], ls[2 * hh + 1]
            pr = (e0 - e1 * (lam_ref[0:1, 0:1] * l0 / l1)).astype(BF16)
            ys.append(jnp.dot(pr, v, preferred_element_type=F32) * (1.0 / l0))
        first = lane < HEAD_DIM
        y = jnp.where(first, ys[0], ys[1])
        ysq = y * y
        ms = jnp.where(first, jnp.sum(jnp.where(first, ysq, 0.0), axis=-1, keepdims=True),
                       jnp.sum(jnp.where(first, 0.0, ysq), axis=-1, keepdims=True)) * (1.0 / HEAD_DIM)
        o_ref[0] = y * lax.rsqrt(ms + LN_EPS) * g_ref[...] * out_scale

    if ctx_tiles == 0:
        attend(k_ref.shape[1])
    else:
        is_ctx = pl.program_id(2) < ctx_tiles
        pl.when(is_ctx)(lambda: attend(n_ctx_keys))
        pl.when(jnp.logical_not(is_ctx))(lambda: attend(k_ref.shape[1]))


def _attention(q, k, v, v_col0, lam, norm_g, out_scale, q_tile0, tc):
    b, t, gw = q.shape
    n_tiles = t // ATTN_TILE - q_tile0
    ctx_tiles = max(tc // ATTN_TILE - q_tile0, 0)
    qspec = pl.BlockSpec((1, ATTN_TILE, LANES), lambda bi, pi, i: (bi, q_tile0 + i, pi))
    kspec = pl.BlockSpec((1, t, LANES), lambda bi, pi, i: (bi, 0, pi))
    vspec = pl.BlockSpec((1, t, LANES), lambda bi, pi, i: (bi, 0, v_col0 + pi))
    vec = pl.BlockSpec((1, LANES), lambda bi, pi, i: (0, 0))
    return pl.pallas_call(
        functools.partial(_attn_body, out_scale=out_scale, ctx_tiles=ctx_tiles, n_ctx_keys=tc),
        grid=(b, gw // LANES, n_tiles),
        in_specs=[qspec, kspec, vspec, vec, vec],
        out_specs=pl.BlockSpec((1, ATTN_TILE, LANES), lambda bi, pi, i: (bi, i, pi)),
        out_shape=jax.ShapeDtypeStruct((b, n_tiles * ATTN_TILE, gw), F32),
        compiler_params=_cparams(("parallel", "parallel", "parallel")),
        name="diff_attn",
    )(q, k, v, jnp.full((1, LANES), lam, F32), jnp.tile(norm_g.reshape(1, HEAD_DIM), (1, LANES // HEAD_DIM)))


def _layernorm(z, g, b):
    z = z - jnp.mean(z, axis=-1, keepdims=True)
    return z * lax.rsqrt(jnp.mean(z * z, axis=-1, keepdims=True) + LN_EPS) * g + b


def _outproj_body(ya_ref, yb_ref, yc_ref, hf_ref, hb_ref, og_ref, ng_ref, h_ref, gl_ref, gc_ref, w_ref, g_ref,
                  b_ref, o_ref, *, ctx_tiles, alpha):
    is_ctx = pl.program_id(1) < ctx_tiles
    gate = jnp.where(is_ctx, gc_ref[0], gl_ref[0])
    mean_block = _head_block(1.0 / HEAD_DIM).astype(BF16)
    yd = hf_ref[0] + hb_ref[0]
    yd = yd - _head_sum(yd, mean_block)
    yd = yd * lax.rsqrt(_head_sum(yd * yd, mean_block) + LN_EPS) * ng_ref[...] * jax.nn.sigmoid(og_ref[0])
    y = None
    for m, ym in enumerate((ya_ref[0], yb_ref[0], yc_ref[0], yd)):
        part = jnp.dot(ym.astype(BF16), w_ref[m * GROUP_WIDTH:(m + 1) * GROUP_WIDTH, :],
                       preferred_element_type=F32)
        y = part if y is None else y + part
    o_ref[0] = _layernorm(alpha * h_ref[0] + gate * y, g_ref[...], b_ref[...])


def _outproj(ya, yb, yc, hf, hb, p_mlstm, mlstm_norm_g, h, gate_l, gate_c, w, ln_g, ln_b, tile0, ctx_tiles, alpha):
    b, t, d = h.shape
    gw = GROUP_WIDTH
    n_tiles = t // ROW_TILE - tile0
    yrow = lambda c: pl.BlockSpec((1, ROW_TILE, gw), lambda bi, i: (bi, tile0 + i, c))
    yb_row = pl.BlockSpec((1, ROW_TILE, gw), lambda bi, i: (bi, i, 0))
    hrow = pl.BlockSpec((1, ROW_TILE, d), lambda bi, i: (bi, tile0 + i, 0))
    lat = pl.BlockSpec((1, 1, d), lambda bi, i: (bi, 0, 0))
    ctx = pl.BlockSpec((1, 1, d), lambda bi, i: (0, 0, 0))
    vec = lambda n: pl.BlockSpec((1, n), lambda bi, i: (0, 0))
    return pl.pallas_call(
        functools.partial(_outproj_body, ctx_tiles=ctx_tiles, alpha=alpha),
        grid=(b, n_tiles),
        in_specs=[yrow(0), yb_row, yrow(0), yrow(0), yrow(0), yrow(3), vec(gw), hrow, lat, ctx,
                  pl.BlockSpec(w.shape, lambda bi, i: (0, 0)), vec(d), vec(d)],
        out_specs=pl.BlockSpec((1, ROW_TILE, d), lambda bi, i: (bi, i, 0)),
        out_shape=jax.ShapeDtypeStruct((b, n_tiles * ROW_TILE, d), F32),
        compiler_params=_cparams(("parallel", "parallel")),
        name="outproj_ln",
    )(ya, yb, yc, hf, hb, p_mlstm, mlstm_norm_g.reshape(1, gw), h, gate_l, gate_c, w, ln_g.reshape(1, d),
      ln_b.reshape(1, d))


FFN_CHUNKS = 2


def _ffn_body(x_ref, xp_ref, xn_ref, sl_ref, cl_ref, gl_ref, sc_ref, cc_ref, gc_ref,
              wu_ref, cw_ref, cb_ref, wd_ref, g_ref, b_ref, o_ref, *, ctx_tiles, n_tiles, alpha, d_ff):
    i = pl.program_id(1)
    is_ctx = i < ctx_tiles
    shift = jnp.where(is_ctx, sc_ref[0], sl_ref[0])
    scale = jnp.where(is_ctx, cc_ref[0], cl_ref[0])
    gate_mod = jnp.where(is_ctx, gc_ref[0], gl_ref[0])
    seg_start = jnp.logical_or(i == 0, i == ctx_tiles)
    seg_end = jnp.logical_or(i == ctx_tiles - 1, i == n_tiles - 1)

    x = x_ref[0]
    xm = (x * (1.0 + scale) + shift).astype(BF16)
    xp = (xp_ref[0] * (1.0 + scale) + shift).astype(BF16)
    xn = (xn_ref[0] * (1.0 + scale) + shift).astype(BF16)
    rows = x.shape[0]
    row_id = lax.broadcasted_iota(jnp.int32, (rows, 1), 0)
    ch = d_ff // FFN_CHUNKS
    f = jnp.zeros(x.shape, F32)
    for c in range(FFN_CHUNKS):
        wg = wu_ref[:, c * ch:(c + 1) * ch]
        wv = wu_ref[:, d_ff + c * ch:d_ff + (c + 1) * ch]
        gate = jnp.dot(xm, wg, preferred_element_type=F32)
        val = jnp.dot(xm, wv, preferred_element_type=F32)
        gp = jnp.dot(xp, wg, preferred_element_type=F32)[7:8, :]
        gn = jnp.dot(xn, wg, preferred_element_type=F32)[0:1, :]
        gp = jnp.where(seg_start, 0.0, gp)
        gn = jnp.where(seg_end, 0.0, gn)
        prev = jnp.where(row_id == 0, gp, pltpu.roll(gate, 1, 0))
        nxt = jnp.where(row_id == rows - 1, gn, pltpu.roll(gate, rows - 1, 0))
        cw = cw_ref[:, c * ch:(c + 1) * ch]
        conv = prev * cw[0:1, :] + gate * cw[1:2, :] + nxt * cw[2:3, :] + cb_ref[:, c * ch:(c + 1) * ch]
        act = (jax.nn.gelu(conv) * val).astype(BF16)
        f = f + jnp.dot(act, wd_ref[c * ch:(c + 1) * ch, :], preferred_element_type=F32)
    o_ref[0] = _layernorm(alpha * x + gate_mod * f, g_ref[...], b_ref[...])


def _ffn(h, mods_l, mods_c, w_up, conv_w, conv_b, w_down, ln_g, ln_b, ctx_tiles, alpha):
    b, t, d = h.shape
    d_ff = w_down.shape[0]
    n_tiles = t // ROW_TILE
    sub = ROW_TILE // 8
    n_sub = t // 8
    row = pl.BlockSpec((1, ROW_TILE, d), lambda bi, i: (bi, i, 0))
    prev = pl.BlockSpec((1, 8, d), lambda bi, i: (bi, jnp.maximum(i * sub - 1, 0), 0))
    nxt = pl.BlockSpec((1, 8, d), lambda bi, i: (bi, jnp.minimum((i + 1) * sub, n_sub - 1), 0))
    lat = pl.BlockSpec((1, 1, d), lambda bi, i: (bi, 0, 0))
    ctx = pl.BlockSpec((1, 1, d), lambda bi, i: (0, 0, 0))
    vec = pl.BlockSpec((1, d), lambda bi, i: (0, 0))
    full = lambda a: pl.BlockSpec(a.shape, lambda bi, i: (0,) * a.ndim)
    cb = conv_b.reshape(1, d_ff)
    return pl.pallas_call(
        functools.partial(_ffn_body, ctx_tiles=ctx_tiles, n_tiles=n_tiles, alpha=alpha, d_ff=d_ff),
        grid=(b, n_tiles),
        in_specs=[row, prev, nxt, lat, lat, lat, ctx, ctx, ctx,
                  full(w_up), full(conv_w), full(cb), full(w_down), vec, vec],
        out_specs=row,
        out_shape=jax.ShapeDtypeStruct((b, t, d), F32),
        compiler_params=_cparams(("parallel", "parallel")),
        name="ffn",
    )(h, h, h, *mods_l, *mods_c, w_up, conv_w, cb, w_down, ln_g.reshape(1, d), ln_b.reshape(1, d))


def _rwkv7(p, tc, mu, w0, w2, a0, a2, g2, k_k, k_a, r_k, lnx_g, lnx_b):
    b = p.shape[0]
    f = _rwkv_features(p, tc, mu, w0, w2, a0, a2, g2, k_k, k_a, r_k)
    col = RWKV_FEATS.index
    y = _scan("rwkv",
              [_to_scan(f, col("r"), col("r"), "k", tc), _to_scan(f, col("w0"), col("w1"), "k", tc),
               _to_scan(f, col("k0"), col("k1"), "k", tc), _to_scan(f, col("a"), col("a"), "k", tc),
               _to_scan(f, col("b0"), col("b1"), "k", tc)],
              _to_scan(f, col("v"), col("v"), "v", tc))
    return _from_scan(y, b, tc, "rwkv", f, (col("bonus"), col("g")), (lnx_g, lnx_b))


def _rope_tables(n):
    rows = n // GRID_W
    row = jnp.repeat(jnp.arange(rows, dtype=F32), GRID_W)
    col = jnp.tile(jnp.arange(GRID_W, dtype=F32), rows)
    n_freq = DIFF_QK_DIM // 4
    inv_freq = ROPE_BASE ** (-jnp.arange(n_freq, dtype=F32) / n_freq)
    ang = jnp.concatenate([row[:, None] * inv_freq, col[:, None] * inv_freq], axis=-1)
    return jnp.cos(ang), jnp.sin(ang)


def _rope(x, cos, sin, tc):
    b, t, gw = x.shape
    half = DIFF_QK_DIM // 2
    cos = jnp.concatenate([jnp.ones((tc, half), F32), cos], axis=0)[:, None, :]
    sin = jnp.concatenate([jnp.zeros((tc, half), F32), sin], axis=0)[:, None, :]
    x = x.reshape(b, t, gw // DIFF_QK_DIM, DIFF_QK_DIM)
    x1, x2 = x[..., :half], x[..., half:]
    return jnp.concatenate([x1 * cos - x2 * sin, x2 * cos + x1 * sin], axis=-1).reshape(b, t, gw)


def _diff_attn(p, tc, lam_vecs, norm_g, layer, need_ctx):
    b, t, _ = p.shape
    gw = GROUP_WIDTH
    lam_init = 0.8 - 0.6 * math.exp(-0.3 * layer)
    lv = lam_vecs.astype(F32)
    lam = jnp.exp(jnp.sum(lv[0] * lv[1])) - jnp.exp(jnp.sum(lv[2] * lv[3])) + lam_init
    cos, sin = _rope_tables(t - tc)
    q = _rope(p[..., :gw], cos, sin, tc)
    k = _rope(p[..., gw:2 * gw], cos, sin, tc)
    q_tile0 = 0 if need_ctx else tc // ATTN_TILE
    return _attention(q, k, p, 2 * gw // LANES, lam, norm_g, 1.0 - lam_init, q_tile0, tc)


def _hgrn2(p, tc, lb, norm_g):
    b = p.shape[0]
    f = _hgrn_features(p, lb)
    col = HGRN_FEATS.index
    o = _scan("gla",
              [_to_scan(f, col("q"), col("q"), "k", tc), _to_scan(f, col("w0"), col("w1"), "k", tc),
               _to_scan(f, col("k0"), col("k1"), "k", tc)],
              _to_scan(p, 3, 3, "v", tc))
    return _from_scan(o, b, tc, "gla", f, (col("gate"),), (jnp.tile(norm_g, N_HEADS),))


RWKV_COLS = 3 * GROUP_WIDTH + 2 * RWKV_DECAY_LORA + 2 * RWKV_ICLR_LORA + RWKV_GATE_LORA
DIFF_COLS = 3 * GROUP_WIDTH
HGRN_COLS = 5 * GROUP_WIDTH
MLSTM_COLS = 4 * GROUP_WIDTH + 4 * N_HEADS
SLAB_COLS = (RWKV_COLS, DIFF_COLS, HGRN_COLS, MLSTM_COLS)
SLAB_PAD = (5 * GROUP_WIDTH, DIFF_COLS, HGRN_COLS, 4 * GROUP_WIDTH + LANES)


def _split_cols(w):
    parts, off = [], 0
    for n, n_pad in zip(SLAB_COLS, SLAB_PAD):
        parts.append(jnp.pad(w[:, off:off + n], ((0, 0), (0, n_pad - n))).astype(BF16))
        off += n
    return parts


def kernel(x, c, ctx, c_ctx, ada_w, ada_b, w_in, rwkv_mu, rwkv_w0, rwkv_w2, rwkv_a0, rwkv_a2, rwkv_g2, rwkv_k_k, rwkv_k_a, rwkv_r_k, rwkv_lnx_g, rwkv_lnx_b, diff_lambda, diff_norm_g, hgrn_lb_logits, hgrn_norm_g, mlstm_conv_w, mlstm_conv_b, mlstm_gate_b, mlstm_norm_g, w_out, ffn_w_up, ffn_conv_w, ffn_conv_b, ffn_w_down, ln_g, ln_b):
    depth = w_in.shape[0]
    b, seq, d = x.shape
    tc = ctx.shape[1]
    assert tc == ROW_TILE == ATTN_TILE and seq % ROW_TILE == 0 and 2 * b * N_HEADS <= LANES
    alpha = (2.0 * depth) ** 0.25

    lb_w = jax.nn.softmax(hgrn_lb_logits.astype(F32), axis=0)
    lower_bounds = jnp.cumsum(lb_w, axis=0) - lb_w[0]

    cond = jnp.zeros((8, d), F32).at[:b].set(c).at[b].set(c_ctx)
    h = jnp.concatenate([ctx, x], axis=1)
    for layer in range(depth):
        last = layer == depth - 1
        mod = _adaln(cond, ada_w[layer], ada_b[layer]).reshape(8, 6, d)
        mods_l = [mod[:b, j][:, None, :] for j in range(6)]
        mods_c = [mod[b:b + 1, j][:, None, :] for j in range(6)]

        p_rwkv, p_diff, p_hgrn, p_mlstm = _inproj(h, mods_l[0], mods_l[1], mods_c[0], mods_c[1],
                                                  _split_cols(w_in[layer]), tc // ROW_TILE)
        y_a = _rwkv7(p_rwkv, tc, rwkv_mu[layer], rwkv_w0[layer], rwkv_w2[layer], rwkv_a0[layer], rwkv_a2[layer],
                     rwkv_g2[layer], rwkv_k_k[layer], rwkv_k_a[layer], rwkv_r_k[layer], rwkv_lnx_g[layer],
                     rwkv_lnx_b[layer])
        y_b = _diff_attn(p_diff, tc, diff_lambda[layer], diff_norm_g[layer], layer, not last)
        y_c = _hgrn2(p_hgrn, tc, lower_bounds[layer], hgrn_norm_g[layer])
        h_f, h_b = (_mlstm_chunked(p_mlstm, mlstm_conv_w[layer], mlstm_conv_b[layer], mlstm_gate_b[layer], rev, tc)
                    for rev in (False, True))

        tile0, ctx_tiles = (tc // ROW_TILE, 0) if last else (0, tc // ROW_TILE)
        h = _outproj(y_a, y_b, y_c, h_f, h_b, p_mlstm, mlstm_norm_g[layer], h, mods_l[2], mods_c[2],
                     w_out[layer].astype(BF16), ln_g[layer, 0], ln_b[layer, 0], tile0, ctx_tiles, alpha)
        h = _ffn(h, mods_l[3:6], mods_c[3:6], ffn_w_up[layer].astype(BF16), ffn_conv_w[layer], ffn_conv_b[layer],
                 ffn_w_down[layer].astype(BF16), ln_g[layer, 1], ln_b[layer, 1], ctx_tiles, alpha)
    return h
```

```python
import functools
import math

import jax
import jax.numpy as jnp
import numpy as np
from jax import lax
from jax.experimental import pallas as pl
from jax.experimental.pallas import tpu as pltpu

F32 = jnp.float32
BF16 = jnp.bfloat16
HIGHEST = lax.Precision.HIGHEST

HEAD_DIM = 64
N_HEADS = 4
GROUP_WIDTH = N_HEADS * HEAD_DIM
DIFF_QK_DIM = HEAD_DIM // 2
GRID_W = 64
ROPE_BASE = 10000.0
RWKV_DECAY_LORA = 64
RWKV_ICLR_LORA = 64
RWKV_GATE_LORA = 160
RWKV_LNX_EPS = 64e-5
LN_EPS = 1e-5
LB_FLOOR = 1e-30
MAX_NEG_LOG_STAB = 60.0

LANES = 128
ROW_TILE = 256
ATTN_TILE = 256
TIME_BLOCK = 128
SCAN_TB = 64
VMEM_LIMIT = 56 * 1024 * 1024


def _cparams(sem):
    return pltpu.CompilerParams(dimension_semantics=sem, vmem_limit_bytes=VMEM_LIMIT)


def _rev_block(i, n_ctx, n_all):
    return jnp.where(i < n_ctx, n_ctx - 1 - i, n_ctx + n_all - 1 - i)


def _split3(x):
    hi = x.astype(BF16)
    r1 = x - hi.astype(F32)
    mid = r1.astype(BF16)
    return hi, mid, (r1 - mid.astype(F32)).astype(BF16)


def _dot01_left(a01, x):
    a = a01.astype(BF16)
    p = [jnp.dot(a, part, preferred_element_type=F32) for part in _split3(x)]
    return (p[0] + p[1]) + p[2]


def _dot01_right(x, b01):
    b = b01.astype(BF16)
    p = [jnp.dot(part, b, preferred_element_type=F32) for part in _split3(x)]
    return (p[0] + p[1]) + p[2]


def _flip_rows(x):
    n = x.shape[0]
    r = lax.broadcasted_iota(jnp.int32, (n, n), 0)
    c = lax.broadcasted_iota(jnp.int32, (n, n), 1)
    perm = jnp.where(r + c == n - 1, 1.0, 0.0).astype(F32)
    return jnp.dot(perm, x, precision=HIGHEST, preferred_element_type=F32)


def _adaln_body(c_ref, w_ref, b_ref, o_ref):
    x = c_ref[...]
    x = (x * jax.nn.sigmoid(x)).astype(BF16)
    o_ref[...] = jnp.dot(x, w_ref[...].astype(BF16), preferred_element_type=F32) + b_ref[...]


def _adaln(cond, w, b):
    m, d = cond.shape
    n = w.shape[1]
    tn = 1536
    return pl.pallas_call(
        _adaln_body,
        grid=(n // tn,),
        in_specs=[pl.BlockSpec((m, d), lambda j: (0, 0)),
                  pl.BlockSpec((d, tn), lambda j: (0, j)),
                  pl.BlockSpec((1, tn), lambda j: (0, j))],
        out_specs=pl.BlockSpec((m, tn), lambda j: (0, j)),
        out_shape=jax.ShapeDtypeStruct((m, n), F32),
        compiler_params=_cparams(("arbitrary",)),
        name="adaln",
    )(cond, w, b.reshape(1, n))


def _inproj_body(x_ref, sl_ref, cl_ref, sc_ref, cc_ref, *refs, ctx_tiles):
    n = len(refs) // 2
    is_ctx = pl.program_id(1) < ctx_tiles
    shift = jnp.where(is_ctx, sc_ref[0], sl_ref[0])
    scale = jnp.where(is_ctx, cc_ref[0], cl_ref[0])
    xm = (x_ref[0] * (1.0 + scale) + shift).astype(BF16)
    for w_ref, o_ref in zip(refs[:n], refs[n:]):
        o_ref[0] = jnp.dot(xm, w_ref[...], preferred_element_type=F32)


def _inproj(h, shift_l, scale_l, shift_c, scale_c, ws, ctx_tiles):
    b, t, d = h.shape
    row = lambda n: pl.BlockSpec((1, ROW_TILE, n), lambda bi, i: (bi, i, 0))
    lat = pl.BlockSpec((1, 1, d), lambda bi, i: (bi, 0, 0))
    ctx = pl.BlockSpec((1, 1, d), lambda bi, i: (0, 0, 0))
    return pl.pallas_call(
        functools.partial(_inproj_body, ctx_tiles=ctx_tiles),
        grid=(b, t // ROW_TILE),
        in_specs=[row(d), lat, lat, ctx, ctx] + [pl.BlockSpec(w.shape, lambda bi, i: (0, 0)) for w in ws],
        out_specs=[row(w.shape[1]) for w in ws],
        out_shape=[jax.ShapeDtypeStruct((b, t, w.shape[1]), F32) for w in ws],
        compiler_params=_cparams(("parallel", "parallel")),
        name="inproj",
    )(h, shift_l, scale_l, shift_c, scale_c, *ws)


def _softplus(x):
    return jnp.maximum(x, 0.0) + jnp.log(1.0 + jnp.exp(-jnp.abs(x)))


def _head_block(scale=1.0):
    r = lax.broadcasted_iota(jnp.int32, (GROUP_WIDTH, GROUP_WIDTH), 0) // HEAD_DIM
    c = lax.broadcasted_iota(jnp.int32, (GROUP_WIDTH, GROUP_WIDTH), 1) // HEAD_DIM
    return jnp.where(r == c, scale, 0.0).astype(F32)


def _head_sum(x, block_bf16):
    hi = x.astype(BF16)
    lo = (x - hi.astype(F32)).astype(BF16)
    return (jnp.dot(hi, block_bf16, preferred_element_type=F32)
            + jnp.dot(lo, block_bf16, preferred_element_type=F32))


def _shifted_rows(x, prev8, next8, seg_start, seg_end):
    rows = x.shape[0]
    row_id = lax.broadcasted_iota(jnp.int32, (rows, 1), 0)
    prev_row = jnp.where(seg_start, 0.0, prev8[7:8, :])
    next_row = jnp.where(seg_end, 0.0, next8[0:1, :])
    prev = jnp.where(row_id == 0, prev_row, pltpu.roll(x, 1, 0))
    nxt = jnp.where(row_id == rows - 1, next_row, pltpu.roll(x, rows - 1, 0))
    return prev, nxt


RWKV_FEATS = ("r", "v", "w0", "w1", "k0", "k1", "a", "b0", "b1", "g", "bonus")


def _rwkv_feat_body(p_ref, pp_ref, pn_ref, mu_ref, w0_ref, w2_ref, a0_ref, a2_ref, g2_ref, kk_ref, ka_ref,
                    rk_ref, o_ref, *, ctx_tiles, n_tiles):
    i = pl.program_id(1)
    seg_start = jnp.logical_or(i == 0, i == ctx_tiles)
    seg_end = jnp.logical_or(i == ctx_tiles - 1, i == n_tiles - 1)
    gw = GROUP_WIDTH
    p = p_ref[0]
    prev, nxt = _shifted_rows(p, pp_ref[0], pn_ref[0], seg_start, seg_end)
    p = p + (0.5 * (prev + nxt) - p) * mu_ref[...]
    r, k, v = p[:, :gw], p[:, gw:2 * gw], p[:, 2 * gw:3 * gw]
    wd, ad, gd = p[:, 3 * gw:3 * gw + LANES], p[:, 3 * gw + LANES:4 * gw], p[:, 4 * gw:5 * gw]
    lora_w = jnp.dot(jnp.tanh(wd).astype(BF16), w2_ref[...], preferred_element_type=F32)
    lora_a = jnp.dot(ad.astype(BF16), a2_ref[...], preferred_element_type=F32)
    decay = jnp.exp(-jnp.exp(-_softplus(-(w0_ref[...] + lora_w)) - 0.5))
    a = jax.nn.sigmoid(a0_ref[...] + lora_a)
    g = jnp.dot(jax.nn.sigmoid(gd).astype(BF16), g2_ref[...], preferred_element_type=F32)
    block = _head_block().astype(BF16)
    kk = k * kk_ref[...]
    kk = kk / jnp.maximum(jnp.sqrt(_head_sum(kk * kk, block)), 1e-12)
    k0 = k * (1.0 + (a[:, :gw] - 1.0) * ka_ref[...])
    k1 = k * (1.0 + (a[:, gw:] - 1.0) * ka_ref[...])
    bonus = _head_sum(r * (k0 + k1) * rk_ref[...], block) * v
    feats = dict(r=r, v=v, w0=decay[:, :gw], w1=decay[:, gw:], k0=k0, k1=k1, a=-kk, b0=kk * a[:, :gw],
                 b1=kk * a[:, gw:], g=g, bonus=bonus)
    for j, name in enumerate(RWKV_FEATS):
        o_ref[0, :, j * gw:(j + 1) * gw] = feats[name]


def _block_diag2(w):
    z = jnp.zeros_like(w[0])
    return jnp.concatenate([jnp.concatenate([w[0], z], axis=1), jnp.concatenate([z, w[1]], axis=1)], axis=0)


def _rwkv_features(p, tc, mu, w0, w2, a0, a2, g2, k_k, k_a, r_k):
    b, t, cols = p.shape
    gw = GROUP_WIDTH
    n_tiles, sub, n_sub = t // ROW_TILE, ROW_TILE // 8, t // 8
    row = lambda w: pl.BlockSpec((1, ROW_TILE, w), lambda bi, i: (bi, i, 0))
    prev = pl.BlockSpec((1, 8, cols), lambda bi, i: (bi, jnp.maximum(i * sub - 1, 0), 0))
    nxt = pl.BlockSpec((1, 8, cols), lambda bi, i: (bi, jnp.minimum((i + 1) * sub, n_sub - 1), 0))
    full = lambda a: pl.BlockSpec(a.shape, lambda bi, i: (0,) * a.ndim)
    vec = lambda a: a.reshape(1, -1).astype(F32)
    g2_pad = jnp.pad(g2, ((0, gw - g2.shape[0]), (0, 0))).astype(BF16)
    args = [jnp.pad(vec(mu), ((0, 0), (0, cols - mu.shape[0]))), vec(w0), _block_diag2(w2).astype(BF16), vec(a0),
            _block_diag2(a2).astype(BF16), g2_pad, vec(k_k), vec(k_a), vec(r_k)]
    return pl.pallas_call(
        functools.partial(_rwkv_feat_body, ctx_tiles=tc // ROW_TILE, n_tiles=n_tiles),
        grid=(b, n_tiles),
        in_specs=[row(cols), prev, nxt] + [full(a) for a in args],
        out_specs=row(len(RWKV_FEATS) * gw),
        out_shape=jax.ShapeDtypeStruct((b, t, len(RWKV_FEATS) * gw), F32),
        compiler_params=_cparams(("parallel", "parallel")),
        name="rwkv_feat",
    )(p, p, p, *args)


HGRN_FEATS = ("q", "w0", "w1", "k0", "k1", "gate")


def _hgrn_feat_body(p_ref, llb_ref, l1m_ref, oml_ref, o_ref):
    gw = GROUP_WIDTH
    p = p_ref[0]
    q, f_f, f_b, g = p[:, :gw], p[:, gw:2 * gw], p[:, 2 * gw:3 * gw], p[:, 4 * gw:5 * gw]
    feats = dict(q=q * jax.nn.sigmoid(q), gate=g * jax.nn.sigmoid(g))
    for d, f in enumerate((f_f, f_b)):
        x = llb_ref[...]
        y = l1m_ref[...] - _softplus(-f)
        log_f = jnp.maximum(x, y) + jnp.log(1.0 + jnp.exp(-jnp.abs(x - y)))
        feats["w%d" % d] = jnp.exp(log_f)
        feats["k%d" % d] = oml_ref[...] * jax.nn.sigmoid(-f)
    for j, name in enumerate(HGRN_FEATS):
        o_ref[0, :, j * gw:(j + 1) * gw] = feats[name]


def _hgrn_features(p, lb):
    b, t, cols = p.shape
    gw = GROUP_WIDTH
    row = lambda w: pl.BlockSpec((1, ROW_TILE, w), lambda bi, i: (bi, i, 0))
    vecs = [jnp.log(jnp.maximum(lb, LB_FLOOR)).reshape(1, gw), jnp.log1p(-lb).reshape(1, gw), (1.0 - lb).reshape(1, gw)]
    return pl.pallas_call(
        _hgrn_feat_body,
        grid=(b, t // ROW_TILE),
        in_specs=[row(cols)] + [pl.BlockSpec((1, gw), lambda bi, i: (0, 0))] * 3,
        out_specs=row(len(HGRN_FEATS) * gw),
        out_shape=jax.ShapeDtypeStruct((b, t, len(HGRN_FEATS) * gw), F32),
        compiler_params=_cparams(("parallel", "parallel")),
        name="hgrn_feat",
    )(p, *vecs)


def _to_scan_body(x0_ref, x1_ref, o_ref, r_ref, *, kind, nb):
    n_scan = 2 * nb * N_HEADS
    rep = LANES // n_scan
    for b in range(nb):
        r_ref[pl.ds(b * GROUP_WIDTH, GROUP_WIDTH), :] = x0_ref[b].T
        r_ref[pl.ds((nb + b) * GROUP_WIDTH, GROUP_WIDTH), :] = _flip_rows(x1_ref[b]).T
    if kind == "k":
        for k in range(HEAD_DIM):
            rows = r_ref[pl.ds(k, n_scan, stride=HEAD_DIM), :]
            o_ref[k] = jnp.concatenate([rows] * rep, axis=0).T
    else:
        for vh in range(HEAD_DIM // rep):
            rows = [r_ref[pl.ds(vh * rep + vl, n_scan, stride=HEAD_DIM), :] for vl in range(rep)]
            o_ref[:, vh, :] = jnp.concatenate(rows, axis=0).T


def _to_scan(x, col0, col1, kind, tc):
    nb, t, _ = x.shape
    gw = GROUP_WIDTH
    x0 = x1 = x
    n_all, n_ctx = t // TIME_BLOCK, tc // TIME_BLOCK
    n_scan = 2 * nb * N_HEADS
    rep = LANES // n_scan
    fwd = pl.BlockSpec((nb, TIME_BLOCK, gw), lambda i: (0, i, col0))
    bwd = pl.BlockSpec((nb, TIME_BLOCK, gw), lambda i: (0, _rev_block(i, n_ctx, n_all), col1))
    if kind == "k":
        out_shape = (HEAD_DIM, t, LANES)
        out_spec = pl.BlockSpec((HEAD_DIM, TIME_BLOCK, LANES), lambda i: (0, i, 0))
    else:
        out_shape = (t, HEAD_DIM // rep, LANES)
        out_spec = pl.BlockSpec((TIME_BLOCK, HEAD_DIM // rep, LANES), lambda i: (i, 0, 0))
    return pl.pallas_call(
        functools.partial(_to_scan_body, kind=kind, nb=nb),
        grid=(n_all,),
        in_specs=[fwd, bwd],
        out_specs=out_spec,
        out_shape=jax.ShapeDtypeStruct(out_shape, F32),
        scratch_shapes=[pltpu.VMEM((n_scan * HEAD_DIM, TIME_BLOCK), F32)],
        compiler_params=_cparams(("parallel",)),
        name="to_scan_" + kind,
    )(x0, x1)


def _from_scan_body(yf_ref, yb_ref, *rest, nb, readout):
    if readout == "rwkv":
        bonus_ref, g_ref, ng_ref, nb_ref, o_ref, r_ref = rest
    else:
        gate_ref, ng_ref, o_ref, r_ref = rest
    n_scan = 2 * nb * N_HEADS
    rep = LANES // n_scan
    for vh in range(HEAD_DIM // rep):
        r_ref[0, pl.ds(vh * LANES, LANES), :] = yf_ref[:, vh, :].T
        r_ref[1, pl.ds(vh * LANES, LANES), :] = yb_ref[:, vh, :].T
    mean_block = _head_block(1.0 / HEAD_DIM).astype(BF16)
    for b in range(nb):
        slab = lambda d: jnp.concatenate(
            [r_ref[d, pl.ds((d * nb + b) * N_HEADS + h, HEAD_DIM, stride=n_scan), :] for h in range(N_HEADS)],
            axis=0).T
        y = slab(0) + _flip_rows(slab(1))
        if readout == "rwkv":
            y = y - _head_sum(y, mean_block)
            y = y * lax.rsqrt(_head_sum(y * y, mean_block) + RWKV_LNX_EPS) * ng_ref[...] + nb_ref[...]
            o_ref[b] = (y + bonus_ref[b]) * g_ref[b]
        else:
            y = y * lax.rsqrt(_head_sum(y * y, mean_block) + LN_EPS) * ng_ref[...]
            o_ref[b] = y * gate_ref[b]


def _from_scan(y, nb, tc, readout, feats, cols, vecs):
    t, n_vh, _ = y.shape
    gw = GROUP_WIDTH
    n_all, n_ctx = t // TIME_BLOCK, tc // TIME_BLOCK
    fwd = pl.BlockSpec((TIME_BLOCK, n_vh, LANES), lambda i: (i, 0, 0))
    bwd = pl.BlockSpec((TIME_BLOCK, n_vh, LANES), lambda i: (_rev_block(i, n_ctx, n_all), 0, 0))
    tok = lambda c: pl.BlockSpec((nb, TIME_BLOCK, gw), lambda i: (0, i, c))
    vec = pl.BlockSpec((1, gw), lambda i: (0, 0))
    return pl.pallas_call(
        functools.partial(_from_scan_body, nb=nb, readout=readout),
        grid=(n_all,),
        in_specs=[fwd, bwd] + [tok(c) for c in cols] + [vec] * len(vecs),
        out_specs=tok(0),
        out_shape=jax.ShapeDtypeStruct((nb, t, gw), F32),
        scratch_shapes=[pltpu.VMEM((2, n_vh * LANES, TIME_BLOCK), F32)],
        compiler_params=_cparams(("parallel",)),
        name="from_scan_" + readout,
    )(y, y, *([feats] * len(cols)), *[v.reshape(1, gw) for v in vecs])


N_ACC = 4


def _acc_add(acc, i, x):
    acc[i % N_ACC] = x if acc[i % N_ACC] is None else acc[i % N_ACC] + x


def _acc_total(acc):
    return (acc[0] + acc[1]) + (acc[2] + acc[3])


def _scan_body(*refs, mode, tb, n_vh):
    if mode == "rwkv":
        r_ref, w_ref, k_ref, a_ref, b_ref, v_ref, y_ref, s_ref = refs
    else:
        r_ref, w_ref, k_ref, v_ref, y_ref, s_ref = refs

    @pl.when(pl.program_id(0) == 0)
    def _():
        s_ref[...] = jnp.zeros_like(s_ref)

    row = lambda ref, k, j: ref[k, pl.ds(j, 1), :]
    wide = lambda x: jnp.broadcast_to(x, (n_vh, LANES))

    def rwkv_step(j, sa):
        v = v_ref[j]
        j_next = jnp.minimum(j + 1, tb - 1)
        ys, sas = [None] * N_ACC, [None] * N_ACC
        for k in range(HEAD_DIM):
            s = s_ref[k] * wide(row(w_ref, k, j)) + sa * wide(row(b_ref, k, j)) + v * wide(row(k_ref, k, j))
            s_ref[k] = s
            _acc_add(ys, k, s * wide(row(r_ref, k, j)))
            _acc_add(sas, k, s * wide(row(a_ref, k, j_next)))
        y_ref[j] = _acc_total(ys)
        return _acc_total(sas)

    def gla_step(j, carry):
        v = v_ref[j]
        ys = [None] * N_ACC
        for k in range(HEAD_DIM):
            s = s_ref[k] * wide(row(w_ref, k, j)) + v * wide(row(k_ref, k, j))
            s_ref[k] = s
            _acc_add(ys, k, s * wide(row(r_ref, k, j)))
        y_ref[j] = _acc_total(ys)
        return carry

    if mode == "rwkv":
        sa0 = [None] * N_ACC
        for k in range(HEAD_DIM):
            _acc_add(sa0, k, s_ref[k] * wide(row(a_ref, k, 0)))
        lax.fori_loop(0, tb, rwkv_step, _acc_total(sa0), unroll=8)
    else:
        lax.fori_loop(0, tb, gla_step, 0, unroll=8)


def _scan(mode, k_inputs, v_in):
    t, n_vh, _ = v_in.shape
    tb = SCAN_TB
    big = pl.BlockSpec((HEAD_DIM, tb, LANES), lambda i: (0, i, 0))
    small = pl.BlockSpec((tb, n_vh, LANES), lambda i: (i, 0, 0))
    return pl.pallas_call(
        functools.partial(_scan_body, mode=mode, tb=tb, n_vh=n_vh),
        grid=(t // tb,),
        in_specs=[big] * len(k_inputs) + [small],
        out_specs=small,
        out_shape=jax.ShapeDtypeStruct((t, n_vh, LANES), F32),
        scratch_shapes=[pltpu.VMEM((HEAD_DIM, n_vh, LANES), F32)],
        compiler_params=_cparams(("arbitrary",)),
        name="scan_" + mode,
    )(*k_inputs, v_in)


MLSTM_CHUNK = 64


def _prefix_max(x, reverse):
    n = x.shape[0]
    row = lax.broadcasted_iota(jnp.int32, x.shape, 0)
    sh = 1
    while sh < n:
        if reverse:
            x = jnp.where(row < n - sh, jnp.maximum(x, pltpu.roll(x, n - sh, 0)), x)
        else:
            x = jnp.where(row >= sh, jnp.maximum(x, pltpu.roll(x, sh, 0)), x)
        sh *= 2
    return x


def _mlstm_body(qk_ref, qkp_ref, qkn_ref, v_ref, gt_ref, cw_ref, cb_ref, sel_ref, gb_ref, o_ref, st_ref, m_ref, *,
                reverse, n_sub, n_ctx, n_blocks):
    c, gw = MLSTM_CHUNK, GROUP_WIDTH
    i = pl.program_id(1)

    @pl.when(i == 0)
    def _():
        st_ref[...] = jnp.zeros_like(st_ref)
        m_ref[...] = jnp.zeros_like(m_ref)

    blk = _rev_block(i, n_ctx, n_blocks) if reverse else i
    seg_start = jnp.logical_or(blk == 0, blk == n_ctx)
    seg_end = jnp.logical_or(blk == n_ctx - 1, blk == n_blocks - 1)
    qk = qk_ref[0]
    prev, nxt = _shifted_rows(qk, qkp_ref[0], qkn_ref[0], seg_start, seg_end)
    qk = prev * cw_ref[0:1, :] + qk * cw_ref[1:2, :] + nxt * cw_ref[2:3, :] + cb_ref[...]
    qk = qk * jax.nn.sigmoid(qk)
    q_all, k_all = qk[:, :gw], qk[:, gw:] * HEAD_DIM ** -0.5
    v_all = v_ref[0]
    gt = gt_ref[0]
    ig_all = _dot01_right(gt, sel_ref[0]) + gb_ref[0:1, :]
    fg_all = -_softplus(-(_dot01_right(gt, sel_ref[1]) + gb_ref[1:2, :]))

    tt = lax.broadcasted_iota(jnp.int32, (c, c), 0)
    ss = lax.broadcasted_iota(jnp.int32, (c, c), 1)
    tri = jnp.where((ss >= tt) if reverse else (ss <= tt), 1.0, 0.0).astype(F32)
    row = lax.broadcasted_iota(jnp.int32, (c, gw), 0)
    s_of_lane = lax.broadcasted_iota(jnp.int32, (c, gw), 1) % c
    causal = (s_of_lane >= row) if reverse else (s_of_lane <= row)
    diag = s_of_lane == row
    block = _head_block()
    block2 = jnp.concatenate([block, block], axis=1)
    ones8 = jnp.ones((8, c), F32)
    nt = (((1,), (1,)), ((), ()))
    last = 0 if reverse else c - 1
    for u in (range(n_sub - 1, -1, -1) if reverse else range(n_sub)):
        sl = slice(u * c, (u + 1) * c)
        q, k, v, ig, fg = q_all[sl], k_all[sl], v_all[sl], ig_all[sl], fg_all[sl]
        b = _dot01_left(tri, fg)
        g = ig - b
        m_prev = m_ref[0:1, :]
        m_t = b + jnp.maximum(m_prev, _prefix_max(g, reverse))
        w_inter = jnp.exp(b + m_prev - m_t)
        g_row = _dot01_left(ones8, jnp.where(diag, g, 0.0))[0:1]
        qb = q.astype(BF16)
        k_bd = (jnp.concatenate([k] * N_HEADS, axis=0) * block).astype(BF16)
        scores = lax.dot_general(qb, k_bd, nt, preferred_element_type=F32)
        w = jnp.where(causal, jnp.exp((b - m_t) + g_row), 0.0) * scores
        v_bd = jnp.concatenate([jnp.concatenate([v] * N_HEADS, axis=0) * block, block], axis=1).astype(BF16)
        intra = jnp.dot(w.astype(BF16), v_bd, preferred_element_type=F32)
        inter = jnp.dot(qb, st_ref[...].astype(BF16), preferred_element_type=F32)
        num = w_inter * inter[:, :gw] + intra[:, :gw]
        den = w_inter * inter[:, gw:] + intra[:, gw:]
        floor = jnp.exp(jnp.minimum(-m_t, MAX_NEG_LOG_STAB))
        o_ref[0, pl.ds(u * c, c), :] = num / jnp.maximum(jnp.abs(den), floor)
        m_new = m_t[last:last + 1]
        b_end = b[last:last + 1]
        kw = k * jnp.exp(b_end - b + ig - m_new)
        decay = jnp.exp(b_end + m_prev - m_new)
        v_one = jnp.concatenate([v, jnp.ones_like(v)], axis=1).astype(BF16)
        upd = jnp.dot(kw.T.astype(BF16), v_one, preferred_element_type=F32)
        st_ref[...] = st_ref[...] * jnp.concatenate([decay, decay], axis=1) + upd * block2
        m_ref[0:1, :] = m_new


def _mlstm_chunked(p, conv_w, conv_b, gate_b, reverse, tc):
    b, t, _ = p.shape
    gw = GROUP_WIDTH
    n_blocks, n_ctx = t // ROW_TILE, tc // ROW_TILE
    sub, n_sub8 = ROW_TILE // 8, t // 8
    blk = (lambda i: _rev_block(i, n_ctx, n_blocks)) if reverse else (lambda i: i)
    d = 1 if reverse else 0
    lane_head = np.arange(gw) // HEAD_DIM
    sel = np.zeros((2, LANES, gw), np.float32)
    for j, base in enumerate((d * N_HEADS, (2 + d) * N_HEADS)):
        sel[j, base + lane_head, np.arange(gw)] = 1.0
    gate_bias = jnp.stack([jnp.repeat(gate_b[d], HEAD_DIM), jnp.repeat(gate_b[2 + d], HEAD_DIM)], axis=0)
    full = lambda a: pl.BlockSpec(a.shape, lambda bi, i: (0,) * a.ndim)
    cb = conv_b.reshape(1, 2 * gw)
    return pl.pallas_call(
        functools.partial(_mlstm_body, reverse=reverse, n_sub=ROW_TILE // MLSTM_CHUNK, n_ctx=n_ctx,
                          n_blocks=n_blocks),
        grid=(b, n_blocks),
        in_specs=[pl.BlockSpec((1, ROW_TILE, 2 * gw), lambda bi, i: (bi, blk(i), 0)),
                  pl.BlockSpec((1, 8, 2 * gw), lambda bi, i: (bi, jnp.maximum(blk(i) * sub - 1, 0), 0)),
                  pl.BlockSpec((1, 8, 2 * gw), lambda bi, i: (bi, jnp.minimum((blk(i) + 1) * sub, n_sub8 - 1), 0)),
                  pl.BlockSpec((1, ROW_TILE, gw), lambda bi, i: (bi, blk(i), 2)),
                  pl.BlockSpec((1, ROW_TILE, LANES), lambda bi, i: (bi, blk(i), 4 * gw // LANES)),
                  full(conv_w), full(cb), full(sel), full(gate_bias)],
        out_specs=pl.BlockSpec((1, ROW_TILE, gw), lambda bi, i: (bi, blk(i), 0)),
        out_shape=jax.ShapeDtypeStruct((b, t, gw), F32),
        scratch_shapes=[pltpu.VMEM((gw, 2 * gw), F32), pltpu.VMEM((8, gw), F32)],
        compiler_params=_cparams(("parallel", "arbitrary")),
        name="mlstm_chunk",
    )(p, p, p, p, p, conv_w, cb, jnp.asarray(sel), gate_bias)


def _attn_body(q_ref, k_ref, v_ref, lam_ref, g_ref, o_ref, *, out_scale, ctx_tiles, n_ctx_keys):
    def attend(n_keys):
        q = q_ref[0] * (DIFF_QK_DIM ** -0.5 * math.log2(math.e))
        k = k_ref[0, :n_keys, :].astype(BF16)
        v = v_ref[0, :n_keys, :].astype(BF16)
        lane = lax.broadcasted_iota(jnp.int32, q.shape, 1)
        comp = lane // DIFF_QK_DIM
        nt = (((1,), (1,)), ((), ()))
        n_maps = LANES // DIFF_QK_DIM
        ss = [lax.dot_general(jnp.where(comp == c, q, 0.0).astype(BF16), k, nt, preferred_element_type=F32)
              for c in range(n_maps)]
        es = [jnp.exp2(s - jnp.max(s, axis=-1, keepdims=True)) for s in ss]
        ls = [jnp.sum(e, axis=-1, keepdims=True) for e in es]
        ys = []
        for hh in range(LANES // HEAD_DIM):
            e0, e1, l0, l1 = es[2 * hh], es[2 * hh + 1], ls[2 * hh], ls[2 * hh + 1]
            pr = (e0 - e1 * (lam_ref[0:1, 0:1] * l0 / l1)).astype(BF16)
            ys.append(jnp.dot(pr, v, preferred_element_type=F32) * (1.0 / l0))
        first = lane < HEAD_DIM
        y = jnp.where(first, ys[0], ys[1])
        ysq = y * y
        ms = jnp.where(first, jnp.sum(jnp.where(first, ysq, 0.0), axis=-1, keepdims=True),
                       jnp.sum(jnp.where(first, 0.0, ysq), axis=-1, keepdims=True)) * (1.0 / HEAD_DIM)
        o_ref[0] = y * lax.rsqrt(ms + LN_EPS) * g_ref[...] * out_scale

    if ctx_tiles == 0:
        attend(k_ref.shape[1])
    else:
        is_ctx = pl.program_id(2) < ctx_tiles
        pl.when(is_ctx)(lambda: attend(n_ctx_keys))
        pl.when(jnp.logical_not(is_ctx))(lambda: attend(k_ref.shape[1]))


def _attention(q, k, v, v_col0, lam, norm_g, out_scale, q_tile0, tc):
    b, t, gw = q.shape
    n_tiles = t // ATTN_TILE - q_tile0
    ctx_tiles = max(tc // ATTN_TILE - q_tile0, 0)
    qspec = pl.BlockSpec((1, ATTN_TILE, LANES), lambda bi, pi, i: (bi, q_tile0 + i, pi))
    kspec = pl.BlockSpec((1, t, LANES), lambda bi, pi, i: (bi, 0, pi))
    vspec = pl.BlockSpec((1, t, LANES), lambda bi, pi, i: (bi, 0, v_col0 + pi))
    vec = pl.BlockSpec((1, LANES), lambda bi, pi, i: (0, 0))
    return pl.pallas_call(
        functools.partial(_attn_body, out_scale=out_scale, ctx_tiles=ctx_tiles, n_ctx_keys=tc),
        grid=(b, gw // LANES, n_tiles),
        in_specs=[qspec, kspec, vspec, vec, vec],
        out_specs=pl.BlockSpec((1, ATTN_TILE, LANES), lambda bi, pi, i: (bi, i, pi)),
        out_shape=jax.ShapeDtypeStruct((b, n_tiles * ATTN_TILE, gw), F32),
        compiler_params=_cparams(("parallel", "parallel", "parallel")),
        name="diff_attn",
    )(q, k, v, jnp.full((1, LANES), lam, F32), jnp.tile(norm_g.reshape(1, HEAD_DIM), (1, LANES // HEAD_DIM)))


def _layernorm(z, g, b):
    z = z - jnp.mean(z, axis=-1, keepdims=True)
    return z * lax.rsqrt(jnp.mean(z * z, axis=-1, keepdims=True) + LN_EPS) * g + b


def _outproj_body(ya_ref, yb_ref, yc_ref, hf_ref, hb_ref, og_ref, ng_ref, h_ref, gl_ref, gc_ref, w_ref, g_ref,
                  b_ref, o_ref, *, ctx_tiles, alpha):
    is_ctx = pl.program_id(1) < ctx_tiles
    gate = jnp.where(is_ctx, gc_ref[0], gl_ref[0])
    mean_block = _head_block(1.0 / HEAD_DIM).astype(BF16)
    yd = hf_ref[0] + hb_ref[0]
    yd = yd - _head_sum(yd, mean_block)
    yd = yd * lax.rsqrt(_head_sum(yd * yd, mean_block) + LN_EPS) * ng_ref[...] * jax.nn.sigmoid(og_ref[0])
    y = None
    for m, ym in enumerate((ya_ref[0], yb_ref[0], yc_ref[0], yd)):
        part = jnp.dot(ym.astype(BF16), w_ref[m * GROUP_WIDTH:(m + 1) * GROUP_WIDTH, :],
                       preferred_element_type=F32)
        y = part if y is None else y + part
    o_ref[0] = _layernorm(alpha * h_ref[0] + gate * y, g_ref[...], b_ref[...])


def _outproj(ya, yb, yc, hf, hb, p_mlstm, mlstm_norm_g, h, gate_l, gate_c, w, ln_g, ln_b, tile0, ctx_tiles, alpha):
    b, t, d = h.shape
    gw = GROUP_WIDTH
    n_tiles = t // ROW_TILE - tile0
    yrow = lambda c: pl.BlockSpec((1, ROW_TILE, gw), lambda bi, i: (bi, tile0 + i, c))
    yb_row = pl.BlockSpec((1, ROW_TILE, gw), lambda bi, i: (bi, i, 0))
    hrow = pl.BlockSpec((1, ROW_TILE, d), lambda bi, i: (bi, tile0 + i, 0))
    lat = pl.BlockSpec((1, 1, d), lambda bi, i: (bi, 0, 0))
    ctx = pl.BlockSpec((1, 1, d), lambda bi, i: (0, 0, 0))
    vec = lambda n: pl.BlockSpec((1, n), lambda bi, i: (0, 0))
    return pl.pallas_call(
        functools.partial(_outproj_body, ctx_tiles=ctx_tiles, alpha=alpha),
        grid=(b, n_tiles),
        in_specs=[yrow(0), yb_row, yrow(0), yrow(0), yrow(0), yrow(3), vec(gw), hrow, lat, ctx,
                  pl.BlockSpec(w.shape, lambda bi, i: (0, 0)), vec(d), vec(d)],
        out_specs=pl.BlockSpec((1, ROW_TILE, d), lambda bi, i: (bi, i, 0)),
        out_shape=jax.ShapeDtypeStruct((b, n_tiles * ROW_TILE, d), F32),
        compiler_params=_cparams(("parallel", "parallel")),
        name="outproj_ln",
    )(ya, yb, yc, hf, hb, p_mlstm, mlstm_norm_g.reshape(1, gw), h, gate_l, gate_c, w, ln_g.reshape(1, d),
      ln_b.reshape(1, d))


FFN_CHUNKS = 1


def _ffn_body(x_ref, xp_ref, xn_ref, sl_ref, cl_ref, gl_ref, sc_ref, cc_ref, gc_ref,
              wu_ref, cw_ref, cb_ref, wd_ref, g_ref, b_ref, o_ref, *, ctx_tiles, n_tiles, alpha, d_ff):
    i = pl.program_id(1)
    is_ctx = i < ctx_tiles
    shift = jnp.where(is_ctx, sc_ref[0], sl_ref[0])
    scale = jnp.where(is_ctx, cc_ref[0], cl_ref[0])
    gate_mod = jnp.where(is_ctx, gc_ref[0], gl_ref[0])
    seg_start = jnp.logical_or(i == 0, i == ctx_tiles)
    seg_end = jnp.logical_or(i == ctx_tiles - 1, i == n_tiles - 1)

    x = x_ref[0]
    xm = (x * (1.0 + scale) + shift).astype(BF16)
    xp = (xp_ref[0] * (1.0 + scale) + shift).astype(BF16)
    xn = (xn_ref[0] * (1.0 + scale) + shift).astype(BF16)
    rows = x.shape[0]
    row_id = lax.broadcasted_iota(jnp.int32, (rows, 1), 0)
    ch = d_ff // FFN_CHUNKS
    f = jnp.zeros(x.shape, F32)
    for c in range(FFN_CHUNKS):
        wg = wu_ref[:, c * ch:(c + 1) * ch]
        wv = wu_ref[:, d_ff + c * ch:d_ff + (c + 1) * ch]
        gate = jnp.dot(xm, wg, preferred_element_type=F32)
        val = jnp.dot(xm, wv, preferred_element_type=F32)
        gp = jnp.dot(xp, wg, preferred_element_type=F32)[7:8, :]
        gn = jnp.dot(xn, wg, preferred_element_type=F32)[0:1, :]
        gp = jnp.where(seg_start, 0.0, gp)
        gn = jnp.where(seg_end, 0.0, gn)
        prev = jnp.where(row_id == 0, gp, pltpu.roll(gate, 1, 0))
        nxt = jnp.where(row_id == rows - 1, gn, pltpu.roll(gate, rows - 1, 0))
        cw = cw_ref[:, c * ch:(c + 1) * ch]
        conv = prev * cw[0:1, :] + gate * cw[1:2, :] + nxt * cw[2:3, :] + cb_ref[:, c * ch:(c + 1) * ch]
        act = (jax.nn.gelu(conv) * val).astype(BF16)
        f = f + jnp.dot(act, wd_ref[c * ch:(c + 1) * ch, :], preferred_element_type=F32)
    o_ref[0] = _layernorm(alpha * x + gate_mod * f, g_ref[...], b_ref[...])


def _ffn(h, mods_l, mods_c, w_up, conv_w, conv_b, w_down, ln_g, ln_b, ctx_tiles, alpha):
    b, t, d = h.shape
    d_ff = w_down.shape[0]
    n_tiles = t // ROW_TILE
    sub = ROW_TILE // 8
    n_sub = t // 8
    row = pl.BlockSpec((1, ROW_TILE, d), lambda bi, i: (bi, i, 0))
    prev = pl.BlockSpec((1, 8, d), lambda bi, i: (bi, jnp.maximum(i * sub - 1, 0), 0))
    nxt = pl.BlockSpec((1, 8, d), lambda bi, i: (bi, jnp.minimum((i + 1) * sub, n_sub - 1), 0))
    lat = pl.BlockSpec((1, 1, d), lambda bi, i: (bi, 0, 0))
    ctx = pl.BlockSpec((1, 1, d), lambda bi, i: (0, 0, 0))
    vec = pl.BlockSpec((1, d), lambda bi, i: (0, 0))
    full = lambda a: pl.BlockSpec(a.shape, lambda bi, i: (0,) * a.ndim)
    cb = conv_b.reshape(1, d_ff)
    return pl.pallas_call(
        functools.partial(_ffn_body, ctx_tiles=ctx_tiles, n_tiles=n_tiles, alpha=alpha, d_ff=d_ff),
        grid=(b, n_tiles),
        in_specs=[row, prev, nxt, lat, lat, lat, ctx, ctx, ctx,
                  full(w_up), full(conv_w), full(cb), full(w_down), vec, vec],
        out_specs=row,
        out_shape=jax.ShapeDtypeStruct((b, t, d), F32),
        compiler_params=_cparams(("parallel", "parallel")),
        name="ffn",
    )(h, h, h, *mods_l, *mods_c, w_up, conv_w, cb, w_down, ln_g.reshape(1, d), ln_b.reshape(1, d))


def _rwkv7(p, tc, mu, w0, w2, a0, a2, g2, k_k, k_a, r_k, lnx_g, lnx_b):
    b = p.shape[0]
    f = _rwkv_features(p, tc, mu, w0, w2, a0, a2, g2, k_k, k_a, r_k)
    col = RWKV_FEATS.index
    y = _scan("rwkv",
              [_to_scan(f, col("r"), col("r"), "k", tc), _to_scan(f, col("w0"), col("w1"), "k", tc),
               _to_scan(f, col("k0"), col("k1"), "k", tc), _to_scan(f, col("a"), col("a"), "k", tc),
               _to_scan(f, col("b0"), col("b1"), "k", tc)],
              _to_scan(f, col("v"), col("v"), "v", tc))
    return _from_scan(y, b, tc, "rwkv", f, (col("bonus"), col("g")), (lnx_g, lnx_b))


def _rope_tables(n):
    rows = n // GRID_W
    row = jnp.repeat(jnp.arange(rows, dtype=F32), GRID_W)
    col = jnp.tile(jnp.arange(GRID_W, dtype=F32), rows)
    n_freq = DIFF_QK_DIM // 4
    inv_freq = ROPE_BASE ** (-jnp.arange(n_freq, dtype=F32) / n_freq)
    ang = jnp.concatenate([row[:, None] * inv_freq, col[:, None] * inv_freq], axis=-1)
    return jnp.cos(ang), jnp.sin(ang)


def _rope(x, cos, sin, tc):
    b, t, gw = x.shape
    half = DIFF_QK_DIM // 2
    cos = jnp.concatenate([jnp.ones((tc, half), F32), cos], axis=0)[:, None, :]
    sin = jnp.concatenate([jnp.zeros((tc, half), F32), sin], axis=0)[:, None, :]
    x = x.reshape(b, t, gw // DIFF_QK_DIM, DIFF_QK_DIM)
    x1, x2 = x[..., :half], x[..., half:]
    return jnp.concatenate([x1 * cos - x2 * sin, x2 * cos + x1 * sin], axis=-1).reshape(b, t, gw)


def _diff_attn(p, tc, lam_vecs, norm_g, layer, need_ctx):
    b, t, _ = p.shape
    gw = GROUP_WIDTH
    lam_init = 0.8 - 0.6 * math.exp(-0.3 * layer)
    lv = lam_vecs.astype(F32)
    lam = jnp.exp(jnp.sum(lv[0] * lv[1])) - jnp.exp(jnp.sum(lv[2] * lv[3])) + lam_init
    cos, sin = _rope_tables(t - tc)
    q = _rope(p[..., :gw], cos, sin, tc)
    k = _rope(p[..., gw:2 * gw], cos, sin, tc)
    q_tile0 = 0 if need_ctx else tc // ATTN_TILE
    return _attention(q, k, p, 2 * gw // LANES, lam, norm_g, 1.0 - lam_init, q_tile0, tc)


def _hgrn2(p, tc, lb, norm_g):
    b = p.shape[0]
    f = _hgrn_features(p, lb)
    col = HGRN_FEATS.index
    o = _scan("gla",
              [_to_scan(f, col("q"), col("q"), "k", tc), _to_scan(f, col("w0"), col("w1"), "k", tc),
               _to_scan(f, col("k0"), col("k1"), "k", tc)],
              _to_scan(p, 3, 3, "v", tc))
    return _from_scan(o, b, tc, "gla", f, (col("gate"),), (jnp.tile(norm_g, N_HEADS),))


RWKV_COLS = 3 * GROUP_WIDTH + 2 * RWKV_DECAY_LORA + 2 * RWKV_ICLR_LORA + RWKV_GATE_LORA
DIFF_COLS = 3 * GROUP_WIDTH
HGRN_COLS = 5 * GROUP_WIDTH
MLSTM_COLS = 4 * GROUP_WIDTH + 4 * N_HEADS
SLAB_COLS = (RWKV_COLS, DIFF_COLS, HGRN_COLS, MLSTM_COLS)
SLAB_PAD = (5 * GROUP_WIDTH, DIFF_COLS, HGRN_COLS, 4 * GROUP_WIDTH + LANES)


def _split_cols(w):
    parts, off = [], 0
    for n, n_pad in zip(SLAB_COLS, SLAB_PAD):
        parts.append(jnp.pad(w[:, off:off + n], ((0, 0), (0, n_pad - n))).astype(BF16))
        off += n
    return parts


def kernel(x, c, ctx, c_ctx, ada_w, ada_b, w_in, rwkv_mu, rwkv_w0, rwkv_w2, rwkv_a0, rwkv_a2, rwkv_g2, rwkv_k_k, rwkv_k_a, rwkv_r_k, rwkv_lnx_g, rwkv_lnx_b, diff_lambda, diff_norm_g, hgrn_lb_logits, hgrn_norm_g, mlstm_conv_w, mlstm_conv_b, mlstm_gate_b, mlstm_norm_g, w_out, ffn_w_up, ffn_conv_w, ffn_conv_b, ffn_w_down, ln_g, ln_b):
    depth = w_in.shape[0]
    b, seq, d = x.shape
    tc = ctx.shape[1]
    assert tc == ROW_TILE == ATTN_TILE and seq % ROW_TILE == 0 and 2 * b * N_HEADS <= LANES
    alpha = (2.0 * depth) ** 0.25

    lb_w = jax.nn.softmax(hgrn_lb_logits.astype(F32), axis=0)
    lower_bounds = jnp.cumsum(lb_w, axis=0) - lb_w[0]

    cond = jnp.zeros((8, d), F32).at[:b].set(c).at[b].set(c_ctx)
    h = jnp.concatenate([ctx, x], axis=1)
    for layer in range(depth):
        last = layer == depth - 1
        mod = _adaln(cond, ada_w[layer], ada_b[layer]).reshape(8, 6, d)
        mods_l = [mod[:b, j][:, None, :] for j in range(6)]
        mods_c = [mod[b:b + 1, j][:, None, :] for j in range(6)]

        p_rwkv, p_diff, p_hgrn, p_mlstm = _inproj(h, mods_l[0], mods_l[1], mods_c[0], mods_c[1],
                                                  _split_cols(w_in[layer]), tc // ROW_TILE)
        y_a = _rwkv7(p_rwkv, tc, rwkv_mu[layer], rwkv_w0[layer], rwkv_w2[layer], rwkv_a0[layer], rwkv_a2[layer],
                     rwkv_g2[layer], rwkv_k_k[layer], rwkv_k_a[layer], rwkv_r_k[layer], rwkv_lnx_g[layer],
                     rwkv_lnx_b[layer])
        y_b = _diff_attn(p_diff, tc, diff_lambda[layer], diff_norm_g[layer], layer, not last)
        y_c = _hgrn2(p_hgrn, tc, lower_bounds[layer], hgrn_norm_g[layer])
        h_f, h_b = (_mlstm_chunked(p_mlstm, mlstm_conv_w[layer], mlstm_conv_b[layer], mlstm_gate_b[layer], rev, tc)
                    for rev in (False, True))

        tile0, ctx_tiles = (tc // ROW_TILE, 0) if last else (0, tc // ROW_TILE)
        h = _outproj(y_a, y_b, y_c, h_f, h_b, p_mlstm, mlstm_norm_g[layer], h, mods_l[2], mods_c[2],
                     w_out[layer].astype(BF16), ln_g[layer, 0], ln_b[layer, 0], tile0, ctx_tiles, alpha)
        h = _ffn(h, mods_l[3:6], mods_c[3:6], ffn_w_up[layer].astype(BF16), ffn_conv_w[layer], ffn_conv_b[layer],
                 ffn_w_down[layer].astype(BF16), ln_g[layer, 1], ln_b[layer, 1], ctx_tiles, alpha)
    return h
```

```python
import functools
import math

import jax
import jax.numpy as jnp
import numpy as np
from jax import lax
from jax.experimental import pallas as pl
from jax.experimental.pallas import tpu as pltpu

F32 = jnp.float32
BF16 = jnp.bfloat16

HEAD_DIM = 64
N_HEADS = 4
GROUP_WIDTH = N_HEADS * HEAD_DIM
DIFF_QK_DIM = HEAD_DIM // 2
GRID_W = 64
ROPE_BASE = 10000.0
RWKV_DECAY_LORA = 64
RWKV_ICLR_LORA = 64
RWKV_GATE_LORA = 160
RWKV_LNX_EPS = 64e-5
LN_EPS = 1e-5
LB_FLOOR = 1e-30
MAX_NEG_LOG_STAB = 60.0

LANES = 128
ROW_TILE = 256
ATTN_TILE = 256
TIME_BLOCK = 128
SCAN_TB = 64
VMEM_LIMIT = 56 * 1024 * 1024


def _cparams(sem):
    return pltpu.CompilerParams(dimension_semantics=sem, vmem_limit_bytes=VMEM_LIMIT)


def _rev_block(i, n_ctx, n_all):
    return jnp.where(i < n_ctx, n_ctx - 1 - i, n_ctx + n_all - 1 - i)


def _split3(x):
    hi = x.astype(BF16)
    r1 = x - hi.astype(F32)
    mid = r1.astype(BF16)
    return hi, mid, (r1 - mid.astype(F32)).astype(BF16)


def _dot01_left(a01, x):
    a = a01.astype(BF16)
    p = [jnp.dot(a, part, preferred_element_type=F32) for part in _split3(x)]
    return (p[0] + p[1]) + p[2]


def _dot01_right(x, b01):
    b = b01.astype(BF16)
    p = [jnp.dot(part, b, preferred_element_type=F32) for part in _split3(x)]
    return (p[0] + p[1]) + p[2]


def _flip_rows(x):
    n = x.shape[0]
    r = lax.broadcasted_iota(jnp.int32, (n, n), 0)
    c = lax.broadcasted_iota(jnp.int32, (n, n), 1)
    return _dot01_left(jnp.where(r + c == n - 1, 1.0, 0.0).astype(F32), x)


def _adaln_body(c_ref, w_ref, b_ref, o_ref):
    x = c_ref[...]
    x = (x * jax.nn.sigmoid(x)).astype(BF16)
    o_ref[...] = jnp.dot(x, w_ref[0].astype(BF16), preferred_element_type=F32) + b_ref[0]


def _adaln(cond, w, b, layer):
    m, d = cond.shape
    n = w.shape[2]
    tn = 1536
    return pl.pallas_call(
        _adaln_body,
        grid=(n // tn,),
        in_specs=[pl.BlockSpec((m, d), lambda j: (0, 0)),
                  pl.BlockSpec((1, d, tn), lambda j: (layer, 0, j)),
                  pl.BlockSpec((1, 1, tn), lambda j: (layer, 0, j))],
        out_specs=pl.BlockSpec((m, tn), lambda j: (0, j)),
        out_shape=jax.ShapeDtypeStruct((m, n), F32),
        compiler_params=_cparams(("arbitrary",)),
        name="adaln",
    )(cond, w, b.reshape(b.shape[0], 1, n))


def _inproj_body(x_ref, sl_ref, cl_ref, sc_ref, cc_ref, *refs, ctx_tiles):
    n = len(refs) // 2
    is_ctx = pl.program_id(1) < ctx_tiles
    shift = jnp.where(is_ctx, sc_ref[0], sl_ref[0])
    scale = jnp.where(is_ctx, cc_ref[0], cl_ref[0])
    xm = (x_ref[0] * (1.0 + scale) + shift).astype(BF16)
    for w_ref, o_ref in zip(refs[:n], refs[n:]):
        o_ref[0] = jnp.dot(xm, w_ref[...], preferred_element_type=F32)


def _inproj(h, shift_l, scale_l, shift_c, scale_c, ws, ctx_tiles):
    b, t, d = h.shape
    row = lambda n: pl.BlockSpec((1, ROW_TILE, n), lambda bi, i: (bi, i, 0))
    lat = pl.BlockSpec((1, 1, d), lambda bi, i: (bi, 0, 0))
    ctx = pl.BlockSpec((1, 1, d), lambda bi, i: (0, 0, 0))
    return pl.pallas_call(
        functools.partial(_inproj_body, ctx_tiles=ctx_tiles),
        grid=(b, t // ROW_TILE),
        in_specs=[row(d), lat, lat, ctx, ctx] + [pl.BlockSpec(w.shape, lambda bi, i: (0, 0)) for w in ws],
        out_specs=[row(w.shape[1]) for w in ws],
        out_shape=[jax.ShapeDtypeStruct((b, t, w.shape[1]), F32) for w in ws],
        compiler_params=_cparams(("parallel", "parallel")),
        name="inproj",
    )(h, shift_l, scale_l, shift_c, scale_c, *ws)


def _softplus(x):
    return jnp.maximum(x, 0.0) + jnp.log(1.0 + jnp.exp(-jnp.abs(x)))


def _head_block(scale=1.0):
    r = lax.broadcasted_iota(jnp.int32, (GROUP_WIDTH, GROUP_WIDTH), 0) // HEAD_DIM
    c = lax.broadcasted_iota(jnp.int32, (GROUP_WIDTH, GROUP_WIDTH), 1) // HEAD_DIM
    return jnp.where(r == c, scale, 0.0).astype(F32)


def _head_sum(x, block_bf16):
    hi = x.astype(BF16)
    lo = (x - hi.astype(F32)).astype(BF16)
    return (jnp.dot(hi, block_bf16, preferred_element_type=F32)
            + jnp.dot(lo, block_bf16, preferred_element_type=F32))


def _shifted_rows(x, prev8, next8, seg_start, seg_end):
    rows = x.shape[0]
    row_id = lax.broadcasted_iota(jnp.int32, (rows, 1), 0)
    prev_row = jnp.where(seg_start, 0.0, prev8[7:8, :])
    next_row = jnp.where(seg_end, 0.0, next8[0:1, :])
    prev = jnp.where(row_id == 0, prev_row, pltpu.roll(x, 1, 0))
    nxt = jnp.where(row_id == rows - 1, next_row, pltpu.roll(x, rows - 1, 0))
    return prev, nxt


RWKV_FEATS = ("r", "v", "w0", "w1", "k0", "k1", "a", "b0", "b1", "g", "bonus")


def _rwkv_feat_body(p_ref, pp_ref, pn_ref, mu_ref, w0_ref, w2_ref, a0_ref, a2_ref, g2_ref, kk_ref, ka_ref,
                    rk_ref, o_ref, *, ctx_tiles, n_tiles):
    i = pl.program_id(1)
    seg_start = jnp.logical_or(i == 0, i == ctx_tiles)
    seg_end = jnp.logical_or(i == ctx_tiles - 1, i == n_tiles - 1)
    gw = GROUP_WIDTH
    p = p_ref[0]
    prev, nxt = _shifted_rows(p, pp_ref[0], pn_ref[0], seg_start, seg_end)
    p = p + (0.5 * (prev + nxt) - p) * mu_ref[...]
    r, k, v = p[:, :gw], p[:, gw:2 * gw], p[:, 2 * gw:3 * gw]
    wd, ad, gd = p[:, 3 * gw:3 * gw + LANES], p[:, 3 * gw + LANES:4 * gw], p[:, 4 * gw:5 * gw]
    lora_w = jnp.dot(jnp.tanh(wd).astype(BF16), w2_ref[...], preferred_element_type=F32)
    lora_a = jnp.dot(ad.astype(BF16), a2_ref[...], preferred_element_type=F32)
    decay = jnp.exp(-jnp.exp(-_softplus(-(w0_ref[...] + lora_w)) - 0.5))
    a = jax.nn.sigmoid(a0_ref[...] + lora_a)
    g = jnp.dot(jax.nn.sigmoid(gd).astype(BF16), g2_ref[...], preferred_element_type=F32)
    block = _head_block().astype(BF16)
    kk = k * kk_ref[...]
    kk = kk / jnp.maximum(jnp.sqrt(_head_sum(kk * kk, block)), 1e-12)
    k0 = k * (1.0 + (a[:, :gw] - 1.0) * ka_ref[...])
    k1 = k * (1.0 + (a[:, gw:] - 1.0) * ka_ref[...])
    bonus = _head_sum(r * (k0 + k1) * rk_ref[...], block) * v
    feats = dict(r=r, v=v, w0=decay[:, :gw], w1=decay[:, gw:], k0=k0, k1=k1, a=-kk, b0=kk * a[:, :gw],
                 b1=kk * a[:, gw:], g=g, bonus=bonus)
    for j, name in enumerate(RWKV_FEATS):
        o_ref[0, :, j * gw:(j + 1) * gw] = feats[name]


def _block_diag2(w):
    z = jnp.zeros_like(w[0])
    return jnp.concatenate([jnp.concatenate([w[0], z], axis=1), jnp.concatenate([z, w[1]], axis=1)], axis=0)


def _rwkv_features(p, tc, mu, w0, w2, a0, a2, g2, k_k, k_a, r_k):
    b, t, cols = p.shape
    gw = GROUP_WIDTH
    n_tiles, sub, n_sub = t // ROW_TILE, ROW_TILE // 8, t // 8
    row = lambda w: pl.BlockSpec((1, ROW_TILE, w), lambda bi, i: (bi, i, 0))
    prev = pl.BlockSpec((1, 8, cols), lambda bi, i: (bi, jnp.maximum(i * sub - 1, 0), 0))
    nxt = pl.BlockSpec((1, 8, cols), lambda bi, i: (bi, jnp.minimum((i + 1) * sub, n_sub - 1), 0))
    full = lambda a: pl.BlockSpec(a.shape, lambda bi, i: (0,) * a.ndim)
    vec = lambda a: a.reshape(1, -1).astype(F32)
    g2_pad = jnp.pad(g2, ((0, gw - g2.shape[0]), (0, 0))).astype(BF16)
    args = [jnp.pad(vec(mu), ((0, 0), (0, cols - mu.shape[0]))), vec(w0), _block_diag2(w2).astype(BF16), vec(a0),
            _block_diag2(a2).astype(BF16), g2_pad, vec(k_k), vec(k_a), vec(r_k)]
    return pl.pallas_call(
        functools.partial(_rwkv_feat_body, ctx_tiles=tc // ROW_TILE, n_tiles=n_tiles),
        grid=(b, n_tiles),
        in_specs=[row(cols), prev, nxt] + [full(a) for a in args],
        out_specs=row(len(RWKV_FEATS) * gw),
        out_shape=jax.ShapeDtypeStruct((b, t, len(RWKV_FEATS) * gw), F32),
        compiler_params=_cparams(("parallel", "parallel")),
        name="rwkv_feat",
    )(p, p, p, *args)


HGRN_FEATS = ("q", "w0", "w1", "k0", "k1", "gate")


def _hgrn_feat_body(p_ref, llb_ref, l1m_ref, oml_ref, o_ref):
    gw = GROUP_WIDTH
    p = p_ref[0]
    q, f_f, f_b, g = p[:, :gw], p[:, gw:2 * gw], p[:, 2 * gw:3 * gw], p[:, 4 * gw:5 * gw]
    feats = dict(q=q * jax.nn.sigmoid(q), gate=g * jax.nn.sigmoid(g))
    for d, f in enumerate((f_f, f_b)):
        x = llb_ref[...]
        y = l1m_ref[...] - _softplus(-f)
        log_f = jnp.maximum(x, y) + jnp.log(1.0 + jnp.exp(-jnp.abs(x - y)))
        feats["w%d" % d] = jnp.exp(log_f)
        feats["k%d" % d] = oml_ref[...] * jax.nn.sigmoid(-f)
    for j, name in enumerate(HGRN_FEATS):
        o_ref[0, :, j * gw:(j + 1) * gw] = feats[name]


def _hgrn_features(p, lb):
    b, t, cols = p.shape
    gw = GROUP_WIDTH
    row = lambda w: pl.BlockSpec((1, ROW_TILE, w), lambda bi, i: (bi, i, 0))
    vecs = [jnp.log(jnp.maximum(lb, LB_FLOOR)).reshape(1, gw), jnp.log1p(-lb).reshape(1, gw), (1.0 - lb).reshape(1, gw)]
    return pl.pallas_call(
        _hgrn_feat_body,
        grid=(b, t // ROW_TILE),
        in_specs=[row(cols)] + [pl.BlockSpec((1, gw), lambda bi, i: (0, 0))] * 3,
        out_specs=row(len(HGRN_FEATS) * gw),
        out_shape=jax.ShapeDtypeStruct((b, t, len(HGRN_FEATS) * gw), F32),
        compiler_params=_cparams(("parallel", "parallel")),
        name="hgrn_feat",
    )(p, *vecs)


def _to_scan_body(x0_ref, x1_ref, o_ref, r_ref, *, kind, nb):
    n_scan = 2 * nb * N_HEADS
    rep = LANES // n_scan
    for b in range(nb):
        r_ref[pl.ds(b * GROUP_WIDTH, GROUP_WIDTH), :] = x0_ref[b].T
        r_ref[pl.ds((nb + b) * GROUP_WIDTH, GROUP_WIDTH), :] = _flip_rows(x1_ref[b]).T
    if kind == "k":
        for k in range(HEAD_DIM):
            rows = r_ref[pl.ds(k, n_scan, stride=HEAD_DIM), :]
            o_ref[k] = jnp.concatenate([rows] * rep, axis=0).T
    else:
        for vh in range(HEAD_DIM // rep):
            rows = [r_ref[pl.ds(vh * rep + vl, n_scan, stride=HEAD_DIM), :] for vl in range(rep)]
            o_ref[:, vh, :] = jnp.concatenate(rows, axis=0).T


def _to_scan(x, col0, col1, kind, tc):
    nb, t, _ = x.shape
    gw = GROUP_WIDTH
    x0 = x1 = x
    n_all, n_ctx = t // TIME_BLOCK, tc // TIME_BLOCK
    n_scan = 2 * nb * N_HEADS
    rep = LANES // n_scan
    fwd = pl.BlockSpec((nb, TIME_BLOCK, gw), lambda i: (0, i, col0))
    bwd = pl.BlockSpec((nb, TIME_BLOCK, gw), lambda i: (0, _rev_block(i, n_ctx, n_all), col1))
    if kind == "k":
        out_shape = (HEAD_DIM, t, LANES)
        out_spec = pl.BlockSpec((HEAD_DIM, TIME_BLOCK, LANES), lambda i: (0, i, 0))
    else:
        out_shape = (t, HEAD_DIM // rep, LANES)
        out_spec = pl.BlockSpec((TIME_BLOCK, HEAD_DIM // rep, LANES), lambda i: (i, 0, 0))
    return pl.pallas_call(
        functools.partial(_to_scan_body, kind=kind, nb=nb),
        grid=(n_all,),
        in_specs=[fwd, bwd],
        out_specs=out_spec,
        out_shape=jax.ShapeDtypeStruct(out_shape, F32),
        scratch_shapes=[pltpu.VMEM((n_scan * HEAD_DIM, TIME_BLOCK), F32)],
        compiler_params=_cparams(("parallel",)),
        name="to_scan_" + kind,
    )(x0, x1)


def _from_scan_body(yf_ref, yb_ref, *rest, nb, readout):
    if readout == "rwkv":
        bonus_ref, g_ref, ng_ref, nb_ref, o_ref, r_ref = rest
    else:
        gate_ref, ng_ref, o_ref, r_ref = rest
    n_scan = 2 * nb * N_HEADS
    rep = LANES // n_scan
    for vh in range(HEAD_DIM // rep):
        r_ref[0, pl.ds(vh * LANES, LANES), :] = yf_ref[:, vh, :].T
        r_ref[1, pl.ds(vh * LANES, LANES), :] = yb_ref[:, vh, :].T
    mean_block = _head_block(1.0 / HEAD_DIM).astype(BF16)
    for b in range(nb):
        slab = lambda d: jnp.concatenate(
            [r_ref[d, pl.ds((d * nb + b) * N_HEADS + h, HEAD_DIM, stride=n_scan), :] for h in range(N_HEADS)],
            axis=0).T
        y = slab(0) + _flip_rows(slab(1))
        if readout == "rwkv":
            y = y - _head_sum(y, mean_block)
            y = y * lax.rsqrt(_head_sum(y * y, mean_block) + RWKV_LNX_EPS) * ng_ref[...] + nb_ref[...]
            o_ref[b] = (y + bonus_ref[b]) * g_ref[b]
        else:
            y = y * lax.rsqrt(_head_sum(y * y, mean_block) + LN_EPS) * ng_ref[...]
            o_ref[b] = y * gate_ref[b]


def _from_scan(y, nb, tc, readout, feats, cols, vecs):
    t, n_vh, _ = y.shape
    gw = GROUP_WIDTH
    n_all, n_ctx = t // TIME_BLOCK, tc // TIME_BLOCK
    fwd = pl.BlockSpec((TIME_BLOCK, n_vh, LANES), lambda i: (i, 0, 0))
    bwd = pl.BlockSpec((TIME_BLOCK, n_vh, LANES), lambda i: (_rev_block(i, n_ctx, n_all), 0, 0))
    tok = lambda c: pl.BlockSpec((nb, TIME_BLOCK, gw), lambda i: (0, i, c))
    vec = pl.BlockSpec((1, gw), lambda i: (0, 0))
    return pl.pallas_call(
        functools.partial(_from_scan_body, nb=nb, readout=readout),
        grid=(n_all,),
        in_specs=[fwd, bwd] + [tok(c) for c in cols] + [vec] * len(vecs),
        out_specs=tok(0),
        out_shape=jax.ShapeDtypeStruct((nb, t, gw), F32),
        scratch_shapes=[pltpu.VMEM((2, n_vh * LANES, TIME_BLOCK), F32)],
        compiler_params=_cparams(("parallel",)),
        name="from_scan_" + readout,
    )(y, y, *([feats] * len(cols)), *[v.reshape(1, gw) for v in vecs])


N_ACC = 4


def _acc_add(acc, i, x):
    acc[i % N_ACC] = x if acc[i % N_ACC] is None else acc[i % N_ACC] + x


def _acc_total(acc):
    return (acc[0] + acc[1]) + (acc[2] + acc[3])


def _scan_body(*refs, mode, tb, n_vh):
    if mode == "rwkv":
        r_ref, w_ref, k_ref, a_ref, b_ref, v_ref, y_ref, s_ref = refs
    else:
        r_ref, w_ref, k_ref, v_ref, y_ref, s_ref = refs

    @pl.when(pl.program_id(0) == 0)
    def _():
        s_ref[...] = jnp.zeros_like(s_ref)

    row = lambda ref, k, j: ref[k, pl.ds(j, 1), :]
    wide = lambda x: jnp.broadcast_to(x, (n_vh, LANES))

    def rwkv_step(j, sa):
        v = v_ref[j]
        j_next = jnp.minimum(j + 1, tb - 1)
        ys, sas = [None] * N_ACC, [None] * N_ACC
        for k in range(HEAD_DIM):
            s = s_ref[k] * wide(row(w_ref, k, j)) + sa * wide(row(b_ref, k, j)) + v * wide(row(k_ref, k, j))
            s_ref[k] = s
            _acc_add(ys, k, s * wide(row(r_ref, k, j)))
            _acc_add(sas, k, s * wide(row(a_ref, k, j_next)))
        y_ref[j] = _acc_total(ys)
        return _acc_total(sas)

    def gla_step(j, carry):
        v = v_ref[j]
        ys = [None] * N_ACC
        for k in range(HEAD_DIM):
            s = s_ref[k] * wide(row(w_ref, k, j)) + v * wide(row(k_ref, k, j))
            s_ref[k] = s
            _acc_add(ys, k, s * wide(row(r_ref, k, j)))
        y_ref[j] = _acc_total(ys)
        return carry

    if mode == "rwkv":
        sa0 = [None] * N_ACC
        for k in range(HEAD_DIM):
            _acc_add(sa0, k, s_ref[k] * wide(row(a_ref, k, 0)))
        lax.fori_loop(0, tb, rwkv_step, _acc_total(sa0), unroll=8)
    else:
        lax.fori_loop(0, tb, gla_step, 0, unroll=8)


def _scan(mode, k_inputs, v_in):
    t, n_vh, _ = v_in.shape
    tb = SCAN_TB
    big = pl.BlockSpec((HEAD_DIM, tb, LANES), lambda i: (0, i, 0))
    small = pl.BlockSpec((tb, n_vh, LANES), lambda i: (i, 0, 0))
    return pl.pallas_call(
        functools.partial(_scan_body, mode=mode, tb=tb, n_vh=n_vh),
        grid=(t // tb,),
        in_specs=[big] * len(k_inputs) + [small],
        out_specs=small,
        out_shape=jax.ShapeDtypeStruct((t, n_vh, LANES), F32),
        scratch_shapes=[pltpu.VMEM((HEAD_DIM, n_vh, LANES), F32)],
        compiler_params=_cparams(("arbitrary",)),
        name="scan_" + mode,
    )(*k_inputs, v_in)


MLSTM_CHUNK = 64


def _prefix_max(x, reverse):
    n = x.shape[0]
    row = lax.broadcasted_iota(jnp.int32, x.shape, 0)
    sh = 1
    while sh < n:
        if reverse:
            x = jnp.where(row < n - sh, jnp.maximum(x, pltpu.roll(x, n - sh, 0)), x)
        else:
            x = jnp.where(row >= sh, jnp.maximum(x, pltpu.roll(x, sh, 0)), x)
        sh *= 2
    return x


def _mlstm_body(qk_ref, qkp_ref, qkn_ref, v_ref, gt_ref, cw_ref, cb_ref, sel_ref, gb_ref, o_ref, st_ref, m_ref, *,
                reverse, n_sub, n_ctx, n_blocks):
    c, gw = MLSTM_CHUNK, GROUP_WIDTH
    i = pl.program_id(1)

    @pl.when(i == 0)
    def _():
        st_ref[...] = jnp.zeros_like(st_ref)
        m_ref[...] = jnp.zeros_like(m_ref)

    blk = _rev_block(i, n_ctx, n_blocks) if reverse else i
    seg_start = jnp.logical_or(blk == 0, blk == n_ctx)
    seg_end = jnp.logical_or(blk == n_ctx - 1, blk == n_blocks - 1)
    qk = qk_ref[0]
    prev, nxt = _shifted_rows(qk, qkp_ref[0], qkn_ref[0], seg_start, seg_end)
    qk = prev * cw_ref[0:1, :] + qk * cw_ref[1:2, :] + nxt * cw_ref[2:3, :] + cb_ref[...]
    qk = qk * jax.nn.sigmoid(qk)
    q_all, k_all = qk[:, :gw], qk[:, gw:] * HEAD_DIM ** -0.5
    v_all = v_ref[0]
    gt = gt_ref[0]
    ig_all = _dot01_right(gt, sel_ref[0]) + gb_ref[0:1, :]
    fg_all = -_softplus(-(_dot01_right(gt, sel_ref[1]) + gb_ref[1:2, :]))

    tt = lax.broadcasted_iota(jnp.int32, (c, c), 0)
    ss = lax.broadcasted_iota(jnp.int32, (c, c), 1)
    tri = jnp.where((ss >= tt) if reverse else (ss <= tt), 1.0, 0.0).astype(F32)
    row = lax.broadcasted_iota(jnp.int32, (c, gw), 0)
    s_of_lane = lax.broadcasted_iota(jnp.int32, (c, gw), 1) % c
    causal = (s_of_lane >= row) if reverse else (s_of_lane <= row)
    diag = s_of_lane == row
    block = _head_block()
    block2 = jnp.concatenate([block, block], axis=1)
    ones8 = jnp.ones((8, c), F32)
    nt = (((1,), (1,)), ((), ()))
    last = 0 if reverse else c - 1
    for u in (range(n_sub - 1, -1, -1) if reverse else range(n_sub)):
        sl = slice(u * c, (u + 1) * c)
        q, k, v, ig, fg = q_all[sl], k_all[sl], v_all[sl], ig_all[sl], fg_all[sl]
        b = _dot01_left(tri, fg)
        g = ig - b
        m_prev = m_ref[0:1, :]
        m_t = b + jnp.maximum(m_prev, _prefix_max(g, reverse))
        w_inter = jnp.exp(b + m_prev - m_t)
        g_row = _dot01_left(ones8, jnp.where(diag, g, 0.0))[0:1]
        qb = q.astype(BF16)
        k_bd = (jnp.concatenate([k] * N_HEADS, axis=0) * block).astype(BF16)
        scores = lax.dot_general(qb, k_bd, nt, preferred_element_type=F32)
        w = jnp.where(causal, jnp.exp((b - m_t) + g_row), 0.0) * scores
        v_bd = jnp.concatenate([jnp.concatenate([v] * N_HEADS, axis=0) * block, block], axis=1).astype(BF16)
        intra = jnp.dot(w.astype(BF16), v_bd, preferred_element_type=F32)
        inter = jnp.dot(qb, st_ref[...].astype(BF16), preferred_element_type=F32)
        num = w_inter * inter[:, :gw] + intra[:, :gw]
        den = w_inter * inter[:, gw:] + intra[:, gw:]
        floor = jnp.exp(jnp.minimum(-m_t, MAX_NEG_LOG_STAB))
        o_ref[0, pl.ds(u * c, c), :] = num / jnp.maximum(jnp.abs(den), floor)
        m_new = m_t[last:last + 1]
        b_end = b[last:last + 1]
        kw = k * jnp.exp(b_end - b + ig - m_new)
        decay = jnp.exp(b_end + m_prev - m_new)
        v_one = jnp.concatenate([v, jnp.ones_like(v)], axis=1).astype(BF16)
        upd = jnp.dot(kw.T.astype(BF16), v_one, preferred_element_type=F32)
        st_ref[...] = st_ref[...] * jnp.concatenate([decay, decay], axis=1) + upd * block2
        m_ref[0:1, :] = m_new


def _mlstm_chunked(p, conv_w, conv_b, gate_b, reverse, tc):
    b, t, _ = p.shape
    gw = GROUP_WIDTH
    n_blocks, n_ctx = t // ROW_TILE, tc // ROW_TILE
    sub, n_sub8 = ROW_TILE // 8, t // 8
    blk = (lambda i: _rev_block(i, n_ctx, n_blocks)) if reverse else (lambda i: i)
    d = 1 if reverse else 0
    lane_head = np.arange(gw) // HEAD_DIM
    sel = np.zeros((2, LANES, gw), np.float32)
    for j, base in enumerate((d * N_HEADS, (2 + d) * N_HEADS)):
        sel[j, base + lane_head, np.arange(gw)] = 1.0
    gate_bias = jnp.stack([jnp.repeat(gate_b[d], HEAD_DIM), jnp.repeat(gate_b[2 + d], HEAD_DIM)], axis=0)
    full = lambda a: pl.BlockSpec(a.shape, lambda bi, i: (0,) * a.ndim)
    cb = conv_b.reshape(1, 2 * gw)
    return pl.pallas_call(
        functools.partial(_mlstm_body, reverse=reverse, n_sub=ROW_TILE // MLSTM_CHUNK, n_ctx=n_ctx,
                          n_blocks=n_blocks),
        grid=(b, n_blocks),
        in_specs=[pl.BlockSpec((1, ROW_TILE, 2 * gw), lambda bi, i: (bi, blk(i), 0)),
                  pl.BlockSpec((1, 8, 2 * gw), lambda bi, i: (bi, jnp.maximum(blk(i) * sub - 1, 0), 0)),
                  pl.BlockSpec((1, 8, 2 * gw), lambda bi, i: (bi, jnp.minimum((blk(i) + 1) * sub, n_sub8 - 1), 0)),
                  pl.BlockSpec((1, ROW_TILE, gw), lambda bi, i: (bi, blk(i), 2)),
                  pl.BlockSpec((1, ROW_TILE, LANES), lambda bi, i: (bi, blk(i), 4 * gw // LANES)),
                  full(conv_w), full(cb), full(sel), full(gate_bias)],
        out_specs=pl.BlockSpec((1, ROW_TILE, gw), lambda bi, i: (bi, blk(i), 0)),
        out_shape=jax.ShapeDtypeStruct((b, t, gw), F32),
        scratch_shapes=[pltpu.VMEM((gw, 2 * gw), F32), pltpu.VMEM((8, gw), F32)],
        compiler_params=_cparams(("parallel", "arbitrary")),
        name="mlstm_chunk",
    )(p, p, p, p, p, conv_w, cb, jnp.asarray(sel), gate_bias)


def _attn_body(q_ref, k_ref, v_ref, lam_ref, g_ref, o_ref, *, out_scale, ctx_tiles, n_ctx_keys):
    def attend(n_keys):
        q = q_ref[0]
        k = k_ref[0, :n_keys, :]
        v = v_ref[0, :n_keys, :].astype(BF16)
        lane = lax.broadcasted_iota(jnp.int32, q.shape, 1)
        comp = lane // DIFF_QK_DIM
        nt = (((1,), (1,)), ((), ()))
        n_maps = LANES // DIFF_QK_DIM
        ss = [lax.dot_general(jnp.where(comp == c, q, 0.0).astype(BF16), k, nt, preferred_element_type=F32)
              for c in range(n_maps)]
        es = [jnp.exp2(s - jnp.max(s, axis=-1, keepdims=True)) for s in ss]
        ls = [jnp.sum(e, axis=-1, keepdims=True) for e in es]
        ys = []
        for hh in range(LANES // HEAD_DIM):
            e0, e1, l0, l1 = es[2 * hh], es[2 * hh + 1], ls[2 * hh], ls[2 * hh + 1]
            pr = (e0 - e1 * (lam_ref[0:1, 0:1] * l0 / l1)).astype(BF16)
            ys.append(jnp.dot(pr, v, preferred_element_type=F32) * (1.0 / l0))
        first = lane < HEAD_DIM
        y = jnp.where(first, ys[0], ys[1])
        ysq = y * y
        ms = jnp.where(first, jnp.sum(jnp.where(first, ysq, 0.0), axis=-1, keepdims=True),
                       jnp.sum(jnp.where(first, 0.0, ysq), axis=-1, keepdims=True)) * (1.0 / HEAD_DIM)
        o_ref[0] = y * lax.rsqrt(ms + LN_EPS) * g_ref[...] * out_scale

    if ctx_tiles == 0:
        attend(k_ref.shape[1])
    else:
        is_ctx = pl.program_id(2) < ctx_tiles
        pl.when(is_ctx)(lambda: attend(n_ctx_keys))
        pl.when(jnp.logical_not(is_ctx))(lambda: attend(k_ref.shape[1]))


def _attention(q, k, v, v_col0, lam, norm_g, out_scale, q_tile0, tc):
    b, t, gw = q.shape
    n_tiles = t // ATTN_TILE - q_tile0
    ctx_tiles = max(tc // ATTN_TILE - q_tile0, 0)
    qspec = pl.BlockSpec((1, ATTN_TILE, LANES), lambda bi, pi, i: (bi, q_tile0 + i, pi))
    kspec = pl.BlockSpec((1, t, LANES), lambda bi, pi, i: (bi, 0, pi))
    vspec = pl.BlockSpec((1, t, LANES), lambda bi, pi, i: (bi, 0, v_col0 + pi))
    vec = pl.BlockSpec((1, LANES), lambda bi, pi, i: (0, 0))
    return pl.pallas_call(
        functools.partial(_attn_body, out_scale=out_scale, ctx_tiles=ctx_tiles, n_ctx_keys=tc),
        grid=(b, gw // LANES, n_tiles),
        in_specs=[qspec, kspec, vspec, vec, vec],
        out_specs=pl.BlockSpec((1, ATTN_TILE, LANES), lambda bi, pi, i: (bi, i, pi)),
        out_shape=jax.ShapeDtypeStruct((b, n_tiles * ATTN_TILE, gw), F32),
        compiler_params=_cparams(("parallel", "parallel", "parallel")),
        name="diff_attn",
    )(q, k, v, jnp.full((1, LANES), lam, F32), jnp.tile(norm_g.reshape(1, HEAD_DIM), (1, LANES // HEAD_DIM)))


def _layernorm(z, g, b):
    z = z - jnp.mean(z, axis=-1, keepdims=True)
    return z * lax.rsqrt(jnp.mean(z * z, axis=-1, keepdims=True) + LN_EPS) * g + b


def _outproj_body(ya_ref, yb_ref, yc_ref, hf_ref, hb_ref, og_ref, ng_ref, h_ref, gl_ref, gc_ref, w_ref, g_ref,
                  b_ref, o_ref, *, ctx_tiles, alpha):
    is_ctx = pl.program_id(1) < ctx_tiles
    gate = jnp.where(is_ctx, gc_ref[0], gl_ref[0])
    mean_block = _head_block(1.0 / HEAD_DIM).astype(BF16)
    yd = hf_ref[0] + hb_ref[0]
    yd = yd - _head_sum(yd, mean_block)
    yd = yd * lax.rsqrt(_head_sum(yd * yd, mean_block) + LN_EPS) * ng_ref[...] * jax.nn.sigmoid(og_ref[0])
    y = None
    for m, ym in enumerate((ya_ref[0], yb_ref[0], yc_ref[0], yd)):
        part = jnp.dot(ym.astype(BF16), w_ref[m * GROUP_WIDTH:(m + 1) * GROUP_WIDTH, :],
                       preferred_element_type=F32)
        y = part if y is None else y + part
    o_ref[0] = _layernorm(alpha * h_ref[0] + gate * y, g_ref[...], b_ref[...])


def _outproj(ya, yb, yc, hf, hb, p_mlstm, mlstm_norm_g, h, gate_l, gate_c, w, ln_g, ln_b, tile0, ctx_tiles, alpha):
    b, t, d = h.shape
    gw = GROUP_WIDTH
    n_tiles = t // ROW_TILE - tile0
    yrow = lambda c: pl.BlockSpec((1, ROW_TILE, gw), lambda bi, i: (bi, tile0 + i, c))
    yb_row = pl.BlockSpec((1, ROW_TILE, gw), lambda bi, i: (bi, i, 0))
    hrow = pl.BlockSpec((1, ROW_TILE, d), lambda bi, i: (bi, tile0 + i, 0))
    lat = pl.BlockSpec((1, 1, d), lambda bi, i: (bi, 0, 0))
    ctx = pl.BlockSpec((1, 1, d), lambda bi, i: (0, 0, 0))
    vec = lambda n: pl.BlockSpec((1, n), lambda bi, i: (0, 0))
    return pl.pallas_call(
        functools.partial(_outproj_body, ctx_tiles=ctx_tiles, alpha=alpha),
        grid=(b, n_tiles),
        in_specs=[yrow(0), yb_row, yrow(0), yrow(0), yrow(0), yrow(3), vec(gw), hrow, lat, ctx,
                  pl.BlockSpec(w.shape, lambda bi, i: (0, 0)), vec(d), vec(d)],
        out_specs=pl.BlockSpec((1, ROW_TILE, d), lambda bi, i: (bi, i, 0)),
        out_shape=jax.ShapeDtypeStruct((b, n_tiles * ROW_TILE, d), F32),
        compiler_params=_cparams(("parallel", "parallel")),
        name="outproj_ln",
    )(ya, yb, yc, hf, hb, p_mlstm, mlstm_norm_g.reshape(1, gw), h, gate_l, gate_c, w, ln_g.reshape(1, d),
      ln_b.reshape(1, d))


FFN_CHUNKS = 1


def _ffn_body(x_ref, xp_ref, xn_ref, sl_ref, cl_ref, gl_ref, sc_ref, cc_ref, gc_ref,
              wu_ref, cw_ref, cb_ref, wd_ref, g_ref, b_ref, o_ref, *, ctx_tiles, n_tiles, alpha, d_ff):
    i = pl.program_id(1)
    is_ctx = i < ctx_tiles
    shift = jnp.where(is_ctx, sc_ref[0], sl_ref[0])
    scale = jnp.where(is_ctx, cc_ref[0], cl_ref[0])
    gate_mod = jnp.where(is_ctx, gc_ref[0], gl_ref[0])
    seg_start = jnp.logical_or(i == 0, i == ctx_tiles)
    seg_end = jnp.logical_or(i == ctx_tiles - 1, i == n_tiles - 1)

    x = x_ref[0]
    xm = (x * (1.0 + scale) + shift).astype(BF16)
    xp = (xp_ref[0] * (1.0 + scale) + shift).astype(BF16)
    xn = (xn_ref[0] * (1.0 + scale) + shift).astype(BF16)
    rows = x.shape[0]
    row_id = lax.broadcasted_iota(jnp.int32, (rows, 1), 0)
    ch = d_ff // FFN_CHUNKS
    f = jnp.zeros(x.shape, F32)
    for c in range(FFN_CHUNKS):
        wg = wu_ref[:, c * ch:(c + 1) * ch]
        wv = wu_ref[:, d_ff + c * ch:d_ff + (c + 1) * ch]
        gate = jnp.dot(xm, wg, preferred_element_type=F32)
        val = jnp.dot(xm, wv, preferred_element_type=F32)
        gp = jnp.dot(xp, wg, preferred_element_type=F32)[7:8, :]
        gn = jnp.dot(xn, wg, preferred_element_type=F32)[0:1, :]
        gp = jnp.where(seg_start, 0.0, gp)
        gn = jnp.where(seg_end, 0.0, gn)
        prev = jnp.where(row_id == 0, gp, pltpu.roll(gate, 1, 0))
        nxt = jnp.where(row_id == rows - 1, gn, pltpu.roll(gate, rows - 1, 0))
        cw = cw_ref[:, c * ch:(c + 1) * ch]
        conv = prev * cw[0:1, :] + gate * cw[1:2, :] + nxt * cw[2:3, :] + cb_ref[:, c * ch:(c + 1) * ch]
        act = (jax.nn.gelu(conv) * val).astype(BF16)
        f = f + jnp.dot(act, wd_ref[c * ch:(c + 1) * ch, :], preferred_element_type=F32)
    o_ref[0] = _layernorm(alpha * x + gate_mod * f, g_ref[...], b_ref[...])


def _ffn(h, mods_l, mods_c, w_up, conv_w, conv_b, w_down, ln_g, ln_b, ctx_tiles, alpha):
    b, t, d = h.shape
    d_ff = w_down.shape[0]
    n_tiles = t // ROW_TILE
    sub = ROW_TILE // 8
    n_sub = t // 8
    row = pl.BlockSpec((1, ROW_TILE, d), lambda bi, i: (bi, i, 0))
    prev = pl.BlockSpec((1, 8, d), lambda bi, i: (bi, jnp.maximum(i * sub - 1, 0), 0))
    nxt = pl.BlockSpec((1, 8, d), lambda bi, i: (bi, jnp.minimum((i + 1) * sub, n_sub - 1), 0))
    lat = pl.BlockSpec((1, 1, d), lambda bi, i: (bi, 0, 0))
    ctx = pl.BlockSpec((1, 1, d), lambda bi, i: (0, 0, 0))
    vec = pl.BlockSpec((1, d), lambda bi, i: (0, 0))
    full = lambda a: pl.BlockSpec(a.shape, lambda bi, i: (0,) * a.ndim)
    cb = conv_b.reshape(1, d_ff)
    return pl.pallas_call(
        functools.partial(_ffn_body, ctx_tiles=ctx_tiles, n_tiles=n_tiles, alpha=alpha, d_ff=d_ff),
        grid=(b, n_tiles),
        in_specs=[row, prev, nxt, lat, lat, lat, ctx, ctx, ctx,
                  full(w_up), full(conv_w), full(cb), full(w_down), vec, vec],
        out_specs=row,
        out_shape=jax.ShapeDtypeStruct((b, t, d), F32),
        compiler_params=_cparams(("parallel", "parallel")),
        name="ffn",
    )(h, h, h, *mods_l, *mods_c, w_up, conv_w, cb, w_down, ln_g.reshape(1, d), ln_b.reshape(1, d))


def _rwkv7(p, tc, mu, w0, w2, a0, a2, g2, k_k, k_a, r_k, lnx_g, lnx_b):
    b = p.shape[0]
    f = _rwkv_features(p, tc, mu, w0, w2, a0, a2, g2, k_k, k_a, r_k)
    col = RWKV_FEATS.index
    y = _scan("rwkv",
              [_to_scan(f, col("r"), col("r"), "k", tc), _to_scan(f, col("w0"), col("w1"), "k", tc),
               _to_scan(f, col("k0"), col("k1"), "k", tc), _to_scan(f, col("a"), col("a"), "k", tc),
               _to_scan(f, col("b0"), col("b1"), "k", tc)],
              _to_scan(f, col("v"), col("v"), "v", tc))
    return _from_scan(y, b, tc, "rwkv", f, (col("bonus"), col("g")), (lnx_g, lnx_b))


def _rope_tables(n):
    rows = n // GRID_W
    row = jnp.repeat(jnp.arange(rows, dtype=F32), GRID_W)
    col = jnp.tile(jnp.arange(GRID_W, dtype=F32), rows)
    n_freq = DIFF_QK_DIM // 4
    inv_freq = ROPE_BASE ** (-jnp.arange(n_freq, dtype=F32) / n_freq)
    ang = jnp.concatenate([row[:, None] * inv_freq, col[:, None] * inv_freq], axis=-1)
    return jnp.cos(ang), jnp.sin(ang)


def _rope(x, cos, sin, tc):
    b, t, gw = x.shape
    half = DIFF_QK_DIM // 2
    cos = jnp.concatenate([jnp.ones((tc, half), F32), cos], axis=0)[:, None, :]
    sin = jnp.concatenate([jnp.zeros((tc, half), F32), sin], axis=0)[:, None, :]
    x = x.reshape(b, t, gw // DIFF_QK_DIM, DIFF_QK_DIM)
    x1, x2 = x[..., :half], x[..., half:]
    return jnp.concatenate([x1 * cos - x2 * sin, x2 * cos + x1 * sin], axis=-1).reshape(b, t, gw)


def _diff_attn(p, tc, lam_vecs, norm_g, layer, need_ctx):
    b, t, _ = p.shape
    gw = GROUP_WIDTH
    lam_init = 0.8 - 0.6 * math.exp(-0.3 * layer)
    lv = lam_vecs.astype(F32)
    lam = jnp.exp(jnp.sum(lv[0] * lv[1])) - jnp.exp(jnp.sum(lv[2] * lv[3])) + lam_init
    cos, sin = _rope_tables(t - tc)
    q = (_rope(p[..., :gw], cos, sin, tc) * (DIFF_QK_DIM ** -0.5 * math.log2(math.e))).astype(BF16)
    k = _rope(p[..., gw:2 * gw], cos, sin, tc).astype(BF16)
    q_tile0 = 0 if need_ctx else tc // ATTN_TILE
    return _attention(q, k, p, 2 * gw // LANES, lam, norm_g, 1.0 - lam_init, q_tile0, tc)


def _hgrn2(p, tc, lb, norm_g):
    b = p.shape[0]
    f = _hgrn_features(p, lb)
    col = HGRN_FEATS.index
    o = _scan("gla",
              [_to_scan(f, col("q"), col("q"), "k", tc), _to_scan(f, col("w0"), col("w1"), "k", tc),
               _to_scan(f, col("k0"), col("k1"), "k", tc)],
              _to_scan(p, 3, 3, "v", tc))
    return _from_scan(o, b, tc, "gla", f, (col("gate"),), (jnp.tile(norm_g, N_HEADS),))


RWKV_COLS = 3 * GROUP_WIDTH + 2 * RWKV_DECAY_LORA + 2 * RWKV_ICLR_LORA + RWKV_GATE_LORA
DIFF_COLS = 3 * GROUP_WIDTH
HGRN_COLS = 5 * GROUP_WIDTH
MLSTM_COLS = 4 * GROUP_WIDTH + 4 * N_HEADS
SLAB_COLS = (RWKV_COLS, DIFF_COLS, HGRN_COLS, MLSTM_COLS)
SLAB_PAD = (5 * GROUP_WIDTH, DIFF_COLS, HGRN_COLS, 4 * GROUP_WIDTH + LANES)


def _split_cols(w):
    parts, off = [], 0
    for n, n_pad in zip(SLAB_COLS, SLAB_PAD):
        parts.append(jnp.pad(w[:, off:off + n], ((0, 0), (0, n_pad - n))).astype(BF16))
        off += n
    return parts


def kernel(x, c, ctx, c_ctx, ada_w, ada_b, w_in, rwkv_mu, rwkv_w0, rwkv_w2, rwkv_a0, rwkv_a2, rwkv_g2, rwkv_k_k, rwkv_k_a, rwkv_r_k, rwkv_lnx_g, rwkv_lnx_b, diff_lambda, diff_norm_g, hgrn_lb_logits, hgrn_norm_g, mlstm_conv_w, mlstm_conv_b, mlstm_gate_b, mlstm_norm_g, w_out, ffn_w_up, ffn_conv_w, ffn_conv_b, ffn_w_down, ln_g, ln_b):
    depth = w_in.shape[0]
    b, seq, d = x.shape
    tc = ctx.shape[1]
    assert tc == ROW_TILE == ATTN_TILE and seq % ROW_TILE == 0 and 2 * b * N_HEADS <= LANES
    alpha = (2.0 * depth) ** 0.25

    lb_w = jax.nn.softmax(hgrn_lb_logits.astype(F32), axis=0)
    lower_bounds = jnp.cumsum(lb_w, axis=0) - lb_w[0]

    cond = jnp.zeros((8, d), F32).at[:b].set(c).at[b].set(c_ctx)
    h = jnp.concatenate([ctx, x], axis=1)
    for layer in range(depth):
        last = layer == depth - 1
        mod = _adaln(cond, ada_w, ada_b, layer).reshape(8, 6, d)
        mods_l = [mod[:b, j][:, None, :] for j in range(6)]
        mods_c = [mod[b:b + 1, j][:, None, :] for j in range(6)]

        p_rwkv, p_diff, p_hgrn, p_mlstm = _inproj(h, mods_l[0], mods_l[1], mods_c[0], mods_c[1],
                                                  _split_cols(w_in[layer]), tc // ROW_TILE)
        y_a = _rwkv7(p_rwkv, tc, rwkv_mu[layer], rwkv_w0[layer], rwkv_w2[layer], rwkv_a0[layer], rwkv_a2[layer],
                     rwkv_g2[layer], rwkv_k_k[layer], rwkv_k_a[layer], rwkv_r_k[layer], rwkv_lnx_g[layer],
                     rwkv_lnx_b[layer])
        y_b = _diff_attn(p_diff, tc, diff_lambda[layer], diff_norm_g[layer], layer, not last)
        y_c = _hgrn2(p_hgrn, tc, lower_bounds[layer], hgrn_norm_g[layer])
        h_f, h_b = (_mlstm_chunked(p_mlstm, mlstm_conv_w[layer], mlstm_conv_b[layer], mlstm_gate_b[layer], rev, tc)
                    for rev in (False, True))

        tile0, ctx_tiles = (tc // ROW_TILE, 0) if last else (0, tc // ROW_TILE)
        h = _outproj(y_a, y_b, y_c, h_f, h_b, p_mlstm, mlstm_norm_g[layer], h, mods_l[2], mods_c[2],
                     w_out[layer].astype(BF16), ln_g[layer, 0], ln_b[layer, 0], tile0, ctx_tiles, alpha)
        h = _ffn(h, mods_l[3:6], mods_c[3:6], ffn_w_up[layer].astype(BF16), ffn_conv_w[layer], ffn_conv_b[layer],
                 ffn_w_down[layer].astype(BF16), ln_g[layer, 1], ln_b[layer, 1], ctx_tiles, alpha)
    return h
```

```python
import functools
import math

import jax
import jax.numpy as jnp
import numpy as np
from jax import lax
from jax.experimental import pallas as pl
from jax.experimental.pallas import tpu as pltpu

F32 = jnp.float32
BF16 = jnp.bfloat16

HEAD_DIM = 64
N_HEADS = 4
GROUP_WIDTH = N_HEADS * HEAD_DIM
DIFF_QK_DIM = HEAD_DIM // 2
GRID_W = 64
ROPE_BASE = 10000.0
RWKV_DECAY_LORA = 64
RWKV_ICLR_LORA = 64
RWKV_GATE_LORA = 160
RWKV_LNX_EPS = 64e-5
LN_EPS = 1e-5
LB_FLOOR = 1e-30
MAX_NEG_LOG_STAB = 60.0

LANES = 128
ROW_TILE = 256
ATTN_TILE = 256
TIME_BLOCK = 128
SCAN_TB = 64
VMEM_LIMIT = 56 * 1024 * 1024


def _cparams(sem):
    return pltpu.CompilerParams(dimension_semantics=sem, vmem_limit_bytes=VMEM_LIMIT)


def _rev_block(i, n_ctx, n_all):
    return jnp.where(i < n_ctx, n_ctx - 1 - i, n_ctx + n_all - 1 - i)


def _split3(x):
    hi = x.astype(BF16)
    r1 = x - hi.astype(F32)
    mid = r1.astype(BF16)
    return hi, mid, (r1 - mid.astype(F32)).astype(BF16)


def _dot01_left(a01, x):
    a = a01.astype(BF16)
    p = [jnp.dot(a, part, preferred_element_type=F32) for part in _split3(x)]
    return (p[0] + p[1]) + p[2]


def _dot01_right(x, b01):
    b = b01.astype(BF16)
    p = [jnp.dot(part, b, preferred_element_type=F32) for part in _split3(x)]
    return (p[0] + p[1]) + p[2]


def _flip_rows(x):
    n = x.shape[0]
    r = lax.broadcasted_iota(jnp.int32, (n, n), 0)
    c = lax.broadcasted_iota(jnp.int32, (n, n), 1)
    return _dot01_left(jnp.where(r + c == n - 1, 1.0, 0.0).astype(F32), x)


def _adaln_body(c_ref, w_ref, b_ref, o_ref):
    x = c_ref[...]
    x = (x * jax.nn.sigmoid(x)).astype(BF16)
    o_ref[...] = jnp.dot(x, w_ref[0].astype(BF16), preferred_element_type=F32) + b_ref[0]


def _adaln(cond, w, b, layer):
    m, d = cond.shape
    n = w.shape[2]
    tn = 1536
    return pl.pallas_call(
        _adaln_body,
        grid=(n // tn,),
        in_specs=[pl.BlockSpec((m, d), lambda j: (0, 0)),
                  pl.BlockSpec((1, d, tn), lambda j: (layer, 0, j)),
                  pl.BlockSpec((1, 1, tn), lambda j: (layer, 0, j))],
        out_specs=pl.BlockSpec((m, tn), lambda j: (0, j)),
        out_shape=jax.ShapeDtypeStruct((m, n), F32),
        compiler_params=_cparams(("arbitrary",)),
        name="adaln",
    )(cond, w, b.reshape(b.shape[0], 1, n))


def _softplus(x):
    return jnp.maximum(x, 0.0) + jnp.log(1.0 + jnp.exp(-jnp.abs(x)))


def _head_block(scale=1.0):
    r = lax.broadcasted_iota(jnp.int32, (GROUP_WIDTH, GROUP_WIDTH), 0) // HEAD_DIM
    c = lax.broadcasted_iota(jnp.int32, (GROUP_WIDTH, GROUP_WIDTH), 1) // HEAD_DIM
    return jnp.where(r == c, scale, 0.0).astype(F32)


def _head_sum(x, block_bf16):
    hi = x.astype(BF16)
    lo = (x - hi.astype(F32)).astype(BF16)
    return (jnp.dot(hi, block_bf16, preferred_element_type=F32)
            + jnp.dot(lo, block_bf16, preferred_element_type=F32))


def _shifted_rows(x, prev8, next8, seg_start, seg_end):
    rows = x.shape[0]
    row_id = lax.broadcasted_iota(jnp.int32, (rows, 1), 0)
    prev_row = jnp.where(seg_start, 0.0, prev8[7:8, :])
    next_row = jnp.where(seg_end, 0.0, next8[0:1, :])
    prev = jnp.where(row_id == 0, prev_row, pltpu.roll(x, 1, 0))
    nxt = jnp.where(row_id == rows - 1, next_row, pltpu.roll(x, rows - 1, 0))
    return prev, nxt


RWKV_FEATS = ("r", "v", "w0", "w1", "k0", "k1", "a", "b0", "b1", "g", "bonus")


def _rwkv_feats(p, prev8, next8, seg_start, seg_end, mu_ref, w0_ref, w2_ref, a0_ref, a2_ref, g2_ref, kk_ref,
                ka_ref, rk_ref):
    gw = GROUP_WIDTH
    prev, nxt = _shifted_rows(p, prev8, next8, seg_start, seg_end)
    p = p + (0.5 * (prev + nxt) - p) * mu_ref[...]
    r, k, v = p[:, :gw], p[:, gw:2 * gw], p[:, 2 * gw:3 * gw]
    wd, ad, gd = p[:, 3 * gw:3 * gw + LANES], p[:, 3 * gw + LANES:4 * gw], p[:, 4 * gw:5 * gw]
    lora_w = jnp.dot(jnp.tanh(wd).astype(BF16), w2_ref[...], preferred_element_type=F32)
    lora_a = jnp.dot(ad.astype(BF16), a2_ref[...], preferred_element_type=F32)
    decay = jnp.exp(-jnp.exp(-_softplus(-(w0_ref[...] + lora_w)) - 0.5))
    a = jax.nn.sigmoid(a0_ref[...] + lora_a)
    g = jnp.dot(jax.nn.sigmoid(gd).astype(BF16), g2_ref[...], preferred_element_type=F32)
    block = _head_block().astype(BF16)
    kk = k * kk_ref[...]
    kk = kk / jnp.maximum(jnp.sqrt(_head_sum(kk * kk, block)), 1e-12)
    k0 = k * (1.0 + (a[:, :gw] - 1.0) * ka_ref[...])
    k1 = k * (1.0 + (a[:, gw:] - 1.0) * ka_ref[...])
    bonus = _head_sum(r * (k0 + k1) * rk_ref[...], block) * v
    return dict(r=r, v=v, w0=decay[:, :gw], w1=decay[:, gw:], k0=k0, k1=k1, a=-kk, b0=kk * a[:, :gw],
                b1=kk * a[:, gw:], g=g, bonus=bonus)


def _block_diag2(w):
    z = jnp.zeros_like(w[0])
    return jnp.concatenate([jnp.concatenate([w[0], z], axis=1), jnp.concatenate([z, w[1]], axis=1)], axis=0)


HGRN_FEATS = ("q", "w0", "w1", "k0", "k1", "gate", "v")


def _hgrn_feats(p, llb_ref, l1m_ref, oml_ref):
    gw = GROUP_WIDTH
    q, f_f, f_b, v, g = (p[:, j * gw:(j + 1) * gw] for j in range(5))
    feats = dict(q=q * jax.nn.sigmoid(q), gate=g * jax.nn.sigmoid(g), v=v)
    for d, f in enumerate((f_f, f_b)):
        x = llb_ref[...]
        y = l1m_ref[...] - _softplus(-f)
        log_f = jnp.maximum(x, y) + jnp.log(1.0 + jnp.exp(-jnp.abs(x - y)))
        feats["w%d" % d] = jnp.exp(log_f)
        feats["k%d" % d] = oml_ref[...] * jax.nn.sigmoid(-f)
    return feats


N_RWKV_PRM, N_HGRN_PRM = 9, 3


def _inproj_body(x_ref, xp_ref, xn_ref, sl_ref, cl_ref, sc_ref, cc_ref, wr_ref, wd_ref, wh_ref, wm_ref, *refs,
                 ctx_tiles, n_tiles):
    rwkv_prm, hgrn_prm = refs[:N_RWKV_PRM], refs[N_RWKV_PRM:N_RWKV_PRM + N_HGRN_PRM]
    fr_ref, pd_ref, fh_ref, pm_ref = refs[N_RWKV_PRM + N_HGRN_PRM:]
    i = pl.program_id(1)
    is_ctx = i < ctx_tiles
    seg_start = jnp.logical_or(i == 0, i == ctx_tiles)
    seg_end = jnp.logical_or(i == ctx_tiles - 1, i == n_tiles - 1)
    shift = jnp.where(is_ctx, sc_ref[0], sl_ref[0])
    scale = jnp.where(is_ctx, cc_ref[0], cl_ref[0])
    mod = lambda x: (x * (1.0 + scale) + shift).astype(BF16)
    proj = lambda x, w_ref: jnp.dot(x, w_ref[...], preferred_element_type=F32)
    gw = GROUP_WIDTH
    xm = mod(x_ref[0])
    feats = _rwkv_feats(proj(xm, wr_ref), proj(mod(xp_ref[0]), wr_ref), proj(mod(xn_ref[0]), wr_ref),
                        seg_start, seg_end, *rwkv_prm)
    for j, name in enumerate(RWKV_FEATS):
        fr_ref[0, :, j * gw:(j + 1) * gw] = feats[name]
    pd_ref[0] = proj(xm, wd_ref)
    feats = _hgrn_feats(proj(xm, wh_ref), *hgrn_prm)
    for j, name in enumerate(HGRN_FEATS):
        fh_ref[0, :, j * gw:(j + 1) * gw] = feats[name]
    pm_ref[0] = proj(xm, wm_ref)


def _inproj(h, shift_l, scale_l, shift_c, scale_c, ws, rwkv_prm, lb, ctx_tiles):
    b, t, d = h.shape
    gw = GROUP_WIDTH
    n_tiles, sub, n_sub = t // ROW_TILE, ROW_TILE // 8, t // 8
    row = lambda n: pl.BlockSpec((1, ROW_TILE, n), lambda bi, i: (bi, i, 0))
    prev = pl.BlockSpec((1, 8, d), lambda bi, i: (bi, jnp.maximum(i * sub - 1, 0), 0))
    nxt = pl.BlockSpec((1, 8, d), lambda bi, i: (bi, jnp.minimum((i + 1) * sub, n_sub - 1), 0))
    lat = pl.BlockSpec((1, 1, d), lambda bi, i: (bi, 0, 0))
    ctx = pl.BlockSpec((1, 1, d), lambda bi, i: (0, 0, 0))
    full = lambda a: pl.BlockSpec(a.shape, lambda bi, i: (0,) * a.ndim)
    vec = lambda a: a.reshape(1, -1).astype(F32)
    mu, w0, w2, a0, a2, g2, k_k, k_a, r_k = rwkv_prm
    cols = ws[0].shape[1]
    prm = [jnp.pad(vec(mu), ((0, 0), (0, cols - mu.shape[0]))), vec(w0), _block_diag2(w2).astype(BF16), vec(a0),
           _block_diag2(a2).astype(BF16), jnp.pad(g2, ((0, gw - g2.shape[0]), (0, 0))).astype(BF16), vec(k_k),
           vec(k_a), vec(r_k),
           vec(jnp.log(jnp.maximum(lb, LB_FLOOR))), vec(jnp.log1p(-lb)), vec(1.0 - lb)]
    assert len(prm) == N_RWKV_PRM + N_HGRN_PRM
    widths = (len(RWKV_FEATS) * gw, ws[1].shape[1], len(HGRN_FEATS) * gw, ws[3].shape[1])
    return pl.pallas_call(
        functools.partial(_inproj_body, ctx_tiles=ctx_tiles, n_tiles=n_tiles),
        grid=(b, n_tiles),
        in_specs=[row(d), prev, nxt, lat, lat, ctx, ctx] + [full(a) for a in list(ws) + prm],
        out_specs=[row(n) for n in widths],
        out_shape=[jax.ShapeDtypeStruct((b, t, n), F32) for n in widths],
        compiler_params=_cparams(("parallel", "parallel")),
        name="inproj",
    )(h, h, h, shift_l, scale_l, shift_c, scale_c, *ws, *prm)


def _to_scan_body(x0_ref, x1_ref, o_ref, r_ref, *, kind, nb):
    n_scan = 2 * nb * N_HEADS
    rep = LANES // n_scan
    for b in range(nb):
        r_ref[pl.ds(b * GROUP_WIDTH, GROUP_WIDTH), :] = x0_ref[b].T
        r_ref[pl.ds((nb + b) * GROUP_WIDTH, GROUP_WIDTH), :] = _flip_rows(x1_ref[b]).T
    if kind == "k":
        for k in range(HEAD_DIM):
            rows = r_ref[pl.ds(k, n_scan, stride=HEAD_DIM), :]
            o_ref[k] = jnp.concatenate([rows] * rep, axis=0).T
    else:
        for vh in range(HEAD_DIM // rep):
            rows = [r_ref[pl.ds(vh * rep + vl, n_scan, stride=HEAD_DIM), :] for vl in range(rep)]
            o_ref[:, vh, :] = jnp.concatenate(rows, axis=0).T


def _to_scan(x, col0, col1, kind, tc):
    nb, t, _ = x.shape
    gw = GROUP_WIDTH
    x0 = x1 = x
    n_all, n_ctx = t // TIME_BLOCK, tc // TIME_BLOCK
    n_scan = 2 * nb * N_HEADS
    rep = LANES // n_scan
    fwd = pl.BlockSpec((nb, TIME_BLOCK, gw), lambda i: (0, i, col0))
    bwd = pl.BlockSpec((nb, TIME_BLOCK, gw), lambda i: (0, _rev_block(i, n_ctx, n_all), col1))
    if kind == "k":
        out_shape = (HEAD_DIM, t, LANES)
        out_spec = pl.BlockSpec((HEAD_DIM, TIME_BLOCK, LANES), lambda i: (0, i, 0))
    else:
        out_shape = (t, HEAD_DIM // rep, LANES)
        out_spec = pl.BlockSpec((TIME_BLOCK, HEAD_DIM // rep, LANES), lambda i: (i, 0, 0))
    return pl.pallas_call(
        functools.partial(_to_scan_body, kind=kind, nb=nb),
        grid=(n_all,),
        in_specs=[fwd, bwd],
        out_specs=out_spec,
        out_shape=jax.ShapeDtypeStruct(out_shape, F32),
        scratch_shapes=[pltpu.VMEM((n_scan * HEAD_DIM, TIME_BLOCK), F32)],
        compiler_params=_cparams(("parallel",)),
        name="to_scan_" + kind,
    )(x0, x1)


def _from_scan_body(yf_ref, yb_ref, *rest, nb, readout):
    if readout == "rwkv":
        bonus_ref, g_ref, ng_ref, nb_ref, o_ref, r_ref = rest
    else:
        gate_ref, ng_ref, o_ref, r_ref = rest
    n_scan = 2 * nb * N_HEADS
    rep = LANES // n_scan
    for vh in range(HEAD_DIM // rep):
        r_ref[0, pl.ds(vh * LANES, LANES), :] = yf_ref[:, vh, :].T
        r_ref[1, pl.ds(vh * LANES, LANES), :] = yb_ref[:, vh, :].T
    mean_block = _head_block(1.0 / HEAD_DIM).astype(BF16)
    for b in range(nb):
        slab = lambda d: jnp.concatenate(
            [r_ref[d, pl.ds((d * nb + b) * N_HEADS + h, HEAD_DIM, stride=n_scan), :] for h in range(N_HEADS)],
            axis=0).T
        y = slab(0) + _flip_rows(slab(1))
        if readout == "rwkv":
            y = y - _head_sum(y, mean_block)
            y = y * lax.rsqrt(_head_sum(y * y, mean_block) + RWKV_LNX_EPS) * ng_ref[...] + nb_ref[...]
            o_ref[b] = (y + bonus_ref[b]) * g_ref[b]
        else:
            y = y * lax.rsqrt(_head_sum(y * y, mean_block) + LN_EPS) * ng_ref[...]
            o_ref[b] = y * gate_ref[b]


def _from_scan(y, nb, tc, readout, feats, cols, vecs):
    t, n_vh, _ = y.shape
    gw = GROUP_WIDTH
    n_all, n_ctx = t // TIME_BLOCK, tc // TIME_BLOCK
    fwd = pl.BlockSpec((TIME_BLOCK, n_vh, LANES), lambda i: (i, 0, 0))
    bwd = pl.BlockSpec((TIME_BLOCK, n_vh, LANES), lambda i: (_rev_block(i, n_ctx, n_all), 0, 0))
    tok = lambda c: pl.BlockSpec((nb, TIME_BLOCK, gw), lambda i: (0, i, c))
    vec = pl.BlockSpec((1, gw), lambda i: (0, 0))
    return pl.pallas_call(
        functools.partial(_from_scan_body, nb=nb, readout=readout),
        grid=(n_all,),
        in_specs=[fwd, bwd] + [tok(c) for c in cols] + [vec] * len(vecs),
        out_specs=tok(0),
        out_shape=jax.ShapeDtypeStruct((nb, t, gw), F32),
        scratch_shapes=[pltpu.VMEM((2, n_vh * LANES, TIME_BLOCK), F32)],
        compiler_params=_cparams(("parallel",)),
        name="from_scan_" + readout,
    )(y, y, *([feats] * len(cols)), *[v.reshape(1, gw) for v in vecs])


N_ACC = 4


def _acc_add(acc, i, x):
    acc[i % N_ACC] = x if acc[i % N_ACC] is None else acc[i % N_ACC] + x


def _acc_total(acc):
    return (acc[0] + acc[1]) + (acc[2] + acc[3])


def _scan_body(*refs, mode, tb, n_vh):
    if mode == "rwkv":
        r_ref, w_ref, k_ref, a_ref, b_ref, v_ref, y_ref, s_ref = refs
    else:
        r_ref, w_ref, k_ref, v_ref, y_ref, s_ref = refs

    @pl.when(pl.program_id(0) == 0)
    def _():
        s_ref[...] = jnp.zeros_like(s_ref)

    row = lambda ref, k, j: ref[k, pl.ds(j, 1), :]
    wide = lambda x: jnp.broadcast_to(x, (n_vh, LANES))

    def rwkv_step(j, sa):
        v = v_ref[j]
        j_next = jnp.minimum(j + 1, tb - 1)
        ys, sas = [None] * N_ACC, [None] * N_ACC
        for k in range(HEAD_DIM):
            s = s_ref[k] * wide(row(w_ref, k, j)) + sa * wide(row(b_ref, k, j)) + v * wide(row(k_ref, k, j))
            s_ref[k] = s
            _acc_add(ys, k, s * wide(row(r_ref, k, j)))
            _acc_add(sas, k, s * wide(row(a_ref, k, j_next)))
        y_ref[j] = _acc_total(ys)
        return _acc_total(sas)

    def gla_step(j, carry):
        v = v_ref[j]
        ys = [None] * N_ACC
        for k in range(HEAD_DIM):
            s = s_ref[k] * wide(row(w_ref, k, j)) + v * wide(row(k_ref, k, j))
            s_ref[k] = s
            _acc_add(ys, k, s * wide(row(r_ref, k, j)))
        y_ref[j] = _acc_total(ys)
        return carry

    if mode == "rwkv":
        sa0 = [None] * N_ACC
        for k in range(HEAD_DIM):
            _acc_add(sa0, k, s_ref[k] * wide(row(a_ref, k, 0)))
        lax.fori_loop(0, tb, rwkv_step, _acc_total(sa0), unroll=8)
    else:
        lax.fori_loop(0, tb, gla_step, 0, unroll=8)


def _scan(mode, k_inputs, v_in):
    t, n_vh, _ = v_in.shape
    tb = SCAN_TB
    big = pl.BlockSpec((HEAD_DIM, tb, LANES), lambda i: (0, i, 0))
    small = pl.BlockSpec((tb, n_vh, LANES), lambda i: (i, 0, 0))
    return pl.pallas_call(
        functools.partial(_scan_body, mode=mode, tb=tb, n_vh=n_vh),
        grid=(t // tb,),
        in_specs=[big] * len(k_inputs) + [small],
        out_specs=small,
        out_shape=jax.ShapeDtypeStruct((t, n_vh, LANES), F32),
        scratch_shapes=[pltpu.VMEM((HEAD_DIM, n_vh, LANES), F32)],
        compiler_params=_cparams(("arbitrary",)),
        name="scan_" + mode,
    )(*k_inputs, v_in)


MLSTM_CHUNK = 64


def _prefix_max(x, reverse):
    n = x.shape[0]
    row = lax.broadcasted_iota(jnp.int32, x.shape, 0)
    sh = 1
    while sh < n:
        if reverse:
            x = jnp.where(row < n - sh, jnp.maximum(x, pltpu.roll(x, n - sh, 0)), x)
        else:
            x = jnp.where(row >= sh, jnp.maximum(x, pltpu.roll(x, sh, 0)), x)
        sh *= 2
    return x


def _mlstm_body(qk_ref, qkp_ref, qkn_ref, v_ref, gt_ref, cw_ref, cb_ref, sel_ref, gb_ref, o_ref, st_ref, m_ref, *,
                reverse, n_sub, n_ctx, n_blocks):
    c, gw = MLSTM_CHUNK, GROUP_WIDTH
    i = pl.program_id(1)

    @pl.when(i == 0)
    def _():
        st_ref[...] = jnp.zeros_like(st_ref)
        m_ref[...] = jnp.zeros_like(m_ref)

    blk = _rev_block(i, n_ctx, n_blocks) if reverse else i
    seg_start = jnp.logical_or(blk == 0, blk == n_ctx)
    seg_end = jnp.logical_or(blk == n_ctx - 1, blk == n_blocks - 1)
    qk = qk_ref[0]
    prev, nxt = _shifted_rows(qk, qkp_ref[0], qkn_ref[0], seg_start, seg_end)
    qk = prev * cw_ref[0:1, :] + qk * cw_ref[1:2, :] + nxt * cw_ref[2:3, :] + cb_ref[...]
    qk = qk * jax.nn.sigmoid(qk)
    q_all, k_all = qk[:, :gw], qk[:, gw:] * HEAD_DIM ** -0.5
    v_all = v_ref[0]
    gt = gt_ref[0]
    ig_all = _dot01_right(gt, sel_ref[0]) + gb_ref[0:1, :]
    fg_all = -_softplus(-(_dot01_right(gt, sel_ref[1]) + gb_ref[1:2, :]))

    tt = lax.broadcasted_iota(jnp.int32, (c, c), 0)
    ss = lax.broadcasted_iota(jnp.int32, (c, c), 1)
    tri = jnp.where((ss >= tt) if reverse else (ss <= tt), 1.0, 0.0).astype(F32)
    row = lax.broadcasted_iota(jnp.int32, (c, gw), 0)
    s_of_lane = lax.broadcasted_iota(jnp.int32, (c, gw), 1) % c
    causal = (s_of_lane >= row) if reverse else (s_of_lane <= row)
    diag = s_of_lane == row
    block = _head_block()
    block2 = jnp.concatenate([block, block], axis=1)
    ones8 = jnp.ones((8, c), F32)
    nt = (((1,), (1,)), ((), ()))
    last = 0 if reverse else c - 1
    for u in (range(n_sub - 1, -1, -1) if reverse else range(n_sub)):
        sl = slice(u * c, (u + 1) * c)
        q, k, v, ig, fg = q_all[sl], k_all[sl], v_all[sl], ig_all[sl], fg_all[sl]
        b = _dot01_left(tri, fg)
        g = ig - b
        m_prev = m_ref[0:1, :]
        m_t = b + jnp.maximum(m_prev, _prefix_max(g, reverse))
        w_inter = jnp.exp(b + m_prev - m_t)
        g_row = _dot01_left(ones8, jnp.where(diag, g, 0.0))[0:1]
        qb = q.astype(BF16)
        k_bd = (jnp.concatenate([k] * N_HEADS, axis=0) * block).astype(BF16)
        scores = lax.dot_general(qb, k_bd, nt, preferred_element_type=F32)
        w = jnp.where(causal, jnp.exp((b - m_t) + g_row), 0.0) * scores
        v_bd = jnp.concatenate([jnp.concatenate([v] * N_HEADS, axis=0) * block, block], axis=1).astype(BF16)
        intra = jnp.dot(w.astype(BF16), v_bd, preferred_element_type=F32)
        inter = jnp.dot(qb, st_ref[...].astype(BF16), preferred_element_type=F32)
        num = w_inter * inter[:, :gw] + intra[:, :gw]
        den = w_inter * inter[:, gw:] + intra[:, gw:]
        floor = jnp.exp(jnp.minimum(-m_t, MAX_NEG_LOG_STAB))
        o_ref[0, pl.ds(u * c, c), :] = num / jnp.maximum(jnp.abs(den), floor)
        m_new = m_t[last:last + 1]
        b_end = b[last:last + 1]
        kw = k * jnp.exp(b_end - b + ig - m_new)
        decay = jnp.exp(b_end + m_prev - m_new)
        v_one = jnp.concatenate([v, jnp.ones_like(v)], axis=1).astype(BF16)
        upd = jnp.dot(kw.T.astype(BF16), v_one, preferred_element_type=F32)
        st_ref[...] = st_ref[...] * jnp.concatenate([decay, decay], axis=1) + upd * block2
        m_ref[0:1, :] = m_new


def _mlstm_chunked(p, conv_w, conv_b, gate_b, reverse, tc):
    b, t, _ = p.shape
    gw = GROUP_WIDTH
    n_blocks, n_ctx = t // ROW_TILE, tc // ROW_TILE
    sub, n_sub8 = ROW_TILE // 8, t // 8
    blk = (lambda i: _rev_block(i, n_ctx, n_blocks)) if reverse else (lambda i: i)
    d = 1 if reverse else 0
    lane_head = np.arange(gw) // HEAD_DIM
    sel = np.zeros((2, LANES, gw), np.float32)
    for j, base in enumerate((d * N_HEADS, (2 + d) * N_HEADS)):
        sel[j, base + lane_head, np.arange(gw)] = 1.0
    gate_bias = jnp.stack([jnp.repeat(gate_b[d], HEAD_DIM), jnp.repeat(gate_b[2 + d], HEAD_DIM)], axis=0)
    full = lambda a: pl.BlockSpec(a.shape, lambda bi, i: (0,) * a.ndim)
    cb = conv_b.reshape(1, 2 * gw)
    return pl.pallas_call(
        functools.partial(_mlstm_body, reverse=reverse, n_sub=ROW_TILE // MLSTM_CHUNK, n_ctx=n_ctx,
                          n_blocks=n_blocks),
        grid=(b, n_blocks),
        in_specs=[pl.BlockSpec((1, ROW_TILE, 2 * gw), lambda bi, i: (bi, blk(i), 0)),
                  pl.BlockSpec((1, 8, 2 * gw), lambda bi, i: (bi, jnp.maximum(blk(i) * sub - 1, 0), 0)),
                  pl.BlockSpec((1, 8, 2 * gw), lambda bi, i: (bi, jnp.minimum((blk(i) + 1) * sub, n_sub8 - 1), 0)),
                  pl.BlockSpec((1, ROW_TILE, gw), lambda bi, i: (bi, blk(i), 2)),
                  pl.BlockSpec((1, ROW_TILE, LANES), lambda bi, i: (bi, blk(i), 4 * gw // LANES)),
                  full(conv_w), full(cb), full(sel), full(gate_bias)],
        out_specs=pl.BlockSpec((1, ROW_TILE, gw), lambda bi, i: (bi, blk(i), 0)),
        out_shape=jax.ShapeDtypeStruct((b, t, gw), F32),
        scratch_shapes=[pltpu.VMEM((gw, 2 * gw), F32), pltpu.VMEM((8, gw), F32)],
        compiler_params=_cparams(("parallel", "arbitrary")),
        name="mlstm_chunk",
    )(p, p, p, p, p, conv_w, cb, jnp.asarray(sel), gate_bias)


def _attn_body(q_ref, k_ref, v_ref, lam_ref, g_ref, o_ref, *, out_scale, ctx_tiles, n_ctx_keys):
    def attend(n_keys):
        q = q_ref[0]
        k = k_ref[0, :n_keys, :]
        v = v_ref[0, :n_keys, :].astype(BF16)
        lane = lax.broadcasted_iota(jnp.int32, q.shape, 1)
        comp = lane // DIFF_QK_DIM
        nt = (((1,), (1,)), ((), ()))
        n_maps = LANES // DIFF_QK_DIM
        ss = [lax.dot_general(jnp.where(comp == c, q, 0.0).astype(BF16), k, nt, preferred_element_type=F32)
              for c in range(n_maps)]
        es = [jnp.exp2(s - jnp.max(s, axis=-1, keepdims=True)) for s in ss]
        ls = [jnp.sum(e, axis=-1, keepdims=True) for e in es]
        ys = []
        for hh in range(LANES // HEAD_DIM):
            e0, e1, l0, l1 = es[2 * hh], es[2 * hh + 1], ls[2 * hh], ls[2 * hh + 1]
            pr = (e0 - e1 * (lam_ref[0:1, 0:1] * l0 / l1)).astype(BF16)
            ys.append(jnp.dot(pr, v, preferred_element_type=F32) * (1.0 / l0))
        first = lane < HEAD_DIM
        y = jnp.where(first, ys[0], ys[1])
        ysq = y * y
        ms = jnp.where(first, jnp.sum(jnp.where(first, ysq, 0.0), axis=-1, keepdims=True),
                       jnp.sum(jnp.where(first, 0.0, ysq), axis=-1, keepdims=True)) * (1.0 / HEAD_DIM)
        o_ref[0] = y * lax.rsqrt(ms + LN_EPS) * g_ref[...] * out_scale

    if ctx_tiles == 0:
        attend(k_ref.shape[1])
    else:
        is_ctx = pl.program_id(2) < ctx_tiles
        pl.when(is_ctx)(lambda: attend(n_ctx_keys))
        pl.when(jnp.logical_not(is_ctx))(lambda: attend(k_ref.shape[1]))


def _attention(q, k, v, v_col0, lam, norm_g, out_scale, q_tile0, tc):
    b, t, gw = q.shape
    n_tiles = t // ATTN_TILE - q_tile0
    ctx_tiles = max(tc // ATTN_TILE - q_tile0, 0)
    qspec = pl.BlockSpec((1, ATTN_TILE, LANES), lambda bi, pi, i: (bi, q_tile0 + i, pi))
    kspec = pl.BlockSpec((1, t, LANES), lambda bi, pi, i: (bi, 0, pi))
    vspec = pl.BlockSpec((1, t, LANES), lambda bi, pi, i: (bi, 0, v_col0 + pi))
    vec = pl.BlockSpec((1, LANES), lambda bi, pi, i: (0, 0))
    return pl.pallas_call(
        functools.partial(_attn_body, out_scale=out_scale, ctx_tiles=ctx_tiles, n_ctx_keys=tc),
        grid=(b, gw // LANES, n_tiles),
        in_specs=[qspec, kspec, vspec, vec, vec],
        out_specs=pl.BlockSpec((1, ATTN_TILE, LANES), lambda bi, pi, i: (bi, i, pi)),
        out_shape=jax.ShapeDtypeStruct((b, n_tiles * ATTN_TILE, gw), F32),
        compiler_params=_cparams(("parallel", "parallel", "parallel")),
        name="diff_attn",
    )(q, k, v, jnp.full((1, LANES), lam, F32), jnp.tile(norm_g.reshape(1, HEAD_DIM), (1, LANES // HEAD_DIM)))


def _layernorm(z, g, b):
    z = z - jnp.mean(z, axis=-1, keepdims=True)
    return z * lax.rsqrt(jnp.mean(z * z, axis=-1, keepdims=True) + LN_EPS) * g + b


def _outproj_body(ya_ref, yb_ref, yc_ref, hf_ref, hb_ref, og_ref, ng_ref, h_ref, gl_ref, gc_ref, w_ref, g_ref,
                  b_ref, o_ref, *, ctx_tiles, alpha):
    is_ctx = pl.program_id(1) < ctx_tiles
    gate = jnp.where(is_ctx, gc_ref[0], gl_ref[0])
    mean_block = _head_block(1.0 / HEAD_DIM).astype(BF16)
    yd = hf_ref[0] + hb_ref[0]
    yd = yd - _head_sum(yd, mean_block)
    yd = yd * lax.rsqrt(_head_sum(yd * yd, mean_block) + LN_EPS) * ng_ref[...] * jax.nn.sigmoid(og_ref[0])
    y = None
    for m, ym in enumerate((ya_ref[0], yb_ref[0], yc_ref[0], yd)):
        part = jnp.dot(ym.astype(BF16), w_ref[m * GROUP_WIDTH:(m + 1) * GROUP_WIDTH, :],
                       preferred_element_type=F32)
        y = part if y is None else y + part
    o_ref[0] = _layernorm(alpha * h_ref[0] + gate * y, g_ref[...], b_ref[...])


def _outproj(ya, yb, yc, hf, hb, p_mlstm, mlstm_norm_g, h, gate_l, gate_c, w, ln_g, ln_b, tile0, ctx_tiles, alpha):
    b, t, d = h.shape
    gw = GROUP_WIDTH
    n_tiles = t // ROW_TILE - tile0
    yrow = lambda c: pl.BlockSpec((1, ROW_TILE, gw), lambda bi, i: (bi, tile0 + i, c))
    yb_row = pl.BlockSpec((1, ROW_TILE, gw), lambda bi, i: (bi, i, 0))
    hrow = pl.BlockSpec((1, ROW_TILE, d), lambda bi, i: (bi, tile0 + i, 0))
    lat = pl.BlockSpec((1, 1, d), lambda bi, i: (bi, 0, 0))
    ctx = pl.BlockSpec((1, 1, d), lambda bi, i: (0, 0, 0))
    vec = lambda n: pl.BlockSpec((1, n), lambda bi, i: (0, 0))
    return pl.pallas_call(
        functools.partial(_outproj_body, ctx_tiles=ctx_tiles, alpha=alpha),
        grid=(b, n_tiles),
        in_specs=[yrow(0), yb_row, yrow(0), yrow(0), yrow(0), yrow(3), vec(gw), hrow, lat, ctx,
                  pl.BlockSpec(w.shape, lambda bi, i: (0, 0)), vec(d), vec(d)],
        out_specs=pl.BlockSpec((1, ROW_TILE, d), lambda bi, i: (bi, i, 0)),
        out_shape=jax.ShapeDtypeStruct((b, n_tiles * ROW_TILE, d), F32),
        compiler_params=_cparams(("parallel", "parallel")),
        name="outproj_ln",
    )(ya, yb, yc, hf, hb, p_mlstm, mlstm_norm_g.reshape(1, gw), h, gate_l, gate_c, w, ln_g.reshape(1, d),
      ln_b.reshape(1, d))


FFN_CHUNKS = 1


def _ffn_body(x_ref, xp_ref, xn_ref, sl_ref, cl_ref, gl_ref, sc_ref, cc_ref, gc_ref,
              wu_ref, cw_ref, cb_ref, wd_ref, g_ref, b_ref, o_ref, *, ctx_tiles, n_tiles, alpha, d_ff):
    i = pl.program_id(1)
    is_ctx = i < ctx_tiles
    shift = jnp.where(is_ctx, sc_ref[0], sl_ref[0])
    scale = jnp.where(is_ctx, cc_ref[0], cl_ref[0])
    gate_mod = jnp.where(is_ctx, gc_ref[0], gl_ref[0])
    seg_start = jnp.logical_or(i == 0, i == ctx_tiles)
    seg_end = jnp.logical_or(i == ctx_tiles - 1, i == n_tiles - 1)

    x = x_ref[0]
    xm = (x * (1.0 + scale) + shift).astype(BF16)
    xp = (xp_ref[0] * (1.0 + scale) + shift).astype(BF16)
    xn = (xn_ref[0] * (1.0 + scale) + shift).astype(BF16)
    rows = x.shape[0]
    row_id = lax.broadcasted_iota(jnp.int32, (rows, 1), 0)
    ch = d_ff // FFN_CHUNKS
    f = jnp.zeros(x.shape, F32)
    for c in range(FFN_CHUNKS):
        wg = wu_ref[:, c * ch:(c + 1) * ch]
        wv = wu_ref[:, d_ff + c * ch:d_ff + (c + 1) * ch]
        gate = jnp.dot(xm, wg, preferred_element_type=F32)
        val = jnp.dot(xm, wv, preferred_element_type=F32)
        gp = jnp.dot(xp, wg, preferred_element_type=F32)[7:8, :]
        gn = jnp.dot(xn, wg, preferred_element_type=F32)[0:1, :]
        gp = jnp.where(seg_start, 0.0, gp)
        gn = jnp.where(seg_end, 0.0, gn)
        prev = jnp.where(row_id == 0, gp, pltpu.roll(gate, 1, 0))
        nxt = jnp.where(row_id == rows - 1, gn, pltpu.roll(gate, rows - 1, 0))
        cw = cw_ref[:, c * ch:(c + 1) * ch]
        conv = prev * cw[0:1, :] + gate * cw[1:2, :] + nxt * cw[2:3, :] + cb_ref[:, c * ch:(c + 1) * ch]
        act = (jax.nn.gelu(conv) * val).astype(BF16)
        f = f + jnp.dot(act, wd_ref[c * ch:(c + 1) * ch, :], preferred_element_type=F32)
    o_ref[0] = _layernorm(alpha * x + gate_mod * f, g_ref[...], b_ref[...])


def _ffn(h, mods_l, mods_c, w_up, conv_w, conv_b, w_down, ln_g, ln_b, ctx_tiles, alpha):
    b, t, d = h.shape
    d_ff = w_down.shape[0]
    n_tiles = t // ROW_TILE
    sub = ROW_TILE // 8
    n_sub = t // 8
    row = pl.BlockSpec((1, ROW_TILE, d), lambda bi, i: (bi, i, 0))
    prev = pl.BlockSpec((1, 8, d), lambda bi, i: (bi, jnp.maximum(i * sub - 1, 0), 0))
    nxt = pl.BlockSpec((1, 8, d), lambda bi, i: (bi, jnp.minimum((i + 1) * sub, n_sub - 1), 0))
    lat = pl.BlockSpec((1, 1, d), lambda bi, i: (bi, 0, 0))
    ctx = pl.BlockSpec((1, 1, d), lambda bi, i: (0, 0, 0))
    vec = pl.BlockSpec((1, d), lambda bi, i: (0, 0))
    full = lambda a: pl.BlockSpec(a.shape, lambda bi, i: (0,) * a.ndim)
    cb = conv_b.reshape(1, d_ff)
    return pl.pallas_call(
        functools.partial(_ffn_body, ctx_tiles=ctx_tiles, n_tiles=n_tiles, alpha=alpha, d_ff=d_ff),
        grid=(b, n_tiles),
        in_specs=[row, prev, nxt, lat, lat, lat, ctx, ctx, ctx,
                  full(w_up), full(conv_w), full(cb), full(w_down), vec, vec],
        out_specs=row,
        out_shape=jax.ShapeDtypeStruct((b, t, d), F32),
        compiler_params=_cparams(("parallel", "parallel")),
        name="ffn",
    )(h, h, h, *mods_l, *mods_c, w_up, conv_w, cb, w_down, ln_g.reshape(1, d), ln_b.reshape(1, d))


def _rwkv7(f, tc, lnx_g, lnx_b):
    b = f.shape[0]
    col = RWKV_FEATS.index
    y = _scan("rwkv",
              [_to_scan(f, col("r"), col("r"), "k", tc), _to_scan(f, col("w0"), col("w1"), "k", tc),
               _to_scan(f, col("k0"), col("k1"), "k", tc), _to_scan(f, col("a"), col("a"), "k", tc),
               _to_scan(f, col("b0"), col("b1"), "k", tc)],
              _to_scan(f, col("v"), col("v"), "v", tc))
    return _from_scan(y, b, tc, "rwkv", f, (col("bonus"), col("g")), (lnx_g, lnx_b))


def _rope_tables(n):
    rows = n // GRID_W
    row = jnp.repeat(jnp.arange(rows, dtype=F32), GRID_W)
    col = jnp.tile(jnp.arange(GRID_W, dtype=F32), rows)
    n_freq = DIFF_QK_DIM // 4
    inv_freq = ROPE_BASE ** (-jnp.arange(n_freq, dtype=F32) / n_freq)
    ang = jnp.concatenate([row[:, None] * inv_freq, col[:, None] * inv_freq], axis=-1)
    return jnp.cos(ang), jnp.sin(ang)


def _rope(x, cos, sin, tc):
    b, t, gw = x.shape
    half = DIFF_QK_DIM // 2
    cos = jnp.concatenate([jnp.ones((tc, half), F32), cos], axis=0)[:, None, :]
    sin = jnp.concatenate([jnp.zeros((tc, half), F32), sin], axis=0)[:, None, :]
    x = x.reshape(b, t, gw // DIFF_QK_DIM, DIFF_QK_DIM)
    x1, x2 = x[..., :half], x[..., half:]
    return jnp.concatenate([x1 * cos - x2 * sin, x2 * cos + x1 * sin], axis=-1).reshape(b, t, gw)


def _diff_attn(p, tc, lam_vecs, norm_g, layer, need_ctx):
    b, t, _ = p.shape
    gw = GROUP_WIDTH
    lam_init = 0.8 - 0.6 * math.exp(-0.3 * layer)
    lv = lam_vecs.astype(F32)
    lam = jnp.exp(jnp.sum(lv[0] * lv[1])) - jnp.exp(jnp.sum(lv[2] * lv[3])) + lam_init
    cos, sin = _rope_tables(t - tc)
    q = (_rope(p[..., :gw], cos, sin, tc) * (DIFF_QK_DIM ** -0.5 * math.log2(math.e))).astype(BF16)
    k = _rope(p[..., gw:2 * gw], cos, sin, tc).astype(BF16)
    q_tile0 = 0 if need_ctx else tc // ATTN_TILE
    return _attention(q, k, p, 2 * gw // LANES, lam, norm_g, 1.0 - lam_init, q_tile0, tc)


def _hgrn2(f, tc, norm_g):
    b = f.shape[0]
    col = HGRN_FEATS.index
    o = _scan("gla",
              [_to_scan(f, col("q"), col("q"), "k", tc), _to_scan(f, col("w0"), col("w1"), "k", tc),
               _to_scan(f, col("k0"), col("k1"), "k", tc)],
              _to_scan(f, col("v"), col("v"), "v", tc))
    return _from_scan(o, b, tc, "gla", f, (col("gate"),), (jnp.tile(norm_g, N_HEADS),))


RWKV_COLS = 3 * GROUP_WIDTH + 2 * RWKV_DECAY_LORA + 2 * RWKV_ICLR_LORA + RWKV_GATE_LORA
DIFF_COLS = 3 * GROUP_WIDTH
HGRN_COLS = 5 * GROUP_WIDTH
MLSTM_COLS = 4 * GROUP_WIDTH + 4 * N_HEADS
SLAB_COLS = (RWKV_COLS, DIFF_COLS, HGRN_COLS, MLSTM_COLS)
SLAB_PAD = (5 * GROUP_WIDTH, DIFF_COLS, HGRN_COLS, 4 * GROUP_WIDTH + LANES)


def _split_cols(w):
    parts, off = [], 0
    for n, n_pad in zip(SLAB_COLS, SLAB_PAD):
        parts.append(jnp.pad(w[:, off:off + n], ((0, 0), (0, n_pad - n))).astype(BF16))
        off += n
    return parts


def kernel(x, c, ctx, c_ctx, ada_w, ada_b, w_in, rwkv_mu, rwkv_w0, rwkv_w2, rwkv_a0, rwkv_a2, rwkv_g2, rwkv_k_k, rwkv_k_a, rwkv_r_k, rwkv_lnx_g, rwkv_lnx_b, diff_lambda, diff_norm_g, hgrn_lb_logits, hgrn_norm_g, mlstm_conv_w, mlstm_conv_b, mlstm_gate_b, mlstm_norm_g, w_out, ffn_w_up, ffn_conv_w, ffn_conv_b, ffn_w_down, ln_g, ln_b):
    depth = w_in.shape[0]
    b, seq, d = x.shape
    tc = ctx.shape[1]
    assert tc == ROW_TILE == ATTN_TILE and seq % ROW_TILE == 0 and 2 * b * N_HEADS <= LANES
    alpha = (2.0 * depth) ** 0.25

    lb_w = jax.nn.softmax(hgrn_lb_logits.astype(F32), axis=0)
    lower_bounds = jnp.cumsum(lb_w, axis=0) - lb_w[0]

    cond = jnp.zeros((8, d), F32).at[:b].set(c).at[b].set(c_ctx)
    h = jnp.concatenate([ctx, x], axis=1)
    for layer in range(depth):
        last = layer == depth - 1
        mod = _adaln(cond, ada_w, ada_b, layer).reshape(8, 6, d)
        mods_l = [mod[:b, j][:, None, :] for j in range(6)]
        mods_c = [mod[b:b + 1, j][:, None, :] for j in range(6)]

        rwkv_prm = (rwkv_mu[layer], rwkv_w0[layer], rwkv_w2[layer], rwkv_a0[layer], rwkv_a2[layer], rwkv_g2[layer],
                    rwkv_k_k[layer], rwkv_k_a[layer], rwkv_r_k[layer])
        f_rwkv, p_diff, f_hgrn, p_mlstm = _inproj(h, mods_l[0], mods_l[1], mods_c[0], mods_c[1],
                                                  _split_cols(w_in[layer]), rwkv_prm, lower_bounds[layer],
                                                  tc // ROW_TILE)
        y_a = _rwkv7(f_rwkv, tc, rwkv_lnx_g[layer], rwkv_lnx_b[layer])
        y_b = _diff_attn(p_diff, tc, diff_lambda[layer], diff_norm_g[layer], layer, not last)
        y_c = _hgrn2(f_hgrn, tc, hgrn_norm_g[layer])
        h_f, h_b = (_mlstm_chunked(p_mlstm, mlstm_conv_w[layer], mlstm_conv_b[layer], mlstm_gate_b[layer], rev, tc)
                    for rev in (False, True))

        tile0, ctx_tiles = (tc // ROW_TILE, 0) if last else (0, tc // ROW_TILE)
        h = _outproj(y_a, y_b, y_c, h_f, h_b, p_mlstm, mlstm_norm_g[layer], h, mods_l[2], mods_c[2],
                     w_out[layer].astype(BF16), ln_g[layer, 0], ln_b[layer, 0], tile0, ctx_tiles, alpha)
        h = _ffn(h, mods_l[3:6], mods_c[3:6], ffn_w_up[layer].astype(BF16), ffn_conv_w[layer], ffn_conv_b[layer],
                 ffn_w_down[layer].astype(BF16), ln_g[layer, 1], ln_b[layer, 1], ctx_tiles, alpha)
    return h
```

```python
import functools
import math

import jax
import jax.numpy as jnp
import numpy as np
from jax import lax
from jax.experimental import pallas as pl
from jax.experimental.pallas import tpu as pltpu

F32 = jnp.float32
BF16 = jnp.bfloat16

HEAD_DIM = 64
N_HEADS = 4
GROUP_WIDTH = N_HEADS * HEAD_DIM
DIFF_QK_DIM = HEAD_DIM // 2
GRID_W = 64
ROPE_BASE = 10000.0
RWKV_DECAY_LORA = 64
RWKV_ICLR_LORA = 64
RWKV_GATE_LORA = 160
RWKV_LNX_EPS = 64e-5
LN_EPS = 1e-5
LB_FLOOR = 1e-30
MAX_NEG_LOG_STAB = 60.0

LANES = 128
ROW_TILE = 256
ATTN_TILE = 256
TIME_BLOCK = 128
SCAN_TB = 64
VMEM_LIMIT = 56 * 1024 * 1024


def _cparams(sem):
    return pltpu.CompilerParams(dimension_semantics=sem, vmem_limit_bytes=VMEM_LIMIT)


def _rev_block(i, n_ctx, n_all):
    return jnp.where(i < n_ctx, n_ctx - 1 - i, n_ctx + n_all - 1 - i)


def _split3(x):
    hi = x.astype(BF16)
    r1 = x - hi.astype(F32)
    mid = r1.astype(BF16)
    return hi, mid, (r1 - mid.astype(F32)).astype(BF16)


def _dot01_left(a01, x):
    a = a01.astype(BF16)
    p = [jnp.dot(a, part, preferred_element_type=F32) for part in _split3(x)]
    return (p[0] + p[1]) + p[2]


def _dot01_right(x, b01):
    b = b01.astype(BF16)
    p = [jnp.dot(part, b, preferred_element_type=F32) for part in _split3(x)]
    return (p[0] + p[1]) + p[2]


def _flip_rows(x):
    n = x.shape[0]
    r = lax.broadcasted_iota(jnp.int32, (n, n), 0)
    c = lax.broadcasted_iota(jnp.int32, (n, n), 1)
    return _dot01_left(jnp.where(r + c == n - 1, 1.0, 0.0).astype(F32), x)


def _adaln_body(c_ref, w_ref, b_ref, o_ref):
    x = c_ref[...]
    x = (x * jax.nn.sigmoid(x)).astype(BF16)
    o_ref[...] = jnp.dot(x, w_ref[0].astype(BF16), preferred_element_type=F32) + b_ref[0]


def _adaln(cond, w, b, layer):
    m, d = cond.shape
    n = w.shape[2]
    tn = 1536
    return pl.pallas_call(
        _adaln_body,
        grid=(n // tn,),
        in_specs=[pl.BlockSpec((m, d), lambda j: (0, 0)),
                  pl.BlockSpec((1, d, tn), lambda j: (layer, 0, j)),
                  pl.BlockSpec((1, 1, tn), lambda j: (layer, 0, j))],
        out_specs=pl.BlockSpec((m, tn), lambda j: (0, j)),
        out_shape=jax.ShapeDtypeStruct((m, n), F32),
        compiler_params=_cparams(("arbitrary",)),
        name="adaln",
    )(cond, w, b.reshape(b.shape[0], 1, n))


def _softplus(x):
    return jnp.maximum(x, 0.0) + jnp.log(1.0 + jnp.exp(-jnp.abs(x)))


def _head_block(scale=1.0):
    r = lax.broadcasted_iota(jnp.int32, (GROUP_WIDTH, GROUP_WIDTH), 0) // HEAD_DIM
    c = lax.broadcasted_iota(jnp.int32, (GROUP_WIDTH, GROUP_WIDTH), 1) // HEAD_DIM
    return jnp.where(r == c, scale, 0.0).astype(F32)


def _head_sum(x, block_bf16):
    hi = x.astype(BF16)
    lo = (x - hi.astype(F32)).astype(BF16)
    return (jnp.dot(hi, block_bf16, preferred_element_type=F32)
            + jnp.dot(lo, block_bf16, preferred_element_type=F32))


def _shifted_rows(x, prev8, next8, seg_start, seg_end):
    rows = x.shape[0]
    row_id = lax.broadcasted_iota(jnp.int32, (rows, 1), 0)
    prev_row = jnp.where(seg_start, 0.0, prev8[7:8, :])
    next_row = jnp.where(seg_end, 0.0, next8[0:1, :])
    prev = jnp.where(row_id == 0, prev_row, pltpu.roll(x, 1, 0))
    nxt = jnp.where(row_id == rows - 1, next_row, pltpu.roll(x, rows - 1, 0))
    return prev, nxt


RWKV_FEATS = ("r", "v", "w0", "w1", "k0", "k1", "a", "b0", "b1", "g", "bonus")


def _rwkv_feats(p, prev8, next8, seg_start, seg_end, mu_ref, w0_ref, w2_ref, a0_ref, a2_ref, g2_ref, kk_ref,
                ka_ref, rk_ref):
    gw = GROUP_WIDTH
    prev, nxt = _shifted_rows(p, prev8, next8, seg_start, seg_end)
    p = p + (0.5 * (prev + nxt) - p) * mu_ref[...]
    r, k, v = p[:, :gw], p[:, gw:2 * gw], p[:, 2 * gw:3 * gw]
    wd, ad, gd = p[:, 3 * gw:3 * gw + LANES], p[:, 3 * gw + LANES:4 * gw], p[:, 4 * gw:5 * gw]
    lora_w = jnp.dot(jnp.tanh(wd).astype(BF16), w2_ref[...], preferred_element_type=F32)
    lora_a = jnp.dot(ad.astype(BF16), a2_ref[...], preferred_element_type=F32)
    decay = jnp.exp(-jnp.exp(-_softplus(-(w0_ref[...] + lora_w)) - 0.5))
    a = jax.nn.sigmoid(a0_ref[...] + lora_a)
    g = jnp.dot(jax.nn.sigmoid(gd).astype(BF16), g2_ref[...], preferred_element_type=F32)
    block = _head_block().astype(BF16)
    kk = k * kk_ref[...]
    kk = kk / jnp.maximum(jnp.sqrt(_head_sum(kk * kk, block)), 1e-12)
    k0 = k * (1.0 + (a[:, :gw] - 1.0) * ka_ref[...])
    k1 = k * (1.0 + (a[:, gw:] - 1.0) * ka_ref[...])
    bonus = _head_sum(r * (k0 + k1) * rk_ref[...], block) * v
    return dict(r=r, v=v, w0=decay[:, :gw], w1=decay[:, gw:], k0=k0, k1=k1, a=-kk, b0=kk * a[:, :gw],
                b1=kk * a[:, gw:], g=g, bonus=bonus)


def _block_diag2(w):
    z = jnp.zeros_like(w[0])
    return jnp.concatenate([jnp.concatenate([w[0], z], axis=1), jnp.concatenate([z, w[1]], axis=1)], axis=0)


HGRN_FEATS = ("q", "w0", "w1", "k0", "k1", "gate", "v")


def _hgrn_feats(p, llb_ref, l1m_ref, oml_ref):
    gw = GROUP_WIDTH
    q, f_f, f_b, v, g = (p[:, j * gw:(j + 1) * gw] for j in range(5))
    feats = dict(q=q * jax.nn.sigmoid(q), gate=g * jax.nn.sigmoid(g), v=v)
    for d, f in enumerate((f_f, f_b)):
        x = llb_ref[...]
        y = l1m_ref[...] - _softplus(-f)
        log_f = jnp.maximum(x, y) + jnp.log(1.0 + jnp.exp(-jnp.abs(x - y)))
        feats["w%d" % d] = jnp.exp(log_f)
        feats["k%d" % d] = oml_ref[...] * jax.nn.sigmoid(-f)
    return feats


N_RWKV_PRM, N_HGRN_PRM = 9, 3


def _inproj_body(x_ref, xp_ref, xn_ref, sl_ref, cl_ref, sc_ref, cc_ref, wr_ref, wd_ref, wh_ref, wm_ref, *refs,
                 ctx_tiles, n_tiles):
    rwkv_prm, hgrn_prm = refs[:N_RWKV_PRM], refs[N_RWKV_PRM:N_RWKV_PRM + N_HGRN_PRM]
    fr_ref, pd_ref, fh_ref, pm_ref = refs[N_RWKV_PRM + N_HGRN_PRM:]
    i = pl.program_id(1)
    is_ctx = i < ctx_tiles
    seg_start = jnp.logical_or(i == 0, i == ctx_tiles)
    seg_end = jnp.logical_or(i == ctx_tiles - 1, i == n_tiles - 1)
    shift = jnp.where(is_ctx, sc_ref[0], sl_ref[0])
    scale = jnp.where(is_ctx, cc_ref[0], cl_ref[0])
    mod = lambda x: (x * (1.0 + scale) + shift).astype(BF16)
    proj = lambda x, w_ref: jnp.dot(x, w_ref[...], preferred_element_type=F32)
    gw = GROUP_WIDTH
    xm = mod(x_ref[0])
    feats = _rwkv_feats(proj(xm, wr_ref), proj(mod(xp_ref[0]), wr_ref), proj(mod(xn_ref[0]), wr_ref),
                        seg_start, seg_end, *rwkv_prm)
    for j, name in enumerate(RWKV_FEATS):
        fr_ref[0, :, j * gw:(j + 1) * gw] = feats[name]
    pd_ref[0] = proj(xm, wd_ref)
    feats = _hgrn_feats(proj(xm, wh_ref), *hgrn_prm)
    for j, name in enumerate(HGRN_FEATS):
        fh_ref[0, :, j * gw:(j + 1) * gw] = feats[name]
    pm_ref[0] = proj(xm, wm_ref)


def _inproj(h, shift_l, scale_l, shift_c, scale_c, ws, rwkv_prm, lb, ctx_tiles):
    b, t, d = h.shape
    gw = GROUP_WIDTH
    n_tiles, sub, n_sub = t // ROW_TILE, ROW_TILE // 8, t // 8
    row = lambda n: pl.BlockSpec((1, ROW_TILE, n), lambda bi, i: (bi, i, 0))
    prev = pl.BlockSpec((1, 8, d), lambda bi, i: (bi, jnp.maximum(i * sub - 1, 0), 0))
    nxt = pl.BlockSpec((1, 8, d), lambda bi, i: (bi, jnp.minimum((i + 1) * sub, n_sub - 1), 0))
    lat = pl.BlockSpec((1, 1, d), lambda bi, i: (bi, 0, 0))
    ctx = pl.BlockSpec((1, 1, d), lambda bi, i: (0, 0, 0))
    full = lambda a: pl.BlockSpec(a.shape, lambda bi, i: (0,) * a.ndim)
    vec = lambda a: a.reshape(1, -1).astype(F32)
    mu, w0, w2, a0, a2, g2, k_k, k_a, r_k = rwkv_prm
    cols = ws[0].shape[1]
    prm = [jnp.pad(vec(mu), ((0, 0), (0, cols - mu.shape[0]))), vec(w0), _block_diag2(w2).astype(BF16), vec(a0),
           _block_diag2(a2).astype(BF16), jnp.pad(g2, ((0, gw - g2.shape[0]), (0, 0))).astype(BF16), vec(k_k),
           vec(k_a), vec(r_k),
           vec(jnp.log(jnp.maximum(lb, LB_FLOOR))), vec(jnp.log1p(-lb)), vec(1.0 - lb)]
    assert len(prm) == N_RWKV_PRM + N_HGRN_PRM
    widths = (len(RWKV_FEATS) * gw, ws[1].shape[1], len(HGRN_FEATS) * gw, ws[3].shape[1])
    return pl.pallas_call(
        functools.partial(_inproj_body, ctx_tiles=ctx_tiles, n_tiles=n_tiles),
        grid=(b, n_tiles),
        in_specs=[row(d), prev, nxt, lat, lat, ctx, ctx] + [full(a) for a in list(ws) + prm],
        out_specs=[row(n) for n in widths],
        out_shape=[jax.ShapeDtypeStruct((b, t, n), F32) for n in widths],
        compiler_params=_cparams(("parallel", "parallel")),
        name="inproj",
    )(h, h, h, shift_l, scale_l, shift_c, scale_c, *ws, *prm)


def _to_scan_body(x0_ref, x1_ref, o_ref, r_ref, *, kind, nb):
    n_scan = 2 * nb * N_HEADS
    rep = LANES // n_scan
    for b in range(nb):
        r_ref[pl.ds(b * GROUP_WIDTH, GROUP_WIDTH), :] = x0_ref[b].T
        r_ref[pl.ds((nb + b) * GROUP_WIDTH, GROUP_WIDTH), :] = _flip_rows(x1_ref[b]).T
    if kind == "k":
        for k in range(HEAD_DIM):
            rows = r_ref[pl.ds(k, n_scan, stride=HEAD_DIM), :]
            o_ref[k] = jnp.concatenate([rows] * rep, axis=0).T
    else:
        for vh in range(HEAD_DIM // rep):
            rows = [r_ref[pl.ds(vh * rep + vl, n_scan, stride=HEAD_DIM), :] for vl in range(rep)]
            o_ref[:, vh, :] = jnp.concatenate(rows, axis=0).T


def _to_scan(x, col0, col1, kind, tc):
    nb, t, _ = x.shape
    gw = GROUP_WIDTH
    x0 = x1 = x
    n_all, n_ctx = t // TIME_BLOCK, tc // TIME_BLOCK
    n_scan = 2 * nb * N_HEADS
    rep = LANES // n_scan
    fwd = pl.BlockSpec((nb, TIME_BLOCK, gw), lambda i: (0, i, col0))
    bwd = pl.BlockSpec((nb, TIME_BLOCK, gw), lambda i: (0, _rev_block(i, n_ctx, n_all), col1))
    if kind == "k":
        out_shape = (HEAD_DIM, t, LANES)
        out_spec = pl.BlockSpec((HEAD_DIM, TIME_BLOCK, LANES), lambda i: (0, i, 0))
    else:
        out_shape = (t, HEAD_DIM // rep, LANES)
        out_spec = pl.BlockSpec((TIME_BLOCK, HEAD_DIM // rep, LANES), lambda i: (i, 0, 0))
    return pl.pallas_call(
        functools.partial(_to_scan_body, kind=kind, nb=nb),
        grid=(n_all,),
        in_specs=[fwd, bwd],
        out_specs=out_spec,
        out_shape=jax.ShapeDtypeStruct(out_shape, F32),
        scratch_shapes=[pltpu.VMEM((n_scan * HEAD_DIM, TIME_BLOCK), F32)],
        compiler_params=_cparams(("parallel",)),
        name="to_scan_" + kind,
    )(x0, x1)


def _from_scan_body(yf_ref, yb_ref, *rest, nb, readout):
    if readout == "rwkv":
        bonus_ref, g_ref, ng_ref, nb_ref, o_ref, r_ref = rest
    else:
        gate_ref, ng_ref, o_ref, r_ref = rest
    n_scan = 2 * nb * N_HEADS
    rep = LANES // n_scan
    for vh in range(HEAD_DIM // rep):
        r_ref[0, pl.ds(vh * LANES, LANES), :] = yf_ref[vh].T
        r_ref[1, pl.ds(vh * LANES, LANES), :] = yb_ref[vh].T
    mean_block = _head_block(1.0 / HEAD_DIM).astype(BF16)
    for b in range(nb):
        slab = lambda d: jnp.concatenate(
            [r_ref[d, pl.ds((d * nb + b) * N_HEADS + h, HEAD_DIM, stride=n_scan), :] for h in range(N_HEADS)],
            axis=0).T
        y = slab(0) + _flip_rows(slab(1))
        if readout == "rwkv":
            y = y - _head_sum(y, mean_block)
            y = y * lax.rsqrt(_head_sum(y * y, mean_block) + RWKV_LNX_EPS) * ng_ref[...] + nb_ref[...]
            o_ref[b] = (y + bonus_ref[b]) * g_ref[b]
        else:
            y = y * lax.rsqrt(_head_sum(y * y, mean_block) + LN_EPS) * ng_ref[...]
            o_ref[b] = y * gate_ref[b]


def _from_scan(y, nb, tc, readout, feats, cols, vecs):
    n_vh, t, _ = y.shape
    gw = GROUP_WIDTH
    n_all, n_ctx = t // TIME_BLOCK, tc // TIME_BLOCK
    fwd = pl.BlockSpec((n_vh, TIME_BLOCK, LANES), lambda i: (0, i, 0))
    bwd = pl.BlockSpec((n_vh, TIME_BLOCK, LANES), lambda i: (0, _rev_block(i, n_ctx, n_all), 0))
    tok = lambda c: pl.BlockSpec((nb, TIME_BLOCK, gw), lambda i: (0, i, c))
    vec = pl.BlockSpec((1, gw), lambda i: (0, 0))
    return pl.pallas_call(
        functools.partial(_from_scan_body, nb=nb, readout=readout),
        grid=(n_all,),
        in_specs=[fwd, bwd] + [tok(c) for c in cols] + [vec] * len(vecs),
        out_specs=tok(0),
        out_shape=jax.ShapeDtypeStruct((nb, t, gw), F32),
        scratch_shapes=[pltpu.VMEM((2, n_vh * LANES, TIME_BLOCK), F32)],
        compiler_params=_cparams(("parallel",)),
        name="from_scan_" + readout,
    )(y, y, *([feats] * len(cols)), *[v.reshape(1, gw) for v in vecs])


N_ACC = 4


def _acc_add(acc, i, x):
    acc[i % N_ACC] = x if acc[i % N_ACC] is None else acc[i % N_ACC] + x


def _acc_total(acc):
    return (acc[0] + acc[1]) + (acc[2] + acc[3])


def _scan_body(*refs, mode, tb, n_vh):
    if mode == "rwkv":
        r_ref, w_ref, k_ref, a_ref, b_ref, v_ref, y_ref, s_ref = refs
    else:
        r_ref, w_ref, k_ref, v_ref, y_ref, s_ref = refs

    @pl.when(pl.program_id(0) == 0)
    def _():
        s_ref[...] = jnp.zeros_like(s_ref)

    row = lambda ref, k, j: ref[k, pl.ds(j, 1), :]
    wide = lambda x: jnp.broadcast_to(x, (n_vh, LANES))

    def store_y(j, y):
        for vh in range(n_vh):
            y_ref[vh, pl.ds(j, 1), :] = y[vh:vh + 1, :]

    def rwkv_step(j, sa):
        v = v_ref[j]
        j_next = jnp.minimum(j + 1, tb - 1)
        ys, sas = [None] * N_ACC, [None] * N_ACC
        for k in range(HEAD_DIM):
            s = s_ref[k] * wide(row(w_ref, k, j)) + sa * wide(row(b_ref, k, j)) + v * wide(row(k_ref, k, j))
            s_ref[k] = s
            _acc_add(ys, k, s * wide(row(r_ref, k, j)))
            _acc_add(sas, k, s * wide(row(a_ref, k, j_next)))
        store_y(j, _acc_total(ys))
        return _acc_total(sas)

    def gla_step(j, carry):
        v = v_ref[j]
        ys = [None] * N_ACC
        for k in range(HEAD_DIM):
            s = s_ref[k] * wide(row(w_ref, k, j)) + v * wide(row(k_ref, k, j))
            s_ref[k] = s
            _acc_add(ys, k, s * wide(row(r_ref, k, j)))
        store_y(j, _acc_total(ys))
        return carry

    if mode == "rwkv":
        sa0 = [None] * N_ACC
        for k in range(HEAD_DIM):
            _acc_add(sa0, k, s_ref[k] * wide(row(a_ref, k, 0)))
        lax.fori_loop(0, tb, rwkv_step, _acc_total(sa0), unroll=8)
    else:
        lax.fori_loop(0, tb, gla_step, 0, unroll=8)


def _scan(mode, k_inputs, v_in):
    t, n_vh, _ = v_in.shape
    tb = SCAN_TB
    big = pl.BlockSpec((HEAD_DIM, tb, LANES), lambda i: (0, i, 0))
    small = pl.BlockSpec((tb, n_vh, LANES), lambda i: (i, 0, 0))
    return pl.pallas_call(
        functools.partial(_scan_body, mode=mode, tb=tb, n_vh=n_vh),
        grid=(t // tb,),
        in_specs=[big] * len(k_inputs) + [small],
        out_specs=pl.BlockSpec((n_vh, tb, LANES), lambda i: (0, i, 0)),
        out_shape=jax.ShapeDtypeStruct((n_vh, t, LANES), F32),
        scratch_shapes=[pltpu.VMEM((HEAD_DIM, n_vh, LANES), F32)],
        compiler_params=_cparams(("arbitrary",)),
        name="scan_" + mode,
    )(*k_inputs, v_in)


MLSTM_CHUNK = 64


def _prefix_max(x, reverse):
    n = x.shape[0]
    row = lax.broadcasted_iota(jnp.int32, x.shape, 0)
    sh = 1
    while sh < n:
        if reverse:
            x = jnp.where(row < n - sh, jnp.maximum(x, pltpu.roll(x, n - sh, 0)), x)
        else:
            x = jnp.where(row >= sh, jnp.maximum(x, pltpu.roll(x, sh, 0)), x)
        sh *= 2
    return x


def _mlstm_body(qk_ref, qkp_ref, qkn_ref, v_ref, gt_ref, cw_ref, cb_ref, sel_ref, gb_ref, o_ref, st_ref, m_ref, *,
                reverse, n_sub, n_ctx, n_blocks):
    c, gw = MLSTM_CHUNK, GROUP_WIDTH
    i = pl.program_id(1)

    @pl.when(i == 0)
    def _():
        st_ref[...] = jnp.zeros_like(st_ref)
        m_ref[...] = jnp.zeros_like(m_ref)

    blk = _rev_block(i, n_ctx, n_blocks) if reverse else i
    seg_start = jnp.logical_or(blk == 0, blk == n_ctx)
    seg_end = jnp.logical_or(blk == n_ctx - 1, blk == n_blocks - 1)
    qk = qk_ref[0]
    prev, nxt = _shifted_rows(qk, qkp_ref[0], qkn_ref[0], seg_start, seg_end)
    qk = prev * cw_ref[0:1, :] + qk * cw_ref[1:2, :] + nxt * cw_ref[2:3, :] + cb_ref[...]
    qk = qk * jax.nn.sigmoid(qk)
    q_all, k_all = qk[:, :gw], qk[:, gw:] * HEAD_DIM ** -0.5
    v_all = v_ref[0]
    gt = gt_ref[0]
    ig_all = _dot01_right(gt, sel_ref[0]) + gb_ref[0:1, :]
    fg_all = -_softplus(-(_dot01_right(gt, sel_ref[1]) + gb_ref[1:2, :]))

    tt = lax.broadcasted_iota(jnp.int32, (c, c), 0)
    ss = lax.broadcasted_iota(jnp.int32, (c, c), 1)
    tri = jnp.where((ss >= tt) if reverse else (ss <= tt), 1.0, 0.0).astype(F32)
    row = lax.broadcasted_iota(jnp.int32, (c, gw), 0)
    s_of_lane = lax.broadcasted_iota(jnp.int32, (c, gw), 1) % c
    causal = (s_of_lane >= row) if reverse else (s_of_lane <= row)
    diag = s_of_lane == row
    block = _head_block()
    block2 = jnp.concatenate([block, block], axis=1)
    ones8 = jnp.ones((8, c), F32)
    nt = (((1,), (1,)), ((), ()))
    last = 0 if reverse else c - 1
    for u in (range(n_sub - 1, -1, -1) if reverse else range(n_sub)):
        sl = slice(u * c, (u + 1) * c)
        q, k, v, ig, fg = q_all[sl], k_all[sl], v_all[sl], ig_all[sl], fg_all[sl]
        b = _dot01_left(tri, fg)
        g = ig - b
        m_prev = m_ref[0:1, :]
        m_t = b + jnp.maximum(m_prev, _prefix_max(g, reverse))
        w_inter = jnp.exp(b + m_prev - m_t)
        g_row = _dot01_left(ones8, jnp.where(diag, g, 0.0))[0:1]
        qb = q.astype(BF16)
        k_bd = (jnp.concatenate([k] * N_HEADS, axis=0) * block).astype(BF16)
        scores = lax.dot_general(qb, k_bd, nt, preferred_element_type=F32)
        w = jnp.where(causal, jnp.exp((b - m_t) + g_row), 0.0) * scores
        v_bd = jnp.concatenate([jnp.concatenate([v] * N_HEADS, axis=0) * block, block], axis=1).astype(BF16)
        intra = jnp.dot(w.astype(BF16), v_bd, preferred_element_type=F32)
        inter = jnp.dot(qb, st_ref[...].astype(BF16), preferred_element_type=F32)
        num = w_inter * inter[:, :gw] + intra[:, :gw]
        den = w_inter * inter[:, gw:] + intra[:, gw:]
        floor = jnp.exp(jnp.minimum(-m_t, MAX_NEG_LOG_STAB))
        o_ref[0, pl.ds(u * c, c), :] = num / jnp.maximum(jnp.abs(den), floor)
        m_new = m_t[last:last + 1]
        b_end = b[last:last + 1]
        kw = k * jnp.exp(b_end - b + ig - m_new)
        decay = jnp.exp(b_end + m_prev - m_new)
        v_one = jnp.concatenate([v, jnp.ones_like(v)], axis=1).astype(BF16)
        upd = jnp.dot(kw.T.astype(BF16), v_one, preferred_element_type=F32)
        st_ref[...] = st_ref[...] * jnp.concatenate([decay, decay], axis=1) + upd * block2
        m_ref[0:1, :] = m_new


def _mlstm_chunked(p, conv_w, conv_b, gate_b, reverse, tc):
    b, t, _ = p.shape
    gw = GROUP_WIDTH
    n_blocks, n_ctx = t // ROW_TILE, tc // ROW_TILE
    sub, n_sub8 = ROW_TILE // 8, t // 8
    blk = (lambda i: _rev_block(i, n_ctx, n_blocks)) if reverse else (lambda i: i)
    d = 1 if reverse else 0
    lane_head = np.arange(gw) // HEAD_DIM
    sel = np.zeros((2, LANES, gw), np.float32)
    for j, base in enumerate((d * N_HEADS, (2 + d) * N_HEADS)):
        sel[j, base + lane_head, np.arange(gw)] = 1.0
    gate_bias = jnp.stack([jnp.repeat(gate_b[d], HEAD_DIM), jnp.repeat(gate_b[2 + d], HEAD_DIM)], axis=0)
    full = lambda a: pl.BlockSpec(a.shape, lambda bi, i: (0,) * a.ndim)
    cb = conv_b.reshape(1, 2 * gw)
    return pl.pallas_call(
        functools.partial(_mlstm_body, reverse=reverse, n_sub=ROW_TILE // MLSTM_CHUNK, n_ctx=n_ctx,
                          n_blocks=n_blocks),
        grid=(b, n_blocks),
        in_specs=[pl.BlockSpec((1, ROW_TILE, 2 * gw), lambda bi, i: (bi, blk(i), 0)),
                  pl.BlockSpec((1, 8, 2 * gw), lambda bi, i: (bi, jnp.maximum(blk(i) * sub - 1, 0), 0)),
                  pl.BlockSpec((1, 8, 2 * gw), lambda bi, i: (bi, jnp.minimum((blk(i) + 1) * sub, n_sub8 - 1), 0)),
                  pl.BlockSpec((1, ROW_TILE, gw), lambda bi, i: (bi, blk(i), 2)),
                  pl.BlockSpec((1, ROW_TILE, LANES), lambda bi, i: (bi, blk(i), 4 * gw // LANES)),
                  full(conv_w), full(cb), full(sel), full(gate_bias)],
        out_specs=pl.BlockSpec((1, ROW_TILE, gw), lambda bi, i: (bi, blk(i), 0)),
        out_shape=jax.ShapeDtypeStruct((b, t, gw), F32),
        scratch_shapes=[pltpu.VMEM((gw, 2 * gw), F32), pltpu.VMEM((8, gw), F32)],
        compiler_params=_cparams(("parallel", "arbitrary")),
        name="mlstm_chunk",
    )(p, p, p, p, p, conv_w, cb, jnp.asarray(sel), gate_bias)


def _attn_body(q_ref, k_ref, v_ref, lam_ref, g_ref, o_ref, *, out_scale, ctx_tiles, n_ctx_keys):
    def attend(n_keys):
        q = q_ref[0]
        k = k_ref[0, :n_keys, :]
        v = v_ref[0, :n_keys, :].astype(BF16)
        lane = lax.broadcasted_iota(jnp.int32, q.shape, 1)
        comp = lane // DIFF_QK_DIM
        nt = (((1,), (1,)), ((), ()))
        n_maps = LANES // DIFF_QK_DIM
        ss = [lax.dot_general(jnp.where(comp == c, q, 0.0).astype(BF16), k, nt, preferred_element_type=F32)
              for c in range(n_maps)]
        es = [jnp.exp2(s - jnp.max(s, axis=-1, keepdims=True)) for s in ss]
        ls = [jnp.sum(e, axis=-1, keepdims=True) for e in es]
        ys = []
        for hh in range(LANES // HEAD_DIM):
            e0, e1, l0, l1 = es[2 * hh], es[2 * hh + 1], ls[2 * hh], ls[2 * hh + 1]
            pr = (e0 - e1 * (lam_ref[0:1, 0:1] * l0 / l1)).astype(BF16)
            ys.append(jnp.dot(pr, v, preferred_element_type=F32) * (1.0 / l0))
        first = lane < HEAD_DIM
        y = jnp.where(first, ys[0], ys[1])
        ysq = y * y
        ms = jnp.where(first, jnp.sum(jnp.where(first, ysq, 0.0), axis=-1, keepdims=True),
                       jnp.sum(jnp.where(first, 0.0, ysq), axis=-1, keepdims=True)) * (1.0 / HEAD_DIM)
        o_ref[0] = y * lax.rsqrt(ms + LN_EPS) * g_ref[...] * out_scale

    if ctx_tiles == 0:
        attend(k_ref.shape[1])
    else:
        is_ctx = pl.program_id(2) < ctx_tiles
        pl.when(is_ctx)(lambda: attend(n_ctx_keys))
        pl.when(jnp.logical_not(is_ctx))(lambda: attend(k_ref.shape[1]))


def _attention(q, k, v, v_col0, lam, norm_g, out_scale, q_tile0, tc):
    b, t, gw = q.shape
    n_tiles = t // ATTN_TILE - q_tile0
    ctx_tiles = max(tc // ATTN_TILE - q_tile0, 0)
    qspec = pl.BlockSpec((1, ATTN_TILE, LANES), lambda bi, pi, i: (bi, q_tile0 + i, pi))
    kspec = pl.BlockSpec((1, t, LANES), lambda bi, pi, i: (bi, 0, pi))
    vspec = pl.BlockSpec((1, t, LANES), lambda bi, pi, i: (bi, 0, v_col0 + pi))
    vec = pl.BlockSpec((1, LANES), lambda bi, pi, i: (0, 0))
    return pl.pallas_call(
        functools.partial(_attn_body, out_scale=out_scale, ctx_tiles=ctx_tiles, n_ctx_keys=tc),
        grid=(b, gw // LANES, n_tiles),
        in_specs=[qspec, kspec, vspec, vec, vec],
        out_specs=pl.BlockSpec((1, ATTN_TILE, LANES), lambda bi, pi, i: (bi, i, pi)),
        out_shape=jax.ShapeDtypeStruct((b, n_tiles * ATTN_TILE, gw), F32),
        compiler_params=_cparams(("parallel", "parallel", "parallel")),
        name="diff_attn",
    )(q, k, v, jnp.full((1, LANES), lam, F32), jnp.tile(norm_g.reshape(1, HEAD_DIM), (1, LANES // HEAD_DIM)))


def _layernorm(z, g, b):
    z = z - jnp.mean(z, axis=-1, keepdims=True)
    return z * lax.rsqrt(jnp.mean(z * z, axis=-1, keepdims=True) + LN_EPS) * g + b


def _outproj_body(ya_ref, yb_ref, yc_ref, hf_ref, hb_ref, og_ref, ng_ref, h_ref, gl_ref, gc_ref, w_ref, g_ref,
                  b_ref, o_ref, *, ctx_tiles, alpha):
    is_ctx = pl.program_id(1) < ctx_tiles
    gate = jnp.where(is_ctx, gc_ref[0], gl_ref[0])
    mean_block = _head_block(1.0 / HEAD_DIM).astype(BF16)
    yd = hf_ref[0] + hb_ref[0]
    yd = yd - _head_sum(yd, mean_block)
    yd = yd * lax.rsqrt(_head_sum(yd * yd, mean_block) + LN_EPS) * ng_ref[...] * jax.nn.sigmoid(og_ref[0])
    y = None
    for m, ym in enumerate((ya_ref[0], yb_ref[0], yc_ref[0], yd)):
        part = jnp.dot(ym.astype(BF16), w_ref[m * GROUP_WIDTH:(m + 1) * GROUP_WIDTH, :],
                       preferred_element_type=F32)
        y = part if y is None else y + part
    o_ref[0] = _layernorm(alpha * h_ref[0] + gate * y, g_ref[...], b_ref[...])


def _outproj(ya, yb, yc, hf, hb, p_mlstm, mlstm_norm_g, h, gate_l, gate_c, w, ln_g, ln_b, tile0, ctx_tiles, alpha):
    b, t, d = h.shape
    gw = GROUP_WIDTH
    n_tiles = t // ROW_TILE - tile0
    yrow = lambda c: pl.BlockSpec((1, ROW_TILE, gw), lambda bi, i: (bi, tile0 + i, c))
    yb_row = pl.BlockSpec((1, ROW_TILE, gw), lambda bi, i: (bi, i, 0))
    hrow = pl.BlockSpec((1, ROW_TILE, d), lambda bi, i: (bi, tile0 + i, 0))
    lat = pl.BlockSpec((1, 1, d), lambda bi, i: (bi, 0, 0))
    ctx = pl.BlockSpec((1, 1, d), lambda bi, i: (0, 0, 0))
    vec = lambda n: pl.BlockSpec((1, n), lambda bi, i: (0, 0))
    return pl.pallas_call(
        functools.partial(_outproj_body, ctx_tiles=ctx_tiles, alpha=alpha),
        grid=(b, n_tiles),
        in_specs=[yrow(0), yb_row, yrow(0), yrow(0), yrow(0), yrow(3), vec(gw), hrow, lat, ctx,
                  pl.BlockSpec(w.shape, lambda bi, i: (0, 0)), vec(d), vec(d)],
        out_specs=pl.BlockSpec((1, ROW_TILE, d), lambda bi, i: (bi, i, 0)),
        out_shape=jax.ShapeDtypeStruct((b, n_tiles * ROW_TILE, d), F32),
        compiler_params=_cparams(("parallel", "parallel")),
        name="outproj_ln",
    )(ya, yb, yc, hf, hb, p_mlstm, mlstm_norm_g.reshape(1, gw), h, gate_l, gate_c, w, ln_g.reshape(1, d),
      ln_b.reshape(1, d))


FFN_CHUNKS = 1


def _ffn_body(x_ref, xp_ref, xn_ref, sl_ref, cl_ref, gl_ref, sc_ref, cc_ref, gc_ref,
              wu_ref, cw_ref, cb_ref, wd_ref, g_ref, b_ref, o_ref, *, ctx_tiles, n_tiles, alpha, d_ff):
    i = pl.program_id(1)
    is_ctx = i < ctx_tiles
    shift = jnp.where(is_ctx, sc_ref[0], sl_ref[0])
    scale = jnp.where(is_ctx, cc_ref[0], cl_ref[0])
    gate_mod = jnp.where(is_ctx, gc_ref[0], gl_ref[0])
    seg_start = jnp.logical_or(i == 0, i == ctx_tiles)
    seg_end = jnp.logical_or(i == ctx_tiles - 1, i == n_tiles - 1)

    x = x_ref[0]
    xm = (x * (1.0 + scale) + shift).astype(BF16)
    xp = (xp_ref[0] * (1.0 + scale) + shift).astype(BF16)
    xn = (xn_ref[0] * (1.0 + scale) + shift).astype(BF16)
    rows = x.shape[0]
    row_id = lax.broadcasted_iota(jnp.int32, (rows, 1), 0)
    ch = d_ff // FFN_CHUNKS
    f = jnp.zeros(x.shape, F32)
    for c in range(FFN_CHUNKS):
        wg = wu_ref[:, c * ch:(c + 1) * ch]
        wv = wu_ref[:, d_ff + c * ch:d_ff + (c + 1) * ch]
        gate = jnp.dot(xm, wg, preferred_element_type=F32)
        val = jnp.dot(xm, wv, preferred_element_type=F32)
        gp = jnp.dot(xp, wg, preferred_element_type=F32)[7:8, :]
        gn = jnp.dot(xn, wg, preferred_element_type=F32)[0:1, :]
        gp = jnp.where(seg_start, 0.0, gp)
        gn = jnp.where(seg_end, 0.0, gn)
        prev = jnp.where(row_id == 0, gp, pltpu.roll(gate, 1, 0))
        nxt = jnp.where(row_id == rows - 1, gn, pltpu.roll(gate, rows - 1, 0))
        cw = cw_ref[:, c * ch:(c + 1) * ch]
        conv = prev * cw[0:1, :] + gate * cw[1:2, :] + nxt * cw[2:3, :] + cb_ref[:, c * ch:(c + 1) * ch]
        act = (jax.nn.gelu(conv) * val).astype(BF16)
        f = f + jnp.dot(act, wd_ref[c * ch:(c + 1) * ch, :], preferred_element_type=F32)
    o_ref[0] = _layernorm(alpha * x + gate_mod * f, g_ref[...], b_ref[...])


def _ffn(h, mods_l, mods_c, w_up, conv_w, conv_b, w_down, ln_g, ln_b, ctx_tiles, alpha):
    b, t, d = h.shape
    d_ff = w_down.shape[0]
    tile = 2 * ROW_TILE if ctx_tiles == 0 and t % (2 * ROW_TILE) == 0 else ROW_TILE
    n_tiles = t // tile
    sub = tile // 8
    n_sub = t // 8
    row = pl.BlockSpec((1, tile, d), lambda bi, i: (bi, i, 0))
    prev = pl.BlockSpec((1, 8, d), lambda bi, i: (bi, jnp.maximum(i * sub - 1, 0), 0))
    nxt = pl.BlockSpec((1, 8, d), lambda bi, i: (bi, jnp.minimum((i + 1) * sub, n_sub - 1), 0))
    lat = pl.BlockSpec((1, 1, d), lambda bi, i: (bi, 0, 0))
    ctx = pl.BlockSpec((1, 1, d), lambda bi, i: (0, 0, 0))
    vec = pl.BlockSpec((1, d), lambda bi, i: (0, 0))
    full = lambda a: pl.BlockSpec(a.shape, lambda bi, i: (0,) * a.ndim)
    cb = conv_b.reshape(1, d_ff)
    return pl.pallas_call(
        functools.partial(_ffn_body, ctx_tiles=ctx_tiles, n_tiles=n_tiles, alpha=alpha, d_ff=d_ff),
        grid=(b, n_tiles),
        in_specs=[row, prev, nxt, lat, lat, lat, ctx, ctx, ctx,
                  full(w_up), full(conv_w), full(cb), full(w_down), vec, vec],
        out_specs=row,
        out_shape=jax.ShapeDtypeStruct((b, t, d), F32),
        compiler_params=_cparams(("parallel", "parallel")),
        name="ffn",
    )(h, h, h, *mods_l, *mods_c, w_up, conv_w, cb, w_down, ln_g.reshape(1, d), ln_b.reshape(1, d))


def _rwkv7(f, tc, lnx_g, lnx_b):
    b = f.shape[0]
    col = RWKV_FEATS.index
    y = _scan("rwkv",
              [_to_scan(f, col("r"), col("r"), "k", tc), _to_scan(f, col("w0"), col("w1"), "k", tc),
               _to_scan(f, col("k0"), col("k1"), "k", tc), _to_scan(f, col("a"), col("a"), "k", tc),
               _to_scan(f, col("b0"), col("b1"), "k", tc)],
              _to_scan(f, col("v"), col("v"), "v", tc))
    return _from_scan(y, b, tc, "rwkv", f, (col("bonus"), col("g")), (lnx_g, lnx_b))


def _rope_tables(n):
    rows = n // GRID_W
    row = jnp.repeat(jnp.arange(rows, dtype=F32), GRID_W)
    col = jnp.tile(jnp.arange(GRID_W, dtype=F32), rows)
    n_freq = DIFF_QK_DIM // 4
    inv_freq = ROPE_BASE ** (-jnp.arange(n_freq, dtype=F32) / n_freq)
    ang = jnp.concatenate([row[:, None] * inv_freq, col[:, None] * inv_freq], axis=-1)
    return jnp.cos(ang), jnp.sin(ang)


def _rope(x, cos, sin, tc):
    b, t, gw = x.shape
    half = DIFF_QK_DIM // 2
    cos = jnp.concatenate([jnp.ones((tc, half), F32), cos], axis=0)[:, None, :]
    sin = jnp.concatenate([jnp.zeros((tc, half), F32), sin], axis=0)[:, None, :]
    x = x.reshape(b, t, gw // DIFF_QK_DIM, DIFF_QK_DIM)
    x1, x2 = x[..., :half], x[..., half:]
    return jnp.concatenate([x1 * cos - x2 * sin, x2 * cos + x1 * sin], axis=-1).reshape(b, t, gw)


def _diff_attn(p, tc, lam_vecs, norm_g, layer, need_ctx):
    b, t, _ = p.shape
    gw = GROUP_WIDTH
    lam_init = 0.8 - 0.6 * math.exp(-0.3 * layer)
    lv = lam_vecs.astype(F32)
    lam = jnp.exp(jnp.sum(lv[0] * lv[1])) - jnp.exp(jnp.sum(lv[2] * lv[3])) + lam_init
    cos, sin = _rope_tables(t - tc)
    q = (_rope(p[..., :gw], cos, sin, tc) * (DIFF_QK_DIM ** -0.5 * math.log2(math.e))).astype(BF16)
    k = _rope(p[..., gw:2 * gw], cos, sin, tc).astype(BF16)
    q_tile0 = 0 if need_ctx else tc // ATTN_TILE
    return _attention(q, k, p, 2 * gw // LANES, lam, norm_g, 1.0 - lam_init, q_tile0, tc)


def _hgrn2(f, tc, norm_g):
    b = f.shape[0]
    col = HGRN_FEATS.index
    o = _scan("gla",
              [_to_scan(f, col("q"), col("q"), "k", tc), _to_scan(f, col("w0"), col("w1"), "k", tc),
               _to_scan(f, col("k0"), col("k1"), "k", tc)],
              _to_scan(f, col("v"), col("v"), "v", tc))
    return _from_scan(o, b, tc, "gla", f, (col("gate"),), (jnp.tile(norm_g, N_HEADS),))


RWKV_COLS = 3 * GROUP_WIDTH + 2 * RWKV_DECAY_LORA + 2 * RWKV_ICLR_LORA + RWKV_GATE_LORA
DIFF_COLS = 3 * GROUP_WIDTH
HGRN_COLS = 5 * GROUP_WIDTH
MLSTM_COLS = 4 * GROUP_WIDTH + 4 * N_HEADS
SLAB_COLS = (RWKV_COLS, DIFF_COLS, HGRN_COLS, MLSTM_COLS)
SLAB_PAD = (5 * GROUP_WIDTH, DIFF_COLS, HGRN_COLS, 4 * GROUP_WIDTH + LANES)


def _split_cols(w):
    parts, off = [], 0
    for n, n_pad in zip(SLAB_COLS, SLAB_PAD):
        parts.append(jnp.pad(w[:, off:off + n], ((0, 0), (0, n_pad - n))).astype(BF16))
        off += n
    return parts


def kernel(x, c, ctx, c_ctx, ada_w, ada_b, w_in, rwkv_mu, rwkv_w0, rwkv_w2, rwkv_a0, rwkv_a2, rwkv_g2, rwkv_k_k, rwkv_k_a, rwkv_r_k, rwkv_lnx_g, rwkv_lnx_b, diff_lambda, diff_norm_g, hgrn_lb_logits, hgrn_norm_g, mlstm_conv_w, mlstm_conv_b, mlstm_gate_b, mlstm_norm_g, w_out, ffn_w_up, ffn_conv_w, ffn_conv_b, ffn_w_down, ln_g, ln_b):
    depth = w_in.shape[0]
    b, seq, d = x.shape
    tc = ctx.shape[1]
    assert tc == ROW_TILE == ATTN_TILE and seq % ROW_TILE == 0 and 2 * b * N_HEADS <= LANES
    alpha = (2.0 * depth) ** 0.25

    lb_w = jax.nn.softmax(hgrn_lb_logits.astype(F32), axis=0)
    lower_bounds = jnp.cumsum(lb_w, axis=0) - lb_w[0]

    cond = jnp.zeros((8, d), F32).at[:b].set(c).at[b].set(c_ctx)
    h = jnp.concatenate([ctx, x], axis=1)
    for layer in range(depth):
        last = layer == depth - 1
        mod = _adaln(cond, ada_w, ada_b, layer).reshape(8, 6, d)
        mods_l = [mod[:b, j][:, None, :] for j in range(6)]
        mods_c = [mod[b:b + 1, j][:, None, :] for j in range(6)]

        rwkv_prm = (rwkv_mu[layer], rwkv_w0[layer], rwkv_w2[layer], rwkv_a0[layer], rwkv_a2[layer], rwkv_g2[layer],
                    rwkv_k_k[layer], rwkv_k_a[layer], rwkv_r_k[layer])
        f_rwkv, p_diff, f_hgrn, p_mlstm = _inproj(h, mods_l[0], mods_l[1], mods_c[0], mods_c[1],
                                                  _split_cols(w_in[layer]), rwkv_prm, lower_bounds[layer],
                                                  tc // ROW_TILE)
        y_a = _rwkv7(f_rwkv, tc, rwkv_lnx_g[layer], rwkv_lnx_b[layer])
        y_b = _diff_attn(p_diff, tc, diff_lambda[layer], diff_norm_g[layer], layer, not last)
        y_c = _hgrn2(f_hgrn, tc, hgrn_norm_g[layer])
        h_f, h_b = (_mlstm_chunked(p_mlstm, mlstm_conv_w[layer], mlstm_conv_b[layer], mlstm_gate_b[layer], rev, tc)
                    for rev in (False, True))

        tile0, ctx_tiles = (tc // ROW_TILE, 0) if last else (0, tc // ROW_TILE)
        h = _outproj(y_a, y_b, y_c, h_f, h_b, p_mlstm, mlstm_norm_g[layer], h, mods_l[2], mods_c[2],
                     w_out[layer].astype(BF16), ln_g[layer, 0], ln_b[layer, 0], tile0, ctx_tiles, alpha)
        h = _ffn(h, mods_l[3:6], mods_c[3:6], ffn_w_up[layer].astype(BF16), ffn_conv_w[layer], ffn_conv_b[layer],
                 ffn_w_down[layer].astype(BF16), ln_g[layer, 1], ln_b[layer, 1], ctx_tiles, alpha)
    return h
```

```python
import functools
import math

import jax
import jax.numpy as jnp
import numpy as np
from jax import lax
from jax.experimental import pallas as pl
from jax.experimental.pallas import tpu as pltpu

F32 = jnp.float32
BF16 = jnp.bfloat16

HEAD_DIM = 64
N_HEADS = 4
GROUP_WIDTH = N_HEADS * HEAD_DIM
DIFF_QK_DIM = HEAD_DIM // 2
GRID_W = 64
ROPE_BASE = 10000.0
RWKV_DECAY_LORA = 64
RWKV_ICLR_LORA = 64
RWKV_GATE_LORA = 160
RWKV_LNX_EPS = 64e-5
LN_EPS = 1e-5
LB_FLOOR = 1e-30
MAX_NEG_LOG_STAB = 60.0

LANES = 128
ROW_TILE = 256
ATTN_TILE = 256
TIME_BLOCK = 128
SCAN_TB = 64
VMEM_LIMIT = 56 * 1024 * 1024


def _cparams(sem):
    return pltpu.CompilerParams(dimension_semantics=sem, vmem_limit_bytes=VMEM_LIMIT)


def _rev_block(i, n_ctx, n_all):
    return jnp.where(i < n_ctx, n_ctx - 1 - i, n_ctx + n_all - 1 - i)


def _split3(x):
    hi = x.astype(BF16)
    r1 = x - hi.astype(F32)
    mid = r1.astype(BF16)
    return hi, mid, (r1 - mid.astype(F32)).astype(BF16)


def _dot01_left(a01, x):
    a = a01.astype(BF16)
    p = [jnp.dot(a, part, preferred_element_type=F32) for part in _split3(x)]
    return (p[0] + p[1]) + p[2]


def _dot01_right(x, b01):
    b = b01.astype(BF16)
    p = [jnp.dot(part, b, preferred_element_type=F32) for part in _split3(x)]
    return (p[0] + p[1]) + p[2]


def _flip_rows(x):
    n = x.shape[0]
    r = lax.broadcasted_iota(jnp.int32, (n, n), 0)
    c = lax.broadcasted_iota(jnp.int32, (n, n), 1)
    return _dot01_left(jnp.where(r + c == n - 1, 1.0, 0.0).astype(F32), x)


def _adaln_body(c_ref, w_ref, b_ref, o_ref):
    x = c_ref[...]
    x = (x * jax.nn.sigmoid(x)).astype(BF16)
    o_ref[...] = jnp.dot(x, w_ref[0].astype(BF16), preferred_element_type=F32) + b_ref[0]


def _adaln(cond, w, b, layer):
    m, d = cond.shape
    n = w.shape[2]
    tn = 1536
    return pl.pallas_call(
        _adaln_body,
        grid=(n // tn,),
        in_specs=[pl.BlockSpec((m, d), lambda j: (0, 0)),
                  pl.BlockSpec((1, d, tn), lambda j: (layer, 0, j)),
                  pl.BlockSpec((1, 1, tn), lambda j: (layer, 0, j))],
        out_specs=pl.BlockSpec((m, tn), lambda j: (0, j)),
        out_shape=jax.ShapeDtypeStruct((m, n), F32),
        compiler_params=_cparams(("arbitrary",)),
        name="adaln",
    )(cond, w, b.reshape(b.shape[0], 1, n))


def _softplus(x):
    return jnp.maximum(x, 0.0) + jnp.log(1.0 + jnp.exp(-jnp.abs(x)))


def _head_block(scale=1.0):
    r = lax.broadcasted_iota(jnp.int32, (GROUP_WIDTH, GROUP_WIDTH), 0) // HEAD_DIM
    c = lax.broadcasted_iota(jnp.int32, (GROUP_WIDTH, GROUP_WIDTH), 1) // HEAD_DIM
    return jnp.where(r == c, scale, 0.0).astype(F32)


def _head_sum(x, block_bf16):
    hi = x.astype(BF16)
    lo = (x - hi.astype(F32)).astype(BF16)
    return (jnp.dot(hi, block_bf16, preferred_element_type=F32)
            + jnp.dot(lo, block_bf16, preferred_element_type=F32))


def _shifted_rows(x, prev8, next8, seg_start, seg_end):
    rows = x.shape[0]
    row_id = lax.broadcasted_iota(jnp.int32, (rows, 1), 0)
    prev_row = jnp.where(seg_start, 0.0, prev8[7:8, :])
    next_row = jnp.where(seg_end, 0.0, next8[0:1, :])
    prev = jnp.where(row_id == 0, prev_row, pltpu.roll(x, 1, 0))
    nxt = jnp.where(row_id == rows - 1, next_row, pltpu.roll(x, rows - 1, 0))
    return prev, nxt


RWKV_FEATS = ("r", "v", "w0", "w1", "k0", "k1", "a", "b0", "b1", "g", "bonus")


def _rwkv_feats(p, prev8, next8, seg_start, seg_end, mu_ref, w0_ref, w2_ref, a0_ref, a2_ref, g2_ref, kk_ref,
                ka_ref, rk_ref):
    gw = GROUP_WIDTH
    prev, nxt = _shifted_rows(p, prev8, next8, seg_start, seg_end)
    p = p + (0.5 * (prev + nxt) - p) * mu_ref[...]
    r, k, v = p[:, :gw], p[:, gw:2 * gw], p[:, 2 * gw:3 * gw]
    wd, ad, gd = p[:, 3 * gw:3 * gw + LANES], p[:, 3 * gw + LANES:4 * gw], p[:, 4 * gw:5 * gw]
    lora_w = jnp.dot(jnp.tanh(wd).astype(BF16), w2_ref[...], preferred_element_type=F32)
    lora_a = jnp.dot(ad.astype(BF16), a2_ref[...], preferred_element_type=F32)
    decay = jnp.exp(-jnp.exp(-_softplus(-(w0_ref[...] + lora_w)) - 0.5))
    a = jax.nn.sigmoid(a0_ref[...] + lora_a)
    g = jnp.dot(jax.nn.sigmoid(gd).astype(BF16), g2_ref[...], preferred_element_type=F32)
    block = _head_block().astype(BF16)
    kk = k * kk_ref[...]
    kk = kk / jnp.maximum(jnp.sqrt(_head_sum(kk * kk, block)), 1e-12)
    k0 = k * (1.0 + (a[:, :gw] - 1.0) * ka_ref[...])
    k1 = k * (1.0 + (a[:, gw:] - 1.0) * ka_ref[...])
    bonus = _head_sum(r * (k0 + k1) * rk_ref[...], block) * v
    return dict(r=r, v=v, w0=decay[:, :gw], w1=decay[:, gw:], k0=k0, k1=k1, a=-kk, b0=kk * a[:, :gw],
                b1=kk * a[:, gw:], g=g, bonus=bonus)


def _block_diag2(w):
    z = jnp.zeros_like(w[0])
    return jnp.concatenate([jnp.concatenate([w[0], z], axis=1), jnp.concatenate([z, w[1]], axis=1)], axis=0)


HGRN_FEATS = ("q", "w0", "w1", "k0", "k1", "gate", "v")


def _hgrn_feats(p, llb_ref, l1m_ref, oml_ref):
    gw = GROUP_WIDTH
    q, f_f, f_b, v, g = (p[:, j * gw:(j + 1) * gw] for j in range(5))
    feats = dict(q=q * jax.nn.sigmoid(q), gate=g * jax.nn.sigmoid(g), v=v)
    for d, f in enumerate((f_f, f_b)):
        x = llb_ref[...]
        y = l1m_ref[...] - _softplus(-f)
        log_f = jnp.maximum(x, y) + jnp.log(1.0 + jnp.exp(-jnp.abs(x - y)))
        feats["w%d" % d] = jnp.exp(log_f)
        feats["k%d" % d] = oml_ref[...] * jax.nn.sigmoid(-f)
    return feats


N_RWKV_PRM, N_HGRN_PRM = 9, 3


def _rope(x, cos, sin_signed):
    half = DIFF_QK_DIM // 2
    lane = lax.broadcasted_iota(jnp.int32, x.shape, 1)
    n = x.shape[1]
    partner = jnp.where(lane % DIFF_QK_DIM < half, pltpu.roll(x, n - half, 1), pltpu.roll(x, half, 1))
    return x * cos + partner * sin_signed


def _inproj_body(x_ref, xp_ref, xn_ref, sl_ref, cl_ref, sc_ref, cc_ref, cos_ref, sin_ref, wr_ref, wd_ref, wh_ref,
                 wm_ref, *refs, ctx_tiles, n_tiles):
    rwkv_prm, hgrn_prm = refs[:N_RWKV_PRM], refs[N_RWKV_PRM:N_RWKV_PRM + N_HGRN_PRM]
    fr_ref, q_ref, k_ref, v_ref, fh_ref, pm_ref = refs[N_RWKV_PRM + N_HGRN_PRM:]
    i = pl.program_id(1)
    is_ctx = i < ctx_tiles
    seg_start = jnp.logical_or(i == 0, i == ctx_tiles)
    seg_end = jnp.logical_or(i == ctx_tiles - 1, i == n_tiles - 1)
    shift = jnp.where(is_ctx, sc_ref[0], sl_ref[0])
    scale = jnp.where(is_ctx, cc_ref[0], cl_ref[0])
    mod = lambda x: (x * (1.0 + scale) + shift).astype(BF16)
    proj = lambda x, w_ref: jnp.dot(x, w_ref[...], preferred_element_type=F32)
    gw = GROUP_WIDTH
    xm = mod(x_ref[0])
    feats = _rwkv_feats(proj(xm, wr_ref), proj(mod(xp_ref[0]), wr_ref), proj(mod(xn_ref[0]), wr_ref),
                        seg_start, seg_end, *rwkv_prm)
    for j, name in enumerate(RWKV_FEATS):
        fr_ref[0, :, j * gw:(j + 1) * gw] = feats[name]
    pd = proj(xm, wd_ref)
    q_ref[0] = (_rope(pd[:, :gw], cos_ref[...], sin_ref[...])
                * (DIFF_QK_DIM ** -0.5 * math.log2(math.e))).astype(BF16)
    k_ref[0] = _rope(pd[:, gw:2 * gw], cos_ref[...], sin_ref[...]).astype(BF16)
    v_ref[0] = pd[:, 2 * gw:]
    feats = _hgrn_feats(proj(xm, wh_ref), *hgrn_prm)
    for j, name in enumerate(HGRN_FEATS):
        fh_ref[0, :, j * gw:(j + 1) * gw] = feats[name]
    pm_ref[0] = proj(xm, wm_ref)


def _inproj(h, shift_l, scale_l, shift_c, scale_c, rope, ws, rwkv_prm, lb, ctx_tiles):
    b, t, d = h.shape
    gw = GROUP_WIDTH
    n_tiles, sub, n_sub = t // ROW_TILE, ROW_TILE // 8, t // 8
    row = lambda n: pl.BlockSpec((1, ROW_TILE, n), lambda bi, i: (bi, i, 0))
    prev = pl.BlockSpec((1, 8, d), lambda bi, i: (bi, jnp.maximum(i * sub - 1, 0), 0))
    nxt = pl.BlockSpec((1, 8, d), lambda bi, i: (bi, jnp.minimum((i + 1) * sub, n_sub - 1), 0))
    lat = pl.BlockSpec((1, 1, d), lambda bi, i: (bi, 0, 0))
    ctx = pl.BlockSpec((1, 1, d), lambda bi, i: (0, 0, 0))
    full = lambda a: pl.BlockSpec(a.shape, lambda bi, i: (0,) * a.ndim)
    vec = lambda a: a.reshape(1, -1).astype(F32)
    mu, w0, w2, a0, a2, g2, k_k, k_a, r_k = rwkv_prm
    cols = ws[0].shape[1]
    prm = [jnp.pad(vec(mu), ((0, 0), (0, cols - mu.shape[0]))), vec(w0), _block_diag2(w2).astype(BF16), vec(a0),
           _block_diag2(a2).astype(BF16), jnp.pad(g2, ((0, gw - g2.shape[0]), (0, 0))).astype(BF16), vec(k_k),
           vec(k_a), vec(r_k),
           vec(jnp.log(jnp.maximum(lb, LB_FLOOR))), vec(jnp.log1p(-lb)), vec(1.0 - lb)]
    assert len(prm) == N_RWKV_PRM + N_HGRN_PRM
    outs = [(len(RWKV_FEATS) * gw, F32), (gw, BF16), (gw, BF16), (gw, F32), (len(HGRN_FEATS) * gw, F32),
            (ws[3].shape[1], F32)]
    tab = pl.BlockSpec((ROW_TILE, gw), lambda bi, i: (i, 0))
    return pl.pallas_call(
        functools.partial(_inproj_body, ctx_tiles=ctx_tiles, n_tiles=n_tiles),
        grid=(b, n_tiles),
        in_specs=[row(d), prev, nxt, lat, lat, ctx, ctx, tab, tab] + [full(a) for a in list(ws) + prm],
        out_specs=[row(n) for n, _ in outs],
        out_shape=[jax.ShapeDtypeStruct((b, t, n), dt) for n, dt in outs],
        compiler_params=_cparams(("parallel", "parallel")),
        name="inproj",
    )(h, h, h, shift_l, scale_l, shift_c, scale_c, *rope, *ws, *prm)


def _to_scan_body(x0_ref, x1_ref, o_ref, r_ref, *, kind, nb):
    n_scan = 2 * nb * N_HEADS
    rep = LANES // n_scan
    for b in range(nb):
        r_ref[pl.ds(b * GROUP_WIDTH, GROUP_WIDTH), :] = x0_ref[b].T
        r_ref[pl.ds((nb + b) * GROUP_WIDTH, GROUP_WIDTH), :] = _flip_rows(x1_ref[b]).T
    if kind == "k":
        for k in range(HEAD_DIM):
            rows = r_ref[pl.ds(k, n_scan, stride=HEAD_DIM), :]
            o_ref[k] = jnp.concatenate([rows] * rep, axis=0).T
    else:
        for vh in range(HEAD_DIM // rep):
            rows = [r_ref[pl.ds(vh * rep + vl, n_scan, stride=HEAD_DIM), :] for vl in range(rep)]
            o_ref[:, vh, :] = jnp.concatenate(rows, axis=0).T


def _to_scan(x, col0, col1, kind, tc):
    nb, t, _ = x.shape
    gw = GROUP_WIDTH
    x0 = x1 = x
    n_all, n_ctx = t // TIME_BLOCK, tc // TIME_BLOCK
    n_scan = 2 * nb * N_HEADS
    rep = LANES // n_scan
    fwd = pl.BlockSpec((nb, TIME_BLOCK, gw), lambda i: (0, i, col0))
    bwd = pl.BlockSpec((nb, TIME_BLOCK, gw), lambda i: (0, _rev_block(i, n_ctx, n_all), col1))
    if kind == "k":
        out_shape = (HEAD_DIM, t, LANES)
        out_spec = pl.BlockSpec((HEAD_DIM, TIME_BLOCK, LANES), lambda i: (0, i, 0))
    else:
        out_shape = (t, HEAD_DIM // rep, LANES)
        out_spec = pl.BlockSpec((TIME_BLOCK, HEAD_DIM // rep, LANES), lambda i: (i, 0, 0))
    return pl.pallas_call(
        functools.partial(_to_scan_body, kind=kind, nb=nb),
        grid=(n_all,),
        in_specs=[fwd, bwd],
        out_specs=out_spec,
        out_shape=jax.ShapeDtypeStruct(out_shape, F32),
        scratch_shapes=[pltpu.VMEM((n_scan * HEAD_DIM, TIME_BLOCK), F32)],
        compiler_params=_cparams(("parallel",)),
        name="to_scan_" + kind,
    )(x0, x1)


def _from_scan_body(yf_ref, yb_ref, *rest, nb, readout):
    if readout == "rwkv":
        bonus_ref, g_ref, ng_ref, nb_ref, o_ref, r_ref = rest
    else:
        gate_ref, ng_ref, o_ref, r_ref = rest
    n_scan = 2 * nb * N_HEADS
    rep = LANES // n_scan
    for vh in range(HEAD_DIM // rep):
        r_ref[0, pl.ds(vh * LANES, LANES), :] = yf_ref[vh].T
        r_ref[1, pl.ds(vh * LANES, LANES), :] = yb_ref[vh].T
    mean_block = _head_block(1.0 / HEAD_DIM).astype(BF16)
    for b in range(nb):
        slab = lambda d: jnp.concatenate(
            [r_ref[d, pl.ds((d * nb + b) * N_HEADS + h, HEAD_DIM, stride=n_scan), :] for h in range(N_HEADS)],
            axis=0).T
        y = slab(0) + _flip_rows(slab(1))
        if readout == "rwkv":
            y = y - _head_sum(y, mean_block)
            y = y * lax.rsqrt(_head_sum(y * y, mean_block) + RWKV_LNX_EPS) * ng_ref[...] + nb_ref[...]
            o_ref[b] = (y + bonus_ref[b]) * g_ref[b]
        else:
            y = y * lax.rsqrt(_head_sum(y * y, mean_block) + LN_EPS) * ng_ref[...]
            o_ref[b] = y * gate_ref[b]


def _from_scan(y, nb, tc, readout, feats, cols, vecs):
    n_vh, t, _ = y.shape
    gw = GROUP_WIDTH
    n_all, n_ctx = t // TIME_BLOCK, tc // TIME_BLOCK
    fwd = pl.BlockSpec((n_vh, TIME_BLOCK, LANES), lambda i: (0, i, 0))
    bwd = pl.BlockSpec((n_vh, TIME_BLOCK, LANES), lambda i: (0, _rev_block(i, n_ctx, n_all), 0))
    tok = lambda c: pl.BlockSpec((nb, TIME_BLOCK, gw), lambda i: (0, i, c))
    vec = pl.BlockSpec((1, gw), lambda i: (0, 0))
    return pl.pallas_call(
        functools.partial(_from_scan_body, nb=nb, readout=readout),
        grid=(n_all,),
        in_specs=[fwd, bwd] + [tok(c) for c in cols] + [vec] * len(vecs),
        out_specs=tok(0),
        out_shape=jax.ShapeDtypeStruct((nb, t, gw), F32),
        scratch_shapes=[pltpu.VMEM((2, n_vh * LANES, TIME_BLOCK), F32)],
        compiler_params=_cparams(("parallel",)),
        name="from_scan_" + readout,
    )(y, y, *([feats] * len(cols)), *[v.reshape(1, gw) for v in vecs])


N_ACC = 4


def _acc_add(acc, i, x):
    acc[i % N_ACC] = x if acc[i % N_ACC] is None else acc[i % N_ACC] + x


def _acc_total(acc):
    return (acc[0] + acc[1]) + (acc[2] + acc[3])


def _scan_body(*refs, mode, tb, n_vh):
    if mode == "rwkv":
        r_ref, w_ref, k_ref, a_ref, b_ref, v_ref, y_ref, s_ref = refs
    else:
        r_ref, w_ref, k_ref, v_ref, y_ref, s_ref = refs

    @pl.when(pl.program_id(0) == 0)
    def _():
        s_ref[...] = jnp.zeros_like(s_ref)

    row = lambda ref, k, j: ref[k, pl.ds(j, 1), :]
    wide = lambda x: jnp.broadcast_to(x, (n_vh, LANES))

    def store_y(j, y):
        for vh in range(n_vh):
            y_ref[vh, pl.ds(j, 1), :] = y[vh:vh + 1, :]

    def rwkv_step(j, sa):
        v = v_ref[j]
        j_next = jnp.minimum(j + 1, tb - 1)
        ys, sas = [None] * N_ACC, [None] * N_ACC
        for k in range(HEAD_DIM):
            s = s_ref[k] * wide(row(w_ref, k, j)) + sa * wide(row(b_ref, k, j)) + v * wide(row(k_ref, k, j))
            s_ref[k] = s
            _acc_add(ys, k, s * wide(row(r_ref, k, j)))
            _acc_add(sas, k, s * wide(row(a_ref, k, j_next)))
        store_y(j, _acc_total(ys))
        return _acc_total(sas)

    def gla_step(j, carry):
        v = v_ref[j]
        ys = [None] * N_ACC
        for k in range(HEAD_DIM):
            s = s_ref[k] * wide(row(w_ref, k, j)) + v * wide(row(k_ref, k, j))
            s_ref[k] = s
            _acc_add(ys, k, s * wide(row(r_ref, k, j)))
        store_y(j, _acc_total(ys))
        return carry

    if mode == "rwkv":
        sa0 = [None] * N_ACC
        for k in range(HEAD_DIM):
            _acc_add(sa0, k, s_ref[k] * wide(row(a_ref, k, 0)))
        lax.fori_loop(0, tb, rwkv_step, _acc_total(sa0), unroll=8)
    else:
        lax.fori_loop(0, tb, gla_step, 0, unroll=8)


def _scan(mode, k_inputs, v_in):
    t, n_vh, _ = v_in.shape
    tb = SCAN_TB
    big = pl.BlockSpec((HEAD_DIM, tb, LANES), lambda i: (0, i, 0))
    small = pl.BlockSpec((tb, n_vh, LANES), lambda i: (i, 0, 0))
    return pl.pallas_call(
        functools.partial(_scan_body, mode=mode, tb=tb, n_vh=n_vh),
        grid=(t // tb,),
        in_specs=[big] * len(k_inputs) + [small],
        out_specs=pl.BlockSpec((n_vh, tb, LANES), lambda i: (0, i, 0)),
        out_shape=jax.ShapeDtypeStruct((n_vh, t, LANES), F32),
        scratch_shapes=[pltpu.VMEM((HEAD_DIM, n_vh, LANES), F32)],
        compiler_params=_cparams(("arbitrary",)),
        name="scan_" + mode,
    )(*k_inputs, v_in)


MLSTM_CHUNK = 64


def _prefix_max(x, reverse):
    n = x.shape[0]
    row = lax.broadcasted_iota(jnp.int32, x.shape, 0)
    sh = 1
    while sh < n:
        if reverse:
            x = jnp.where(row < n - sh, jnp.maximum(x, pltpu.roll(x, n - sh, 0)), x)
        else:
            x = jnp.where(row >= sh, jnp.maximum(x, pltpu.roll(x, sh, 0)), x)
        sh *= 2
    return x


def _mlstm_body(qk_ref, qkp_ref, qkn_ref, v_ref, gt_ref, cw_ref, cb_ref, sel_ref, gb_ref, o_ref, st_ref, m_ref, *,
                reverse, n_sub, n_ctx, n_blocks):
    c, gw = MLSTM_CHUNK, GROUP_WIDTH
    i = pl.program_id(1)

    @pl.when(i == 0)
    def _():
        st_ref[...] = jnp.zeros_like(st_ref)
        m_ref[...] = jnp.zeros_like(m_ref)

    blk = _rev_block(i, n_ctx, n_blocks) if reverse else i
    seg_start = jnp.logical_or(blk == 0, blk == n_ctx)
    seg_end = jnp.logical_or(blk == n_ctx - 1, blk == n_blocks - 1)
    qk = qk_ref[0]
    prev, nxt = _shifted_rows(qk, qkp_ref[0], qkn_ref[0], seg_start, seg_end)
    qk = prev * cw_ref[0:1, :] + qk * cw_ref[1:2, :] + nxt * cw_ref[2:3, :] + cb_ref[...]
    qk = qk * jax.nn.sigmoid(qk)
    q_all, k_all = qk[:, :gw], qk[:, gw:] * HEAD_DIM ** -0.5
    v_all = v_ref[0]
    gt = gt_ref[0]
    ig_all = _dot01_right(gt, sel_ref[0]) + gb_ref[0:1, :]
    fg_all = -_softplus(-(_dot01_right(gt, sel_ref[1]) + gb_ref[1:2, :]))

    tt = lax.broadcasted_iota(jnp.int32, (c, c), 0)
    ss = lax.broadcasted_iota(jnp.int32, (c, c), 1)
    tri = jnp.where((ss >= tt) if reverse else (ss <= tt), 1.0, 0.0).astype(F32)
    row = lax.broadcasted_iota(jnp.int32, (c, gw), 0)
    s_of_lane = lax.broadcasted_iota(jnp.int32, (c, gw), 1) % c
    causal = (s_of_lane >= row) if reverse else (s_of_lane <= row)
    diag = s_of_lane == row
    block = _head_block()
    block2 = jnp.concatenate([block, block], axis=1)
    ones8 = jnp.ones((8, c), F32)
    nt = (((1,), (1,)), ((), ()))
    last = 0 if reverse else c - 1
    for u in (range(n_sub - 1, -1, -1) if reverse else range(n_sub)):
        sl = slice(u * c, (u + 1) * c)
        q, k, v, ig, fg = q_all[sl], k_all[sl], v_all[sl], ig_all[sl], fg_all[sl]
        b = _dot01_left(tri, fg)
        g = ig - b
        m_prev = m_ref[0:1, :]
        m_t = b + jnp.maximum(m_prev, _prefix_max(g, reverse))
        w_inter = jnp.exp(b + m_prev - m_t)
        g_row = _dot01_left(ones8, jnp.where(diag, g, 0.0))[0:1]
        qb = q.astype(BF16)
        k_bd = (jnp.concatenate([k] * N_HEADS, axis=0) * block).astype(BF16)
        scores = lax.dot_general(qb, k_bd, nt, preferred_element_type=F32)
        w = jnp.where(causal, jnp.exp((b - m_t) + g_row), 0.0) * scores
        v_bd = jnp.concatenate([jnp.concatenate([v] * N_HEADS, axis=0) * block, block], axis=1).astype(BF16)
        intra = jnp.dot(w.astype(BF16), v_bd, preferred_element_type=F32)
        inter = jnp.dot(qb, st_ref[...].astype(BF16), preferred_element_type=F32)
        num = w_inter * inter[:, :gw] + intra[:, :gw]
        den = w_inter * inter[:, gw:] + intra[:, gw:]
        floor = jnp.exp(jnp.minimum(-m_t, MAX_NEG_LOG_STAB))
        o_ref[0, pl.ds(u * c, c), :] = num / jnp.maximum(jnp.abs(den), floor)
        m_new = m_t[last:last + 1]
        b_end = b[last:last + 1]
        kw = k * jnp.exp(b_end - b + ig - m_new)
        decay = jnp.exp(b_end + m_prev - m_new)
        v_one = jnp.concatenate([v, jnp.ones_like(v)], axis=1).astype(BF16)
        upd = jnp.dot(kw.T.astype(BF16), v_one, preferred_element_type=F32)
        st_ref[...] = st_ref[...] * jnp.concatenate([decay, decay], axis=1) + upd * block2
        m_ref[0:1, :] = m_new


def _mlstm_chunked(p, conv_w, conv_b, gate_b, reverse, tc):
    b, t, _ = p.shape
    gw = GROUP_WIDTH
    n_blocks, n_ctx = t // ROW_TILE, tc // ROW_TILE
    sub, n_sub8 = ROW_TILE // 8, t // 8
    blk = (lambda i: _rev_block(i, n_ctx, n_blocks)) if reverse else (lambda i: i)
    d = 1 if reverse else 0
    lane_head = np.arange(gw) // HEAD_DIM
    sel = np.zeros((2, LANES, gw), np.float32)
    for j, base in enumerate((d * N_HEADS, (2 + d) * N_HEADS)):
        sel[j, base + lane_head, np.arange(gw)] = 1.0
    gate_bias = jnp.stack([jnp.repeat(gate_b[d], HEAD_DIM), jnp.repeat(gate_b[2 + d], HEAD_DIM)], axis=0)
    full = lambda a: pl.BlockSpec(a.shape, lambda bi, i: (0,) * a.ndim)
    cb = conv_b.reshape(1, 2 * gw)
    return pl.pallas_call(
        functools.partial(_mlstm_body, reverse=reverse, n_sub=ROW_TILE // MLSTM_CHUNK, n_ctx=n_ctx,
                          n_blocks=n_blocks),
        grid=(b, n_blocks),
        in_specs=[pl.BlockSpec((1, ROW_TILE, 2 * gw), lambda bi, i: (bi, blk(i), 0)),
                  pl.BlockSpec((1, 8, 2 * gw), lambda bi, i: (bi, jnp.maximum(blk(i) * sub - 1, 0), 0)),
                  pl.BlockSpec((1, 8, 2 * gw), lambda bi, i: (bi, jnp.minimum((blk(i) + 1) * sub, n_sub8 - 1), 0)),
                  pl.BlockSpec((1, ROW_TILE, gw), lambda bi, i: (bi, blk(i), 2)),
                  pl.BlockSpec((1, ROW_TILE, LANES), lambda bi, i: (bi, blk(i), 4 * gw // LANES)),
                  full(conv_w), full(cb), full(sel), full(gate_bias)],
        out_specs=pl.BlockSpec((1, ROW_TILE, gw), lambda bi, i: (bi, blk(i), 0)),
        out_shape=jax.ShapeDtypeStruct((b, t, gw), F32),
        scratch_shapes=[pltpu.VMEM((gw, 2 * gw), F32), pltpu.VMEM((8, gw), F32)],
        compiler_params=_cparams(("parallel", "arbitrary")),
        name="mlstm_chunk",
    )(p, p, p, p, p, conv_w, cb, jnp.asarray(sel), gate_bias)


def _attn_body(q_ref, k_ref, v_ref, lam_ref, g_ref, o_ref, *, out_scale, ctx_tiles, n_ctx_keys):
    def attend(n_keys):
        q = q_ref[0]
        k = k_ref[0, :n_keys, :]
        v = v_ref[0, :n_keys, :].astype(BF16)
        lane = lax.broadcasted_iota(jnp.int32, q.shape, 1)
        comp = lane // DIFF_QK_DIM
        nt = (((1,), (1,)), ((), ()))
        n_maps = LANES // DIFF_QK_DIM
        ss = [lax.dot_general(jnp.where(comp == c, q, 0.0).astype(BF16), k, nt, preferred_element_type=F32)
              for c in range(n_maps)]
        es = [jnp.exp2(s - jnp.max(s, axis=-1, keepdims=True)) for s in ss]
        ls = [jnp.sum(e, axis=-1, keepdims=True) for e in es]
        ys = []
        for hh in range(LANES // HEAD_DIM):
            e0, e1, l0, l1 = es[2 * hh], es[2 * hh + 1], ls[2 * hh], ls[2 * hh + 1]
            pr = (e0 - e1 * (lam_ref[0:1, 0:1] * l0 / l1)).astype(BF16)
            ys.append(jnp.dot(pr, v, preferred_element_type=F32) * (1.0 / l0))
        first = lane < HEAD_DIM
        y = jnp.where(first, ys[0], ys[1])
        ysq = y * y
        ms = jnp.where(first, jnp.sum(jnp.where(first, ysq, 0.0), axis=-1, keepdims=True),
                       jnp.sum(jnp.where(first, 0.0, ysq), axis=-1, keepdims=True)) * (1.0 / HEAD_DIM)
        o_ref[0] = y * lax.rsqrt(ms + LN_EPS) * g_ref[...] * out_scale

    if ctx_tiles == 0:
        attend(k_ref.shape[1])
    else:
        is_ctx = pl.program_id(2) < ctx_tiles
        pl.when(is_ctx)(lambda: attend(n_ctx_keys))
        pl.when(jnp.logical_not(is_ctx))(lambda: attend(k_ref.shape[1]))


def _attention(q, k, v, v_col0, lam, norm_g, out_scale, q_tile0, tc):
    b, t, gw = q.shape
    n_tiles = t // ATTN_TILE - q_tile0
    ctx_tiles = max(tc // ATTN_TILE - q_tile0, 0)
    qspec = pl.BlockSpec((1, ATTN_TILE, LANES), lambda bi, pi, i: (bi, q_tile0 + i, pi))
    kspec = pl.BlockSpec((1, t, LANES), lambda bi, pi, i: (bi, 0, pi))
    vspec = pl.BlockSpec((1, t, LANES), lambda bi, pi, i: (bi, 0, v_col0 + pi))
    vec = pl.BlockSpec((1, LANES), lambda bi, pi, i: (0, 0))
    return pl.pallas_call(
        functools.partial(_attn_body, out_scale=out_scale, ctx_tiles=ctx_tiles, n_ctx_keys=tc),
        grid=(b, gw // LANES, n_tiles),
        in_specs=[qspec, kspec, vspec, vec, vec],
        out_specs=pl.BlockSpec((1, ATTN_TILE, LANES), lambda bi, pi, i: (bi, i, pi)),
        out_shape=jax.ShapeDtypeStruct((b, n_tiles * ATTN_TILE, gw), F32),
        compiler_params=_cparams(("parallel", "parallel", "parallel")),
        name="diff_attn",
    )(q, k, v, jnp.full((1, LANES), lam, F32), jnp.tile(norm_g.reshape(1, HEAD_DIM), (1, LANES // HEAD_DIM)))


def _layernorm(z, g, b):
    z = z - jnp.mean(z, axis=-1, keepdims=True)
    return z * lax.rsqrt(jnp.mean(z * z, axis=-1, keepdims=True) + LN_EPS) * g + b


def _outproj_body(ya_ref, yb_ref, yc_ref, hf_ref, hb_ref, og_ref, ng_ref, h_ref, gl_ref, gc_ref, w_ref, g_ref,
                  b_ref, o_ref, *, ctx_tiles, alpha):
    is_ctx = pl.program_id(1) < ctx_tiles
    gate = jnp.where(is_ctx, gc_ref[0], gl_ref[0])
    mean_block = _head_block(1.0 / HEAD_DIM).astype(BF16)
    yd = hf_ref[0] + hb_ref[0]
    yd = yd - _head_sum(yd, mean_block)
    yd = yd * lax.rsqrt(_head_sum(yd * yd, mean_block) + LN_EPS) * ng_ref[...] * jax.nn.sigmoid(og_ref[0])
    y = None
    for m, ym in enumerate((ya_ref[0], yb_ref[0], yc_ref[0], yd)):
        part = jnp.dot(ym.astype(BF16), w_ref[m * GROUP_WIDTH:(m + 1) * GROUP_WIDTH, :],
                       preferred_element_type=F32)
        y = part if y is None else y + part
    o_ref[0] = _layernorm(alpha * h_ref[0] + gate * y, g_ref[...], b_ref[...])


def _outproj(ya, yb, yc, hf, hb, p_mlstm, mlstm_norm_g, h, gate_l, gate_c, w, ln_g, ln_b, tile0, ctx_tiles, alpha):
    b, t, d = h.shape
    gw = GROUP_WIDTH
    n_tiles = t // ROW_TILE - tile0
    yrow = lambda c: pl.BlockSpec((1, ROW_TILE, gw), lambda bi, i: (bi, tile0 + i, c))
    yb_row = pl.BlockSpec((1, ROW_TILE, gw), lambda bi, i: (bi, i, 0))
    hrow = pl.BlockSpec((1, ROW_TILE, d), lambda bi, i: (bi, tile0 + i, 0))
    lat = pl.BlockSpec((1, 1, d), lambda bi, i: (bi, 0, 0))
    ctx = pl.BlockSpec((1, 1, d), lambda bi, i: (0, 0, 0))
    vec = lambda n: pl.BlockSpec((1, n), lambda bi, i: (0, 0))
    return pl.pallas_call(
        functools.partial(_outproj_body, ctx_tiles=ctx_tiles, alpha=alpha),
        grid=(b, n_tiles),
        in_specs=[yrow(0), yb_row, yrow(0), yrow(0), yrow(0), yrow(3), vec(gw), hrow, lat, ctx,
                  pl.BlockSpec(w.shape, lambda bi, i: (0, 0)), vec(d), vec(d)],
        out_specs=pl.BlockSpec((1, ROW_TILE, d), lambda bi, i: (bi, i, 0)),
        out_shape=jax.ShapeDtypeStruct((b, n_tiles * ROW_TILE, d), F32),
        compiler_params=_cparams(("parallel", "parallel")),
        name="outproj_ln",
    )(ya, yb, yc, hf, hb, p_mlstm, mlstm_norm_g.reshape(1, gw), h, gate_l, gate_c, w, ln_g.reshape(1, d),
      ln_b.reshape(1, d))


FFN_CHUNKS = 1


def _ffn_body(x_ref, xp_ref, xn_ref, sl_ref, cl_ref, gl_ref, sc_ref, cc_ref, gc_ref,
              wu_ref, cw_ref, cb_ref, wd_ref, g_ref, b_ref, o_ref, *, ctx_tiles, n_tiles, alpha, d_ff):
    i = pl.program_id(1)
    is_ctx = i < ctx_tiles
    shift = jnp.where(is_ctx, sc_ref[0], sl_ref[0])
    scale = jnp.where(is_ctx, cc_ref[0], cl_ref[0])
    gate_mod = jnp.where(is_ctx, gc_ref[0], gl_ref[0])
    seg_start = jnp.logical_or(i == 0, i == ctx_tiles)
    seg_end = jnp.logical_or(i == ctx_tiles - 1, i == n_tiles - 1)

    x = x_ref[0]
    xm = (x * (1.0 + scale) + shift).astype(BF16)
    xp = (xp_ref[0] * (1.0 + scale) + shift).astype(BF16)
    xn = (xn_ref[0] * (1.0 + scale) + shift).astype(BF16)
    rows = x.shape[0]
    row_id = lax.broadcasted_iota(jnp.int32, (rows, 1), 0)
    ch = d_ff // FFN_CHUNKS
    f = jnp.zeros(x.shape, F32)
    for c in range(FFN_CHUNKS):
        wg = wu_ref[:, c * ch:(c + 1) * ch]
        wv = wu_ref[:, d_ff + c * ch:d_ff + (c + 1) * ch]
        gate = jnp.dot(xm, wg, preferred_element_type=F32)
        val = jnp.dot(xm, wv, preferred_element_type=F32)
        gp = jnp.dot(xp, wg, preferred_element_type=F32)[7:8, :]
        gn = jnp.dot(xn, wg, preferred_element_type=F32)[0:1, :]
        gp = jnp.where(seg_start, 0.0, gp)
        gn = jnp.where(seg_end, 0.0, gn)
        prev = jnp.where(row_id == 0, gp, pltpu.roll(gate, 1, 0))
        nxt = jnp.where(row_id == rows - 1, gn, pltpu.roll(gate, rows - 1, 0))
        cw = cw_ref[:, c * ch:(c + 1) * ch]
        conv = prev * cw[0:1, :] + gate * cw[1:2, :] + nxt * cw[2:3, :] + cb_ref[:, c * ch:(c + 1) * ch]
        act = (jax.nn.gelu(conv) * val).astype(BF16)
        f = f + jnp.dot(act, wd_ref[c * ch:(c + 1) * ch, :], preferred_element_type=F32)
    o_ref[0] = _layernorm(alpha * x + gate_mod * f, g_ref[...], b_ref[...])


def _ffn(h, mods_l, mods_c, w_up, conv_w, conv_b, w_down, ln_g, ln_b, ctx_tiles, alpha):
    b, t, d = h.shape
    d_ff = w_down.shape[0]
    tile = 2 * ROW_TILE if ctx_tiles == 0 and t % (2 * ROW_TILE) == 0 else ROW_TILE
    n_tiles = t // tile
    sub = tile // 8
    n_sub = t // 8
    row = pl.BlockSpec((1, tile, d), lambda bi, i: (bi, i, 0))
    prev = pl.BlockSpec((1, 8, d), lambda bi, i: (bi, jnp.maximum(i * sub - 1, 0), 0))
    nxt = pl.BlockSpec((1, 8, d), lambda bi, i: (bi, jnp.minimum((i + 1) * sub, n_sub - 1), 0))
    lat = pl.BlockSpec((1, 1, d), lambda bi, i: (bi, 0, 0))
    ctx = pl.BlockSpec((1, 1, d), lambda bi, i: (0, 0, 0))
    vec = pl.BlockSpec((1, d), lambda bi, i: (0, 0))
    full = lambda a: pl.BlockSpec(a.shape, lambda bi, i: (0,) * a.ndim)
    cb = conv_b.reshape(1, d_ff)
    return pl.pallas_call(
        functools.partial(_ffn_body, ctx_tiles=ctx_tiles, n_tiles=n_tiles, alpha=alpha, d_ff=d_ff),
        grid=(b, n_tiles),
        in_specs=[row, prev, nxt, lat, lat, lat, ctx, ctx, ctx,
                  full(w_up), full(conv_w), full(cb), full(w_down), vec, vec],
        out_specs=row,
        out_shape=jax.ShapeDtypeStruct((b, t, d), F32),
        compiler_params=_cparams(("parallel", "parallel")),
        name="ffn",
    )(h, h, h, *mods_l, *mods_c, w_up, conv_w, cb, w_down, ln_g.reshape(1, d), ln_b.reshape(1, d))


def _rwkv7(f, tc, lnx_g, lnx_b):
    b = f.shape[0]
    col = RWKV_FEATS.index
    y = _scan("rwkv",
              [_to_scan(f, col("r"), col("r"), "k", tc), _to_scan(f, col("w0"), col("w1"), "k", tc),
               _to_scan(f, col("k0"), col("k1"), "k", tc), _to_scan(f, col("a"), col("a"), "k", tc),
               _to_scan(f, col("b0"), col("b1"), "k", tc)],
              _to_scan(f, col("v"), col("v"), "v", tc))
    return _from_scan(y, b, tc, "rwkv", f, (col("bonus"), col("g")), (lnx_g, lnx_b))


def _rope_tables(n_ctx, n_lat):
    rows = n_lat // GRID_W
    row = jnp.repeat(jnp.arange(rows, dtype=F32), GRID_W)
    col = jnp.tile(jnp.arange(GRID_W, dtype=F32), rows)
    n_freq = DIFF_QK_DIM // 4
    inv_freq = ROPE_BASE ** (-jnp.arange(n_freq, dtype=F32) / n_freq)
    ang = jnp.concatenate([row[:, None] * inv_freq, col[:, None] * inv_freq], axis=-1)
    ang = jnp.concatenate([jnp.zeros((n_ctx, ang.shape[1]), F32), ang], axis=0)
    half = DIFF_QK_DIM // 2
    lane = np.arange(GROUP_WIDTH)
    sign = np.where(lane % DIFF_QK_DIM < half, -1.0, 1.0).astype(np.float32)
    return jnp.cos(ang)[:, lane % half], jnp.sin(ang)[:, lane % half] * sign


def _diff_attn(q, k, v, tc, lam_vecs, norm_g, layer, need_ctx):
    lam_init = 0.8 - 0.6 * math.exp(-0.3 * layer)
    lv = lam_vecs.astype(F32)
    lam = jnp.exp(jnp.sum(lv[0] * lv[1])) - jnp.exp(jnp.sum(lv[2] * lv[3])) + lam_init
    q_tile0 = 0 if need_ctx else tc // ATTN_TILE
    return _attention(q, k, v, 0, lam, norm_g, 1.0 - lam_init, q_tile0, tc)


def _hgrn2(f, tc, norm_g):
    b = f.shape[0]
    col = HGRN_FEATS.index
    o = _scan("gla",
              [_to_scan(f, col("q"), col("q"), "k", tc), _to_scan(f, col("w0"), col("w1"), "k", tc),
               _to_scan(f, col("k0"), col("k1"), "k", tc)],
              _to_scan(f, col("v"), col("v"), "v", tc))
    return _from_scan(o, b, tc, "gla", f, (col("gate"),), (jnp.tile(norm_g, N_HEADS),))


RWKV_COLS = 3 * GROUP_WIDTH + 2 * RWKV_DECAY_LORA + 2 * RWKV_ICLR_LORA + RWKV_GATE_LORA
DIFF_COLS = 3 * GROUP_WIDTH
HGRN_COLS = 5 * GROUP_WIDTH
MLSTM_COLS = 4 * GROUP_WIDTH + 4 * N_HEADS
SLAB_COLS = (RWKV_COLS, DIFF_COLS, HGRN_COLS, MLSTM_COLS)
SLAB_PAD = (5 * GROUP_WIDTH, DIFF_COLS, HGRN_COLS, 4 * GROUP_WIDTH + LANES)


def _split_cols(w):
    parts, off = [], 0
    for n, n_pad in zip(SLAB_COLS, SLAB_PAD):
        parts.append(jnp.pad(w[:, off:off + n], ((0, 0), (0, n_pad - n))).astype(BF16))
        off += n
    return parts


def kernel(x, c, ctx, c_ctx, ada_w, ada_b, w_in, rwkv_mu, rwkv_w0, rwkv_w2, rwkv_a0, rwkv_a2, rwkv_g2, rwkv_k_k, rwkv_k_a, rwkv_r_k, rwkv_lnx_g, rwkv_lnx_b, diff_lambda, diff_norm_g, hgrn_lb_logits, hgrn_norm_g, mlstm_conv_w, mlstm_conv_b, mlstm_gate_b, mlstm_norm_g, w_out, ffn_w_up, ffn_conv_w, ffn_conv_b, ffn_w_down, ln_g, ln_b):
    depth = w_in.shape[0]
    b, seq, d = x.shape
    tc = ctx.shape[1]
    assert tc == ROW_TILE == ATTN_TILE and seq % ROW_TILE == 0 and 2 * b * N_HEADS <= LANES
    alpha = (2.0 * depth) ** 0.25

    lb_w = jax.nn.softmax(hgrn_lb_logits.astype(F32), axis=0)
    lower_bounds = jnp.cumsum(lb_w, axis=0) - lb_w[0]

    rope = _rope_tables(tc, seq)
    cond = jnp.zeros((8, d), F32).at[:b].set(c).at[b].set(c_ctx)
    h = jnp.concatenate([ctx, x], axis=1)
    for layer in range(depth):
        last = layer == depth - 1
        mod = _adaln(cond, ada_w, ada_b, layer).reshape(8, 6, d)
        mods_l = [mod[:b, j][:, None, :] for j in range(6)]
        mods_c = [mod[b:b + 1, j][:, None, :] for j in range(6)]

        rwkv_prm = (rwkv_mu[layer], rwkv_w0[layer], rwkv_w2[layer], rwkv_a0[layer], rwkv_a2[layer], rwkv_g2[layer],
                    rwkv_k_k[layer], rwkv_k_a[layer], rwkv_r_k[layer])
        f_rwkv, q, k, v, f_hgrn, p_mlstm = _inproj(h, mods_l[0], mods_l[1], mods_c[0], mods_c[1], rope,
                                                   _split_cols(w_in[layer]), rwkv_prm, lower_bounds[layer],
                                                   tc // ROW_TILE)
        y_a = _rwkv7(f_rwkv, tc, rwkv_lnx_g[layer], rwkv_lnx_b[layer])
        y_b = _diff_attn(q, k, v, tc, diff_lambda[layer], diff_norm_g[layer], layer, not last)
        y_c = _hgrn2(f_hgrn, tc, hgrn_norm_g[layer])
        h_f, h_b = (_mlstm_chunked(p_mlstm, mlstm_conv_w[layer], mlstm_conv_b[layer], mlstm_gate_b[layer], rev, tc)
                    for rev in (False, True))

        tile0, ctx_tiles = (tc // ROW_TILE, 0) if last else (0, tc // ROW_TILE)
        h = _outproj(y_a, y_b, y_c, h_f, h_b, p_mlstm, mlstm_norm_g[layer], h, mods_l[2], mods_c[2],
                     w_out[layer].astype(BF16), ln_g[layer, 0], ln_b[layer, 0], tile0, ctx_tiles, alpha)
        h = _ffn(h, mods_l[3:6], mods_c[3:6], ffn_w_up[layer].astype(BF16), ffn_conv_w[layer], ffn_conv_b[layer],
                 ffn_w_down[layer].astype(BF16), ln_g[layer, 1], ln_b[layer, 1], ctx_tiles, alpha)
    return h
```

```python
import functools
import math

import jax
import jax.numpy as jnp
import numpy as np
from jax import lax
from jax.experimental import pallas as pl
from jax.experimental.pallas import tpu as pltpu

F32 = jnp.float32
BF16 = jnp.bfloat16

HEAD_DIM = 64
N_HEADS = 4
GROUP_WIDTH = N_HEADS * HEAD_DIM
DIFF_QK_DIM = HEAD_DIM // 2
GRID_W = 64
ROPE_BASE = 10000.0
RWKV_DECAY_LORA = 64
RWKV_ICLR_LORA = 64
RWKV_GATE_LORA = 160
RWKV_LNX_EPS = 64e-5
LN_EPS = 1e-5
LB_FLOOR = 1e-30
MAX_NEG_LOG_STAB = 60.0

LANES = 128
SUBLANES = 8
ADALN_TILE = 1536
ROW_TILE = 256
ATTN_TILE = 256
TIME_BLOCK = 128
SCAN_TB = 64
VMEM_LIMIT = 56 * 1024 * 1024


def _cparams(sem):
    return pltpu.CompilerParams(dimension_semantics=sem, vmem_limit_bytes=VMEM_LIMIT)


def _rev_block(i, n_ctx, n_all):
    return jnp.where(i < n_ctx, n_ctx - 1 - i, n_ctx + n_all - 1 - i)


def _split3(x):
    hi = x.astype(BF16)
    r1 = x - hi.astype(F32)
    mid = r1.astype(BF16)
    return hi, mid, (r1 - mid.astype(F32)).astype(BF16)


def _dot01_left(a01, x):
    a = a01.astype(BF16)
    p = [jnp.dot(a, part, preferred_element_type=F32) for part in _split3(x)]
    return (p[0] + p[1]) + p[2]


def _dot01_right(x, b01):
    b = b01.astype(BF16)
    p = [jnp.dot(part, b, preferred_element_type=F32) for part in _split3(x)]
    return (p[0] + p[1]) + p[2]


def _flip_rows(x):
    n = x.shape[0]
    r = lax.broadcasted_iota(jnp.int32, (n, n), 0)
    c = lax.broadcasted_iota(jnp.int32, (n, n), 1)
    return _dot01_left(jnp.where(r + c == n - 1, 1.0, 0.0).astype(F32), x)


def _adaln_body(c_ref, w_ref, b_ref, o_ref):
    x = c_ref[...]
    x = (x * jax.nn.sigmoid(x)).astype(BF16)
    o_ref[...] = jnp.dot(x, w_ref[0].astype(BF16), preferred_element_type=F32) + b_ref[0]


def _adaln(cond, w, b, layer):
    m, d = cond.shape
    n = w.shape[2]
    tn = ADALN_TILE
    return pl.pallas_call(
        _adaln_body,
        grid=(n // tn,),
        in_specs=[pl.BlockSpec((m, d), lambda j: (0, 0)),
                  pl.BlockSpec((1, d, tn), lambda j: (layer, 0, j)),
                  pl.BlockSpec((1, 1, tn), lambda j: (layer, 0, j))],
        out_specs=pl.BlockSpec((m, tn), lambda j: (0, j)),
        out_shape=jax.ShapeDtypeStruct((m, n), F32),
        compiler_params=_cparams(("arbitrary",)),
        name="adaln",
    )(cond, w, b.reshape(b.shape[0], 1, n))


def _softplus(x):
    return jnp.maximum(x, 0.0) + jnp.log(1.0 + jnp.exp(-jnp.abs(x)))


def _head_block(scale=1.0):
    r = lax.broadcasted_iota(jnp.int32, (GROUP_WIDTH, GROUP_WIDTH), 0) // HEAD_DIM
    c = lax.broadcasted_iota(jnp.int32, (GROUP_WIDTH, GROUP_WIDTH), 1) // HEAD_DIM
    return jnp.where(r == c, scale, 0.0).astype(F32)


def _head_sum(x, block_bf16):
    hi = x.astype(BF16)
    lo = (x - hi.astype(F32)).astype(BF16)
    return (jnp.dot(hi, block_bf16, preferred_element_type=F32)
            + jnp.dot(lo, block_bf16, preferred_element_type=F32))


def _shifted_rows(x, prev8, next8, seg_start, seg_end):
    rows = x.shape[0]
    row_id = lax.broadcasted_iota(jnp.int32, (rows, 1), 0)
    prev_row = jnp.where(seg_start, 0.0, prev8[SUBLANES - 1:SUBLANES, :])
    next_row = jnp.where(seg_end, 0.0, next8[0:1, :])
    prev = jnp.where(row_id == 0, prev_row, pltpu.roll(x, 1, 0))
    nxt = jnp.where(row_id == rows - 1, next_row, pltpu.roll(x, rows - 1, 0))
    return prev, nxt


RWKV_FEATS = ("r", "v", "w0", "w1", "k0", "k1", "a", "b0", "b1", "g", "bonus")


def _rwkv_feats(p, prev8, next8, seg_start, seg_end, mu_ref, w0_ref, w2_ref, a0_ref, a2_ref, g2_ref, kk_ref,
                ka_ref, rk_ref):
    gw = GROUP_WIDTH
    prev, nxt = _shifted_rows(p, prev8, next8, seg_start, seg_end)
    p = p + (0.5 * (prev + nxt) - p) * mu_ref[...]
    r, k, v = p[:, :gw], p[:, gw:2 * gw], p[:, 2 * gw:3 * gw]
    wd, ad, gd = p[:, 3 * gw:3 * gw + LANES], p[:, 3 * gw + LANES:4 * gw], p[:, 4 * gw:5 * gw]
    lora_w = jnp.dot(jnp.tanh(wd).astype(BF16), w2_ref[...], preferred_element_type=F32)
    lora_a = jnp.dot(ad.astype(BF16), a2_ref[...], preferred_element_type=F32)
    decay = jnp.exp(-jnp.exp(-_softplus(-(w0_ref[...] + lora_w)) - 0.5))
    a = jax.nn.sigmoid(a0_ref[...] + lora_a)
    g = jnp.dot(jax.nn.sigmoid(gd).astype(BF16), g2_ref[...], preferred_element_type=F32)
    block = _head_block().astype(BF16)
    kk = k * kk_ref[...]
    kk = kk / jnp.maximum(jnp.sqrt(_head_sum(kk * kk, block)), 1e-12)
    k0 = k * (1.0 + (a[:, :gw] - 1.0) * ka_ref[...])
    k1 = k * (1.0 + (a[:, gw:] - 1.0) * ka_ref[...])
    bonus = _head_sum(r * (k0 + k1) * rk_ref[...], block) * v
    return dict(r=r, v=v, w0=decay[:, :gw], w1=decay[:, gw:], k0=k0, k1=k1, a=-kk, b0=kk * a[:, :gw],
                b1=kk * a[:, gw:], g=g, bonus=bonus)


def _block_diag2(w):
    z = jnp.zeros_like(w[0])
    return jnp.concatenate([jnp.concatenate([w[0], z], axis=1), jnp.concatenate([z, w[1]], axis=1)], axis=0)


HGRN_FEATS = ("q", "w0", "w1", "k0", "k1", "gate", "v")


def _hgrn_feats(p, llb_ref, l1m_ref, oml_ref):
    gw = GROUP_WIDTH
    q, f_f, f_b, v, g = (p[:, j * gw:(j + 1) * gw] for j in range(5))
    feats = dict(q=q * jax.nn.sigmoid(q), gate=g * jax.nn.sigmoid(g), v=v)
    for d, f in enumerate((f_f, f_b)):
        x = llb_ref[...]
        y = l1m_ref[...] - _softplus(-f)
        log_f = jnp.maximum(x, y) + jnp.log(1.0 + jnp.exp(-jnp.abs(x - y)))
        feats["w%d" % d] = jnp.exp(log_f)
        feats["k%d" % d] = oml_ref[...] * jax.nn.sigmoid(-f)
    return feats


N_RWKV_PRM, N_HGRN_PRM = 9, 3


def _rope(x, cos, sin_signed):
    half = DIFF_QK_DIM // 2
    lane = lax.broadcasted_iota(jnp.int32, x.shape, 1)
    n = x.shape[1]
    partner = jnp.where(lane % DIFF_QK_DIM < half, pltpu.roll(x, n - half, 1), pltpu.roll(x, half, 1))
    return x * cos + partner * sin_signed


def _inproj_body(x_ref, xp_ref, xn_ref, sl_ref, cl_ref, sc_ref, cc_ref, cos_ref, sin_ref, wr_ref, wd_ref, wh_ref,
                 wm_ref, *refs, ctx_tiles, n_tiles):
    rwkv_prm, hgrn_prm = refs[:N_RWKV_PRM], refs[N_RWKV_PRM:N_RWKV_PRM + N_HGRN_PRM]
    fr_ref, q_ref, k_ref, v_ref, fh_ref, pm_ref = refs[N_RWKV_PRM + N_HGRN_PRM:]
    i = pl.program_id(1)
    is_ctx = i < ctx_tiles
    seg_start = jnp.logical_or(i == 0, i == ctx_tiles)
    seg_end = jnp.logical_or(i == ctx_tiles - 1, i == n_tiles - 1)
    shift = jnp.where(is_ctx, sc_ref[0], sl_ref[0])
    scale = jnp.where(is_ctx, cc_ref[0], cl_ref[0])
    mod = lambda x: (x * (1.0 + scale) + shift).astype(BF16)
    proj = lambda x, w_ref: jnp.dot(x, w_ref[...], preferred_element_type=F32)
    gw = GROUP_WIDTH
    xm = mod(x_ref[0])
    feats = _rwkv_feats(proj(xm, wr_ref), proj(mod(xp_ref[0]), wr_ref), proj(mod(xn_ref[0]), wr_ref),
                        seg_start, seg_end, *rwkv_prm)
    for j, name in enumerate(RWKV_FEATS):
        fr_ref[0, :, j * gw:(j + 1) * gw] = feats[name]
    pd = proj(xm, wd_ref)
    q_ref[0] = (_rope(pd[:, :gw], cos_ref[...], sin_ref[...])
                * (DIFF_QK_DIM ** -0.5 * math.log2(math.e))).astype(BF16)
    k_ref[0] = _rope(pd[:, gw:2 * gw], cos_ref[...], sin_ref[...]).astype(BF16)
    v_ref[0] = pd[:, 2 * gw:]
    feats = _hgrn_feats(proj(xm, wh_ref), *hgrn_prm)
    for j, name in enumerate(HGRN_FEATS):
        fh_ref[0, :, j * gw:(j + 1) * gw] = feats[name]
    pm_ref[0] = proj(xm, wm_ref)


def _inproj(h, shift_l, scale_l, shift_c, scale_c, rope, ws, rwkv_prm, lb, ctx_tiles):
    b, t, d = h.shape
    gw = GROUP_WIDTH
    n_tiles, sub, n_sub = t // ROW_TILE, ROW_TILE // SUBLANES, t // SUBLANES
    row = lambda n: pl.BlockSpec((1, ROW_TILE, n), lambda bi, i: (bi, i, 0))
    prev = pl.BlockSpec((1, SUBLANES, d), lambda bi, i: (bi, jnp.maximum(i * sub - 1, 0), 0))
    nxt = pl.BlockSpec((1, SUBLANES, d), lambda bi, i: (bi, jnp.minimum((i + 1) * sub, n_sub - 1), 0))
    lat = pl.BlockSpec((1, 1, d), lambda bi, i: (bi, 0, 0))
    ctx = pl.BlockSpec((1, 1, d), lambda bi, i: (0, 0, 0))
    full = lambda a: pl.BlockSpec(a.shape, lambda bi, i: (0,) * a.ndim)
    vec = lambda a: a.reshape(1, -1).astype(F32)
    mu, w0, w2, a0, a2, g2, k_k, k_a, r_k = rwkv_prm
    cols = ws[0].shape[1]
    prm = [jnp.pad(vec(mu), ((0, 0), (0, cols - mu.shape[0]))), vec(w0), _block_diag2(w2).astype(BF16), vec(a0),
           _block_diag2(a2).astype(BF16), jnp.pad(g2, ((0, gw - g2.shape[0]), (0, 0))).astype(BF16), vec(k_k),
           vec(k_a), vec(r_k),
           vec(jnp.log(jnp.maximum(lb, LB_FLOOR))), vec(jnp.log1p(-lb)), vec(1.0 - lb)]
    assert len(prm) == N_RWKV_PRM + N_HGRN_PRM
    outs = [(len(RWKV_FEATS) * gw, F32), (gw, BF16), (gw, BF16), (gw, F32), (len(HGRN_FEATS) * gw, F32),
            (ws[3].shape[1], F32)]
    tab = pl.BlockSpec((ROW_TILE, gw), lambda bi, i: (i, 0))
    return pl.pallas_call(
        functools.partial(_inproj_body, ctx_tiles=ctx_tiles, n_tiles=n_tiles),
        grid=(b, n_tiles),
        in_specs=[row(d), prev, nxt, lat, lat, ctx, ctx, tab, tab] + [full(a) for a in list(ws) + prm],
        out_specs=[row(n) for n, _ in outs],
        out_shape=[jax.ShapeDtypeStruct((b, t, n), dt) for n, dt in outs],
        compiler_params=_cparams(("parallel", "parallel")),
        name="inproj",
    )(h, h, h, shift_l, scale_l, shift_c, scale_c, *rope, *ws, *prm)


def _to_scan_body(x0_ref, x1_ref, o_ref, r_ref, *, kind, nb):
    n_scan = 2 * nb * N_HEADS
    rep = LANES // n_scan
    for b in range(nb):
        r_ref[pl.ds(b * GROUP_WIDTH, GROUP_WIDTH), :] = x0_ref[b].T
        r_ref[pl.ds((nb + b) * GROUP_WIDTH, GROUP_WIDTH), :] = _flip_rows(x1_ref[b]).T
    if kind == "k":
        for k in range(HEAD_DIM):
            rows = r_ref[pl.ds(k, n_scan, stride=HEAD_DIM), :]
            o_ref[k] = jnp.concatenate([rows] * rep, axis=0).T
    else:
        for vh in range(HEAD_DIM // rep):
            rows = [r_ref[pl.ds(vh * rep + vl, n_scan, stride=HEAD_DIM), :] for vl in range(rep)]
            o_ref[:, vh, :] = jnp.concatenate(rows, axis=0).T


def _to_scan(x, col0, col1, kind, tc):
    nb, t, _ = x.shape
    gw = GROUP_WIDTH
    x0 = x1 = x
    n_all, n_ctx = t // TIME_BLOCK, tc // TIME_BLOCK
    n_scan = 2 * nb * N_HEADS
    rep = LANES // n_scan
    fwd = pl.BlockSpec((nb, TIME_BLOCK, gw), lambda i: (0, i, col0))
    bwd = pl.BlockSpec((nb, TIME_BLOCK, gw), lambda i: (0, _rev_block(i, n_ctx, n_all), col1))
    if kind == "k":
        out_shape = (HEAD_DIM, t, LANES)
        out_spec = pl.BlockSpec((HEAD_DIM, TIME_BLOCK, LANES), lambda i: (0, i, 0))
    else:
        out_shape = (t, HEAD_DIM // rep, LANES)
        out_spec = pl.BlockSpec((TIME_BLOCK, HEAD_DIM // rep, LANES), lambda i: (i, 0, 0))
    return pl.pallas_call(
        functools.partial(_to_scan_body, kind=kind, nb=nb),
        grid=(n_all,),
        in_specs=[fwd, bwd],
        out_specs=out_spec,
        out_shape=jax.ShapeDtypeStruct(out_shape, F32),
        scratch_shapes=[pltpu.VMEM((n_scan * HEAD_DIM, TIME_BLOCK), F32)],
        compiler_params=_cparams(("parallel",)),
        name="to_scan_" + kind,
    )(x0, x1)


def _from_scan_body(yf_ref, yb_ref, *rest, nb, readout):
    if readout == "rwkv":
        bonus_ref, g_ref, ng_ref, nb_ref, o_ref, r_ref = rest
    else:
        gate_ref, ng_ref, o_ref, r_ref = rest
    n_scan = 2 * nb * N_HEADS
    rep = LANES // n_scan
    for vh in range(HEAD_DIM // rep):
        r_ref[0, pl.ds(vh * LANES, LANES), :] = yf_ref[vh].T
        r_ref[1, pl.ds(vh * LANES, LANES), :] = yb_ref[vh].T
    mean_block = _head_block(1.0 / HEAD_DIM).astype(BF16)
    for b in range(nb):
        slab = lambda d: jnp.concatenate(
            [r_ref[d, pl.ds((d * nb + b) * N_HEADS + h, HEAD_DIM, stride=n_scan), :] for h in range(N_HEADS)],
            axis=0).T
        y = slab(0) + _flip_rows(slab(1))
        if readout == "rwkv":
            y = y - _head_sum(y, mean_block)
            y = y * lax.rsqrt(_head_sum(y * y, mean_block) + RWKV_LNX_EPS) * ng_ref[...] + nb_ref[...]
            o_ref[b] = (y + bonus_ref[b]) * g_ref[b]
        else:
            y = y * lax.rsqrt(_head_sum(y * y, mean_block) + LN_EPS) * ng_ref[...]
            o_ref[b] = y * gate_ref[b]


def _from_scan(y, nb, tc, readout, feats, cols, vecs):
    n_vh, t, _ = y.shape
    gw = GROUP_WIDTH
    n_all, n_ctx = t // TIME_BLOCK, tc // TIME_BLOCK
    fwd = pl.BlockSpec((n_vh, TIME_BLOCK, LANES), lambda i: (0, i, 0))
    bwd = pl.BlockSpec((n_vh, TIME_BLOCK, LANES), lambda i: (0, _rev_block(i, n_ctx, n_all), 0))
    tok = lambda c: pl.BlockSpec((nb, TIME_BLOCK, gw), lambda i: (0, i, c))
    vec = pl.BlockSpec((1, gw), lambda i: (0, 0))
    return pl.pallas_call(
        functools.partial(_from_scan_body, nb=nb, readout=readout),
        grid=(n_all,),
        in_specs=[fwd, bwd] + [tok(c) for c in cols] + [vec] * len(vecs),
        out_specs=tok(0),
        out_shape=jax.ShapeDtypeStruct((nb, t, gw), F32),
        scratch_shapes=[pltpu.VMEM((2, n_vh * LANES, TIME_BLOCK), F32)],
        compiler_params=_cparams(("parallel",)),
        name="from_scan_" + readout,
    )(y, y, *([feats] * len(cols)), *[v.reshape(1, gw) for v in vecs])


N_ACC = 4


def _acc_add(acc, i, x):
    acc[i % N_ACC] = x if acc[i % N_ACC] is None else acc[i % N_ACC] + x


def _acc_total(acc):
    return (acc[0] + acc[1]) + (acc[2] + acc[3])


def _scan_body(*refs, mode, tb, n_vh):
    if mode == "rwkv":
        r_ref, w_ref, k_ref, a_ref, b_ref, v_ref, y_ref, s_ref = refs
    else:
        r_ref, w_ref, k_ref, v_ref, y_ref, s_ref = refs

    @pl.when(pl.program_id(0) == 0)
    def _():
        s_ref[...] = jnp.zeros_like(s_ref)

    row = lambda ref, k, j: ref[k, pl.ds(j, 1), :]
    wide = lambda x: jnp.broadcast_to(x, (n_vh, LANES))

    def store_y(j, y):
        for vh in range(n_vh):
            y_ref[vh, pl.ds(j, 1), :] = y[vh:vh + 1, :]

    def rwkv_step(j, sa):
        v = v_ref[j]
        j_next = jnp.minimum(j + 1, tb - 1)
        ys, sas = [None] * N_ACC, [None] * N_ACC
        for k in range(HEAD_DIM):
            s = s_ref[k] * wide(row(w_ref, k, j)) + sa * wide(row(b_ref, k, j)) + v * wide(row(k_ref, k, j))
            s_ref[k] = s
            _acc_add(ys, k, s * wide(row(r_ref, k, j)))
            _acc_add(sas, k, s * wide(row(a_ref, k, j_next)))
        store_y(j, _acc_total(ys))
        return _acc_total(sas)

    def gla_step(j, carry):
        v = v_ref[j]
        ys = [None] * N_ACC
        for k in range(HEAD_DIM):
            s = s_ref[k] * wide(row(w_ref, k, j)) + v * wide(row(k_ref, k, j))
            s_ref[k] = s
            _acc_add(ys, k, s * wide(row(r_ref, k, j)))
        store_y(j, _acc_total(ys))
        return carry

    if mode == "rwkv":
        sa0 = [None] * N_ACC
        for k in range(HEAD_DIM):
            _acc_add(sa0, k, s_ref[k] * wide(row(a_ref, k, 0)))
        lax.fori_loop(0, tb, rwkv_step, _acc_total(sa0), unroll=8)
    else:
        lax.fori_loop(0, tb, gla_step, 0, unroll=8)


def _scan(mode, k_inputs, v_in):
    t, n_vh, _ = v_in.shape
    tb = SCAN_TB
    big = pl.BlockSpec((HEAD_DIM, tb, LANES), lambda i: (0, i, 0))
    small = pl.BlockSpec((tb, n_vh, LANES), lambda i: (i, 0, 0))
    return pl.pallas_call(
        functools.partial(_scan_body, mode=mode, tb=tb, n_vh=n_vh),
        grid=(t // tb,),
        in_specs=[big] * len(k_inputs) + [small],
        out_specs=pl.BlockSpec((n_vh, tb, LANES), lambda i: (0, i, 0)),
        out_shape=jax.ShapeDtypeStruct((n_vh, t, LANES), F32),
        scratch_shapes=[pltpu.VMEM((HEAD_DIM, n_vh, LANES), F32)],
        compiler_params=_cparams(("arbitrary",)),
        name="scan_" + mode,
    )(*k_inputs, v_in)


MLSTM_CHUNK = 64


def _prefix_max(x, reverse):
    n = x.shape[0]
    row = lax.broadcasted_iota(jnp.int32, x.shape, 0)
    sh = 1
    while sh < n:
        if reverse:
            x = jnp.where(row < n - sh, jnp.maximum(x, pltpu.roll(x, n - sh, 0)), x)
        else:
            x = jnp.where(row >= sh, jnp.maximum(x, pltpu.roll(x, sh, 0)), x)
        sh *= 2
    return x


def _mlstm_body(qk_ref, qkp_ref, qkn_ref, v_ref, gt_ref, cw_ref, cb_ref, sel_ref, gb_ref, o_ref, st_ref, m_ref, *,
                reverse, n_sub, n_ctx, n_blocks):
    c, gw = MLSTM_CHUNK, GROUP_WIDTH
    i = pl.program_id(1)

    @pl.when(i == 0)
    def _():
        st_ref[...] = jnp.zeros_like(st_ref)
        m_ref[...] = jnp.zeros_like(m_ref)

    blk = _rev_block(i, n_ctx, n_blocks) if reverse else i
    seg_start = jnp.logical_or(blk == 0, blk == n_ctx)
    seg_end = jnp.logical_or(blk == n_ctx - 1, blk == n_blocks - 1)
    qk = qk_ref[0]
    prev, nxt = _shifted_rows(qk, qkp_ref[0], qkn_ref[0], seg_start, seg_end)
    qk = prev * cw_ref[0:1, :] + qk * cw_ref[1:2, :] + nxt * cw_ref[2:3, :] + cb_ref[...]
    qk = qk * jax.nn.sigmoid(qk)
    q_all, k_all = qk[:, :gw], qk[:, gw:] * HEAD_DIM ** -0.5
    v_all = v_ref[0]
    gt = gt_ref[0]
    ig_all = _dot01_right(gt, sel_ref[0]) + gb_ref[0:1, :]
    fg_all = -_softplus(-(_dot01_right(gt, sel_ref[1]) + gb_ref[1:2, :]))

    tt = lax.broadcasted_iota(jnp.int32, (c, c), 0)
    ss = lax.broadcasted_iota(jnp.int32, (c, c), 1)
    tri = jnp.where((ss >= tt) if reverse else (ss <= tt), 1.0, 0.0).astype(F32)
    row = lax.broadcasted_iota(jnp.int32, (c, gw), 0)
    s_of_lane = lax.broadcasted_iota(jnp.int32, (c, gw), 1) % c
    causal = (s_of_lane >= row) if reverse else (s_of_lane <= row)
    diag = s_of_lane == row
    block = _head_block()
    block2 = jnp.concatenate([block, block], axis=1)
    ones8 = jnp.ones((SUBLANES, c), F32)
    nt = (((1,), (1,)), ((), ()))
    last = 0 if reverse else c - 1
    for u in (range(n_sub - 1, -1, -1) if reverse else range(n_sub)):
        sl = slice(u * c, (u + 1) * c)
        q, k, v, ig, fg = q_all[sl], k_all[sl], v_all[sl], ig_all[sl], fg_all[sl]
        b = _dot01_left(tri, fg)
        g = ig - b
        m_prev = m_ref[0:1, :]
        m_t = b + jnp.maximum(m_prev, _prefix_max(g, reverse))
        w_inter = jnp.exp(b + m_prev - m_t)
        g_row = _dot01_left(ones8, jnp.where(diag, g, 0.0))[0:1]
        qb = q.astype(BF16)
        k_bd = (jnp.concatenate([k] * N_HEADS, axis=0) * block).astype(BF16)
        scores = lax.dot_general(qb, k_bd, nt, preferred_element_type=F32)
        w = jnp.where(causal, jnp.exp((b - m_t) + g_row), 0.0) * scores
        v_bd = jnp.concatenate([jnp.concatenate([v] * N_HEADS, axis=0) * block, block], axis=1).astype(BF16)
        intra = jnp.dot(w.astype(BF16), v_bd, preferred_element_type=F32)
        inter = jnp.dot(qb, st_ref[...].astype(BF16), preferred_element_type=F32)
        num = w_inter * inter[:, :gw] + intra[:, :gw]
        den = w_inter * inter[:, gw:] + intra[:, gw:]
        floor = jnp.exp(jnp.minimum(-m_t, MAX_NEG_LOG_STAB))
        o_ref[0, pl.ds(u * c, c), :] = num / jnp.maximum(jnp.abs(den), floor)
        m_new = m_t[last:last + 1]
        b_end = b[last:last + 1]
        kw = k * jnp.exp(b_end - b + ig - m_new)
        decay = jnp.exp(b_end + m_prev - m_new)
        v_one = jnp.concatenate([v, jnp.ones_like(v)], axis=1).astype(BF16)
        upd = jnp.dot(kw.T.astype(BF16), v_one, preferred_element_type=F32)
        st_ref[...] = st_ref[...] * jnp.concatenate([decay, decay], axis=1) + upd * block2
        m_ref[0:1, :] = m_new


def _mlstm_chunked(p, conv_w, conv_b, gate_b, reverse, tc):
    b, t, _ = p.shape
    gw = GROUP_WIDTH
    n_blocks, n_ctx = t // ROW_TILE, tc // ROW_TILE
    sub, n_sub8 = ROW_TILE // SUBLANES, t // SUBLANES
    blk = (lambda i: _rev_block(i, n_ctx, n_blocks)) if reverse else (lambda i: i)
    d = 1 if reverse else 0
    lane_head = np.arange(gw) // HEAD_DIM
    sel = np.zeros((2, LANES, gw), np.float32)
    for j, base in enumerate((d * N_HEADS, (2 + d) * N_HEADS)):
        sel[j, base + lane_head, np.arange(gw)] = 1.0
    gate_bias = jnp.stack([jnp.repeat(gate_b[d], HEAD_DIM), jnp.repeat(gate_b[2 + d], HEAD_DIM)], axis=0)
    full = lambda a: pl.BlockSpec(a.shape, lambda bi, i: (0,) * a.ndim)
    cb = conv_b.reshape(1, 2 * gw)
    return pl.pallas_call(
        functools.partial(_mlstm_body, reverse=reverse, n_sub=ROW_TILE // MLSTM_CHUNK, n_ctx=n_ctx,
                          n_blocks=n_blocks),
        grid=(b, n_blocks),
        in_specs=[pl.BlockSpec((1, ROW_TILE, 2 * gw), lambda bi, i: (bi, blk(i), 0)),
                  pl.BlockSpec((1, SUBLANES, 2 * gw), lambda bi, i: (bi, jnp.maximum(blk(i) * sub - 1, 0), 0)),
                  pl.BlockSpec((1, SUBLANES, 2 * gw), lambda bi, i: (bi, jnp.minimum((blk(i) + 1) * sub, n_sub8 - 1), 0)),
                  pl.BlockSpec((1, ROW_TILE, gw), lambda bi, i: (bi, blk(i), 2)),
                  pl.BlockSpec((1, ROW_TILE, LANES), lambda bi, i: (bi, blk(i), 4 * gw // LANES)),
                  full(conv_w), full(cb), full(sel), full(gate_bias)],
        out_specs=pl.BlockSpec((1, ROW_TILE, gw), lambda bi, i: (bi, blk(i), 0)),
        out_shape=jax.ShapeDtypeStruct((b, t, gw), F32),
        scratch_shapes=[pltpu.VMEM((gw, 2 * gw), F32), pltpu.VMEM((SUBLANES, gw), F32)],
        compiler_params=_cparams(("parallel", "arbitrary")),
        name="mlstm_chunk",
    )(p, p, p, p, p, conv_w, cb, jnp.asarray(sel), gate_bias)


def _attn_body(q_ref, k_ref, v_ref, lam_ref, g_ref, o_ref, *, out_scale, ctx_tiles, n_ctx_keys):
    def attend(n_keys):
        q = q_ref[0]
        k = k_ref[0, :n_keys, :]
        v = v_ref[0, :n_keys, :].astype(BF16)
        lane = lax.broadcasted_iota(jnp.int32, q.shape, 1)
        comp = lane // DIFF_QK_DIM
        nt = (((1,), (1,)), ((), ()))
        n_maps = LANES // DIFF_QK_DIM
        ss = [lax.dot_general(jnp.where(comp == c, q, 0.0).astype(BF16), k, nt, preferred_element_type=F32)
              for c in range(n_maps)]
        es = [jnp.exp2(s - jnp.max(s, axis=-1, keepdims=True)) for s in ss]
        ls = [jnp.sum(e, axis=-1, keepdims=True) for e in es]
        ys = []
        for hh in range(LANES // HEAD_DIM):
            e0, e1, l0, l1 = es[2 * hh], es[2 * hh + 1], ls[2 * hh], ls[2 * hh + 1]
            pr = (e0 - e1 * (lam_ref[0:1, 0:1] * l0 / l1)).astype(BF16)
            ys.append(jnp.dot(pr, v, preferred_element_type=F32) * (1.0 / l0))
        first = lane < HEAD_DIM
        y = jnp.where(first, ys[0], ys[1])
        ysq = y * y
        ms = jnp.where(first, jnp.sum(jnp.where(first, ysq, 0.0), axis=-1, keepdims=True),
                       jnp.sum(jnp.where(first, 0.0, ysq), axis=-1, keepdims=True)) * (1.0 / HEAD_DIM)
        o_ref[0] = y * lax.rsqrt(ms + LN_EPS) * g_ref[...] * out_scale

    if ctx_tiles == 0:
        attend(k_ref.shape[1])
    else:
        is_ctx = pl.program_id(2) < ctx_tiles
        pl.when(is_ctx)(lambda: attend(n_ctx_keys))
        pl.when(jnp.logical_not(is_ctx))(lambda: attend(k_ref.shape[1]))


def _attention(q, k, v, v_col0, lam, norm_g, out_scale, q_tile0, tc):
    b, t, gw = q.shape
    n_tiles = t // ATTN_TILE - q_tile0
    ctx_tiles = max(tc // ATTN_TILE - q_tile0, 0)
    qspec = pl.BlockSpec((1, ATTN_TILE, LANES), lambda bi, pi, i: (bi, q_tile0 + i, pi))
    kspec = pl.BlockSpec((1, t, LANES), lambda bi, pi, i: (bi, 0, pi))
    vspec = pl.BlockSpec((1, t, LANES), lambda bi, pi, i: (bi, 0, v_col0 + pi))
    vec = pl.BlockSpec((1, LANES), lambda bi, pi, i: (0, 0))
    return pl.pallas_call(
        functools.partial(_attn_body, out_scale=out_scale, ctx_tiles=ctx_tiles, n_ctx_keys=tc),
        grid=(b, gw // LANES, n_tiles),
        in_specs=[qspec, kspec, vspec, vec, vec],
        out_specs=pl.BlockSpec((1, ATTN_TILE, LANES), lambda bi, pi, i: (bi, i, pi)),
        out_shape=jax.ShapeDtypeStruct((b, n_tiles * ATTN_TILE, gw), F32),
        compiler_params=_cparams(("parallel", "parallel", "parallel")),
        name="diff_attn",
    )(q, k, v, jnp.full((1, LANES), lam, F32), jnp.tile(norm_g.reshape(1, HEAD_DIM), (1, LANES // HEAD_DIM)))


def _layernorm(z, g, b):
    z = z - jnp.mean(z, axis=-1, keepdims=True)
    return z * lax.rsqrt(jnp.mean(z * z, axis=-1, keepdims=True) + LN_EPS) * g + b


def _outproj_body(ya_ref, yb_ref, yc_ref, hf_ref, hb_ref, og_ref, ng_ref, h_ref, gl_ref, gc_ref, w_ref, g_ref,
                  b_ref, o_ref, *, ctx_tiles, alpha):
    is_ctx = pl.program_id(1) < ctx_tiles
    gate = jnp.where(is_ctx, gc_ref[0], gl_ref[0])
    mean_block = _head_block(1.0 / HEAD_DIM).astype(BF16)
    yd = hf_ref[0] + hb_ref[0]
    yd = yd - _head_sum(yd, mean_block)
    yd = yd * lax.rsqrt(_head_sum(yd * yd, mean_block) + LN_EPS) * ng_ref[...] * jax.nn.sigmoid(og_ref[0])
    y = None
    for m, ym in enumerate((ya_ref[0], yb_ref[0], yc_ref[0], yd)):
        part = jnp.dot(ym.astype(BF16), w_ref[m * GROUP_WIDTH:(m + 1) * GROUP_WIDTH, :],
                       preferred_element_type=F32)
        y = part if y is None else y + part
    o_ref[0] = _layernorm(alpha * h_ref[0] + gate * y, g_ref[...], b_ref[...])


def _outproj(ya, yb, yc, hf, hb, p_mlstm, mlstm_norm_g, h, gate_l, gate_c, w, ln_g, ln_b, tile0, ctx_tiles, alpha):
    b, t, d = h.shape
    gw = GROUP_WIDTH
    n_tiles = t // ROW_TILE - tile0
    yrow = lambda c: pl.BlockSpec((1, ROW_TILE, gw), lambda bi, i: (bi, tile0 + i, c))
    yb_row = pl.BlockSpec((1, ROW_TILE, gw), lambda bi, i: (bi, i, 0))
    hrow = pl.BlockSpec((1, ROW_TILE, d), lambda bi, i: (bi, tile0 + i, 0))
    lat = pl.BlockSpec((1, 1, d), lambda bi, i: (bi, 0, 0))
    ctx = pl.BlockSpec((1, 1, d), lambda bi, i: (0, 0, 0))
    vec = lambda n: pl.BlockSpec((1, n), lambda bi, i: (0, 0))
    return pl.pallas_call(
        functools.partial(_outproj_body, ctx_tiles=ctx_tiles, alpha=alpha),
        grid=(b, n_tiles),
        in_specs=[yrow(0), yb_row, yrow(0), yrow(0), yrow(0), yrow(3), vec(gw), hrow, lat, ctx,
                  pl.BlockSpec(w.shape, lambda bi, i: (0, 0)), vec(d), vec(d)],
        out_specs=pl.BlockSpec((1, ROW_TILE, d), lambda bi, i: (bi, i, 0)),
        out_shape=jax.ShapeDtypeStruct((b, n_tiles * ROW_TILE, d), F32),
        compiler_params=_cparams(("parallel", "parallel")),
        name="outproj_ln",
    )(ya, yb, yc, hf, hb, p_mlstm, mlstm_norm_g.reshape(1, gw), h, gate_l, gate_c, w, ln_g.reshape(1, d),
      ln_b.reshape(1, d))


FFN_CHUNKS = 1


def _ffn_body(x_ref, xp_ref, xn_ref, sl_ref, cl_ref, gl_ref, sc_ref, cc_ref, gc_ref,
              wu_ref, cw_ref, cb_ref, wd_ref, g_ref, b_ref, o_ref, *, ctx_tiles, n_tiles, alpha, d_ff):
    i = pl.program_id(1)
    is_ctx = i < ctx_tiles
    shift = jnp.where(is_ctx, sc_ref[0], sl_ref[0])
    scale = jnp.where(is_ctx, cc_ref[0], cl_ref[0])
    gate_mod = jnp.where(is_ctx, gc_ref[0], gl_ref[0])
    seg_start = jnp.logical_or(i == 0, i == ctx_tiles)
    seg_end = jnp.logical_or(i == ctx_tiles - 1, i == n_tiles - 1)

    x = x_ref[0]
    xm = (x * (1.0 + scale) + shift).astype(BF16)
    xp = (xp_ref[0] * (1.0 + scale) + shift).astype(BF16)
    xn = (xn_ref[0] * (1.0 + scale) + shift).astype(BF16)
    rows = x.shape[0]
    row_id = lax.broadcasted_iota(jnp.int32, (rows, 1), 0)
    ch = d_ff // FFN_CHUNKS
    f = jnp.zeros(x.shape, F32)
    for c in range(FFN_CHUNKS):
        wg = wu_ref[:, c * ch:(c + 1) * ch]
        wv = wu_ref[:, d_ff + c * ch:d_ff + (c + 1) * ch]
        gate = jnp.dot(xm, wg, preferred_element_type=F32)
        val = jnp.dot(xm, wv, preferred_element_type=F32)
        gp = jnp.dot(xp, wg, preferred_element_type=F32)[SUBLANES - 1:SUBLANES, :]
        gn = jnp.dot(xn, wg, preferred_element_type=F32)[0:1, :]
        gp = jnp.where(seg_start, 0.0, gp)
        gn = jnp.where(seg_end, 0.0, gn)
        prev = jnp.where(row_id == 0, gp, pltpu.roll(gate, 1, 0))
        nxt = jnp.where(row_id == rows - 1, gn, pltpu.roll(gate, rows - 1, 0))
        cw = cw_ref[:, c * ch:(c + 1) * ch]
        conv = prev * cw[0:1, :] + gate * cw[1:2, :] + nxt * cw[2:3, :] + cb_ref[:, c * ch:(c + 1) * ch]
        act = (jax.nn.gelu(conv) * val).astype(BF16)
        f = f + jnp.dot(act, wd_ref[c * ch:(c + 1) * ch, :], preferred_element_type=F32)
    o_ref[0] = _layernorm(alpha * x + gate_mod * f, g_ref[...], b_ref[...])


def _ffn(h, mods_l, mods_c, w_up, conv_w, conv_b, w_down, ln_g, ln_b, ctx_tiles, alpha):
    b, t, d = h.shape
    d_ff = w_down.shape[0]
    tile = 2 * ROW_TILE if ctx_tiles == 0 and t % (2 * ROW_TILE) == 0 else ROW_TILE
    n_tiles = t // tile
    sub = tile // SUBLANES
    n_sub = t // SUBLANES
    row = pl.BlockSpec((1, tile, d), lambda bi, i: (bi, i, 0))
    prev = pl.BlockSpec((1, SUBLANES, d), lambda bi, i: (bi, jnp.maximum(i * sub - 1, 0), 0))
    nxt = pl.BlockSpec((1, SUBLANES, d), lambda bi, i: (bi, jnp.minimum((i + 1) * sub, n_sub - 1), 0))
    lat = pl.BlockSpec((1, 1, d), lambda bi, i: (bi, 0, 0))
    ctx = pl.BlockSpec((1, 1, d), lambda bi, i: (0, 0, 0))
    vec = pl.BlockSpec((1, d), lambda bi, i: (0, 0))
    full = lambda a: pl.BlockSpec(a.shape, lambda bi, i: (0,) * a.ndim)
    cb = conv_b.reshape(1, d_ff)
    return pl.pallas_call(
        functools.partial(_ffn_body, ctx_tiles=ctx_tiles, n_tiles=n_tiles, alpha=alpha, d_ff=d_ff),
        grid=(b, n_tiles),
        in_specs=[row, prev, nxt, lat, lat, lat, ctx, ctx, ctx,
                  full(w_up), full(conv_w), full(cb), full(w_down), vec, vec],
        out_specs=row,
        out_shape=jax.ShapeDtypeStruct((b, t, d), F32),
        compiler_params=_cparams(("parallel", "parallel")),
        name="ffn",
    )(h, h, h, *mods_l, *mods_c, w_up, conv_w, cb, w_down, ln_g.reshape(1, d), ln_b.reshape(1, d))


def _rwkv7(f, tc, lnx_g, lnx_b):
    b = f.shape[0]
    col = RWKV_FEATS.index
    y = _scan("rwkv",
              [_to_scan(f, col("r"), col("r"), "k", tc), _to_scan(f, col("w0"), col("w1"), "k", tc),
               _to_scan(f, col("k0"), col("k1"), "k", tc), _to_scan(f, col("a"), col("a"), "k", tc),
               _to_scan(f, col("b0"), col("b1"), "k", tc)],
              _to_scan(f, col("v"), col("v"), "v", tc))
    return _from_scan(y, b, tc, "rwkv", f, (col("bonus"), col("g")), (lnx_g, lnx_b))


def _rope_tables(n_ctx, n_lat):
    rows = n_lat // GRID_W
    row = jnp.repeat(jnp.arange(rows, dtype=F32), GRID_W)
    col = jnp.tile(jnp.arange(GRID_W, dtype=F32), rows)
    n_freq = DIFF_QK_DIM // 4
    inv_freq = ROPE_BASE ** (-jnp.arange(n_freq, dtype=F32) / n_freq)
    ang = jnp.concatenate([row[:, None] * inv_freq, col[:, None] * inv_freq], axis=-1)
    ang = jnp.concatenate([jnp.zeros((n_ctx, ang.shape[1]), F32), ang], axis=0)
    half = DIFF_QK_DIM // 2
    lane = np.arange(GROUP_WIDTH)
    sign = np.where(lane % DIFF_QK_DIM < half, -1.0, 1.0).astype(np.float32)
    return jnp.cos(ang)[:, lane % half], jnp.sin(ang)[:, lane % half] * sign


def _diff_attn(q, k, v, tc, lam_vecs, norm_g, layer, need_ctx):
    lam_init = 0.8 - 0.6 * math.exp(-0.3 * layer)
    lv = lam_vecs.astype(F32)
    lam = jnp.exp(jnp.sum(lv[0] * lv[1])) - jnp.exp(jnp.sum(lv[2] * lv[3])) + lam_init
    q_tile0 = 0 if need_ctx else tc // ATTN_TILE
    return _attention(q, k, v, 0, lam, norm_g, 1.0 - lam_init, q_tile0, tc)


def _hgrn2(f, tc, norm_g):
    b = f.shape[0]
    col = HGRN_FEATS.index
    o = _scan("gla",
              [_to_scan(f, col("q"), col("q"), "k", tc), _to_scan(f, col("w0"), col("w1"), "k", tc),
               _to_scan(f, col("k0"), col("k1"), "k", tc)],
              _to_scan(f, col("v"), col("v"), "v", tc))
    return _from_scan(o, b, tc, "gla", f, (col("gate"),), (jnp.tile(norm_g, N_HEADS),))


RWKV_COLS = 3 * GROUP_WIDTH + 2 * RWKV_DECAY_LORA + 2 * RWKV_ICLR_LORA + RWKV_GATE_LORA
DIFF_COLS = 3 * GROUP_WIDTH
HGRN_COLS = 5 * GROUP_WIDTH
MLSTM_COLS = 4 * GROUP_WIDTH + 4 * N_HEADS
SLAB_COLS = (RWKV_COLS, DIFF_COLS, HGRN_COLS, MLSTM_COLS)
SLAB_PAD = (5 * GROUP_WIDTH, DIFF_COLS, HGRN_COLS, 4 * GROUP_WIDTH + LANES)


def _split_cols(w):
    parts, off = [], 0
    for n, n_pad in zip(SLAB_COLS, SLAB_PAD):
        parts.append(jnp.pad(w[:, off:off + n], ((0, 0), (0, n_pad - n))).astype(BF16))
        off += n
    return parts


def kernel(x, c, ctx, c_ctx, ada_w, ada_b, w_in, rwkv_mu, rwkv_w0, rwkv_w2, rwkv_a0, rwkv_a2, rwkv_g2, rwkv_k_k, rwkv_k_a, rwkv_r_k, rwkv_lnx_g, rwkv_lnx_b, diff_lambda, diff_norm_g, hgrn_lb_logits, hgrn_norm_g, mlstm_conv_w, mlstm_conv_b, mlstm_gate_b, mlstm_norm_g, w_out, ffn_w_up, ffn_conv_w, ffn_conv_b, ffn_w_down, ln_g, ln_b):
    depth = w_in.shape[0]
    b, seq, d = x.shape
    tc = ctx.shape[1]
    assert tc == ROW_TILE == ATTN_TILE and seq % ROW_TILE == 0 and 2 * b * N_HEADS <= LANES
    alpha = (2.0 * depth) ** 0.25

    lb_w = jax.nn.softmax(hgrn_lb_logits.astype(F32), axis=0)
    lower_bounds = jnp.cumsum(lb_w, axis=0) - lb_w[0]

    rope = _rope_tables(tc, seq)
    cond = jnp.zeros((8, d), F32).at[:b].set(c).at[b].set(c_ctx)
    h = jnp.concatenate([ctx, x], axis=1)
    for layer in range(depth):
        last = layer == depth - 1
        mod = _adaln(cond, ada_w, ada_b, layer).reshape(8, 6, d)
        mods_l = [mod[:b, j][:, None, :] for j in range(6)]
        mods_c = [mod[b:b + 1, j][:, None, :] for j in range(6)]

        rwkv_prm = (rwkv_mu[layer], rwkv_w0[layer], rwkv_w2[layer], rwkv_a0[layer], rwkv_a2[layer], rwkv_g2[layer],
                    rwkv_k_k[layer], rwkv_k_a[layer], rwkv_r_k[layer])
        f_rwkv, q, k, v, f_hgrn, p_mlstm = _inproj(h, mods_l[0], mods_l[1], mods_c[0], mods_c[1], rope,
                                                   _split_cols(w_in[layer]), rwkv_prm, lower_bounds[layer],
                                                   tc // ROW_TILE)
        y_a = _rwkv7(f_rwkv, tc, rwkv_lnx_g[layer], rwkv_lnx_b[layer])
        y_b = _diff_attn(q, k, v, tc, diff_lambda[layer], diff_norm_g[layer], layer, not last)
        y_c = _hgrn2(f_hgrn, tc, hgrn_norm_g[layer])
        h_f, h_b = (_mlstm_chunked(p_mlstm, mlstm_conv_w[layer], mlstm_conv_b[layer], mlstm_gate_b[layer], rev, tc)
                    for rev in (False, True))

        tile0, ctx_tiles = (tc // ROW_TILE, 0) if last else (0, tc // ROW_TILE)
        h = _outproj(y_a, y_b, y_c, h_f, h_b, p_mlstm, mlstm_norm_g[layer], h, mods_l[2], mods_c[2],
                     w_out[layer].astype(BF16), ln_g[layer, 0], ln_b[layer, 0], tile0, ctx_tiles, alpha)
        h = _ffn(h, mods_l[3:6], mods_c[3:6], ffn_w_up[layer].astype(BF16), ffn_conv_w[layer], ffn_conv_b[layer],
                 ffn_w_down[layer].astype(BF16), ln_g[layer, 1], ln_b[layer, 1], ctx_tiles, alpha)
    return h
```

```python
import functools
import math

import jax
import jax.numpy as jnp
import numpy as np
from jax import lax
from jax.experimental import pallas as pl
from jax.experimental.pallas import tpu as pltpu

F32 = jnp.float32
BF16 = jnp.bfloat16

HEAD_DIM = 64
N_HEADS = 4
GROUP_WIDTH = N_HEADS * HEAD_DIM
DIFF_QK_DIM = HEAD_DIM // 2
GRID_W = 64
ROPE_BASE = 10000.0
RWKV_DECAY_LORA = 64
RWKV_ICLR_LORA = 64
RWKV_GATE_LORA = 160
RWKV_LNX_EPS = 64e-5
LN_EPS = 1e-5
LB_FLOOR = 1e-30
MAX_NEG_LOG_STAB = 60.0

LANES = 128
SUBLANES = 8
ADALN_TILE = 1536
ROW_TILE = 256
ATTN_TILE = 256
TIME_BLOCK = 128
TO_SCAN_BLOCK = 256
SCAN_TB = 64
VMEM_LIMIT = 56 * 1024 * 1024


def _cparams(sem):
    return pltpu.CompilerParams(dimension_semantics=sem, vmem_limit_bytes=VMEM_LIMIT)


def _rev_block(i, n_ctx, n_all):
    return jnp.where(i < n_ctx, n_ctx - 1 - i, n_ctx + n_all - 1 - i)


def _split3(x):
    hi = x.astype(BF16)
    r1 = x - hi.astype(F32)
    mid = r1.astype(BF16)
    return hi, mid, (r1 - mid.astype(F32)).astype(BF16)


def _dot01_left(a01, x):
    a = a01.astype(BF16)
    p = [jnp.dot(a, part, preferred_element_type=F32) for part in _split3(x)]
    return (p[0] + p[1]) + p[2]


def _dot01_right(x, b01):
    b = b01.astype(BF16)
    p = [jnp.dot(part, b, preferred_element_type=F32) for part in _split3(x)]
    return (p[0] + p[1]) + p[2]


def _flip_rows(x):
    n = x.shape[0]
    r = lax.broadcasted_iota(jnp.int32, (n, n), 0)
    c = lax.broadcasted_iota(jnp.int32, (n, n), 1)
    return _dot01_left(jnp.where(r + c == n - 1, 1.0, 0.0).astype(F32), x)


def _adaln_body(c_ref, w_ref, b_ref, o_ref):
    x = c_ref[...]
    x = (x * jax.nn.sigmoid(x)).astype(BF16)
    o_ref[...] = jnp.dot(x, w_ref[0].astype(BF16), preferred_element_type=F32) + b_ref[0]


def _adaln(cond, w, b, layer):
    m, d = cond.shape
    n = w.shape[2]
    tn = ADALN_TILE
    return pl.pallas_call(
        _adaln_body,
        grid=(n // tn,),
        in_specs=[pl.BlockSpec((m, d), lambda j: (0, 0)),
                  pl.BlockSpec((1, d, tn), lambda j: (layer, 0, j)),
                  pl.BlockSpec((1, 1, tn), lambda j: (layer, 0, j))],
        out_specs=pl.BlockSpec((m, tn), lambda j: (0, j)),
        out_shape=jax.ShapeDtypeStruct((m, n), F32),
        compiler_params=_cparams(("arbitrary",)),
        name="adaln",
    )(cond, w, b.reshape(b.shape[0], 1, n))


def _softplus(x):
    return jnp.maximum(x, 0.0) + jnp.log(1.0 + jnp.exp(-jnp.abs(x)))


def _head_block(scale=1.0):
    r = lax.broadcasted_iota(jnp.int32, (GROUP_WIDTH, GROUP_WIDTH), 0) // HEAD_DIM
    c = lax.broadcasted_iota(jnp.int32, (GROUP_WIDTH, GROUP_WIDTH), 1) // HEAD_DIM
    return jnp.where(r == c, scale, 0.0).astype(F32)


def _head_sum(x, block_bf16):
    hi = x.astype(BF16)
    lo = (x - hi.astype(F32)).astype(BF16)
    return (jnp.dot(hi, block_bf16, preferred_element_type=F32)
            + jnp.dot(lo, block_bf16, preferred_element_type=F32))


def _shifted_rows(x, prev8, next8, seg_start, seg_end):
    rows = x.shape[0]
    row_id = lax.broadcasted_iota(jnp.int32, (rows, 1), 0)
    prev_row = jnp.where(seg_start, 0.0, prev8[SUBLANES - 1:SUBLANES, :])
    next_row = jnp.where(seg_end, 0.0, next8[0:1, :])
    prev = jnp.where(row_id == 0, prev_row, pltpu.roll(x, 1, 0))
    nxt = jnp.where(row_id == rows - 1, next_row, pltpu.roll(x, rows - 1, 0))
    return prev, nxt


RWKV_FEATS = ("r", "v", "w0", "w1", "k0", "k1", "a", "b0", "b1", "g", "bonus")


def _rwkv_feats(p, prev8, next8, seg_start, seg_end, mu_ref, w0_ref, w2_ref, a0_ref, a2_ref, g2_ref, kk_ref,
                ka_ref, rk_ref):
    gw = GROUP_WIDTH
    prev, nxt = _shifted_rows(p, prev8, next8, seg_start, seg_end)
    p = p + (0.5 * (prev + nxt) - p) * mu_ref[...]
    r, k, v = p[:, :gw], p[:, gw:2 * gw], p[:, 2 * gw:3 * gw]
    wd, ad, gd = p[:, 3 * gw:3 * gw + LANES], p[:, 3 * gw + LANES:4 * gw], p[:, 4 * gw:5 * gw]
    lora_w = jnp.dot(jnp.tanh(wd).astype(BF16), w2_ref[...], preferred_element_type=F32)
    lora_a = jnp.dot(ad.astype(BF16), a2_ref[...], preferred_element_type=F32)
    decay = jnp.exp(-jnp.exp(-_softplus(-(w0_ref[...] + lora_w)) - 0.5))
    a = jax.nn.sigmoid(a0_ref[...] + lora_a)
    g = jnp.dot(jax.nn.sigmoid(gd).astype(BF16), g2_ref[...], preferred_element_type=F32)
    block = _head_block().astype(BF16)
    kk = k * kk_ref[...]
    kk = kk / jnp.maximum(jnp.sqrt(_head_sum(kk * kk, block)), 1e-12)
    k0 = k * (1.0 + (a[:, :gw] - 1.0) * ka_ref[...])
    k1 = k * (1.0 + (a[:, gw:] - 1.0) * ka_ref[...])
    bonus = _head_sum(r * (k0 + k1) * rk_ref[...], block) * v
    return dict(r=r, v=v, w0=decay[:, :gw], w1=decay[:, gw:], k0=k0, k1=k1, a=-kk, b0=kk * a[:, :gw],
                b1=kk * a[:, gw:], g=g, bonus=bonus)


def _block_diag2(w):
    z = jnp.zeros_like(w[0])
    return jnp.concatenate([jnp.concatenate([w[0], z], axis=1), jnp.concatenate([z, w[1]], axis=1)], axis=0)


HGRN_FEATS = ("q", "w0", "w1", "k0", "k1", "gate", "v")


def _hgrn_feats(p, llb_ref, l1m_ref, oml_ref):
    gw = GROUP_WIDTH
    q, f_f, f_b, v, g = (p[:, j * gw:(j + 1) * gw] for j in range(5))
    feats = dict(q=q * jax.nn.sigmoid(q), gate=g * jax.nn.sigmoid(g), v=v)
    for d, f in enumerate((f_f, f_b)):
        x = llb_ref[...]
        y = l1m_ref[...] - _softplus(-f)
        log_f = jnp.maximum(x, y) + jnp.log(1.0 + jnp.exp(-jnp.abs(x - y)))
        feats["w%d" % d] = jnp.exp(log_f)
        feats["k%d" % d] = oml_ref[...] * jax.nn.sigmoid(-f)
    return feats


N_RWKV_PRM, N_HGRN_PRM = 9, 3


def _rope(x, cos, sin_signed):
    half = DIFF_QK_DIM // 2
    lane = lax.broadcasted_iota(jnp.int32, x.shape, 1)
    n = x.shape[1]
    partner = jnp.where(lane % DIFF_QK_DIM < half, pltpu.roll(x, n - half, 1), pltpu.roll(x, half, 1))
    return x * cos + partner * sin_signed


def _inproj_body(x_ref, xp_ref, xn_ref, sl_ref, cl_ref, sc_ref, cc_ref, cos_ref, sin_ref, wr_ref, wd_ref, wh_ref,
                 wm_ref, *refs, ctx_tiles, n_tiles):
    rwkv_prm, hgrn_prm = refs[:N_RWKV_PRM], refs[N_RWKV_PRM:N_RWKV_PRM + N_HGRN_PRM]
    fr_ref, q_ref, k_ref, v_ref, fh_ref, pm_ref = refs[N_RWKV_PRM + N_HGRN_PRM:]
    i = pl.program_id(1)
    is_ctx = i < ctx_tiles
    seg_start = jnp.logical_or(i == 0, i == ctx_tiles)
    seg_end = jnp.logical_or(i == ctx_tiles - 1, i == n_tiles - 1)
    shift = jnp.where(is_ctx, sc_ref[0], sl_ref[0])
    scale = jnp.where(is_ctx, cc_ref[0], cl_ref[0])
    mod = lambda x: (x * (1.0 + scale) + shift).astype(BF16)
    proj = lambda x, w_ref: jnp.dot(x, w_ref[...], preferred_element_type=F32)
    gw = GROUP_WIDTH
    xm = mod(x_ref[0])
    feats = _rwkv_feats(proj(xm, wr_ref), proj(mod(xp_ref[0]), wr_ref), proj(mod(xn_ref[0]), wr_ref),
                        seg_start, seg_end, *rwkv_prm)
    for j, name in enumerate(RWKV_FEATS):
        fr_ref[0, :, j * gw:(j + 1) * gw] = feats[name]
    pd = proj(xm, wd_ref)
    q_ref[0] = (_rope(pd[:, :gw], cos_ref[...], sin_ref[...])
                * (DIFF_QK_DIM ** -0.5 * math.log2(math.e))).astype(BF16)
    k_ref[0] = _rope(pd[:, gw:2 * gw], cos_ref[...], sin_ref[...]).astype(BF16)
    v_ref[0] = pd[:, 2 * gw:]
    feats = _hgrn_feats(proj(xm, wh_ref), *hgrn_prm)
    for j, name in enumerate(HGRN_FEATS):
        fh_ref[0, :, j * gw:(j + 1) * gw] = feats[name]
    pm_ref[0] = proj(xm, wm_ref)


def _inproj(h, shift_l, scale_l, shift_c, scale_c, rope, ws, rwkv_prm, lb, ctx_tiles):
    b, t, d = h.shape
    gw = GROUP_WIDTH
    n_tiles, sub, n_sub = t // ROW_TILE, ROW_TILE // SUBLANES, t // SUBLANES
    row = lambda n: pl.BlockSpec((1, ROW_TILE, n), lambda bi, i: (bi, i, 0))
    prev = pl.BlockSpec((1, SUBLANES, d), lambda bi, i: (bi, jnp.maximum(i * sub - 1, 0), 0))
    nxt = pl.BlockSpec((1, SUBLANES, d), lambda bi, i: (bi, jnp.minimum((i + 1) * sub, n_sub - 1), 0))
    lat = pl.BlockSpec((1, 1, d), lambda bi, i: (bi, 0, 0))
    ctx = pl.BlockSpec((1, 1, d), lambda bi, i: (0, 0, 0))
    full = lambda a: pl.BlockSpec(a.shape, lambda bi, i: (0,) * a.ndim)
    vec = lambda a: a.reshape(1, -1).astype(F32)
    mu, w0, w2, a0, a2, g2, k_k, k_a, r_k = rwkv_prm
    cols = ws[0].shape[1]
    prm = [jnp.pad(vec(mu), ((0, 0), (0, cols - mu.shape[0]))), vec(w0), _block_diag2(w2).astype(BF16), vec(a0),
           _block_diag2(a2).astype(BF16), jnp.pad(g2, ((0, gw - g2.shape[0]), (0, 0))).astype(BF16), vec(k_k),
           vec(k_a), vec(r_k),
           vec(jnp.log(jnp.maximum(lb, LB_FLOOR))), vec(jnp.log1p(-lb)), vec(1.0 - lb)]
    assert len(prm) == N_RWKV_PRM + N_HGRN_PRM
    outs = [(len(RWKV_FEATS) * gw, F32), (gw, BF16), (gw, BF16), (gw, F32), (len(HGRN_FEATS) * gw, F32),
            (ws[3].shape[1], F32)]
    tab = pl.BlockSpec((ROW_TILE, gw), lambda bi, i: (i, 0))
    return pl.pallas_call(
        functools.partial(_inproj_body, ctx_tiles=ctx_tiles, n_tiles=n_tiles),
        grid=(b, n_tiles),
        in_specs=[row(d), prev, nxt, lat, lat, ctx, ctx, tab, tab] + [full(a) for a in list(ws) + prm],
        out_specs=[row(n) for n, _ in outs],
        out_shape=[jax.ShapeDtypeStruct((b, t, n), dt) for n, dt in outs],
        compiler_params=_cparams(("parallel", "parallel")),
        name="inproj",
    )(h, h, h, shift_l, scale_l, shift_c, scale_c, *rope, *ws, *prm)


def _to_scan_body(x0_ref, x1_ref, o_ref, r_ref, *, kind, nb):
    n_scan = 2 * nb * N_HEADS
    rep = LANES // n_scan
    n_piece = x0_ref.shape[1] // LANES
    for b in range(nb):
        for d, xt in enumerate((x0_ref[b].T, _flip_rows(x1_ref[b]).T)):
            for pc in range(n_piece):
                r_ref[pc, pl.ds((d * nb + b) * GROUP_WIDTH, GROUP_WIDTH), :] = xt[:, pc * LANES:(pc + 1) * LANES]
    for pc in range(n_piece):
        t_sl = slice(pc * LANES, (pc + 1) * LANES)
        if kind == "k":
            for k in range(HEAD_DIM):
                rows = r_ref[pc, pl.ds(k, n_scan, stride=HEAD_DIM), :]
                o_ref[k, t_sl, :] = jnp.concatenate([rows] * rep, axis=0).T
        else:
            for vh in range(HEAD_DIM // rep):
                rows = [r_ref[pc, pl.ds(vh * rep + vl, n_scan, stride=HEAD_DIM), :] for vl in range(rep)]
                o_ref[t_sl, vh, :] = jnp.concatenate(rows, axis=0).T


def _to_scan(x, col0, col1, kind, tc):
    nb, t, _ = x.shape
    gw = GROUP_WIDTH
    x0 = x1 = x
    blk = TO_SCAN_BLOCK if kind == "k" else TIME_BLOCK
    n_all, n_ctx = t // blk, tc // blk
    n_scan = 2 * nb * N_HEADS
    rep = LANES // n_scan
    fwd = pl.BlockSpec((nb, blk, gw), lambda i: (0, i, col0))
    bwd = pl.BlockSpec((nb, blk, gw), lambda i: (0, _rev_block(i, n_ctx, n_all), col1))
    if kind == "k":
        out_shape = (HEAD_DIM, t, LANES)
        out_spec = pl.BlockSpec((HEAD_DIM, blk, LANES), lambda i: (0, i, 0))
    else:
        out_shape = (t, HEAD_DIM // rep, LANES)
        out_spec = pl.BlockSpec((blk, HEAD_DIM // rep, LANES), lambda i: (i, 0, 0))
    return pl.pallas_call(
        functools.partial(_to_scan_body, kind=kind, nb=nb),
        grid=(n_all,),
        in_specs=[fwd, bwd],
        out_specs=out_spec,
        out_shape=jax.ShapeDtypeStruct(out_shape, F32),
        scratch_shapes=[pltpu.VMEM((blk // LANES, n_scan * HEAD_DIM, LANES), F32)],
        compiler_params=_cparams(("parallel",)),
        name="to_scan_" + kind,
    )(x0, x1)


def _from_scan_body(yf_ref, yb_ref, *rest, nb, readout):
    if readout == "rwkv":
        bonus_ref, g_ref, ng_ref, nb_ref, o_ref, r_ref = rest
    else:
        gate_ref, ng_ref, o_ref, r_ref = rest
    n_scan = 2 * nb * N_HEADS
    rep = LANES // n_scan
    for vh in range(HEAD_DIM // rep):
        r_ref[0, pl.ds(vh * LANES, LANES), :] = yf_ref[vh].T
        r_ref[1, pl.ds(vh * LANES, LANES), :] = yb_ref[vh].T
    mean_block = _head_block(1.0 / HEAD_DIM).astype(BF16)
    for b in range(nb):
        slab = lambda d: jnp.concatenate(
            [r_ref[d, pl.ds((d * nb + b) * N_HEADS + h, HEAD_DIM, stride=n_scan), :] for h in range(N_HEADS)],
            axis=0).T
        y = slab(0) + _flip_rows(slab(1))
        if readout == "rwkv":
            y = y - _head_sum(y, mean_block)
            y = y * lax.rsqrt(_head_sum(y * y, mean_block) + RWKV_LNX_EPS) * ng_ref[...] + nb_ref[...]
            o_ref[b] = (y + bonus_ref[b]) * g_ref[b]
        else:
            y = y * lax.rsqrt(_head_sum(y * y, mean_block) + LN_EPS) * ng_ref[...]
            o_ref[b] = y * gate_ref[b]


def _from_scan(y, nb, tc, readout, feats, cols, vecs):
    n_vh, t, _ = y.shape
    gw = GROUP_WIDTH
    n_all, n_ctx = t // TIME_BLOCK, tc // TIME_BLOCK
    fwd = pl.BlockSpec((n_vh, TIME_BLOCK, LANES), lambda i: (0, i, 0))
    bwd = pl.BlockSpec((n_vh, TIME_BLOCK, LANES), lambda i: (0, _rev_block(i, n_ctx, n_all), 0))
    tok = lambda c: pl.BlockSpec((nb, TIME_BLOCK, gw), lambda i: (0, i, c))
    vec = pl.BlockSpec((1, gw), lambda i: (0, 0))
    return pl.pallas_call(
        functools.partial(_from_scan_body, nb=nb, readout=readout),
        grid=(n_all,),
        in_specs=[fwd, bwd] + [tok(c) for c in cols] + [vec] * len(vecs),
        out_specs=tok(0),
        out_shape=jax.ShapeDtypeStruct((nb, t, gw), F32),
        scratch_shapes=[pltpu.VMEM((2, n_vh * LANES, TIME_BLOCK), F32)],
        compiler_params=_cparams(("parallel",)),
        name="from_scan_" + readout,
    )(y, y, *([feats] * len(cols)), *[v.reshape(1, gw) for v in vecs])


N_ACC = 4


def _acc_add(acc, i, x):
    acc[i % N_ACC] = x if acc[i % N_ACC] is None else acc[i % N_ACC] + x


def _acc_total(acc):
    return (acc[0] + acc[1]) + (acc[2] + acc[3])


def _scan_body(*refs, mode, tb, n_vh):
    if mode == "rwkv":
        r_ref, w_ref, k_ref, a_ref, b_ref, v_ref, y_ref, s_ref = refs
    else:
        r_ref, w_ref, k_ref, v_ref, y_ref, s_ref = refs

    @pl.when(pl.program_id(0) == 0)
    def _():
        s_ref[...] = jnp.zeros_like(s_ref)

    row = lambda ref, k, j: ref[k, pl.ds(j, 1), :]
    wide = lambda x: jnp.broadcast_to(x, (n_vh, LANES))

    def store_y(j, y):
        for vh in range(n_vh):
            y_ref[vh, pl.ds(j, 1), :] = y[vh:vh + 1, :]

    def rwkv_step(j, sa):
        v = v_ref[j]
        j_next = jnp.minimum(j + 1, tb - 1)
        ys, sas = [None] * N_ACC, [None] * N_ACC
        for k in range(HEAD_DIM):
            s = s_ref[k] * wide(row(w_ref, k, j)) + sa * wide(row(b_ref, k, j)) + v * wide(row(k_ref, k, j))
            s_ref[k] = s
            _acc_add(ys, k, s * wide(row(r_ref, k, j)))
            _acc_add(sas, k, s * wide(row(a_ref, k, j_next)))
        store_y(j, _acc_total(ys))
        return _acc_total(sas)

    def gla_step(j, carry):
        v = v_ref[j]
        ys = [None] * N_ACC
        for k in range(HEAD_DIM):
            s = s_ref[k] * wide(row(w_ref, k, j)) + v * wide(row(k_ref, k, j))
            s_ref[k] = s
            _acc_add(ys, k, s * wide(row(r_ref, k, j)))
        store_y(j, _acc_total(ys))
        return carry

    if mode == "rwkv":
        sa0 = [None] * N_ACC
        for k in range(HEAD_DIM):
            _acc_add(sa0, k, s_ref[k] * wide(row(a_ref, k, 0)))
        lax.fori_loop(0, tb, rwkv_step, _acc_total(sa0), unroll=8)
    else:
        lax.fori_loop(0, tb, gla_step, 0, unroll=8)


def _scan(mode, k_inputs, v_in):
    t, n_vh, _ = v_in.shape
    tb = SCAN_TB
    big = pl.BlockSpec((HEAD_DIM, tb, LANES), lambda i: (0, i, 0))
    small = pl.BlockSpec((tb, n_vh, LANES), lambda i: (i, 0, 0))
    return pl.pallas_call(
        functools.partial(_scan_body, mode=mode, tb=tb, n_vh=n_vh),
        grid=(t // tb,),
        in_specs=[big] * len(k_inputs) + [small],
        out_specs=pl.BlockSpec((n_vh, tb, LANES), lambda i: (0, i, 0)),
        out_shape=jax.ShapeDtypeStruct((n_vh, t, LANES), F32),
        scratch_shapes=[pltpu.VMEM((HEAD_DIM, n_vh, LANES), F32)],
        compiler_params=_cparams(("arbitrary",)),
        name="scan_" + mode,
    )(*k_inputs, v_in)


MLSTM_CHUNK = 64


def _prefix_max(x, reverse):
    n = x.shape[0]
    row = lax.broadcasted_iota(jnp.int32, x.shape, 0)
    sh = 1
    while sh < n:
        if reverse:
            x = jnp.where(row < n - sh, jnp.maximum(x, pltpu.roll(x, n - sh, 0)), x)
        else:
            x = jnp.where(row >= sh, jnp.maximum(x, pltpu.roll(x, sh, 0)), x)
        sh *= 2
    return x


def _mlstm_body(qk_ref, qkp_ref, qkn_ref, v_ref, gt_ref, cw_ref, cb_ref, sel_ref, gb_ref, o_ref, st_ref, m_ref, *,
                reverse, n_sub, n_ctx, n_blocks):
    c, gw = MLSTM_CHUNK, GROUP_WIDTH
    i = pl.program_id(1)

    @pl.when(i == 0)
    def _():
        st_ref[...] = jnp.zeros_like(st_ref)
        m_ref[...] = jnp.zeros_like(m_ref)

    blk = _rev_block(i, n_ctx, n_blocks) if reverse else i
    seg_start = jnp.logical_or(blk == 0, blk == n_ctx)
    seg_end = jnp.logical_or(blk == n_ctx - 1, blk == n_blocks - 1)
    qk = qk_ref[0]
    prev, nxt = _shifted_rows(qk, qkp_ref[0], qkn_ref[0], seg_start, seg_end)
    qk = prev * cw_ref[0:1, :] + qk * cw_ref[1:2, :] + nxt * cw_ref[2:3, :] + cb_ref[...]
    qk = qk * jax.nn.sigmoid(qk)
    q_all, k_all = qk[:, :gw], qk[:, gw:] * HEAD_DIM ** -0.5
    v_all = v_ref[0]
    gt = gt_ref[0]
    ig_all = _dot01_right(gt, sel_ref[0]) + gb_ref[0:1, :]
    fg_all = -_softplus(-(_dot01_right(gt, sel_ref[1]) + gb_ref[1:2, :]))

    tt = lax.broadcasted_iota(jnp.int32, (c, c), 0)
    ss = lax.broadcasted_iota(jnp.int32, (c, c), 1)
    tri = jnp.where((ss >= tt) if reverse else (ss <= tt), 1.0, 0.0).astype(F32)
    row = lax.broadcasted_iota(jnp.int32, (c, gw), 0)
    s_of_lane = lax.broadcasted_iota(jnp.int32, (c, gw), 1) % c
    causal = (s_of_lane >= row) if reverse else (s_of_lane <= row)
    diag = s_of_lane == row
    block = _head_block()
    block2 = jnp.concatenate([block, block], axis=1)
    ones8 = jnp.ones((SUBLANES, c), F32)
    nt = (((1,), (1,)), ((), ()))
    last = 0 if reverse else c - 1
    for u in (range(n_sub - 1, -1, -1) if reverse else range(n_sub)):
        sl = slice(u * c, (u + 1) * c)
        q, k, v, ig, fg = q_all[sl], k_all[sl], v_all[sl], ig_all[sl], fg_all[sl]
        b = _dot01_left(tri, fg)
        g = ig - b
        m_prev = m_ref[0:1, :]
        m_t = b + jnp.maximum(m_prev, _prefix_max(g, reverse))
        w_inter = jnp.exp(b + m_prev - m_t)
        g_row = _dot01_left(ones8, jnp.where(diag, g, 0.0))[0:1]
        qb = q.astype(BF16)
        k_bd = (jnp.concatenate([k] * N_HEADS, axis=0) * block).astype(BF16)
        scores = lax.dot_general(qb, k_bd, nt, preferred_element_type=F32)
        w = jnp.where(causal, jnp.exp((b - m_t) + g_row), 0.0) * scores
        v_bd = jnp.concatenate([jnp.concatenate([v] * N_HEADS, axis=0) * block, block], axis=1).astype(BF16)
        intra = jnp.dot(w.astype(BF16), v_bd, preferred_element_type=F32)
        inter = jnp.dot(qb, st_ref[...].astype(BF16), preferred_element_type=F32)
        num = w_inter * inter[:, :gw] + intra[:, :gw]
        den = w_inter * inter[:, gw:] + intra[:, gw:]
        floor = jnp.exp(jnp.minimum(-m_t, MAX_NEG_LOG_STAB))
        o_ref[0, pl.ds(u * c, c), :] = num / jnp.maximum(jnp.abs(den), floor)
        m_new = m_t[last:last + 1]
        b_end = b[last:last + 1]
        kw = k * jnp.exp(b_end - b + ig - m_new)
        decay = jnp.exp(b_end + m_prev - m_new)
        v_one = jnp.concatenate([v, jnp.ones_like(v)], axis=1).astype(BF16)
        upd = jnp.dot(kw.T.astype(BF16), v_one, preferred_element_type=F32)
        st_ref[...] = st_ref[...] * jnp.concatenate([decay, decay], axis=1) + upd * block2
        m_ref[0:1, :] = m_new


def _mlstm_chunked(p, conv_w, conv_b, gate_b, reverse, tc):
    b, t, _ = p.shape
    gw = GROUP_WIDTH
    n_blocks, n_ctx = t // ROW_TILE, tc // ROW_TILE
    sub, n_sub8 = ROW_TILE // SUBLANES, t // SUBLANES
    blk = (lambda i: _rev_block(i, n_ctx, n_blocks)) if reverse else (lambda i: i)
    d = 1 if reverse else 0
    lane_head = np.arange(gw) // HEAD_DIM
    sel = np.zeros((2, LANES, gw), np.float32)
    for j, base in enumerate((d * N_HEADS, (2 + d) * N_HEADS)):
        sel[j, base + lane_head, np.arange(gw)] = 1.0
    gate_bias = jnp.stack([jnp.repeat(gate_b[d], HEAD_DIM), jnp.repeat(gate_b[2 + d], HEAD_DIM)], axis=0)
    full = lambda a: pl.BlockSpec(a.shape, lambda bi, i: (0,) * a.ndim)
    cb = conv_b.reshape(1, 2 * gw)
    return pl.pallas_call(
        functools.partial(_mlstm_body, reverse=reverse, n_sub=ROW_TILE // MLSTM_CHUNK, n_ctx=n_ctx,
                          n_blocks=n_blocks),
        grid=(b, n_blocks),
        in_specs=[pl.BlockSpec((1, ROW_TILE, 2 * gw), lambda bi, i: (bi, blk(i), 0)),
                  pl.BlockSpec((1, SUBLANES, 2 * gw), lambda bi, i: (bi, jnp.maximum(blk(i) * sub - 1, 0), 0)),
                  pl.BlockSpec((1, SUBLANES, 2 * gw), lambda bi, i: (bi, jnp.minimum((blk(i) + 1) * sub, n_sub8 - 1), 0)),
                  pl.BlockSpec((1, ROW_TILE, gw), lambda bi, i: (bi, blk(i), 2)),
                  pl.BlockSpec((1, ROW_TILE, LANES), lambda bi, i: (bi, blk(i), 4 * gw // LANES)),
                  full(conv_w), full(cb), full(sel), full(gate_bias)],
        out_specs=pl.BlockSpec((1, ROW_TILE, gw), lambda bi, i: (bi, blk(i), 0)),
        out_shape=jax.ShapeDtypeStruct((b, t, gw), F32),
        scratch_shapes=[pltpu.VMEM((gw, 2 * gw), F32), pltpu.VMEM((SUBLANES, gw), F32)],
        compiler_params=_cparams(("parallel", "arbitrary")),
        name="mlstm_chunk",
    )(p, p, p, p, p, conv_w, cb, jnp.asarray(sel), gate_bias)


def _attn_body(q_ref, k_ref, v_ref, lam_ref, g_ref, o_ref, *, out_scale, ctx_tiles, n_ctx_keys):
    def attend(n_keys):
        q = q_ref[0]
        k = k_ref[0, :n_keys, :]
        v = v_ref[0, :n_keys, :].astype(BF16)
        lane = lax.broadcasted_iota(jnp.int32, q.shape, 1)
        comp = lane // DIFF_QK_DIM
        nt = (((1,), (1,)), ((), ()))
        n_maps = LANES // DIFF_QK_DIM
        ss = [lax.dot_general(jnp.where(comp == c, q, 0.0).astype(BF16), k, nt, preferred_element_type=F32)
              for c in range(n_maps)]
        es = [jnp.exp2(s - jnp.max(s, axis=-1, keepdims=True)) for s in ss]
        ls = [jnp.sum(e, axis=-1, keepdims=True) for e in es]
        ys = []
        for hh in range(LANES // HEAD_DIM):
            e0, e1, l0, l1 = es[2 * hh], es[2 * hh + 1], ls[2 * hh], ls[2 * hh + 1]
            pr = (e0 - e1 * (lam_ref[0:1, 0:1] * l0 / l1)).astype(BF16)
            ys.append(jnp.dot(pr, v, preferred_element_type=F32) * (1.0 / l0))
        first = lane < HEAD_DIM
        y = jnp.where(first, ys[0], ys[1])
        ysq = y * y
        ms = jnp.where(first, jnp.sum(jnp.where(first, ysq, 0.0), axis=-1, keepdims=True),
                       jnp.sum(jnp.where(first, 0.0, ysq), axis=-1, keepdims=True)) * (1.0 / HEAD_DIM)
        o_ref[0] = y * lax.rsqrt(ms + LN_EPS) * g_ref[...] * out_scale

    if ctx_tiles == 0:
        attend(k_ref.shape[1])
    else:
        is_ctx = pl.program_id(2) < ctx_tiles
        pl.when(is_ctx)(lambda: attend(n_ctx_keys))
        pl.when(jnp.logical_not(is_ctx))(lambda: attend(k_ref.shape[1]))


def _attention(q, k, v, v_col0, lam, norm_g, out_scale, q_tile0, tc):
    b, t, gw = q.shape
    n_tiles = t // ATTN_TILE - q_tile0
    ctx_tiles = max(tc // ATTN_TILE - q_tile0, 0)
    qspec = pl.BlockSpec((1, ATTN_TILE, LANES), lambda bi, pi, i: (bi, q_tile0 + i, pi))
    kspec = pl.BlockSpec((1, t, LANES), lambda bi, pi, i: (bi, 0, pi))
    vspec = pl.BlockSpec((1, t, LANES), lambda bi, pi, i: (bi, 0, v_col0 + pi))
    vec = pl.BlockSpec((1, LANES), lambda bi, pi, i: (0, 0))
    return pl.pallas_call(
        functools.partial(_attn_body, out_scale=out_scale, ctx_tiles=ctx_tiles, n_ctx_keys=tc),
        grid=(b, gw // LANES, n_tiles),
        in_specs=[qspec, kspec, vspec, vec, vec],
        out_specs=pl.BlockSpec((1, ATTN_TILE, LANES), lambda bi, pi, i: (bi, i, pi)),
        out_shape=jax.ShapeDtypeStruct((b, n_tiles * ATTN_TILE, gw), F32),
        compiler_params=_cparams(("parallel", "parallel", "parallel")),
        name="diff_attn",
    )(q, k, v, jnp.full((1, LANES), lam, F32), jnp.tile(norm_g.reshape(1, HEAD_DIM), (1, LANES // HEAD_DIM)))


def _layernorm(z, g, b):
    z = z - jnp.mean(z, axis=-1, keepdims=True)
    return z * lax.rsqrt(jnp.mean(z * z, axis=-1, keepdims=True) + LN_EPS) * g + b


def _outproj_body(ya_ref, yb_ref, yc_ref, hf_ref, hb_ref, og_ref, ng_ref, h_ref, gl_ref, gc_ref, w_ref, g_ref,
                  b_ref, o_ref, *, ctx_tiles, alpha):
    is_ctx = pl.program_id(1) < ctx_tiles
    gate = jnp.where(is_ctx, gc_ref[0], gl_ref[0])
    mean_block = _head_block(1.0 / HEAD_DIM).astype(BF16)
    yd = hf_ref[0] + hb_ref[0]
    yd = yd - _head_sum(yd, mean_block)
    yd = yd * lax.rsqrt(_head_sum(yd * yd, mean_block) + LN_EPS) * ng_ref[...] * jax.nn.sigmoid(og_ref[0])
    y = None
    for m, ym in enumerate((ya_ref[0], yb_ref[0], yc_ref[0], yd)):
        part = jnp.dot(ym.astype(BF16), w_ref[m * GROUP_WIDTH:(m + 1) * GROUP_WIDTH, :],
                       preferred_element_type=F32)
        y = part if y is None else y + part
    o_ref[0] = _layernorm(alpha * h_ref[0] + gate * y, g_ref[...], b_ref[...])


def _outproj(ya, yb, yc, hf, hb, p_mlstm, mlstm_norm_g, h, gate_l, gate_c, w, ln_g, ln_b, tile0, ctx_tiles, alpha):
    b, t, d = h.shape
    gw = GROUP_WIDTH
    n_tiles = t // ROW_TILE - tile0
    yrow = lambda c: pl.BlockSpec((1, ROW_TILE, gw), lambda bi, i: (bi, tile0 + i, c))
    yb_row = pl.BlockSpec((1, ROW_TILE, gw), lambda bi, i: (bi, i, 0))
    hrow = pl.BlockSpec((1, ROW_TILE, d), lambda bi, i: (bi, tile0 + i, 0))
    lat = pl.BlockSpec((1, 1, d), lambda bi, i: (bi, 0, 0))
    ctx = pl.BlockSpec((1, 1, d), lambda bi, i: (0, 0, 0))
    vec = lambda n: pl.BlockSpec((1, n), lambda bi, i: (0, 0))
    return pl.pallas_call(
        functools.partial(_outproj_body, ctx_tiles=ctx_tiles, alpha=alpha),
        grid=(b, n_tiles),
        in_specs=[yrow(0), yb_row, yrow(0), yrow(0), yrow(0), yrow(3), vec(gw), hrow, lat, ctx,
                  pl.BlockSpec(w.shape, lambda bi, i: (0, 0)), vec(d), vec(d)],
        out_specs=pl.BlockSpec((1, ROW_TILE, d), lambda bi, i: (bi, i, 0)),
        out_shape=jax.ShapeDtypeStruct((b, n_tiles * ROW_TILE, d), F32),
        compiler_params=_cparams(("parallel", "parallel")),
        name="outproj_ln",
    )(ya, yb, yc, hf, hb, p_mlstm, mlstm_norm_g.reshape(1, gw), h, gate_l, gate_c, w, ln_g.reshape(1, d),
      ln_b.reshape(1, d))


FFN_CHUNKS = 1


def _ffn_body(x_ref, xp_ref, xn_ref, sl_ref, cl_ref, gl_ref, sc_ref, cc_ref, gc_ref,
              wu_ref, cw_ref, cb_ref, wd_ref, g_ref, b_ref, o_ref, *, ctx_tiles, n_tiles, alpha, d_ff):
    i = pl.program_id(1)
    is_ctx = i < ctx_tiles
    shift = jnp.where(is_ctx, sc_ref[0], sl_ref[0])
    scale = jnp.where(is_ctx, cc_ref[0], cl_ref[0])
    gate_mod = jnp.where(is_ctx, gc_ref[0], gl_ref[0])
    seg_start = jnp.logical_or(i == 0, i == ctx_tiles)
    seg_end = jnp.logical_or(i == ctx_tiles - 1, i == n_tiles - 1)

    x = x_ref[0]
    xm = (x * (1.0 + scale) + shift).astype(BF16)
    xp = (xp_ref[0] * (1.0 + scale) + shift).astype(BF16)
    xn = (xn_ref[0] * (1.0 + scale) + shift).astype(BF16)
    rows = x.shape[0]
    row_id = lax.broadcasted_iota(jnp.int32, (rows, 1), 0)
    ch = d_ff // FFN_CHUNKS
    f = jnp.zeros(x.shape, F32)
    for c in range(FFN_CHUNKS):
        wg = wu_ref[:, c * ch:(c + 1) * ch]
        wv = wu_ref[:, d_ff + c * ch:d_ff + (c + 1) * ch]
        gate = jnp.dot(xm, wg, preferred_element_type=F32)
        val = jnp.dot(xm, wv, preferred_element_type=F32)
        gp = jnp.dot(xp, wg, preferred_element_type=F32)[SUBLANES - 1:SUBLANES, :]
        gn = jnp.dot(xn, wg, preferred_element_type=F32)[0:1, :]
        gp = jnp.where(seg_start, 0.0, gp)
        gn = jnp.where(seg_end, 0.0, gn)
        prev = jnp.where(row_id == 0, gp, pltpu.roll(gate, 1, 0))
        nxt = jnp.where(row_id == rows - 1, gn, pltpu.roll(gate, rows - 1, 0))
        cw = cw_ref[:, c * ch:(c + 1) * ch]
        conv = prev * cw[0:1, :] + gate * cw[1:2, :] + nxt * cw[2:3, :] + cb_ref[:, c * ch:(c + 1) * ch]
        act = (jax.nn.gelu(conv) * val).astype(BF16)
        f = f + jnp.dot(act, wd_ref[c * ch:(c + 1) * ch, :], preferred_element_type=F32)
    o_ref[0] = _layernorm(alpha * x + gate_mod * f, g_ref[...], b_ref[...])


def _ffn(h, mods_l, mods_c, w_up, conv_w, conv_b, w_down, ln_g, ln_b, ctx_tiles, alpha):
    b, t, d = h.shape
    d_ff = w_down.shape[0]
    tile = 2 * ROW_TILE if ctx_tiles == 0 and t % (2 * ROW_TILE) == 0 else ROW_TILE
    n_tiles = t // tile
    sub = tile // SUBLANES
    n_sub = t // SUBLANES
    row = pl.BlockSpec((1, tile, d), lambda bi, i: (bi, i, 0))
    prev = pl.BlockSpec((1, SUBLANES, d), lambda bi, i: (bi, jnp.maximum(i * sub - 1, 0), 0))
    nxt = pl.BlockSpec((1, SUBLANES, d), lambda bi, i: (bi, jnp.minimum((i + 1) * sub, n_sub - 1), 0))
    lat = pl.BlockSpec((1, 1, d), lambda bi, i: (bi, 0, 0))
    ctx = pl.BlockSpec((1, 1, d), lambda bi, i: (0, 0, 0))
    vec = pl.BlockSpec((1, d), lambda bi, i: (0, 0))
    full = lambda a: pl.BlockSpec(a.shape, lambda bi, i: (0,) * a.ndim)
    cb = conv_b.reshape(1, d_ff)
    return pl.pallas_call(
        functools.partial(_ffn_body, ctx_tiles=ctx_tiles, n_tiles=n_tiles, alpha=alpha, d_ff=d_ff),
        grid=(b, n_tiles),
        in_specs=[row, prev, nxt, lat, lat, lat, ctx, ctx, ctx,
                  full(w_up), full(conv_w), full(cb), full(w_down), vec, vec],
        out_specs=row,
        out_shape=jax.ShapeDtypeStruct((b, t, d), F32),
        compiler_params=_cparams(("parallel", "parallel")),
        name="ffn",
    )(h, h, h, *mods_l, *mods_c, w_up, conv_w, cb, w_down, ln_g.reshape(1, d), ln_b.reshape(1, d))


def _rwkv7(f, tc, lnx_g, lnx_b):
    b = f.shape[0]
    col = RWKV_FEATS.index
    y = _scan("rwkv",
              [_to_scan(f, col("r"), col("r"), "k", tc), _to_scan(f, col("w0"), col("w1"), "k", tc),
               _to_scan(f, col("k0"), col("k1"), "k", tc), _to_scan(f, col("a"), col("a"), "k", tc),
               _to_scan(f, col("b0"), col("b1"), "k", tc)],
              _to_scan(f, col("v"), col("v"), "v", tc))
    return _from_scan(y, b, tc, "rwkv", f, (col("bonus"), col("g")), (lnx_g, lnx_b))


def _rope_tables(n_ctx, n_lat):
    rows = n_lat // GRID_W
    row = jnp.repeat(jnp.arange(rows, dtype=F32), GRID_W)
    col = jnp.tile(jnp.arange(GRID_W, dtype=F32), rows)
    n_freq = DIFF_QK_DIM // 4
    inv_freq = ROPE_BASE ** (-jnp.arange(n_freq, dtype=F32) / n_freq)
    ang = jnp.concatenate([row[:, None] * inv_freq, col[:, None] * inv_freq], axis=-1)
    ang = jnp.concatenate([jnp.zeros((n_ctx, ang.shape[1]), F32), ang], axis=0)
    half = DIFF_QK_DIM // 2
    lane = np.arange(GROUP_WIDTH)
    sign = np.where(lane % DIFF_QK_DIM < half, -1.0, 1.0).astype(np.float32)
    return jnp.cos(ang)[:, lane % half], jnp.sin(ang)[:, lane % half] * sign


def _diff_attn(q, k, v, tc, lam_vecs, norm_g, layer, need_ctx):
    lam_init = 0.8 - 0.6 * math.exp(-0.3 * layer)
    lv = lam_vecs.astype(F32)
    lam = jnp.exp(jnp.sum(lv[0] * lv[1])) - jnp.exp(jnp.sum(lv[2] * lv[3])) + lam_init
    q_tile0 = 0 if need_ctx else tc // ATTN_TILE
    return _attention(q, k, v, 0, lam, norm_g, 1.0 - lam_init, q_tile0, tc)


def _hgrn2(f, tc, norm_g):
    b = f.shape[0]
    col = HGRN_FEATS.index
    o = _scan("gla",
              [_to_scan(f, col("q"), col("q"), "k", tc), _to_scan(f, col("w0"), col("w1"), "k", tc),
               _to_scan(f, col("k0"), col("k1"), "k", tc)],
              _to_scan(f, col("v"), col("v"), "v", tc))
    return _from_scan(o, b, tc, "gla", f, (col("gate"),), (jnp.tile(norm_g, N_HEADS),))


RWKV_COLS = 3 * GROUP_WIDTH + 2 * RWKV_DECAY_LORA + 2 * RWKV_ICLR_LORA + RWKV_GATE_LORA
DIFF_COLS = 3 * GROUP_WIDTH
HGRN_COLS = 5 * GROUP_WIDTH
MLSTM_COLS = 4 * GROUP_WIDTH + 4 * N_HEADS
SLAB_COLS = (RWKV_COLS, DIFF_COLS, HGRN_COLS, MLSTM_COLS)
SLAB_PAD = (5 * GROUP_WIDTH, DIFF_COLS, HGRN_COLS, 4 * GROUP_WIDTH + LANES)


def _split_cols(w):
    parts, off = [], 0
    for n, n_pad in zip(SLAB_COLS, SLAB_PAD):
        parts.append(jnp.pad(w[:, off:off + n], ((0, 0), (0, n_pad - n))).astype(BF16))
        off += n
    return parts


def kernel(x, c, ctx, c_ctx, ada_w, ada_b, w_in, rwkv_mu, rwkv_w0, rwkv_w2, rwkv_a0, rwkv_a2, rwkv_g2, rwkv_k_k, rwkv_k_a, rwkv_r_k, rwkv_lnx_g, rwkv_lnx_b, diff_lambda, diff_norm_g, hgrn_lb_logits, hgrn_norm_g, mlstm_conv_w, mlstm_conv_b, mlstm_gate_b, mlstm_norm_g, w_out, ffn_w_up, ffn_conv_w, ffn_conv_b, ffn_w_down, ln_g, ln_b):
    depth = w_in.shape[0]
    b, seq, d = x.shape
    tc = ctx.shape[1]
    assert tc == ROW_TILE == ATTN_TILE and seq % ROW_TILE == 0 and 2 * b * N_HEADS <= LANES
    alpha = (2.0 * depth) ** 0.25

    lb_w = jax.nn.softmax(hgrn_lb_logits.astype(F32), axis=0)
    lower_bounds = jnp.cumsum(lb_w, axis=0) - lb_w[0]

    rope = _rope_tables(tc, seq)
    cond = jnp.zeros((8, d), F32).at[:b].set(c).at[b].set(c_ctx)
    h = jnp.concatenate([ctx, x], axis=1)
    for layer in range(depth):
        last = layer == depth - 1
        mod = _adaln(cond, ada_w, ada_b, layer).reshape(8, 6, d)
        mods_l = [mod[:b, j][:, None, :] for j in range(6)]
        mods_c = [mod[b:b + 1, j][:, None, :] for j in range(6)]

        rwkv_prm = (rwkv_mu[layer], rwkv_w0[layer], rwkv_w2[layer], rwkv_a0[layer], rwkv_a2[layer], rwkv_g2[layer],
                    rwkv_k_k[layer], rwkv_k_a[layer], rwkv_r_k[layer])
        f_rwkv, q, k, v, f_hgrn, p_mlstm = _inproj(h, mods_l[0], mods_l[1], mods_c[0], mods_c[1], rope,
                                                   _split_cols(w_in[layer]), rwkv_prm, lower_bounds[layer],
                                                   tc // ROW_TILE)
        y_a = _rwkv7(f_rwkv, tc, rwkv_lnx_g[layer], rwkv_lnx_b[layer])
        y_b = _diff_attn(q, k, v, tc, diff_lambda[layer], diff_norm_g[layer], layer, not last)
        y_c = _hgrn2(f_hgrn, tc, hgrn_norm_g[layer])
        h_f, h_b = (_mlstm_chunked(p_mlstm, mlstm_conv_w[layer], mlstm_conv_b[layer], mlstm_gate_b[layer], rev, tc)
                    for rev in (False, True))

        tile0, ctx_tiles = (tc // ROW_TILE, 0) if last else (0, tc // ROW_TILE)
        h = _outproj(y_a, y_b, y_c, h_f, h_b, p_mlstm, mlstm_norm_g[layer], h, mods_l[2], mods_c[2],
                     w_out[layer].astype(BF16), ln_g[layer, 0], ln_b[layer, 0], tile0, ctx_tiles, alpha)
        h = _ffn(h, mods_l[3:6], mods_c[3:6], ffn_w_up[layer].astype(BF16), ffn_conv_w[layer], ffn_conv_b[layer],
                 ffn_w_down[layer].astype(BF16), ln_g[layer, 1], ln_b[layer, 1], ctx_tiles, alpha)
    return h
```

```python
import functools
import math

import jax
import jax.numpy as jnp
import numpy as np
from jax import lax
from jax.experimental import pallas as pl
from jax.experimental.pallas import tpu as pltpu

F32 = jnp.float32
BF16 = jnp.bfloat16

HEAD_DIM = 64
N_HEADS = 4
GROUP_WIDTH = N_HEADS * HEAD_DIM
DIFF_QK_DIM = HEAD_DIM // 2
GRID_W = 64
ROPE_BASE = 10000.0
RWKV_DECAY_LORA = 64
RWKV_ICLR_LORA = 64
RWKV_GATE_LORA = 160
RWKV_LNX_EPS = 64e-5
LN_EPS = 1e-5
LB_FLOOR = 1e-30
MAX_NEG_LOG_STAB = 60.0

LANES = 128
SUBLANES = 8
ADALN_TILE = 1536
ROW_TILE = 256
ATTN_TILE = 256
TIME_BLOCK = 128
TO_SCAN_BLOCK = 256
SCAN_TB = 128
VMEM_LIMIT = 56 * 1024 * 1024


def _cparams(sem):
    return pltpu.CompilerParams(dimension_semantics=sem, vmem_limit_bytes=VMEM_LIMIT)


def _rev_block(i, n_ctx, n_all):
    return jnp.where(i < n_ctx, n_ctx - 1 - i, n_ctx + n_all - 1 - i)


def _split3(x):
    hi = x.astype(BF16)
    r1 = x - hi.astype(F32)
    mid = r1.astype(BF16)
    return hi, mid, (r1 - mid.astype(F32)).astype(BF16)


def _dot01_left(a01, x):
    a = a01.astype(BF16)
    p = [jnp.dot(a, part, preferred_element_type=F32) for part in _split3(x)]
    return (p[0] + p[1]) + p[2]


def _dot01_right(x, b01):
    b = b01.astype(BF16)
    p = [jnp.dot(part, b, preferred_element_type=F32) for part in _split3(x)]
    return (p[0] + p[1]) + p[2]


def _flip_rows(x):
    n = x.shape[0]
    r = lax.broadcasted_iota(jnp.int32, (n, n), 0)
    c = lax.broadcasted_iota(jnp.int32, (n, n), 1)
    return _dot01_left(jnp.where(r + c == n - 1, 1.0, 0.0).astype(F32), x)


def _adaln_body(c_ref, w_ref, b_ref, o_ref):
    x = c_ref[...]
    x = (x * jax.nn.sigmoid(x)).astype(BF16)
    o_ref[...] = jnp.dot(x, w_ref[0].astype(BF16), preferred_element_type=F32) + b_ref[0]


def _adaln(cond, w, b, layer):
    m, d = cond.shape
    n = w.shape[2]
    tn = ADALN_TILE
    return pl.pallas_call(
        _adaln_body,
        grid=(n // tn,),
        in_specs=[pl.BlockSpec((m, d), lambda j: (0, 0)),
                  pl.BlockSpec((1, d, tn), lambda j: (layer, 0, j)),
                  pl.BlockSpec((1, 1, tn), lambda j: (layer, 0, j))],
        out_specs=pl.BlockSpec((m, tn), lambda j: (0, j)),
        out_shape=jax.ShapeDtypeStruct((m, n), F32),
        compiler_params=_cparams(("arbitrary",)),
        name="adaln",
    )(cond, w, b.reshape(b.shape[0], 1, n))


def _softplus(x):
    return jnp.maximum(x, 0.0) + jnp.log(1.0 + jnp.exp(-jnp.abs(x)))


def _head_block(scale=1.0):
    r = lax.broadcasted_iota(jnp.int32, (GROUP_WIDTH, GROUP_WIDTH), 0) // HEAD_DIM
    c = lax.broadcasted_iota(jnp.int32, (GROUP_WIDTH, GROUP_WIDTH), 1) // HEAD_DIM
    return jnp.where(r == c, scale, 0.0).astype(F32)


def _head_sum(x, block_bf16):
    hi = x.astype(BF16)
    lo = (x - hi.astype(F32)).astype(BF16)
    return (jnp.dot(hi, block_bf16, preferred_element_type=F32)
            + jnp.dot(lo, block_bf16, preferred_element_type=F32))


def _shifted_rows(x, prev8, next8, seg_start, seg_end):
    rows = x.shape[0]
    row_id = lax.broadcasted_iota(jnp.int32, (rows, 1), 0)
    prev_row = jnp.where(seg_start, 0.0, prev8[SUBLANES - 1:SUBLANES, :])
    next_row = jnp.where(seg_end, 0.0, next8[0:1, :])
    prev = jnp.where(row_id == 0, prev_row, pltpu.roll(x, 1, 0))
    nxt = jnp.where(row_id == rows - 1, next_row, pltpu.roll(x, rows - 1, 0))
    return prev, nxt


RWKV_FEATS = ("r", "v", "w0", "w1", "k0", "k1", "a", "b0", "b1", "g", "bonus")


def _rwkv_feats(p, prev8, next8, seg_start, seg_end, mu_ref, w0_ref, w2_ref, a0_ref, a2_ref, g2_ref, kk_ref,
                ka_ref, rk_ref):
    gw = GROUP_WIDTH
    prev, nxt = _shifted_rows(p, prev8, next8, seg_start, seg_end)
    p = p + (0.5 * (prev + nxt) - p) * mu_ref[...]
    r, k, v = p[:, :gw], p[:, gw:2 * gw], p[:, 2 * gw:3 * gw]
    wd, ad, gd = p[:, 3 * gw:3 * gw + LANES], p[:, 3 * gw + LANES:4 * gw], p[:, 4 * gw:5 * gw]
    lora_w = jnp.dot(jnp.tanh(wd).astype(BF16), w2_ref[...], preferred_element_type=F32)
    lora_a = jnp.dot(ad.astype(BF16), a2_ref[...], preferred_element_type=F32)
    decay = jnp.exp(-jnp.exp(-_softplus(-(w0_ref[...] + lora_w)) - 0.5))
    a = jax.nn.sigmoid(a0_ref[...] + lora_a)
    g = jnp.dot(jax.nn.sigmoid(gd).astype(BF16), g2_ref[...], preferred_element_type=F32)
    block = _head_block().astype(BF16)
    kk = k * kk_ref[...]
    kk = kk / jnp.maximum(jnp.sqrt(_head_sum(kk * kk, block)), 1e-12)
    k0 = k * (1.0 + (a[:, :gw] - 1.0) * ka_ref[...])
    k1 = k * (1.0 + (a[:, gw:] - 1.0) * ka_ref[...])
    bonus = _head_sum(r * (k0 + k1) * rk_ref[...], block) * v
    return dict(r=r, v=v, w0=decay[:, :gw], w1=decay[:, gw:], k0=k0, k1=k1, a=-kk, b0=kk * a[:, :gw],
                b1=kk * a[:, gw:], g=g, bonus=bonus)


def _block_diag2(w):
    z = jnp.zeros_like(w[0])
    return jnp.concatenate([jnp.concatenate([w[0], z], axis=1), jnp.concatenate([z, w[1]], axis=1)], axis=0)


HGRN_FEATS = ("q", "w0", "w1", "k0", "k1", "gate", "v")


def _hgrn_feats(p, llb_ref, l1m_ref, oml_ref):
    gw = GROUP_WIDTH
    q, f_f, f_b, v, g = (p[:, j * gw:(j + 1) * gw] for j in range(5))
    feats = dict(q=q * jax.nn.sigmoid(q), gate=g * jax.nn.sigmoid(g), v=v)
    for d, f in enumerate((f_f, f_b)):
        x = llb_ref[...]
        y = l1m_ref[...] - _softplus(-f)
        log_f = jnp.maximum(x, y) + jnp.log(1.0 + jnp.exp(-jnp.abs(x - y)))
        feats["w%d" % d] = jnp.exp(log_f)
        feats["k%d" % d] = oml_ref[...] * jax.nn.sigmoid(-f)
    return feats


N_RWKV_PRM, N_HGRN_PRM = 9, 3


def _rope(x, cos, sin_signed):
    half = DIFF_QK_DIM // 2
    lane = lax.broadcasted_iota(jnp.int32, x.shape, 1)
    n = x.shape[1]
    partner = jnp.where(lane % DIFF_QK_DIM < half, pltpu.roll(x, n - half, 1), pltpu.roll(x, half, 1))
    return x * cos + partner * sin_signed


def _inproj_body(x_ref, xp_ref, xn_ref, sl_ref, cl_ref, sc_ref, cc_ref, cos_ref, sin_ref, wr_ref, wd_ref, wh_ref,
                 wm_ref, *refs, ctx_tiles, n_tiles):
    rwkv_prm, hgrn_prm = refs[:N_RWKV_PRM], refs[N_RWKV_PRM:N_RWKV_PRM + N_HGRN_PRM]
    fr_ref, q_ref, k_ref, v_ref, fh_ref, pm_ref = refs[N_RWKV_PRM + N_HGRN_PRM:]
    i = pl.program_id(1)
    is_ctx = i < ctx_tiles
    seg_start = jnp.logical_or(i == 0, i == ctx_tiles)
    seg_end = jnp.logical_or(i == ctx_tiles - 1, i == n_tiles - 1)
    shift = jnp.where(is_ctx, sc_ref[0], sl_ref[0])
    scale = jnp.where(is_ctx, cc_ref[0], cl_ref[0])
    mod = lambda x: (x * (1.0 + scale) + shift).astype(BF16)
    proj = lambda x, w_ref: jnp.dot(x, w_ref[...], preferred_element_type=F32)
    gw = GROUP_WIDTH
    xm = mod(x_ref[0])
    feats = _rwkv_feats(proj(xm, wr_ref), proj(mod(xp_ref[0]), wr_ref), proj(mod(xn_ref[0]), wr_ref),
                        seg_start, seg_end, *rwkv_prm)
    for j, name in enumerate(RWKV_FEATS):
        fr_ref[0, :, j * gw:(j + 1) * gw] = feats[name]
    pd = proj(xm, wd_ref)
    q_ref[0] = (_rope(pd[:, :gw], cos_ref[...], sin_ref[...])
                * (DIFF_QK_DIM ** -0.5 * math.log2(math.e))).astype(BF16)
    k_ref[0] = _rope(pd[:, gw:2 * gw], cos_ref[...], sin_ref[...]).astype(BF16)
    v_ref[0] = pd[:, 2 * gw:]
    feats = _hgrn_feats(proj(xm, wh_ref), *hgrn_prm)
    for j, name in enumerate(HGRN_FEATS):
        fh_ref[0, :, j * gw:(j + 1) * gw] = feats[name]
    pm_ref[0] = proj(xm, wm_ref)


def _inproj(h, shift_l, scale_l, shift_c, scale_c, rope, ws, rwkv_prm, lb, ctx_tiles):
    b, t, d = h.shape
    gw = GROUP_WIDTH
    n_tiles, sub, n_sub = t // ROW_TILE, ROW_TILE // SUBLANES, t // SUBLANES
    row = lambda n: pl.BlockSpec((1, ROW_TILE, n), lambda bi, i: (bi, i, 0))
    prev = pl.BlockSpec((1, SUBLANES, d), lambda bi, i: (bi, jnp.maximum(i * sub - 1, 0), 0))
    nxt = pl.BlockSpec((1, SUBLANES, d), lambda bi, i: (bi, jnp.minimum((i + 1) * sub, n_sub - 1), 0))
    lat = pl.BlockSpec((1, 1, d), lambda bi, i: (bi, 0, 0))
    ctx = pl.BlockSpec((1, 1, d), lambda bi, i: (0, 0, 0))
    full = lambda a: pl.BlockSpec(a.shape, lambda bi, i: (0,) * a.ndim)
    vec = lambda a: a.reshape(1, -1).astype(F32)
    mu, w0, w2, a0, a2, g2, k_k, k_a, r_k = rwkv_prm
    cols = ws[0].shape[1]
    prm = [jnp.pad(vec(mu), ((0, 0), (0, cols - mu.shape[0]))), vec(w0), _block_diag2(w2).astype(BF16), vec(a0),
           _block_diag2(a2).astype(BF16), jnp.pad(g2, ((0, gw - g2.shape[0]), (0, 0))).astype(BF16), vec(k_k),
           vec(k_a), vec(r_k),
           vec(jnp.log(jnp.maximum(lb, LB_FLOOR))), vec(jnp.log1p(-lb)), vec(1.0 - lb)]
    assert len(prm) == N_RWKV_PRM + N_HGRN_PRM
    outs = [(len(RWKV_FEATS) * gw, F32), (gw, BF16), (gw, BF16), (gw, F32), (len(HGRN_FEATS) * gw, F32),
            (ws[3].shape[1], F32)]
    tab = pl.BlockSpec((ROW_TILE, gw), lambda bi, i: (i, 0))
    return pl.pallas_call(
        functools.partial(_inproj_body, ctx_tiles=ctx_tiles, n_tiles=n_tiles),
        grid=(b, n_tiles),
        in_specs=[row(d), prev, nxt, lat, lat, ctx, ctx, tab, tab] + [full(a) for a in list(ws) + prm],
        out_specs=[row(n) for n, _ in outs],
        out_shape=[jax.ShapeDtypeStruct((b, t, n), dt) for n, dt in outs],
        compiler_params=_cparams(("parallel", "parallel")),
        name="inproj",
    )(h, h, h, shift_l, scale_l, shift_c, scale_c, *rope, *ws, *prm)


def _to_scan_body(x0_ref, x1_ref, o_ref, r_ref, *, kind, nb):
    n_scan = 2 * nb * N_HEADS
    rep = LANES // n_scan
    n_piece = x0_ref.shape[1] // LANES
    for b in range(nb):
        for d, xt in enumerate((x0_ref[b].T, _flip_rows(x1_ref[b]).T)):
            for pc in range(n_piece):
                r_ref[pc, pl.ds((d * nb + b) * GROUP_WIDTH, GROUP_WIDTH), :] = xt[:, pc * LANES:(pc + 1) * LANES]
    for pc in range(n_piece):
        t_sl = slice(pc * LANES, (pc + 1) * LANES)
        if kind == "k":
            for k in range(HEAD_DIM):
                rows = r_ref[pc, pl.ds(k, n_scan, stride=HEAD_DIM), :]
                o_ref[k, t_sl, :] = jnp.concatenate([rows] * rep, axis=0).T
        else:
            for vh in range(HEAD_DIM // rep):
                rows = [r_ref[pc, pl.ds(vh * rep + vl, n_scan, stride=HEAD_DIM), :] for vl in range(rep)]
                o_ref[t_sl, vh, :] = jnp.concatenate(rows, axis=0).T


def _to_scan(x, col0, col1, kind, tc):
    nb, t, _ = x.shape
    gw = GROUP_WIDTH
    x0 = x1 = x
    blk = TO_SCAN_BLOCK if kind == "k" else TIME_BLOCK
    n_all, n_ctx = t // blk, tc // blk
    n_scan = 2 * nb * N_HEADS
    rep = LANES // n_scan
    fwd = pl.BlockSpec((nb, blk, gw), lambda i: (0, i, col0))
    bwd = pl.BlockSpec((nb, blk, gw), lambda i: (0, _rev_block(i, n_ctx, n_all), col1))
    if kind == "k":
        out_shape = (HEAD_DIM, t, LANES)
        out_spec = pl.BlockSpec((HEAD_DIM, blk, LANES), lambda i: (0, i, 0))
    else:
        out_shape = (t, HEAD_DIM // rep, LANES)
        out_spec = pl.BlockSpec((blk, HEAD_DIM // rep, LANES), lambda i: (i, 0, 0))
    return pl.pallas_call(
        functools.partial(_to_scan_body, kind=kind, nb=nb),
        grid=(n_all,),
        in_specs=[fwd, bwd],
        out_specs=out_spec,
        out_shape=jax.ShapeDtypeStruct(out_shape, F32),
        scratch_shapes=[pltpu.VMEM((blk // LANES, n_scan * HEAD_DIM, LANES), F32)],
        compiler_params=_cparams(("parallel",)),
        name="to_scan_" + kind,
    )(x0, x1)


def _from_scan_body(yf_ref, yb_ref, *rest, nb, readout):
    if readout == "rwkv":
        bonus_ref, g_ref, ng_ref, nb_ref, o_ref, r_ref = rest
    else:
        gate_ref, ng_ref, o_ref, r_ref = rest
    n_scan = 2 * nb * N_HEADS
    rep = LANES // n_scan
    for vh in range(HEAD_DIM // rep):
        r_ref[0, pl.ds(vh * LANES, LANES), :] = yf_ref[vh].T
        r_ref[1, pl.ds(vh * LANES, LANES), :] = yb_ref[vh].T
    mean_block = _head_block(1.0 / HEAD_DIM).astype(BF16)
    for b in range(nb):
        slab = lambda d: jnp.concatenate(
            [r_ref[d, pl.ds((d * nb + b) * N_HEADS + h, HEAD_DIM, stride=n_scan), :] for h in range(N_HEADS)],
            axis=0).T
        y = slab(0) + _flip_rows(slab(1))
        if readout == "rwkv":
            y = y - _head_sum(y, mean_block)
            y = y * lax.rsqrt(_head_sum(y * y, mean_block) + RWKV_LNX_EPS) * ng_ref[...] + nb_ref[...]
            o_ref[b] = (y + bonus_ref[b]) * g_ref[b]
        else:
            y = y * lax.rsqrt(_head_sum(y * y, mean_block) + LN_EPS) * ng_ref[...]
            o_ref[b] = y * gate_ref[b]


def _from_scan(y, nb, tc, readout, feats, cols, vecs):
    n_vh, t, _ = y.shape
    gw = GROUP_WIDTH
    n_all, n_ctx = t // TIME_BLOCK, tc // TIME_BLOCK
    fwd = pl.BlockSpec((n_vh, TIME_BLOCK, LANES), lambda i: (0, i, 0))
    bwd = pl.BlockSpec((n_vh, TIME_BLOCK, LANES), lambda i: (0, _rev_block(i, n_ctx, n_all), 0))
    tok = lambda c: pl.BlockSpec((nb, TIME_BLOCK, gw), lambda i: (0, i, c))
    vec = pl.BlockSpec((1, gw), lambda i: (0, 0))
    return pl.pallas_call(
        functools.partial(_from_scan_body, nb=nb, readout=readout),
        grid=(n_all,),
        in_specs=[fwd, bwd] + [tok(c) for c in cols] + [vec] * len(vecs),
        out_specs=tok(0),
        out_shape=jax.ShapeDtypeStruct((nb, t, gw), F32),
        scratch_shapes=[pltpu.VMEM((2, n_vh * LANES, TIME_BLOCK), F32)],
        compiler_params=_cparams(("parallel",)),
        name="from_scan_" + readout,
    )(y, y, *([feats] * len(cols)), *[v.reshape(1, gw) for v in vecs])


N_ACC = 4


def _acc_add(acc, i, x):
    acc[i % N_ACC] = x if acc[i % N_ACC] is None else acc[i % N_ACC] + x


def _acc_total(acc):
    return (acc[0] + acc[1]) + (acc[2] + acc[3])


def _scan_body(*refs, mode, tb, n_vh):
    if mode == "rwkv":
        r_ref, w_ref, k_ref, a_ref, b_ref, v_ref, y_ref, s_ref = refs
    else:
        r_ref, w_ref, k_ref, v_ref, y_ref, s_ref = refs

    @pl.when(pl.program_id(0) == 0)
    def _():
        s_ref[...] = jnp.zeros_like(s_ref)

    row = lambda ref, k, j: ref[k, pl.ds(j, 1), :]
    wide = lambda x: jnp.broadcast_to(x, (n_vh, LANES))

    def store_y(j, y):
        for vh in range(n_vh):
            y_ref[vh, pl.ds(j, 1), :] = y[vh:vh + 1, :]

    def rwkv_step(j, sa):
        v = v_ref[j]
        j_next = jnp.minimum(j + 1, tb - 1)
        ys, sas = [None] * N_ACC, [None] * N_ACC
        for k in range(HEAD_DIM):
            s = s_ref[k] * wide(row(w_ref, k, j)) + sa * wide(row(b_ref, k, j)) + v * wide(row(k_ref, k, j))
            s_ref[k] = s
            _acc_add(ys, k, s * wide(row(r_ref, k, j)))
            _acc_add(sas, k, s * wide(row(a_ref, k, j_next)))
        store_y(j, _acc_total(ys))
        return _acc_total(sas)

    def gla_step(j, carry):
        v = v_ref[j]
        ys = [None] * N_ACC
        for k in range(HEAD_DIM):
            s = s_ref[k] * wide(row(w_ref, k, j)) + v * wide(row(k_ref, k, j))
            s_ref[k] = s
            _acc_add(ys, k, s * wide(row(r_ref, k, j)))
        store_y(j, _acc_total(ys))
        return carry

    if mode == "rwkv":
        sa0 = [None] * N_ACC
        for k in range(HEAD_DIM):
            _acc_add(sa0, k, s_ref[k] * wide(row(a_ref, k, 0)))
        lax.fori_loop(0, tb, rwkv_step, _acc_total(sa0), unroll=8)
    else:
        lax.fori_loop(0, tb, gla_step, 0, unroll=8)


def _scan(mode, k_inputs, v_in):
    t, n_vh, _ = v_in.shape
    tb = SCAN_TB
    big = pl.BlockSpec((HEAD_DIM, tb, LANES), lambda i: (0, i, 0))
    small = pl.BlockSpec((tb, n_vh, LANES), lambda i: (i, 0, 0))
    return pl.pallas_call(
        functools.partial(_scan_body, mode=mode, tb=tb, n_vh=n_vh),
        grid=(t // tb,),
        in_specs=[big] * len(k_inputs) + [small],
        out_specs=pl.BlockSpec((n_vh, tb, LANES), lambda i: (0, i, 0)),
        out_shape=jax.ShapeDtypeStruct((n_vh, t, LANES), F32),
        scratch_shapes=[pltpu.VMEM((HEAD_DIM, n_vh, LANES), F32)],
        compiler_params=_cparams(("arbitrary",)),
        name="scan_" + mode,
    )(*k_inputs, v_in)


MLSTM_CHUNK = 64


def _prefix_max(x, reverse):
    n = x.shape[0]
    row = lax.broadcasted_iota(jnp.int32, x.shape, 0)
    sh = 1
    while sh < n:
        if reverse:
            x = jnp.where(row < n - sh, jnp.maximum(x, pltpu.roll(x, n - sh, 0)), x)
        else:
            x = jnp.where(row >= sh, jnp.maximum(x, pltpu.roll(x, sh, 0)), x)
        sh *= 2
    return x


def _mlstm_body(qk_ref, qkp_ref, qkn_ref, v_ref, gt_ref, cw_ref, cb_ref, sel_ref, gb_ref, o_ref, st_ref, m_ref, *,
                reverse, n_sub, n_ctx, n_blocks):
    c, gw = MLSTM_CHUNK, GROUP_WIDTH
    i = pl.program_id(1)

    @pl.when(i == 0)
    def _():
        st_ref[...] = jnp.zeros_like(st_ref)
        m_ref[...] = jnp.zeros_like(m_ref)

    blk = _rev_block(i, n_ctx, n_blocks) if reverse else i
    seg_start = jnp.logical_or(blk == 0, blk == n_ctx)
    seg_end = jnp.logical_or(blk == n_ctx - 1, blk == n_blocks - 1)
    qk = qk_ref[0]
    prev, nxt = _shifted_rows(qk, qkp_ref[0], qkn_ref[0], seg_start, seg_end)
    qk = prev * cw_ref[0:1, :] + qk * cw_ref[1:2, :] + nxt * cw_ref[2:3, :] + cb_ref[...]
    qk = qk * jax.nn.sigmoid(qk)
    q_all, k_all = qk[:, :gw], qk[:, gw:] * HEAD_DIM ** -0.5
    v_all = v_ref[0]
    gt = gt_ref[0]
    ig_all = _dot01_right(gt, sel_ref[0]) + gb_ref[0:1, :]
    fg_all = -_softplus(-(_dot01_right(gt, sel_ref[1]) + gb_ref[1:2, :]))

    tt = lax.broadcasted_iota(jnp.int32, (c, c), 0)
    ss = lax.broadcasted_iota(jnp.int32, (c, c), 1)
    tri = jnp.where((ss >= tt) if reverse else (ss <= tt), 1.0, 0.0).astype(F32)
    row = lax.broadcasted_iota(jnp.int32, (c, gw), 0)
    s_of_lane = lax.broadcasted_iota(jnp.int32, (c, gw), 1) % c
    causal = (s_of_lane >= row) if reverse else (s_of_lane <= row)
    diag = s_of_lane == row
    block = _head_block()
    block2 = jnp.concatenate([block, block], axis=1)
    ones8 = jnp.ones((SUBLANES, c), F32)
    nt = (((1,), (1,)), ((), ()))
    last = 0 if reverse else c - 1
    for u in (range(n_sub - 1, -1, -1) if reverse else range(n_sub)):
        sl = slice(u * c, (u + 1) * c)
        q, k, v, ig, fg = q_all[sl], k_all[sl], v_all[sl], ig_all[sl], fg_all[sl]
        b = _dot01_left(tri, fg)
        g = ig - b
        m_prev = m_ref[0:1, :]
        m_t = b + jnp.maximum(m_prev, _prefix_max(g, reverse))
        w_inter = jnp.exp(b + m_prev - m_t)
        g_row = _dot01_left(ones8, jnp.where(diag, g, 0.0))[0:1]
        qb = q.astype(BF16)
        k_bd = (jnp.concatenate([k] * N_HEADS, axis=0) * block).astype(BF16)
        scores = lax.dot_general(qb, k_bd, nt, preferred_element_type=F32)
        w = jnp.where(causal, jnp.exp((b - m_t) + g_row), 0.0) * scores
        v_bd = jnp.concatenate([jnp.concatenate([v] * N_HEADS, axis=0) * block, block], axis=1).astype(BF16)
        intra = jnp.dot(w.astype(BF16), v_bd, preferred_element_type=F32)
        inter = jnp.dot(qb, st_ref[...].astype(BF16), preferred_element_type=F32)
        num = w_inter * inter[:, :gw] + intra[:, :gw]
        den = w_inter * inter[:, gw:] + intra[:, gw:]
        floor = jnp.exp(jnp.minimum(-m_t, MAX_NEG_LOG_STAB))
        o_ref[0, pl.ds(u * c, c), :] = num / jnp.maximum(jnp.abs(den), floor)
        m_new = m_t[last:last + 1]
        b_end = b[last:last + 1]
        kw = k * jnp.exp(b_end - b + ig - m_new)
        decay = jnp.exp(b_end + m_prev - m_new)
        v_one = jnp.concatenate([v, jnp.ones_like(v)], axis=1).astype(BF16)
        upd = jnp.dot(kw.T.astype(BF16), v_one, preferred_element_type=F32)
        st_ref[...] = st_ref[...] * jnp.concatenate([decay, decay], axis=1) + upd * block2
        m_ref[0:1, :] = m_new


def _mlstm_chunked(p, conv_w, conv_b, gate_b, reverse, tc):
    b, t, _ = p.shape
    gw = GROUP_WIDTH
    n_blocks, n_ctx = t // ROW_TILE, tc // ROW_TILE
    sub, n_sub8 = ROW_TILE // SUBLANES, t // SUBLANES
    blk = (lambda i: _rev_block(i, n_ctx, n_blocks)) if reverse else (lambda i: i)
    d = 1 if reverse else 0
    lane_head = np.arange(gw) // HEAD_DIM
    sel = np.zeros((2, LANES, gw), np.float32)
    for j, base in enumerate((d * N_HEADS, (2 + d) * N_HEADS)):
        sel[j, base + lane_head, np.arange(gw)] = 1.0
    gate_bias = jnp.stack([jnp.repeat(gate_b[d], HEAD_DIM), jnp.repeat(gate_b[2 + d], HEAD_DIM)], axis=0)
    full = lambda a: pl.BlockSpec(a.shape, lambda bi, i: (0,) * a.ndim)
    cb = conv_b.reshape(1, 2 * gw)
    return pl.pallas_call(
        functools.partial(_mlstm_body, reverse=reverse, n_sub=ROW_TILE // MLSTM_CHUNK, n_ctx=n_ctx,
                          n_blocks=n_blocks),
        grid=(b, n_blocks),
        in_specs=[pl.BlockSpec((1, ROW_TILE, 2 * gw), lambda bi, i: (bi, blk(i), 0)),
                  pl.BlockSpec((1, SUBLANES, 2 * gw), lambda bi, i: (bi, jnp.maximum(blk(i) * sub - 1, 0), 0)),
                  pl.BlockSpec((1, SUBLANES, 2 * gw), lambda bi, i: (bi, jnp.minimum((blk(i) + 1) * sub, n_sub8 - 1), 0)),
                  pl.BlockSpec((1, ROW_TILE, gw), lambda bi, i: (bi, blk(i), 2)),
                  pl.BlockSpec((1, ROW_TILE, LANES), lambda bi, i: (bi, blk(i), 4 * gw // LANES)),
                  full(conv_w), full(cb), full(sel), full(gate_bias)],
        out_specs=pl.BlockSpec((1, ROW_TILE, gw), lambda bi, i: (bi, blk(i), 0)),
        out_shape=jax.ShapeDtypeStruct((b, t, gw), F32),
        scratch_shapes=[pltpu.VMEM((gw, 2 * gw), F32), pltpu.VMEM((SUBLANES, gw), F32)],
        compiler_params=_cparams(("parallel", "arbitrary")),
        name="mlstm_chunk",
    )(p, p, p, p, p, conv_w, cb, jnp.asarray(sel), gate_bias)


def _attn_body(q_ref, k_ref, v_ref, lam_ref, g_ref, o_ref, *, out_scale, ctx_tiles, n_ctx_keys):
    def attend(n_keys):
        q = q_ref[0]
        k = k_ref[0, :n_keys, :]
        v = v_ref[0, :n_keys, :].astype(BF16)
        lane = lax.broadcasted_iota(jnp.int32, q.shape, 1)
        comp = lane // DIFF_QK_DIM
        nt = (((1,), (1,)), ((), ()))
        n_maps = LANES // DIFF_QK_DIM
        ss = [lax.dot_general(jnp.where(comp == c, q, 0.0).astype(BF16), k, nt, preferred_element_type=F32)
              for c in range(n_maps)]
        es = [jnp.exp2(s - jnp.max(s, axis=-1, keepdims=True)) for s in ss]
        ls = [jnp.sum(e, axis=-1, keepdims=True) for e in es]
        ys = []
        for hh in range(LANES // HEAD_DIM):
            e0, e1, l0, l1 = es[2 * hh], es[2 * hh + 1], ls[2 * hh], ls[2 * hh + 1]
            pr = (e0 - e1 * (lam_ref[0:1, 0:1] * l0 / l1)).astype(BF16)
            ys.append(jnp.dot(pr, v, preferred_element_type=F32) * (1.0 / l0))
        first = lane < HEAD_DIM
        y = jnp.where(first, ys[0], ys[1])
        ysq = y * y
        ms = jnp.where(first, jnp.sum(jnp.where(first, ysq, 0.0), axis=-1, keepdims=True),
                       jnp.sum(jnp.where(first, 0.0, ysq), axis=-1, keepdims=True)) * (1.0 / HEAD_DIM)
        o_ref[0] = y * lax.rsqrt(ms + LN_EPS) * g_ref[...] * out_scale

    if ctx_tiles == 0:
        attend(k_ref.shape[1])
    else:
        is_ctx = pl.program_id(2) < ctx_tiles
        pl.when(is_ctx)(lambda: attend(n_ctx_keys))
        pl.when(jnp.logical_not(is_ctx))(lambda: attend(k_ref.shape[1]))


def _attention(q, k, v, v_col0, lam, norm_g, out_scale, q_tile0, tc):
    b, t, gw = q.shape
    n_tiles = t // ATTN_TILE - q_tile0
    ctx_tiles = max(tc // ATTN_TILE - q_tile0, 0)
    qspec = pl.BlockSpec((1, ATTN_TILE, LANES), lambda bi, pi, i: (bi, q_tile0 + i, pi))
    kspec = pl.BlockSpec((1, t, LANES), lambda bi, pi, i: (bi, 0, pi))
    vspec = pl.BlockSpec((1, t, LANES), lambda bi, pi, i: (bi, 0, v_col0 + pi))
    vec = pl.BlockSpec((1, LANES), lambda bi, pi, i: (0, 0))
    return pl.pallas_call(
        functools.partial(_attn_body, out_scale=out_scale, ctx_tiles=ctx_tiles, n_ctx_keys=tc),
        grid=(b, gw // LANES, n_tiles),
        in_specs=[qspec, kspec, vspec, vec, vec],
        out_specs=pl.BlockSpec((1, ATTN_TILE, LANES), lambda bi, pi, i: (bi, i, pi)),
        out_shape=jax.ShapeDtypeStruct((b, n_tiles * ATTN_TILE, gw), F32),
        compiler_params=_cparams(("parallel", "parallel", "parallel")),
        name="diff_attn",
    )(q, k, v, jnp.full((1, LANES), lam, F32), jnp.tile(norm_g.reshape(1, HEAD_DIM), (1, LANES // HEAD_DIM)))


def _layernorm(z, g, b):
    z = z - jnp.mean(z, axis=-1, keepdims=True)
    return z * lax.rsqrt(jnp.mean(z * z, axis=-1, keepdims=True) + LN_EPS) * g + b


def _outproj_body(ya_ref, yb_ref, yc_ref, hf_ref, hb_ref, og_ref, ng_ref, h_ref, gl_ref, gc_ref, w_ref, g_ref,
                  b_ref, o_ref, *, ctx_tiles, alpha):
    is_ctx = pl.program_id(1) < ctx_tiles
    gate = jnp.where(is_ctx, gc_ref[0], gl_ref[0])
    mean_block = _head_block(1.0 / HEAD_DIM).astype(BF16)
    yd = hf_ref[0] + hb_ref[0]
    yd = yd - _head_sum(yd, mean_block)
    yd = yd * lax.rsqrt(_head_sum(yd * yd, mean_block) + LN_EPS) * ng_ref[...] * jax.nn.sigmoid(og_ref[0])
    y = None
    for m, ym in enumerate((ya_ref[0], yb_ref[0], yc_ref[0], yd)):
        part = jnp.dot(ym.astype(BF16), w_ref[m * GROUP_WIDTH:(m + 1) * GROUP_WIDTH, :],
                       preferred_element_type=F32)
        y = part if y is None else y + part
    o_ref[0] = _layernorm(alpha * h_ref[0] + gate * y, g_ref[...], b_ref[...])


def _outproj(ya, yb, yc, hf, hb, p_mlstm, mlstm_norm_g, h, gate_l, gate_c, w, ln_g, ln_b, tile0, ctx_tiles, alpha):
    b, t, d = h.shape
    gw = GROUP_WIDTH
    n_tiles = t // ROW_TILE - tile0
    yrow = lambda c: pl.BlockSpec((1, ROW_TILE, gw), lambda bi, i: (bi, tile0 + i, c))
    yb_row = pl.BlockSpec((1, ROW_TILE, gw), lambda bi, i: (bi, i, 0))
    hrow = pl.BlockSpec((1, ROW_TILE, d), lambda bi, i: (bi, tile0 + i, 0))
    lat = pl.BlockSpec((1, 1, d), lambda bi, i: (bi, 0, 0))
    ctx = pl.BlockSpec((1, 1, d), lambda bi, i: (0, 0, 0))
    vec = lambda n: pl.BlockSpec((1, n), lambda bi, i: (0, 0))
    return pl.pallas_call(
        functools.partial(_outproj_body, ctx_tiles=ctx_tiles, alpha=alpha),
        grid=(b, n_tiles),
        in_specs=[yrow(0), yb_row, yrow(0), yrow(0), yrow(0), yrow(3), vec(gw), hrow, lat, ctx,
                  pl.BlockSpec(w.shape, lambda bi, i: (0, 0)), vec(d), vec(d)],
        out_specs=pl.BlockSpec((1, ROW_TILE, d), lambda bi, i: (bi, i, 0)),
        out_shape=jax.ShapeDtypeStruct((b, n_tiles * ROW_TILE, d), F32),
        compiler_params=_cparams(("parallel", "parallel")),
        name="outproj_ln",
    )(ya, yb, yc, hf, hb, p_mlstm, mlstm_norm_g.reshape(1, gw), h, gate_l, gate_c, w, ln_g.reshape(1, d),
      ln_b.reshape(1, d))


FFN_CHUNKS = 1


def _ffn_body(x_ref, xp_ref, xn_ref, sl_ref, cl_ref, gl_ref, sc_ref, cc_ref, gc_ref,
              wu_ref, cw_ref, cb_ref, wd_ref, g_ref, b_ref, o_ref, *, ctx_tiles, n_tiles, alpha, d_ff):
    i = pl.program_id(1)
    is_ctx = i < ctx_tiles
    shift = jnp.where(is_ctx, sc_ref[0], sl_ref[0])
    scale = jnp.where(is_ctx, cc_ref[0], cl_ref[0])
    gate_mod = jnp.where(is_ctx, gc_ref[0], gl_ref[0])
    seg_start = jnp.logical_or(i == 0, i == ctx_tiles)
    seg_end = jnp.logical_or(i == ctx_tiles - 1, i == n_tiles - 1)

    x = x_ref[0]
    xm = (x * (1.0 + scale) + shift).astype(BF16)
    xp = (xp_ref[0] * (1.0 + scale) + shift).astype(BF16)
    xn = (xn_ref[0] * (1.0 + scale) + shift).astype(BF16)
    rows = x.shape[0]
    row_id = lax.broadcasted_iota(jnp.int32, (rows, 1), 0)
    ch = d_ff // FFN_CHUNKS
    f = jnp.zeros(x.shape, F32)
    for c in range(FFN_CHUNKS):
        wg = wu_ref[:, c * ch:(c + 1) * ch]
        wv = wu_ref[:, d_ff + c * ch:d_ff + (c + 1) * ch]
        gate = jnp.dot(xm, wg, preferred_element_type=F32)
        val = jnp.dot(xm, wv, preferred_element_type=F32)
        gp = jnp.dot(xp, wg, preferred_element_type=F32)[SUBLANES - 1:SUBLANES, :]
        gn = jnp.dot(xn, wg, preferred_element_type=F32)[0:1, :]
        gp = jnp.where(seg_start, 0.0, gp)
        gn = jnp.where(seg_end, 0.0, gn)
        prev = jnp.where(row_id == 0, gp, pltpu.roll(gate, 1, 0))
        nxt = jnp.where(row_id == rows - 1, gn, pltpu.roll(gate, rows - 1, 0))
        cw = cw_ref[:, c * ch:(c + 1) * ch]
        conv = prev * cw[0:1, :] + gate * cw[1:2, :] + nxt * cw[2:3, :] + cb_ref[:, c * ch:(c + 1) * ch]
        act = (jax.nn.gelu(conv) * val).astype(BF16)
        f = f + jnp.dot(act, wd_ref[c * ch:(c + 1) * ch, :], preferred_element_type=F32)
    o_ref[0] = _layernorm(alpha * x + gate_mod * f, g_ref[...], b_ref[...])


def _ffn(h, mods_l, mods_c, w_up, conv_w, conv_b, w_down, ln_g, ln_b, ctx_tiles, alpha):
    b, t, d = h.shape
    d_ff = w_down.shape[0]
    tile = 2 * ROW_TILE if ctx_tiles == 0 and t % (2 * ROW_TILE) == 0 else ROW_TILE
    n_tiles = t // tile
    sub = tile // SUBLANES
    n_sub = t // SUBLANES
    row = pl.BlockSpec((1, tile, d), lambda bi, i: (bi, i, 0))
    prev = pl.BlockSpec((1, SUBLANES, d), lambda bi, i: (bi, jnp.maximum(i * sub - 1, 0), 0))
    nxt = pl.BlockSpec((1, SUBLANES, d), lambda bi, i: (bi, jnp.minimum((i + 1) * sub, n_sub - 1), 0))
    lat = pl.BlockSpec((1, 1, d), lambda bi, i: (bi, 0, 0))
    ctx = pl.BlockSpec((1, 1, d), lambda bi, i: (0, 0, 0))
    vec = pl.BlockSpec((1, d), lambda bi, i: (0, 0))
    full = lambda a: pl.BlockSpec(a.shape, lambda bi, i: (0,) * a.ndim)
    cb = conv_b.reshape(1, d_ff)
    return pl.pallas_call(
        functools.partial(_ffn_body, ctx_tiles=ctx_tiles, n_tiles=n_tiles, alpha=alpha, d_ff=d_ff),
        grid=(b, n_tiles),
        in_specs=[row, prev, nxt, lat, lat, lat, ctx, ctx, ctx,
                  full(w_up), full(conv_w), full(cb), full(w_down), vec, vec],
        out_specs=row,
        out_shape=jax.ShapeDtypeStruct((b, t, d), F32),
        compiler_params=_cparams(("parallel", "parallel")),
        name="ffn",
    )(h, h, h, *mods_l, *mods_c, w_up, conv_w, cb, w_down, ln_g.reshape(1, d), ln_b.reshape(1, d))


def _rwkv7(f, tc, lnx_g, lnx_b):
    b = f.shape[0]
    col = RWKV_FEATS.index
    y = _scan("rwkv",
              [_to_scan(f, col("r"), col("r"), "k", tc), _to_scan(f, col("w0"), col("w1"), "k", tc),
               _to_scan(f, col("k0"), col("k1"), "k", tc), _to_scan(f, col("a"), col("a"), "k", tc),
               _to_scan(f, col("b0"), col("b1"), "k", tc)],
              _to_scan(f, col("v"), col("v"), "v", tc))
    return _from_scan(y, b, tc, "rwkv", f, (col("bonus"), col("g")), (lnx_g, lnx_b))


def _rope_tables(n_ctx, n_lat):
    rows = n_lat // GRID_W
    row = jnp.repeat(jnp.arange(rows, dtype=F32), GRID_W)
    col = jnp.tile(jnp.arange(GRID_W, dtype=F32), rows)
    n_freq = DIFF_QK_DIM // 4
    inv_freq = ROPE_BASE ** (-jnp.arange(n_freq, dtype=F32) / n_freq)
    ang = jnp.concatenate([row[:, None] * inv_freq, col[:, None] * inv_freq], axis=-1)
    ang = jnp.concatenate([jnp.zeros((n_ctx, ang.shape[1]), F32), ang], axis=0)
    half = DIFF_QK_DIM // 2
    lane = np.arange(GROUP_WIDTH)
    sign = np.where(lane % DIFF_QK_DIM < half, -1.0, 1.0).astype(np.float32)
    return jnp.cos(ang)[:, lane % half], jnp.sin(ang)[:, lane % half] * sign


def _diff_attn(q, k, v, tc, lam_vecs, norm_g, layer, need_ctx):
    lam_init = 0.8 - 0.6 * math.exp(-0.3 * layer)
    lv = lam_vecs.astype(F32)
    lam = jnp.exp(jnp.sum(lv[0] * lv[1])) - jnp.exp(jnp.sum(lv[2] * lv[3])) + lam_init
    q_tile0 = 0 if need_ctx else tc // ATTN_TILE
    return _attention(q, k, v, 0, lam, norm_g, 1.0 - lam_init, q_tile0, tc)


def _hgrn2(f, tc, norm_g):
    b = f.shape[0]
    col = HGRN_FEATS.index
    o = _scan("gla",
              [_to_scan(f, col("q"), col("q"), "k", tc), _to_scan(f, col("w0"), col("w1"), "k", tc),
               _to_scan(f, col("k0"), col("k1"), "k", tc)],
              _to_scan(f, col("v"), col("v"), "v", tc))
    return _from_scan(o, b, tc, "gla", f, (col("gate"),), (jnp.tile(norm_g, N_HEADS),))


RWKV_COLS = 3 * GROUP_WIDTH + 2 * RWKV_DECAY_LORA + 2 * RWKV_ICLR_LORA + RWKV_GATE_LORA
DIFF_COLS = 3 * GROUP_WIDTH
HGRN_COLS = 5 * GROUP_WIDTH
MLSTM_COLS = 4 * GROUP_WIDTH + 4 * N_HEADS
SLAB_COLS = (RWKV_COLS, DIFF_COLS, HGRN_COLS, MLSTM_COLS)
SLAB_PAD = (5 * GROUP_WIDTH, DIFF_COLS, HGRN_COLS, 4 * GROUP_WIDTH + LANES)


def _split_cols(w):
    parts, off = [], 0
    for n, n_pad in zip(SLAB_COLS, SLAB_PAD):
        parts.append(jnp.pad(w[:, off:off + n], ((0, 0), (0, n_pad - n))).astype(BF16))
        off += n
    return parts


def kernel(x, c, ctx, c_ctx, ada_w, ada_b, w_in, rwkv_mu, rwkv_w0, rwkv_w2, rwkv_a0, rwkv_a2, rwkv_g2, rwkv_k_k, rwkv_k_a, rwkv_r_k, rwkv_lnx_g, rwkv_lnx_b, diff_lambda, diff_norm_g, hgrn_lb_logits, hgrn_norm_g, mlstm_conv_w, mlstm_conv_b, mlstm_gate_b, mlstm_norm_g, w_out, ffn_w_up, ffn_conv_w, ffn_conv_b, ffn_w_down, ln_g, ln_b):
    depth = w_in.shape[0]
    b, seq, d = x.shape
    tc = ctx.shape[1]
    assert tc == ROW_TILE == ATTN_TILE and seq % ROW_TILE == 0 and 2 * b * N_HEADS <= LANES
    alpha = (2.0 * depth) ** 0.25

    lb_w = jax.nn.softmax(hgrn_lb_logits.astype(F32), axis=0)
    lower_bounds = jnp.cumsum(lb_w, axis=0) - lb_w[0]

    rope = _rope_tables(tc, seq)
    cond = jnp.zeros((8, d), F32).at[:b].set(c).at[b].set(c_ctx)
    h = jnp.concatenate([ctx, x], axis=1)
    for layer in range(depth):
        last = layer == depth - 1
        mod = _adaln(cond, ada_w, ada_b, layer).reshape(8, 6, d)
        mods_l = [mod[:b, j][:, None, :] for j in range(6)]
        mods_c = [mod[b:b + 1, j][:, None, :] for j in range(6)]

        rwkv_prm = (rwkv_mu[layer], rwkv_w0[layer], rwkv_w2[layer], rwkv_a0[layer], rwkv_a2[layer], rwkv_g2[layer],
                    rwkv_k_k[layer], rwkv_k_a[layer], rwkv_r_k[layer])
        f_rwkv, q, k, v, f_hgrn, p_mlstm = _inproj(h, mods_l[0], mods_l[1], mods_c[0], mods_c[1], rope,
                                                   _split_cols(w_in[layer]), rwkv_prm, lower_bounds[layer],
                                                   tc // ROW_TILE)
        y_a = _rwkv7(f_rwkv, tc, rwkv_lnx_g[layer], rwkv_lnx_b[layer])
        y_b = _diff_attn(q, k, v, tc, diff_lambda[layer], diff_norm_g[layer], layer, not last)
        y_c = _hgrn2(f_hgrn, tc, hgrn_norm_g[layer])
        h_f, h_b = (_mlstm_chunked(p_mlstm, mlstm_conv_w[layer], mlstm_conv_b[layer], mlstm_gate_b[layer], rev, tc)
                    for rev in (False, True))

        tile0, ctx_tiles = (tc // ROW_TILE, 0) if last else (0, tc // ROW_TILE)
        h = _outproj(y_a, y_b, y_c, h_f, h_b, p_mlstm, mlstm_norm_g[layer], h, mods_l[2], mods_c[2],
                     w_out[layer].astype(BF16), ln_g[layer, 0], ln_b[layer, 0], tile0, ctx_tiles, alpha)
        h = _ffn(h, mods_l[3:6], mods_c[3:6], ffn_w_up[layer].astype(BF16), ffn_conv_w[layer], ffn_conv_b[layer],
                 ffn_w_down[layer].astype(BF16), ln_g[layer, 1], ln_b[layer, 1], ctx_tiles, alpha)
    return h
```

```python
import functools
import math

import jax
import jax.numpy as jnp
import numpy as np
from jax import lax
from jax.experimental import pallas as pl
from jax.experimental.pallas import tpu as pltpu

F32 = jnp.float32
BF16 = jnp.bfloat16

HEAD_DIM = 64
N_HEADS = 4
GROUP_WIDTH = N_HEADS * HEAD_DIM
DIFF_QK_DIM = HEAD_DIM // 2
GRID_W = 64
ROPE_BASE = 10000.0
RWKV_DECAY_LORA = 64
RWKV_ICLR_LORA = 64
RWKV_GATE_LORA = 160
RWKV_LNX_EPS = 64e-5
LN_EPS = 1e-5
LB_FLOOR = 1e-30
MAX_NEG_LOG_STAB = 60.0

LANES = 128
SUBLANES = 8
ADALN_TILE = 1536
ROW_TILE = 256
ATTN_TILE = 256
TIME_BLOCK = 256
SCAN_TB = 128
VMEM_LIMIT = 56 * 1024 * 1024


def _cparams(sem):
    return pltpu.CompilerParams(dimension_semantics=sem, vmem_limit_bytes=VMEM_LIMIT)


def _rev_block(i, n_ctx, n_all):
    return jnp.where(i < n_ctx, n_ctx - 1 - i, n_ctx + n_all - 1 - i)


def _split3(x):
    hi = x.astype(BF16)
    r1 = x - hi.astype(F32)
    mid = r1.astype(BF16)
    return hi, mid, (r1 - mid.astype(F32)).astype(BF16)


def _dot01_left(a01, x):
    a = a01.astype(BF16)
    p = [jnp.dot(a, part, preferred_element_type=F32) for part in _split3(x)]
    return (p[0] + p[1]) + p[2]


def _dot01_right(x, b01):
    b = b01.astype(BF16)
    p = [jnp.dot(part, b, preferred_element_type=F32) for part in _split3(x)]
    return (p[0] + p[1]) + p[2]


def _flip_rows(x):
    n = x.shape[0]
    r = lax.broadcasted_iota(jnp.int32, (n, n), 0)
    c = lax.broadcasted_iota(jnp.int32, (n, n), 1)
    return _dot01_left(jnp.where(r + c == n - 1, 1.0, 0.0).astype(F32), x)


def _adaln_body(c_ref, w_ref, b_ref, o_ref):
    x = c_ref[...]
    x = (x * jax.nn.sigmoid(x)).astype(BF16)
    o_ref[...] = jnp.dot(x, w_ref[0].astype(BF16), preferred_element_type=F32) + b_ref[0]


def _adaln(cond, w, b, layer):
    m, d = cond.shape
    n = w.shape[2]
    tn = ADALN_TILE
    return pl.pallas_call(
        _adaln_body,
        grid=(n // tn,),
        in_specs=[pl.BlockSpec((m, d), lambda j: (0, 0)),
                  pl.BlockSpec((1, d, tn), lambda j: (layer, 0, j)),
                  pl.BlockSpec((1, 1, tn), lambda j: (layer, 0, j))],
        out_specs=pl.BlockSpec((m, tn), lambda j: (0, j)),
        out_shape=jax.ShapeDtypeStruct((m, n), F32),
        compiler_params=_cparams(("arbitrary",)),
        name="adaln",
    )(cond, w, b.reshape(b.shape[0], 1, n))


def _softplus(x):
    return jnp.maximum(x, 0.0) + jnp.log(1.0 + jnp.exp(-jnp.abs(x)))


def _head_block(scale=1.0):
    r = lax.broadcasted_iota(jnp.int32, (GROUP_WIDTH, GROUP_WIDTH), 0) // HEAD_DIM
    c = lax.broadcasted_iota(jnp.int32, (GROUP_WIDTH, GROUP_WIDTH), 1) // HEAD_DIM
    return jnp.where(r == c, scale, 0.0).astype(F32)


def _head_sum(x, block_bf16):
    hi = x.astype(BF16)
    lo = (x - hi.astype(F32)).astype(BF16)
    return (jnp.dot(hi, block_bf16, preferred_element_type=F32)
            + jnp.dot(lo, block_bf16, preferred_element_type=F32))


def _shifted_rows(x, prev8, next8, seg_start, seg_end):
    rows = x.shape[0]
    row_id = lax.broadcasted_iota(jnp.int32, (rows, 1), 0)
    prev_row = jnp.where(seg_start, 0.0, prev8[SUBLANES - 1:SUBLANES, :])
    next_row = jnp.where(seg_end, 0.0, next8[0:1, :])
    prev = jnp.where(row_id == 0, prev_row, pltpu.roll(x, 1, 0))
    nxt = jnp.where(row_id == rows - 1, next_row, pltpu.roll(x, rows - 1, 0))
    return prev, nxt


RWKV_FEATS = ("r", "v", "w0", "w1", "k0", "k1", "a", "b0", "b1", "g", "bonus")


def _rwkv_feats(p, prev8, next8, seg_start, seg_end, mu_ref, w0_ref, w2_ref, a0_ref, a2_ref, g2_ref, kk_ref,
                ka_ref, rk_ref):
    gw = GROUP_WIDTH
    prev, nxt = _shifted_rows(p, prev8, next8, seg_start, seg_end)
    p = p + (0.5 * (prev + nxt) - p) * mu_ref[...]
    r, k, v = p[:, :gw], p[:, gw:2 * gw], p[:, 2 * gw:3 * gw]
    wd, ad, gd = p[:, 3 * gw:3 * gw + LANES], p[:, 3 * gw + LANES:4 * gw], p[:, 4 * gw:5 * gw]
    lora_w = jnp.dot(jnp.tanh(wd).astype(BF16), w2_ref[...], preferred_element_type=F32)
    lora_a = jnp.dot(ad.astype(BF16), a2_ref[...], preferred_element_type=F32)
    decay = jnp.exp(-jnp.exp(-_softplus(-(w0_ref[...] + lora_w)) - 0.5))
    a = jax.nn.sigmoid(a0_ref[...] + lora_a)
    g = jnp.dot(jax.nn.sigmoid(gd).astype(BF16), g2_ref[...], preferred_element_type=F32)
    block = _head_block().astype(BF16)
    kk = k * kk_ref[...]
    kk = kk / jnp.maximum(jnp.sqrt(_head_sum(kk * kk, block)), 1e-12)
    k0 = k * (1.0 + (a[:, :gw] - 1.0) * ka_ref[...])
    k1 = k * (1.0 + (a[:, gw:] - 1.0) * ka_ref[...])
    bonus = _head_sum(r * (k0 + k1) * rk_ref[...], block) * v
    return dict(r=r, v=v, w0=decay[:, :gw], w1=decay[:, gw:], k0=k0, k1=k1, a=-kk, b0=kk * a[:, :gw],
                b1=kk * a[:, gw:], g=g, bonus=bonus)


def _block_diag2(w):
    z = jnp.zeros_like(w[0])
    return jnp.concatenate([jnp.concatenate([w[0], z], axis=1), jnp.concatenate([z, w[1]], axis=1)], axis=0)


HGRN_FEATS = ("q", "w0", "w1", "k0", "k1", "gate", "v")


def _hgrn_feats(p, llb_ref, l1m_ref, oml_ref):
    gw = GROUP_WIDTH
    q, f_f, f_b, v, g = (p[:, j * gw:(j + 1) * gw] for j in range(5))
    feats = dict(q=q * jax.nn.sigmoid(q), gate=g * jax.nn.sigmoid(g), v=v)
    for d, f in enumerate((f_f, f_b)):
        x = llb_ref[...]
        y = l1m_ref[...] - _softplus(-f)
        log_f = jnp.maximum(x, y) + jnp.log(1.0 + jnp.exp(-jnp.abs(x - y)))
        feats["w%d" % d] = jnp.exp(log_f)
        feats["k%d" % d] = oml_ref[...] * jax.nn.sigmoid(-f)
    return feats


N_RWKV_PRM, N_HGRN_PRM = 9, 3


def _rope(x, cos, sin_signed):
    half = DIFF_QK_DIM // 2
    lane = lax.broadcasted_iota(jnp.int32, x.shape, 1)
    n = x.shape[1]
    partner = jnp.where(lane % DIFF_QK_DIM < half, pltpu.roll(x, n - half, 1), pltpu.roll(x, half, 1))
    return x * cos + partner * sin_signed


def _inproj_body(x_ref, xp_ref, xn_ref, sl_ref, cl_ref, sc_ref, cc_ref, cos_ref, sin_ref, wr_ref, wd_ref, wh_ref,
                 wm_ref, *refs, ctx_tiles, n_tiles):
    rwkv_prm, hgrn_prm = refs[:N_RWKV_PRM], refs[N_RWKV_PRM:N_RWKV_PRM + N_HGRN_PRM]
    fr_ref, q_ref, k_ref, v_ref, fh_ref, pm_ref = refs[N_RWKV_PRM + N_HGRN_PRM:]
    i = pl.program_id(1)
    is_ctx = i < ctx_tiles
    seg_start = jnp.logical_or(i == 0, i == ctx_tiles)
    seg_end = jnp.logical_or(i == ctx_tiles - 1, i == n_tiles - 1)
    shift = jnp.where(is_ctx, sc_ref[0], sl_ref[0])
    scale = jnp.where(is_ctx, cc_ref[0], cl_ref[0])
    mod = lambda x: (x * (1.0 + scale) + shift).astype(BF16)
    proj = lambda x, w_ref: jnp.dot(x, w_ref[...], preferred_element_type=F32)
    gw = GROUP_WIDTH
    xm = mod(x_ref[0])
    feats = _rwkv_feats(proj(xm, wr_ref), proj(mod(xp_ref[0]), wr_ref), proj(mod(xn_ref[0]), wr_ref),
                        seg_start, seg_end, *rwkv_prm)
    for j, name in enumerate(RWKV_FEATS):
        fr_ref[0, :, j * gw:(j + 1) * gw] = feats[name]
    pd = proj(xm, wd_ref)
    q_ref[0] = (_rope(pd[:, :gw], cos_ref[...], sin_ref[...])
                * (DIFF_QK_DIM ** -0.5 * math.log2(math.e))).astype(BF16)
    k_ref[0] = _rope(pd[:, gw:2 * gw], cos_ref[...], sin_ref[...]).astype(BF16)
    v_ref[0] = pd[:, 2 * gw:]
    feats = _hgrn_feats(proj(xm, wh_ref), *hgrn_prm)
    for j, name in enumerate(HGRN_FEATS):
        fh_ref[0, :, j * gw:(j + 1) * gw] = feats[name]
    pm_ref[0] = proj(xm, wm_ref)


def _inproj(h, shift_l, scale_l, shift_c, scale_c, rope, ws, rwkv_prm, lb, ctx_tiles):
    b, t, d = h.shape
    gw = GROUP_WIDTH
    n_tiles, sub, n_sub = t // ROW_TILE, ROW_TILE // SUBLANES, t // SUBLANES
    row = lambda n: pl.BlockSpec((1, ROW_TILE, n), lambda bi, i: (bi, i, 0))
    prev = pl.BlockSpec((1, SUBLANES, d), lambda bi, i: (bi, jnp.maximum(i * sub - 1, 0), 0))
    nxt = pl.BlockSpec((1, SUBLANES, d), lambda bi, i: (bi, jnp.minimum((i + 1) * sub, n_sub - 1), 0))
    lat = pl.BlockSpec((1, 1, d), lambda bi, i: (bi, 0, 0))
    ctx = pl.BlockSpec((1, 1, d), lambda bi, i: (0, 0, 0))
    full = lambda a: pl.BlockSpec(a.shape, lambda bi, i: (0,) * a.ndim)
    vec = lambda a: a.reshape(1, -1).astype(F32)
    mu, w0, w2, a0, a2, g2, k_k, k_a, r_k = rwkv_prm
    cols = ws[0].shape[1]
    prm = [jnp.pad(vec(mu), ((0, 0), (0, cols - mu.shape[0]))), vec(w0), _block_diag2(w2).astype(BF16), vec(a0),
           _block_diag2(a2).astype(BF16), jnp.pad(g2, ((0, gw - g2.shape[0]), (0, 0))).astype(BF16), vec(k_k),
           vec(k_a), vec(r_k),
           vec(jnp.log(jnp.maximum(lb, LB_FLOOR))), vec(jnp.log1p(-lb)), vec(1.0 - lb)]
    assert len(prm) == N_RWKV_PRM + N_HGRN_PRM
    outs = [(len(RWKV_FEATS) * gw, F32), (gw, BF16), (gw, BF16), (gw, F32), (len(HGRN_FEATS) * gw, F32),
            (ws[3].shape[1], F32)]
    tab = pl.BlockSpec((ROW_TILE, gw), lambda bi, i: (i, 0))
    return pl.pallas_call(
        functools.partial(_inproj_body, ctx_tiles=ctx_tiles, n_tiles=n_tiles),
        grid=(b, n_tiles),
        in_specs=[row(d), prev, nxt, lat, lat, ctx, ctx, tab, tab] + [full(a) for a in list(ws) + prm],
        out_specs=[row(n) for n, _ in outs],
        out_shape=[jax.ShapeDtypeStruct((b, t, n), dt) for n, dt in outs],
        compiler_params=_cparams(("parallel", "parallel")),
        name="inproj",
    )(h, h, h, shift_l, scale_l, shift_c, scale_c, *rope, *ws, *prm)


def _to_scan_body(x0_ref, x1_ref, o_ref, r_ref, *, kind, nb):
    n_scan = 2 * nb * N_HEADS
    rep = LANES // n_scan
    n_piece = x0_ref.shape[1] // LANES
    for b in range(nb):
        for d, xt in enumerate((x0_ref[b].T, _flip_rows(x1_ref[b]).T)):
            for pc in range(n_piece):
                r_ref[pc, pl.ds((d * nb + b) * GROUP_WIDTH, GROUP_WIDTH), :] = xt[:, pc * LANES:(pc + 1) * LANES]
    for pc in range(n_piece):
        t_sl = slice(pc * LANES, (pc + 1) * LANES)
        if kind == "k":
            for k in range(HEAD_DIM):
                rows = r_ref[pc, pl.ds(k, n_scan, stride=HEAD_DIM), :]
                o_ref[k, t_sl, :] = jnp.concatenate([rows] * rep, axis=0).T
        else:
            for vh in range(HEAD_DIM // rep):
                rows = [r_ref[pc, pl.ds(vh * rep + vl, n_scan, stride=HEAD_DIM), :] for vl in range(rep)]
                o_ref[t_sl, vh, :] = jnp.concatenate(rows, axis=0).T


def _to_scan(x, col0, col1, kind, tc):
    nb, t, _ = x.shape
    gw = GROUP_WIDTH
    x0 = x1 = x
    blk = TIME_BLOCK
    n_all, n_ctx = t // blk, tc // blk
    n_scan = 2 * nb * N_HEADS
    rep = LANES // n_scan
    fwd = pl.BlockSpec((nb, blk, gw), lambda i: (0, i, col0))
    bwd = pl.BlockSpec((nb, blk, gw), lambda i: (0, _rev_block(i, n_ctx, n_all), col1))
    if kind == "k":
        out_shape = (HEAD_DIM, t, LANES)
        out_spec = pl.BlockSpec((HEAD_DIM, blk, LANES), lambda i: (0, i, 0))
    else:
        out_shape = (t, HEAD_DIM // rep, LANES)
        out_spec = pl.BlockSpec((blk, HEAD_DIM // rep, LANES), lambda i: (i, 0, 0))
    return pl.pallas_call(
        functools.partial(_to_scan_body, kind=kind, nb=nb),
        grid=(n_all,),
        in_specs=[fwd, bwd],
        out_specs=out_spec,
        out_shape=jax.ShapeDtypeStruct(out_shape, F32),
        scratch_shapes=[pltpu.VMEM((blk // LANES, n_scan * HEAD_DIM, LANES), F32)],
        compiler_params=_cparams(("parallel",)),
        name="to_scan_" + kind,
    )(x0, x1)


def _from_scan_body(yf_ref, yb_ref, *rest, nb, readout):
    if readout == "rwkv":
        bonus_ref, g_ref, ng_ref, nb_ref, o_ref, r_ref = rest
    else:
        gate_ref, ng_ref, o_ref, r_ref = rest
    n_scan = 2 * nb * N_HEADS
    rep = LANES // n_scan
    n_piece = yf_ref.shape[1] // LANES
    for vh in range(HEAD_DIM // rep):
        for d, yt in enumerate((yf_ref[vh].T, yb_ref[vh].T)):
            for pc in range(n_piece):
                r_ref[d, pc, pl.ds(vh * LANES, LANES), :] = yt[:, pc * LANES:(pc + 1) * LANES]
    mean_block = _head_block(1.0 / HEAD_DIM).astype(BF16)
    for b in range(nb):
        slab = lambda d: jnp.concatenate(
            [jnp.concatenate([r_ref[d, pc, pl.ds((d * nb + b) * N_HEADS + h, HEAD_DIM, stride=n_scan), :]
                              for h in range(N_HEADS)], axis=0).T for pc in range(n_piece)], axis=0)
        y = slab(0) + _flip_rows(slab(1))
        if readout == "rwkv":
            y = y - _head_sum(y, mean_block)
            y = y * lax.rsqrt(_head_sum(y * y, mean_block) + RWKV_LNX_EPS) * ng_ref[...] + nb_ref[...]
            o_ref[b] = (y + bonus_ref[b]) * g_ref[b]
        else:
            y = y * lax.rsqrt(_head_sum(y * y, mean_block) + LN_EPS) * ng_ref[...]
            o_ref[b] = y * gate_ref[b]


def _from_scan(y, nb, tc, readout, feats, cols, vecs):
    n_vh, t, _ = y.shape
    gw = GROUP_WIDTH
    n_all, n_ctx = t // TIME_BLOCK, tc // TIME_BLOCK
    fwd = pl.BlockSpec((n_vh, TIME_BLOCK, LANES), lambda i: (0, i, 0))
    bwd = pl.BlockSpec((n_vh, TIME_BLOCK, LANES), lambda i: (0, _rev_block(i, n_ctx, n_all), 0))
    tok = lambda c: pl.BlockSpec((nb, TIME_BLOCK, gw), lambda i: (0, i, c))
    vec = pl.BlockSpec((1, gw), lambda i: (0, 0))
    return pl.pallas_call(
        functools.partial(_from_scan_body, nb=nb, readout=readout),
        grid=(n_all,),
        in_specs=[fwd, bwd] + [tok(c) for c in cols] + [vec] * len(vecs),
        out_specs=tok(0),
        out_shape=jax.ShapeDtypeStruct((nb, t, gw), F32),
        scratch_shapes=[pltpu.VMEM((2, TIME_BLOCK // LANES, n_vh * LANES, LANES), F32)],
        compiler_params=_cparams(("parallel",)),
        name="from_scan_" + readout,
    )(y, y, *([feats] * len(cols)), *[v.reshape(1, gw) for v in vecs])


N_ACC = 4


def _acc_add(acc, i, x):
    acc[i % N_ACC] = x if acc[i % N_ACC] is None else acc[i % N_ACC] + x


def _acc_total(acc):
    return (acc[0] + acc[1]) + (acc[2] + acc[3])


def _scan_body(*refs, mode, tb, n_vh):
    if mode == "rwkv":
        r_ref, w_ref, k_ref, a_ref, b_ref, v_ref, y_ref, s_ref = refs
    else:
        r_ref, w_ref, k_ref, v_ref, y_ref, s_ref = refs

    @pl.when(pl.program_id(0) == 0)
    def _():
        s_ref[...] = jnp.zeros_like(s_ref)

    row = lambda ref, k, j: ref[k, pl.ds(j, 1), :]
    wide = lambda x: jnp.broadcast_to(x, (n_vh, LANES))

    def store_y(j, y):
        for vh in range(n_vh):
            y_ref[vh, pl.ds(j, 1), :] = y[vh:vh + 1, :]

    def rwkv_step(j, sa):
        v = v_ref[j]
        j_next = jnp.minimum(j + 1, tb - 1)
        ys, sas = [None] * N_ACC, [None] * N_ACC
        for k in range(HEAD_DIM):
            s = s_ref[k] * wide(row(w_ref, k, j)) + sa * wide(row(b_ref, k, j)) + v * wide(row(k_ref, k, j))
            s_ref[k] = s
            _acc_add(ys, k, s * wide(row(r_ref, k, j)))
            _acc_add(sas, k, s * wide(row(a_ref, k, j_next)))
        store_y(j, _acc_total(ys))
        return _acc_total(sas)

    def gla_step(j, carry):
        v = v_ref[j]
        ys = [None] * N_ACC
        for k in range(HEAD_DIM):
            s = s_ref[k] * wide(row(w_ref, k, j)) + v * wide(row(k_ref, k, j))
            s_ref[k] = s
            _acc_add(ys, k, s * wide(row(r_ref, k, j)))
        store_y(j, _acc_total(ys))
        return carry

    if mode == "rwkv":
        sa0 = [None] * N_ACC
        for k in range(HEAD_DIM):
            _acc_add(sa0, k, s_ref[k] * wide(row(a_ref, k, 0)))
        lax.fori_loop(0, tb, rwkv_step, _acc_total(sa0), unroll=8)
    else:
        lax.fori_loop(0, tb, gla_step, 0, unroll=8)


def _scan(mode, k_inputs, v_in):
    t, n_vh, _ = v_in.shape
    tb = SCAN_TB
    big = pl.BlockSpec((HEAD_DIM, tb, LANES), lambda i: (0, i, 0))
    small = pl.BlockSpec((tb, n_vh, LANES), lambda i: (i, 0, 0))
    return pl.pallas_call(
        functools.partial(_scan_body, mode=mode, tb=tb, n_vh=n_vh),
        grid=(t // tb,),
        in_specs=[big] * len(k_inputs) + [small],
        out_specs=pl.BlockSpec((n_vh, tb, LANES), lambda i: (0, i, 0)),
        out_shape=jax.ShapeDtypeStruct((n_vh, t, LANES), F32),
        scratch_shapes=[pltpu.VMEM((HEAD_DIM, n_vh, LANES), F32)],
        compiler_params=_cparams(("arbitrary",)),
        name="scan_" + mode,
    )(*k_inputs, v_in)


MLSTM_CHUNK = 64


def _prefix_max(x, reverse):
    n = x.shape[0]
    row = lax.broadcasted_iota(jnp.int32, x.shape, 0)
    sh = 1
    while sh < n:
        if reverse:
            x = jnp.where(row < n - sh, jnp.maximum(x, pltpu.roll(x, n - sh, 0)), x)
        else:
            x = jnp.where(row >= sh, jnp.maximum(x, pltpu.roll(x, sh, 0)), x)
        sh *= 2
    return x


def _mlstm_body(qk_ref, qkp_ref, qkn_ref, v_ref, gt_ref, cw_ref, cb_ref, sel_ref, gb_ref, o_ref, st_ref, m_ref, *,
                reverse, n_sub, n_ctx, n_blocks):
    c, gw = MLSTM_CHUNK, GROUP_WIDTH
    i = pl.program_id(1)

    @pl.when(i == 0)
    def _():
        st_ref[...] = jnp.zeros_like(st_ref)
        m_ref[...] = jnp.zeros_like(m_ref)

    blk = _rev_block(i, n_ctx, n_blocks) if reverse else i
    seg_start = jnp.logical_or(blk == 0, blk == n_ctx)
    seg_end = jnp.logical_or(blk == n_ctx - 1, blk == n_blocks - 1)
    qk = qk_ref[0]
    prev, nxt = _shifted_rows(qk, qkp_ref[0], qkn_ref[0], seg_start, seg_end)
    qk = prev * cw_ref[0:1, :] + qk * cw_ref[1:2, :] + nxt * cw_ref[2:3, :] + cb_ref[...]
    qk = qk * jax.nn.sigmoid(qk)
    q_all, k_all = qk[:, :gw], qk[:, gw:] * HEAD_DIM ** -0.5
    v_all = v_ref[0]
    gt = gt_ref[0]
    ig_all = _dot01_right(gt, sel_ref[0]) + gb_ref[0:1, :]
    fg_all = -_softplus(-(_dot01_right(gt, sel_ref[1]) + gb_ref[1:2, :]))

    tt = lax.broadcasted_iota(jnp.int32, (c, c), 0)
    ss = lax.broadcasted_iota(jnp.int32, (c, c), 1)
    tri = jnp.where((ss >= tt) if reverse else (ss <= tt), 1.0, 0.0).astype(F32)
    row = lax.broadcasted_iota(jnp.int32, (c, gw), 0)
    s_of_lane = lax.broadcasted_iota(jnp.int32, (c, gw), 1) % c
    causal = (s_of_lane >= row) if reverse else (s_of_lane <= row)
    diag = s_of_lane == row
    block = _head_block()
    block2 = jnp.concatenate([block, block], axis=1)
    ones8 = jnp.ones((SUBLANES, c), F32)
    nt = (((1,), (1,)), ((), ()))
    last = 0 if reverse else c - 1
    for u in (range(n_sub - 1, -1, -1) if reverse else range(n_sub)):
        sl = slice(u * c, (u + 1) * c)
        q, k, v, ig, fg = q_all[sl], k_all[sl], v_all[sl], ig_all[sl], fg_all[sl]
        b = _dot01_left(tri, fg)
        g = ig - b
        m_prev = m_ref[0:1, :]
        m_t = b + jnp.maximum(m_prev, _prefix_max(g, reverse))
        w_inter = jnp.exp(b + m_prev - m_t)
        g_row = _dot01_left(ones8, jnp.where(diag, g, 0.0))[0:1]
        qb = q.astype(BF16)
        k_bd = (jnp.concatenate([k] * N_HEADS, axis=0) * block).astype(BF16)
        scores = lax.dot_general(qb, k_bd, nt, preferred_element_type=F32)
        w = jnp.where(causal, jnp.exp((b - m_t) + g_row), 0.0) * scores
        v_bd = jnp.concatenate([jnp.concatenate([v] * N_HEADS, axis=0) * block, block], axis=1).astype(BF16)
        intra = jnp.dot(w.astype(BF16), v_bd, preferred_element_type=F32)
        inter = jnp.dot(qb, st_ref[...].astype(BF16), preferred_element_type=F32)
        num = w_inter * inter[:, :gw] + intra[:, :gw]
        den = w_inter * inter[:, gw:] + intra[:, gw:]
        floor = jnp.exp(jnp.minimum(-m_t, MAX_NEG_LOG_STAB))
        o_ref[0, pl.ds(u * c, c), :] = num / jnp.maximum(jnp.abs(den), floor)
        m_new = m_t[last:last + 1]
        b_end = b[last:last + 1]
        kw = k * jnp.exp(b_end - b + ig - m_new)
        decay = jnp.exp(b_end + m_prev - m_new)
        v_one = jnp.concatenate([v, jnp.ones_like(v)], axis=1).astype(BF16)
        upd = jnp.dot(kw.T.astype(BF16), v_one, preferred_element_type=F32)
        st_ref[...] = st_ref[...] * jnp.concatenate([decay, decay], axis=1) + upd * block2
        m_ref[0:1, :] = m_new


def _mlstm_chunked(p, conv_w, conv_b, gate_b, reverse, tc):
    b, t, _ = p.shape
    gw = GROUP_WIDTH
    n_blocks, n_ctx = t // ROW_TILE, tc // ROW_TILE
    sub, n_sub8 = ROW_TILE // SUBLANES, t // SUBLANES
    blk = (lambda i: _rev_block(i, n_ctx, n_blocks)) if reverse else (lambda i: i)
    d = 1 if reverse else 0
    lane_head = np.arange(gw) // HEAD_DIM
    sel = np.zeros((2, LANES, gw), np.float32)
    for j, base in enumerate((d * N_HEADS, (2 + d) * N_HEADS)):
        sel[j, base + lane_head, np.arange(gw)] = 1.0
    gate_bias = jnp.stack([jnp.repeat(gate_b[d], HEAD_DIM), jnp.repeat(gate_b[2 + d], HEAD_DIM)], axis=0)
    full = lambda a: pl.BlockSpec(a.shape, lambda bi, i: (0,) * a.ndim)
    cb = conv_b.reshape(1, 2 * gw)
    return pl.pallas_call(
        functools.partial(_mlstm_body, reverse=reverse, n_sub=ROW_TILE // MLSTM_CHUNK, n_ctx=n_ctx,
                          n_blocks=n_blocks),
        grid=(b, n_blocks),
        in_specs=[pl.BlockSpec((1, ROW_TILE, 2 * gw), lambda bi, i: (bi, blk(i), 0)),
                  pl.BlockSpec((1, SUBLANES, 2 * gw), lambda bi, i: (bi, jnp.maximum(blk(i) * sub - 1, 0), 0)),
                  pl.BlockSpec((1, SUBLANES, 2 * gw), lambda bi, i: (bi, jnp.minimum((blk(i) + 1) * sub, n_sub8 - 1), 0)),
                  pl.BlockSpec((1, ROW_TILE, gw), lambda bi, i: (bi, blk(i), 2)),
                  pl.BlockSpec((1, ROW_TILE, LANES), lambda bi, i: (bi, blk(i), 4 * gw // LANES)),
                  full(conv_w), full(cb), full(sel), full(gate_bias)],
        out_specs=pl.BlockSpec((1, ROW_TILE, gw), lambda bi, i: (bi, blk(i), 0)),
        out_shape=jax.ShapeDtypeStruct((b, t, gw), F32),
        scratch_shapes=[pltpu.VMEM((gw, 2 * gw), F32), pltpu.VMEM((SUBLANES, gw), F32)],
        compiler_params=_cparams(("parallel", "arbitrary")),
        name="mlstm_chunk",
    )(p, p, p, p, p, conv_w, cb, jnp.asarray(sel), gate_bias)


def _attn_body(q_ref, k_ref, v_ref, lam_ref, g_ref, o_ref, *, out_scale, ctx_tiles, n_ctx_keys):
    def attend(n_keys):
        q = q_ref[0]
        k = k_ref[0, :n_keys, :]
        v = v_ref[0, :n_keys, :].astype(BF16)
        lane = lax.broadcasted_iota(jnp.int32, q.shape, 1)
        comp = lane // DIFF_QK_DIM
        nt = (((1,), (1,)), ((), ()))
        n_maps = LANES // DIFF_QK_DIM
        ss = [lax.dot_general(jnp.where(comp == c, q, 0.0).astype(BF16), k, nt, preferred_element_type=F32)
              for c in range(n_maps)]
        es = [jnp.exp2(s - jnp.max(s, axis=-1, keepdims=True)) for s in ss]
        ls = [jnp.sum(e, axis=-1, keepdims=True) for e in es]
        ys = []
        for hh in range(LANES // HEAD_DIM):
            e0, e1, l0, l1 = es[2 * hh], es[2 * hh + 1], ls[2 * hh], ls[2 * hh + 1]
            pr = (e0 - e1 * (lam_ref[0:1, 0:1] * l0 / l1)).astype(BF16)
            ys.append(jnp.dot(pr, v, preferred_element_type=F32) * (1.0 / l0))
        first = lane < HEAD_DIM
        y = jnp.where(first, ys[0], ys[1])
        ysq = y * y
        ms = jnp.where(first, jnp.sum(jnp.where(first, ysq, 0.0), axis=-1, keepdims=True),
                       jnp.sum(jnp.where(first, 0.0, ysq), axis=-1, keepdims=True)) * (1.0 / HEAD_DIM)
        o_ref[0] = y * lax.rsqrt(ms + LN_EPS) * g_ref[...] * out_scale

    if ctx_tiles == 0:
        attend(k_ref.shape[1])
    else:
        is_ctx = pl.program_id(2) < ctx_tiles
        pl.when(is_ctx)(lambda: attend(n_ctx_keys))
        pl.when(jnp.logical_not(is_ctx))(lambda: attend(k_ref.shape[1]))


def _attention(q, k, v, v_col0, lam, norm_g, out_scale, q_tile0, tc):
    b, t, gw = q.shape
    n_tiles = t // ATTN_TILE - q_tile0
    ctx_tiles = max(tc // ATTN_TILE - q_tile0, 0)
    qspec = pl.BlockSpec((1, ATTN_TILE, LANES), lambda bi, pi, i: (bi, q_tile0 + i, pi))
    kspec = pl.BlockSpec((1, t, LANES), lambda bi, pi, i: (bi, 0, pi))
    vspec = pl.BlockSpec((1, t, LANES), lambda bi, pi, i: (bi, 0, v_col0 + pi))
    vec = pl.BlockSpec((1, LANES), lambda bi, pi, i: (0, 0))
    return pl.pallas_call(
        functools.partial(_attn_body, out_scale=out_scale, ctx_tiles=ctx_tiles, n_ctx_keys=tc),
        grid=(b, gw // LANES, n_tiles),
        in_specs=[qspec, kspec, vspec, vec, vec],
        out_specs=pl.BlockSpec((1, ATTN_TILE, LANES), lambda bi, pi, i: (bi, i, pi)),
        out_shape=jax.ShapeDtypeStruct((b, n_tiles * ATTN_TILE, gw), F32),
        compiler_params=_cparams(("parallel", "parallel", "parallel")),
        name="diff_attn",
    )(q, k, v, jnp.full((1, LANES), lam, F32), jnp.tile(norm_g.reshape(1, HEAD_DIM), (1, LANES // HEAD_DIM)))


def _layernorm(z, g, b):
    z = z - jnp.mean(z, axis=-1, keepdims=True)
    return z * lax.rsqrt(jnp.mean(z * z, axis=-1, keepdims=True) + LN_EPS) * g + b


def _outproj_body(ya_ref, yb_ref, yc_ref, hf_ref, hb_ref, og_ref, ng_ref, h_ref, gl_ref, gc_ref, w_ref, g_ref,
                  b_ref, o_ref, *, ctx_tiles, alpha):
    is_ctx = pl.program_id(1) < ctx_tiles
    gate = jnp.where(is_ctx, gc_ref[0], gl_ref[0])
    mean_block = _head_block(1.0 / HEAD_DIM).astype(BF16)
    yd = hf_ref[0] + hb_ref[0]
    yd = yd - _head_sum(yd, mean_block)
    yd = yd * lax.rsqrt(_head_sum(yd * yd, mean_block) + LN_EPS) * ng_ref[...] * jax.nn.sigmoid(og_ref[0])
    y = None
    for m, ym in enumerate((ya_ref[0], yb_ref[0], yc_ref[0], yd)):
        part = jnp.dot(ym.astype(BF16), w_ref[m * GROUP_WIDTH:(m + 1) * GROUP_WIDTH, :],
                       preferred_element_type=F32)
        y = part if y is None else y + part
    o_ref[0] = _layernorm(alpha * h_ref[0] + gate * y, g_ref[...], b_ref[...])


def _outproj(ya, yb, yc, hf, hb, p_mlstm, mlstm_norm_g, h, gate_l, gate_c, w, ln_g, ln_b, tile0, ctx_tiles, alpha):
    b, t, d = h.shape
    gw = GROUP_WIDTH
    n_tiles = t // ROW_TILE - tile0
    yrow = lambda c: pl.BlockSpec((1, ROW_TILE, gw), lambda bi, i: (bi, tile0 + i, c))
    yb_row = pl.BlockSpec((1, ROW_TILE, gw), lambda bi, i: (bi, i, 0))
    hrow = pl.BlockSpec((1, ROW_TILE, d), lambda bi, i: (bi, tile0 + i, 0))
    lat = pl.BlockSpec((1, 1, d), lambda bi, i: (bi, 0, 0))
    ctx = pl.BlockSpec((1, 1, d), lambda bi, i: (0, 0, 0))
    vec = lambda n: pl.BlockSpec((1, n), lambda bi, i: (0, 0))
    return pl.pallas_call(
        functools.partial(_outproj_body, ctx_tiles=ctx_tiles, alpha=alpha),
        grid=(b, n_tiles),
        in_specs=[yrow(0), yb_row, yrow(0), yrow(0), yrow(0), yrow(3), vec(gw), hrow, lat, ctx,
                  pl.BlockSpec(w.shape, lambda bi, i: (0, 0)), vec(d), vec(d)],
        out_specs=pl.BlockSpec((1, ROW_TILE, d), lambda bi, i: (bi, i, 0)),
        out_shape=jax.ShapeDtypeStruct((b, n_tiles * ROW_TILE, d), F32),
        compiler_params=_cparams(("parallel", "parallel")),
        name="outproj_ln",
    )(ya, yb, yc, hf, hb, p_mlstm, mlstm_norm_g.reshape(1, gw), h, gate_l, gate_c, w, ln_g.reshape(1, d),
      ln_b.reshape(1, d))


FFN_CHUNKS = 1


def _ffn_body(x_ref, xp_ref, xn_ref, sl_ref, cl_ref, gl_ref, sc_ref, cc_ref, gc_ref,
              wu_ref, cw_ref, cb_ref, wd_ref, g_ref, b_ref, o_ref, *, ctx_tiles, n_tiles, alpha, d_ff):
    i = pl.program_id(1)
    is_ctx = i < ctx_tiles
    shift = jnp.where(is_ctx, sc_ref[0], sl_ref[0])
    scale = jnp.where(is_ctx, cc_ref[0], cl_ref[0])
    gate_mod = jnp.where(is_ctx, gc_ref[0], gl_ref[0])
    seg_start = jnp.logical_or(i == 0, i == ctx_tiles)
    seg_end = jnp.logical_or(i == ctx_tiles - 1, i == n_tiles - 1)

    x = x_ref[0]
    xm = (x * (1.0 + scale) + shift).astype(BF16)
    xp = (xp_ref[0] * (1.0 + scale) + shift).astype(BF16)
    xn = (xn_ref[0] * (1.0 + scale) + shift).astype(BF16)
    rows = x.shape[0]
    row_id = lax.broadcasted_iota(jnp.int32, (rows, 1), 0)
    ch = d_ff // FFN_CHUNKS
    f = jnp.zeros(x.shape, F32)
    for c in range(FFN_CHUNKS):
        wg = wu_ref[:, c * ch:(c + 1) * ch]
        wv = wu_ref[:, d_ff + c * ch:d_ff + (c + 1) * ch]
        gate = jnp.dot(xm, wg, preferred_element_type=F32)
        val = jnp.dot(xm, wv, preferred_element_type=F32)
        gp = jnp.dot(xp, wg, preferred_element_type=F32)[SUBLANES - 1:SUBLANES, :]
        gn = jnp.dot(xn, wg, preferred_element_type=F32)[0:1, :]
        gp = jnp.where(seg_start, 0.0, gp)
        gn = jnp.where(seg_end, 0.0, gn)
        prev = jnp.where(row_id == 0, gp, pltpu.roll(gate, 1, 0))
        nxt = jnp.where(row_id == rows - 1, gn, pltpu.roll(gate, rows - 1, 0))
        cw = cw_ref[:, c * ch:(c + 1) * ch]
        conv = prev * cw[0:1, :] + gate * cw[1:2, :] + nxt * cw[2:3, :] + cb_ref[:, c * ch:(c + 1) * ch]
        act = (jax.nn.gelu(conv) * val).astype(BF16)
        f = f + jnp.dot(act, wd_ref[c * ch:(c + 1) * ch, :], preferred_element_type=F32)
    o_ref[0] = _layernorm(alpha * x + gate_mod * f, g_ref[...], b_ref[...])


def _ffn(h, mods_l, mods_c, w_up, conv_w, conv_b, w_down, ln_g, ln_b, ctx_tiles, alpha):
    b, t, d = h.shape
    d_ff = w_down.shape[0]
    tile = 2 * ROW_TILE if ctx_tiles == 0 and t % (2 * ROW_TILE) == 0 else ROW_TILE
    n_tiles = t // tile
    sub = tile // SUBLANES
    n_sub = t // SUBLANES
    row = pl.BlockSpec((1, tile, d), lambda bi, i: (bi, i, 0))
    prev = pl.BlockSpec((1, SUBLANES, d), lambda bi, i: (bi, jnp.maximum(i * sub - 1, 0), 0))
    nxt = pl.BlockSpec((1, SUBLANES, d), lambda bi, i: (bi, jnp.minimum((i + 1) * sub, n_sub - 1), 0))
    lat = pl.BlockSpec((1, 1, d), lambda bi, i: (bi, 0, 0))
    ctx = pl.BlockSpec((1, 1, d), lambda bi, i: (0, 0, 0))
    vec = pl.BlockSpec((1, d), lambda bi, i: (0, 0))
    full = lambda a: pl.BlockSpec(a.shape, lambda bi, i: (0,) * a.ndim)
    cb = conv_b.reshape(1, d_ff)
    return pl.pallas_call(
        functools.partial(_ffn_body, ctx_tiles=ctx_tiles, n_tiles=n_tiles, alpha=alpha, d_ff=d_ff),
        grid=(b, n_tiles),
        in_specs=[row, prev, nxt, lat, lat, lat, ctx, ctx, ctx,
                  full(w_up), full(conv_w), full(cb), full(w_down), vec, vec],
        out_specs=row,
        out_shape=jax.ShapeDtypeStruct((b, t, d), F32),
        compiler_params=_cparams(("parallel", "parallel")),
        name="ffn",
    )(h, h, h, *mods_l, *mods_c, w_up, conv_w, cb, w_down, ln_g.reshape(1, d), ln_b.reshape(1, d))


def _rwkv7(f, tc, lnx_g, lnx_b):
    b = f.shape[0]
    col = RWKV_FEATS.index
    y = _scan("rwkv",
              [_to_scan(f, col("r"), col("r"), "k", tc), _to_scan(f, col("w0"), col("w1"), "k", tc),
               _to_scan(f, col("k0"), col("k1"), "k", tc), _to_scan(f, col("a"), col("a"), "k", tc),
               _to_scan(f, col("b0"), col("b1"), "k", tc)],
              _to_scan(f, col("v"), col("v"), "v", tc))
    return _from_scan(y, b, tc, "rwkv", f, (col("bonus"), col("g")), (lnx_g, lnx_b))


def _rope_tables(n_ctx, n_lat):
    rows = n_lat // GRID_W
    row = jnp.repeat(jnp.arange(rows, dtype=F32), GRID_W)
    col = jnp.tile(jnp.arange(GRID_W, dtype=F32), rows)
    n_freq = DIFF_QK_DIM // 4
    inv_freq = ROPE_BASE ** (-jnp.arange(n_freq, dtype=F32) / n_freq)
    ang = jnp.concatenate([row[:, None] * inv_freq, col[:, None] * inv_freq], axis=-1)
    ang = jnp.concatenate([jnp.zeros((n_ctx, ang.shape[1]), F32), ang], axis=0)
    half = DIFF_QK_DIM // 2
    lane = np.arange(GROUP_WIDTH)
    sign = np.where(lane % DIFF_QK_DIM < half, -1.0, 1.0).astype(np.float32)
    return jnp.cos(ang)[:, lane % half], jnp.sin(ang)[:, lane % half] * sign


def _diff_attn(q, k, v, tc, lam_vecs, norm_g, layer, need_ctx):
    lam_init = 0.8 - 0.6 * math.exp(-0.3 * layer)
    lv = lam_vecs.astype(F32)
    lam = jnp.exp(jnp.sum(lv[0] * lv[1])) - jnp.exp(jnp.sum(lv[2] * lv[3])) + lam_init
    q_tile0 = 0 if need_ctx else tc // ATTN_TILE
    return _attention(q, k, v, 0, lam, norm_g, 1.0 - lam_init, q_tile0, tc)


def _hgrn2(f, tc, norm_g):
    b = f.shape[0]
    col = HGRN_FEATS.index
    o = _scan("gla",
              [_to_scan(f, col("q"), col("q"), "k", tc), _to_scan(f, col("w0"), col("w1"), "k", tc),
               _to_scan(f, col("k0"), col("k1"), "k", tc)],
              _to_scan(f, col("v"), col("v"), "v", tc))
    return _from_scan(o, b, tc, "gla", f, (col("gate"),), (jnp.tile(norm_g, N_HEADS),))


RWKV_COLS = 3 * GROUP_WIDTH + 2 * RWKV_DECAY_LORA + 2 * RWKV_ICLR_LORA + RWKV_GATE_LORA
DIFF_COLS = 3 * GROUP_WIDTH
HGRN_COLS = 5 * GROUP_WIDTH
MLSTM_COLS = 4 * GROUP_WIDTH + 4 * N_HEADS
SLAB_COLS = (RWKV_COLS, DIFF_COLS, HGRN_COLS, MLSTM_COLS)
SLAB_PAD = (5 * GROUP_WIDTH, DIFF_COLS, HGRN_COLS, 4 * GROUP_WIDTH + LANES)


def _split_cols(w):
    parts, off = [], 0
    for n, n_pad in zip(SLAB_COLS, SLAB_PAD):
        parts.append(jnp.pad(w[:, off:off + n], ((0, 0), (0, n_pad - n))).astype(BF16))
        off += n
    return parts


def kernel(x, c, ctx, c_ctx, ada_w, ada_b, w_in, rwkv_mu, rwkv_w0, rwkv_w2, rwkv_a0, rwkv_a2, rwkv_g2, rwkv_k_k, rwkv_k_a, rwkv_r_k, rwkv_lnx_g, rwkv_lnx_b, diff_lambda, diff_norm_g, hgrn_lb_logits, hgrn_norm_g, mlstm_conv_w, mlstm_conv_b, mlstm_gate_b, mlstm_norm_g, w_out, ffn_w_up, ffn_conv_w, ffn_conv_b, ffn_w_down, ln_g, ln_b):
    depth = w_in.shape[0]
    b, seq, d = x.shape
    tc = ctx.shape[1]
    assert tc == ROW_TILE == ATTN_TILE and seq % ROW_TILE == 0 and 2 * b * N_HEADS <= LANES
    alpha = (2.0 * depth) ** 0.25

    lb_w = jax.nn.softmax(hgrn_lb_logits.astype(F32), axis=0)
    lower_bounds = jnp.cumsum(lb_w, axis=0) - lb_w[0]

    rope = _rope_tables(tc, seq)
    cond = jnp.zeros((8, d), F32).at[:b].set(c).at[b].set(c_ctx)
    h = jnp.concatenate([ctx, x], axis=1)
    for layer in range(depth):
        last = layer == depth - 1
        mod = _adaln(cond, ada_w, ada_b, layer).reshape(8, 6, d)
        mods_l = [mod[:b, j][:, None, :] for j in range(6)]
        mods_c = [mod[b:b + 1, j][:, None, :] for j in range(6)]

        rwkv_prm = (rwkv_mu[layer], rwkv_w0[layer], rwkv_w2[layer], rwkv_a0[layer], rwkv_a2[layer], rwkv_g2[layer],
                    rwkv_k_k[layer], rwkv_k_a[layer], rwkv_r_k[layer])
        f_rwkv, q, k, v, f_hgrn, p_mlstm = _inproj(h, mods_l[0], mods_l[1], mods_c[0], mods_c[1], rope,
                                                   _split_cols(w_in[layer]), rwkv_prm, lower_bounds[layer],
                                                   tc // ROW_TILE)
        y_a = _rwkv7(f_rwkv, tc, rwkv_lnx_g[layer], rwkv_lnx_b[layer])
        y_b = _diff_attn(q, k, v, tc, diff_lambda[layer], diff_norm_g[layer], layer, not last)
        y_c = _hgrn2(f_hgrn, tc, hgrn_norm_g[layer])
        h_f, h_b = (_mlstm_chunked(p_mlstm, mlstm_conv_w[layer], mlstm_conv_b[layer], mlstm_gate_b[layer], rev, tc)
                    for rev in (False, True))

        tile0, ctx_tiles = (tc // ROW_TILE, 0) if last else (0, tc // ROW_TILE)
        h = _outproj(y_a, y_b, y_c, h_f, h_b, p_mlstm, mlstm_norm_g[layer], h, mods_l[2], mods_c[2],
                     w_out[layer].astype(BF16), ln_g[layer, 0], ln_b[layer, 0], tile0, ctx_tiles, alpha)
        h = _ffn(h, mods_l[3:6], mods_c[3:6], ffn_w_up[layer].astype(BF16), ffn_conv_w[layer], ffn_conv_b[layer],
                 ffn_w_down[layer].astype(BF16), ln_g[layer, 1], ln_b[layer, 1], ctx_tiles, alpha)
    return h
```

```python
import functools
import math

import jax
import jax.numpy as jnp
import numpy as np
from jax import lax
from jax.experimental import pallas as pl
from jax.experimental.pallas import tpu as pltpu

F32 = jnp.float32
BF16 = jnp.bfloat16

HEAD_DIM = 64
N_HEADS = 4
GROUP_WIDTH = N_HEADS * HEAD_DIM
DIFF_QK_DIM = HEAD_DIM // 2
GRID_W = 64
ROPE_BASE = 10000.0
RWKV_DECAY_LORA = 64
RWKV_ICLR_LORA = 64
RWKV_GATE_LORA = 160
RWKV_LNX_EPS = 64e-5
LN_EPS = 1e-5
LB_FLOOR = 1e-30
MAX_NEG_LOG_STAB = 60.0

LANES = 128
SUBLANES = 8
ADALN_TILE = 1536
ROW_TILE = 256
ATTN_TILE = 256
TIME_BLOCK = 256
SCAN_TB = 128
VMEM_LIMIT = 56 * 1024 * 1024


def _cparams(sem):
    return pltpu.CompilerParams(dimension_semantics=sem, vmem_limit_bytes=VMEM_LIMIT)


def _rev_block(i, n_ctx, n_all):
    return jnp.where(i < n_ctx, n_ctx - 1 - i, n_ctx + n_all - 1 - i)


def _split3(x):
    hi = x.astype(BF16)
    r1 = x - hi.astype(F32)
    mid = r1.astype(BF16)
    return hi, mid, (r1 - mid.astype(F32)).astype(BF16)


def _dot01_left(a01, x):
    a = a01.astype(BF16)
    p = [jnp.dot(a, part, preferred_element_type=F32) for part in _split3(x)]
    return (p[0] + p[1]) + p[2]


def _dot01_right(x, b01):
    b = b01.astype(BF16)
    p = [jnp.dot(part, b, preferred_element_type=F32) for part in _split3(x)]
    return (p[0] + p[1]) + p[2]


def _flip_rows(x):
    n = x.shape[0]
    r = lax.broadcasted_iota(jnp.int32, (n, n), 0)
    c = lax.broadcasted_iota(jnp.int32, (n, n), 1)
    return _dot01_left(jnp.where(r + c == n - 1, 1.0, 0.0).astype(F32), x)


def _adaln_body(c_ref, w_ref, b_ref, o_ref):
    x = c_ref[...]
    x = (x * jax.nn.sigmoid(x)).astype(BF16)
    o_ref[...] = jnp.dot(x, w_ref[0].astype(BF16), preferred_element_type=F32) + b_ref[0]


def _adaln(cond, w, b, layer):
    m, d = cond.shape
    n = w.shape[2]
    tn = ADALN_TILE
    return pl.pallas_call(
        _adaln_body,
        grid=(n // tn,),
        in_specs=[pl.BlockSpec((m, d), lambda j: (0, 0)),
                  pl.BlockSpec((1, d, tn), lambda j: (layer, 0, j)),
                  pl.BlockSpec((1, 1, tn), lambda j: (layer, 0, j))],
        out_specs=pl.BlockSpec((m, tn), lambda j: (0, j)),
        out_shape=jax.ShapeDtypeStruct((m, n), F32),
        compiler_params=_cparams(("arbitrary",)),
        name="adaln",
    )(cond, w, b.reshape(b.shape[0], 1, n))


def _softplus(x):
    return jnp.maximum(x, 0.0) + jnp.log(1.0 + jnp.exp(-jnp.abs(x)))


def _head_block(scale=1.0):
    r = lax.broadcasted_iota(jnp.int32, (GROUP_WIDTH, GROUP_WIDTH), 0) // HEAD_DIM
    c = lax.broadcasted_iota(jnp.int32, (GROUP_WIDTH, GROUP_WIDTH), 1) // HEAD_DIM
    return jnp.where(r == c, scale, 0.0).astype(F32)


def _head_sum(x, block_bf16):
    hi = x.astype(BF16)
    lo = (x - hi.astype(F32)).astype(BF16)
    return (jnp.dot(hi, block_bf16, preferred_element_type=F32)
            + jnp.dot(lo, block_bf16, preferred_element_type=F32))


def _shifted_rows(x, prev8, next8, seg_start, seg_end):
    rows = x.shape[0]
    row_id = lax.broadcasted_iota(jnp.int32, (rows, 1), 0)
    prev_row = jnp.where(seg_start, 0.0, prev8[SUBLANES - 1:SUBLANES, :])
    next_row = jnp.where(seg_end, 0.0, next8[0:1, :])
    prev = jnp.where(row_id == 0, prev_row, pltpu.roll(x, 1, 0))
    nxt = jnp.where(row_id == rows - 1, next_row, pltpu.roll(x, rows - 1, 0))
    return prev, nxt


RWKV_FEATS = ("r", "v", "w0", "w1", "k0", "k1", "a", "b0", "b1", "g", "bonus")


def _rwkv_feats(p, prev8, next8, seg_start, seg_end, mu_ref, w0_ref, w2_ref, a0_ref, a2_ref, g2_ref, kk_ref,
                ka_ref, rk_ref):
    gw = GROUP_WIDTH
    prev, nxt = _shifted_rows(p, prev8, next8, seg_start, seg_end)
    p = p + (0.5 * (prev + nxt) - p) * mu_ref[...]
    r, k, v = p[:, :gw], p[:, gw:2 * gw], p[:, 2 * gw:3 * gw]
    wd, ad, gd = p[:, 3 * gw:3 * gw + LANES], p[:, 3 * gw + LANES:4 * gw], p[:, 4 * gw:5 * gw]
    lora_w = jnp.dot(jnp.tanh(wd).astype(BF16), w2_ref[...], preferred_element_type=F32)
    lora_a = jnp.dot(ad.astype(BF16), a2_ref[...], preferred_element_type=F32)
    decay = jnp.exp(-jnp.exp(-_softplus(-(w0_ref[...] + lora_w)) - 0.5))
    a = jax.nn.sigmoid(a0_ref[...] + lora_a)
    g = jnp.dot(jax.nn.sigmoid(gd).astype(BF16), g2_ref[...], preferred_element_type=F32)
    block = _head_block().astype(BF16)
    kk = k * kk_ref[...]
    kk = kk / jnp.maximum(jnp.sqrt(_head_sum(kk * kk, block)), 1e-12)
    k0 = k * (1.0 + (a[:, :gw] - 1.0) * ka_ref[...])
    k1 = k * (1.0 + (a[:, gw:] - 1.0) * ka_ref[...])
    bonus = _head_sum(r * (k0 + k1) * rk_ref[...], block) * v
    return dict(r=r, v=v, w0=decay[:, :gw], w1=decay[:, gw:], k0=k0, k1=k1, a=-kk, b0=kk * a[:, :gw],
                b1=kk * a[:, gw:], g=g, bonus=bonus)


def _block_diag2(w):
    z = jnp.zeros_like(w[0])
    return jnp.concatenate([jnp.concatenate([w[0], z], axis=1), jnp.concatenate([z, w[1]], axis=1)], axis=0)


HGRN_FEATS = ("q", "w0", "w1", "k0", "k1", "gate", "v")


def _hgrn_feats(p, llb_ref, l1m_ref, oml_ref):
    gw = GROUP_WIDTH
    q, f_f, f_b, v, g = (p[:, j * gw:(j + 1) * gw] for j in range(5))
    feats = dict(q=q * jax.nn.sigmoid(q), gate=g * jax.nn.sigmoid(g), v=v)
    for d, f in enumerate((f_f, f_b)):
        x = llb_ref[...]
        y = l1m_ref[...] - _softplus(-f)
        log_f = jnp.maximum(x, y) + jnp.log(1.0 + jnp.exp(-jnp.abs(x - y)))
        feats["w%d" % d] = jnp.exp(log_f)
        feats["k%d" % d] = oml_ref[...] * jax.nn.sigmoid(-f)
    return feats


N_RWKV_PRM, N_HGRN_PRM = 9, 3


def _rope(x, cos, sin_signed):
    half = DIFF_QK_DIM // 2
    lane = lax.broadcasted_iota(jnp.int32, x.shape, 1)
    n = x.shape[1]
    partner = jnp.where(lane % DIFF_QK_DIM < half, pltpu.roll(x, n - half, 1), pltpu.roll(x, half, 1))
    return x * cos + partner * sin_signed


def _inproj_body(x_ref, xp_ref, xn_ref, sl_ref, cl_ref, sc_ref, cc_ref, cos_ref, sin_ref, wr_ref, wd_ref, wh_ref,
                 wm_ref, *refs, ctx_tiles, n_tiles):
    rwkv_prm, hgrn_prm = refs[:N_RWKV_PRM], refs[N_RWKV_PRM:N_RWKV_PRM + N_HGRN_PRM]
    fr_ref, q_ref, k_ref, v_ref, fh_ref, pm_ref = refs[N_RWKV_PRM + N_HGRN_PRM:]
    i = pl.program_id(1)
    is_ctx = i < ctx_tiles
    seg_start = jnp.logical_or(i == 0, i == ctx_tiles)
    seg_end = jnp.logical_or(i == ctx_tiles - 1, i == n_tiles - 1)
    shift = jnp.where(is_ctx, sc_ref[0], sl_ref[0])
    scale = jnp.where(is_ctx, cc_ref[0], cl_ref[0])
    mod = lambda x: (x * (1.0 + scale) + shift).astype(BF16)
    proj = lambda x, w_ref: jnp.dot(x, w_ref[...], preferred_element_type=F32)
    gw = GROUP_WIDTH
    xm = mod(x_ref[0])
    feats = _rwkv_feats(proj(xm, wr_ref), proj(mod(xp_ref[0]), wr_ref), proj(mod(xn_ref[0]), wr_ref),
                        seg_start, seg_end, *rwkv_prm)
    for j, name in enumerate(RWKV_FEATS):
        fr_ref[0, :, j * gw:(j + 1) * gw] = feats[name]
    pd = proj(xm, wd_ref)
    q_ref[0] = (_rope(pd[:, :gw], cos_ref[...], sin_ref[...])
                * (DIFF_QK_DIM ** -0.5 * math.log2(math.e))).astype(BF16)
    k_ref[0] = _rope(pd[:, gw:2 * gw], cos_ref[...], sin_ref[...]).astype(BF16)
    v_ref[0] = pd[:, 2 * gw:]
    feats = _hgrn_feats(proj(xm, wh_ref), *hgrn_prm)
    for j, name in enumerate(HGRN_FEATS):
        fh_ref[0, :, j * gw:(j + 1) * gw] = feats[name]
    pm_ref[0] = proj(xm, wm_ref)


def _inproj(h, shift_l, scale_l, shift_c, scale_c, rope, ws, rwkv_prm, lb, ctx_tiles):
    b, t, d = h.shape
    gw = GROUP_WIDTH
    n_tiles, sub, n_sub = t // ROW_TILE, ROW_TILE // SUBLANES, t // SUBLANES
    row = lambda n: pl.BlockSpec((1, ROW_TILE, n), lambda bi, i: (bi, i, 0))
    prev = pl.BlockSpec((1, SUBLANES, d), lambda bi, i: (bi, jnp.maximum(i * sub - 1, 0), 0))
    nxt = pl.BlockSpec((1, SUBLANES, d), lambda bi, i: (bi, jnp.minimum((i + 1) * sub, n_sub - 1), 0))
    lat = pl.BlockSpec((1, 1, d), lambda bi, i: (bi, 0, 0))
    ctx = pl.BlockSpec((1, 1, d), lambda bi, i: (0, 0, 0))
    full = lambda a: pl.BlockSpec(a.shape, lambda bi, i: (0,) * a.ndim)
    vec = lambda a: a.reshape(1, -1).astype(F32)
    mu, w0, w2, a0, a2, g2, k_k, k_a, r_k = rwkv_prm
    cols = ws[0].shape[1]
    prm = [jnp.pad(vec(mu), ((0, 0), (0, cols - mu.shape[0]))), vec(w0), _block_diag2(w2).astype(BF16), vec(a0),
           _block_diag2(a2).astype(BF16), jnp.pad(g2, ((0, gw - g2.shape[0]), (0, 0))).astype(BF16), vec(k_k),
           vec(k_a), vec(r_k),
           vec(jnp.log(jnp.maximum(lb, LB_FLOOR))), vec(jnp.log1p(-lb)), vec(1.0 - lb)]
    assert len(prm) == N_RWKV_PRM + N_HGRN_PRM
    outs = [(len(RWKV_FEATS) * gw, F32), (gw, BF16), (gw, BF16), (gw, F32), (len(HGRN_FEATS) * gw, F32),
            (ws[3].shape[1], F32)]
    tab = pl.BlockSpec((ROW_TILE, gw), lambda bi, i: (i, 0))
    return pl.pallas_call(
        functools.partial(_inproj_body, ctx_tiles=ctx_tiles, n_tiles=n_tiles),
        grid=(b, n_tiles),
        in_specs=[row(d), prev, nxt, lat, lat, ctx, ctx, tab, tab] + [full(a) for a in list(ws) + prm],
        out_specs=[row(n) for n, _ in outs],
        out_shape=[jax.ShapeDtypeStruct((b, t, n), dt) for n, dt in outs],
        compiler_params=_cparams(("parallel", "parallel")),
        name="inproj",
    )(h, h, h, shift_l, scale_l, shift_c, scale_c, *rope, *ws, *prm)


def _to_scan_body(x0_ref, x1_ref, o_ref, r_ref, *, kind, nb):
    n_scan = 2 * nb * N_HEADS
    rep = LANES // n_scan
    n_piece = x0_ref.shape[1] // LANES
    for b in range(nb):
        for d, xt in enumerate((x0_ref[b].T, _flip_rows(x1_ref[b]).T)):
            for pc in range(n_piece):
                r_ref[pc, pl.ds((d * nb + b) * GROUP_WIDTH, GROUP_WIDTH), :] = xt[:, pc * LANES:(pc + 1) * LANES]
    for pc in range(n_piece):
        t_sl = slice(pc * LANES, (pc + 1) * LANES)
        if kind == "k":
            for k in range(HEAD_DIM):
                rows = r_ref[pc, pl.ds(k, n_scan, stride=HEAD_DIM), :]
                o_ref[0, k, t_sl, :] = jnp.concatenate([rows] * rep, axis=0).T
        else:
            for vh in range(HEAD_DIM // rep):
                rows = [r_ref[pc, pl.ds(vh * rep + vl, n_scan, stride=HEAD_DIM), :] for vl in range(rep)]
                o_ref[0, t_sl, vh, :] = jnp.concatenate(rows, axis=0).T


def _pick(a, values):
    out = values[0]
    for j, v in enumerate(values[1:], 1):
        out = jnp.where(a == j, v, out)
    return out


def _to_scan(x, cols, kind, tc):
    nb, t, _ = x.shape
    gw = GROUP_WIDTH
    blk = TIME_BLOCK
    n_all, n_ctx = t // blk, tc // blk
    n_scan = 2 * nb * N_HEADS
    rep = LANES // n_scan
    col0, col1 = [c[0] for c in cols], [c[1] for c in cols]
    fwd = pl.BlockSpec((nb, blk, gw), lambda a, i: (0, i, _pick(a, col0)))
    bwd = pl.BlockSpec((nb, blk, gw), lambda a, i: (0, _rev_block(i, n_ctx, n_all), _pick(a, col1)))
    if kind == "k":
        out_shape = (len(cols), HEAD_DIM, t, LANES)
        out_spec = pl.BlockSpec((1, HEAD_DIM, blk, LANES), lambda a, i: (a, 0, i, 0))
    else:
        out_shape = (len(cols), t, HEAD_DIM // rep, LANES)
        out_spec = pl.BlockSpec((1, blk, HEAD_DIM // rep, LANES), lambda a, i: (a, i, 0, 0))
    return pl.pallas_call(
        functools.partial(_to_scan_body, kind=kind, nb=nb),
        grid=(len(cols), n_all),
        in_specs=[fwd, bwd],
        out_specs=out_spec,
        out_shape=jax.ShapeDtypeStruct(out_shape, F32),
        scratch_shapes=[pltpu.VMEM((blk // LANES, n_scan * HEAD_DIM, LANES), F32)],
        compiler_params=_cparams(("parallel", "parallel")),
        name="to_scan_" + kind,
    )(x, x)


def _from_scan_body(yf_ref, yb_ref, *rest, nb, readout):
    if readout == "rwkv":
        bonus_ref, g_ref, ng_ref, nb_ref, o_ref, r_ref = rest
    else:
        gate_ref, ng_ref, o_ref, r_ref = rest
    n_scan = 2 * nb * N_HEADS
    rep = LANES // n_scan
    n_piece = yf_ref.shape[1] // LANES
    for vh in range(HEAD_DIM // rep):
        for d, yt in enumerate((yf_ref[vh].T, yb_ref[vh].T)):
            for pc in range(n_piece):
                r_ref[d, pc, pl.ds(vh * LANES, LANES), :] = yt[:, pc * LANES:(pc + 1) * LANES]
    mean_block = _head_block(1.0 / HEAD_DIM).astype(BF16)
    for b in range(nb):
        slab = lambda d: jnp.concatenate(
            [jnp.concatenate([r_ref[d, pc, pl.ds((d * nb + b) * N_HEADS + h, HEAD_DIM, stride=n_scan), :]
                              for h in range(N_HEADS)], axis=0).T for pc in range(n_piece)], axis=0)
        y = slab(0) + _flip_rows(slab(1))
        if readout == "rwkv":
            y = y - _head_sum(y, mean_block)
            y = y * lax.rsqrt(_head_sum(y * y, mean_block) + RWKV_LNX_EPS) * ng_ref[...] + nb_ref[...]
            o_ref[b] = (y + bonus_ref[b]) * g_ref[b]
        else:
            y = y * lax.rsqrt(_head_sum(y * y, mean_block) + LN_EPS) * ng_ref[...]
            o_ref[b] = y * gate_ref[b]


def _from_scan(y, nb, tc, readout, feats, cols, vecs):
    n_vh, t, _ = y.shape
    gw = GROUP_WIDTH
    n_all, n_ctx = t // TIME_BLOCK, tc // TIME_BLOCK
    fwd = pl.BlockSpec((n_vh, TIME_BLOCK, LANES), lambda i: (0, i, 0))
    bwd = pl.BlockSpec((n_vh, TIME_BLOCK, LANES), lambda i: (0, _rev_block(i, n_ctx, n_all), 0))
    tok = lambda c: pl.BlockSpec((nb, TIME_BLOCK, gw), lambda i: (0, i, c))
    vec = pl.BlockSpec((1, gw), lambda i: (0, 0))
    return pl.pallas_call(
        functools.partial(_from_scan_body, nb=nb, readout=readout),
        grid=(n_all,),
        in_specs=[fwd, bwd] + [tok(c) for c in cols] + [vec] * len(vecs),
        out_specs=tok(0),
        out_shape=jax.ShapeDtypeStruct((nb, t, gw), F32),
        scratch_shapes=[pltpu.VMEM((2, TIME_BLOCK // LANES, n_vh * LANES, LANES), F32)],
        compiler_params=_cparams(("parallel",)),
        name="from_scan_" + readout,
    )(y, y, *([feats] * len(cols)), *[v.reshape(1, gw) for v in vecs])


N_ACC = 4


def _acc_add(acc, i, x):
    acc[i % N_ACC] = x if acc[i % N_ACC] is None else acc[i % N_ACC] + x


def _acc_total(acc):
    return (acc[0] + acc[1]) + (acc[2] + acc[3])


def _scan_body(*refs, mode, tb, n_vh):
    if mode == "rwkv":
        r_ref, w_ref, k_ref, a_ref, b_ref, v_ref, y_ref, s_ref = refs
    else:
        r_ref, w_ref, k_ref, v_ref, y_ref, s_ref = refs

    @pl.when(pl.program_id(0) == 0)
    def _():
        s_ref[...] = jnp.zeros_like(s_ref)

    row = lambda ref, k, j: ref[k, pl.ds(j, 1), :]
    wide = lambda x: jnp.broadcast_to(x, (n_vh, LANES))

    def store_y(j, y):
        for vh in range(n_vh):
            y_ref[vh, pl.ds(j, 1), :] = y[vh:vh + 1, :]

    def rwkv_step(j, sa):
        v = v_ref[j]
        j_next = jnp.minimum(j + 1, tb - 1)
        ys, sas = [None] * N_ACC, [None] * N_ACC
        for k in range(HEAD_DIM):
            s = s_ref[k] * wide(row(w_ref, k, j)) + sa * wide(row(b_ref, k, j)) + v * wide(row(k_ref, k, j))
            s_ref[k] = s
            _acc_add(ys, k, s * wide(row(r_ref, k, j)))
            _acc_add(sas, k, s * wide(row(a_ref, k, j_next)))
        store_y(j, _acc_total(ys))
        return _acc_total(sas)

    def gla_step(j, carry):
        v = v_ref[j]
        ys = [None] * N_ACC
        for k in range(HEAD_DIM):
            s = s_ref[k] * wide(row(w_ref, k, j)) + v * wide(row(k_ref, k, j))
            s_ref[k] = s
            _acc_add(ys, k, s * wide(row(r_ref, k, j)))
        store_y(j, _acc_total(ys))
        return carry

    if mode == "rwkv":
        sa0 = [None] * N_ACC
        for k in range(HEAD_DIM):
            _acc_add(sa0, k, s_ref[k] * wide(row(a_ref, k, 0)))
        lax.fori_loop(0, tb, rwkv_step, _acc_total(sa0), unroll=8)
    else:
        lax.fori_loop(0, tb, gla_step, 0, unroll=8)


def _scan(mode, k_in, v_in):
    _, t, n_vh, _ = v_in.shape
    tb = SCAN_TB
    big = lambda a: pl.BlockSpec((None, HEAD_DIM, tb, LANES), lambda i: (a, 0, i, 0))
    small = pl.BlockSpec((None, tb, n_vh, LANES), lambda i: (0, i, 0, 0))
    n = k_in.shape[0]
    return pl.pallas_call(
        functools.partial(_scan_body, mode=mode, tb=tb, n_vh=n_vh),
        grid=(t // tb,),
        in_specs=[big(a) for a in range(n)] + [small],
        out_specs=pl.BlockSpec((n_vh, tb, LANES), lambda i: (0, i, 0)),
        out_shape=jax.ShapeDtypeStruct((n_vh, t, LANES), F32),
        scratch_shapes=[pltpu.VMEM((HEAD_DIM, n_vh, LANES), F32)],
        compiler_params=_cparams(("arbitrary",)),
        name="scan_" + mode,
    )(*([k_in] * n), v_in)


MLSTM_CHUNK = 64


def _prefix_max(x, reverse):
    n = x.shape[0]
    row = lax.broadcasted_iota(jnp.int32, x.shape, 0)
    sh = 1
    while sh < n:
        if reverse:
            x = jnp.where(row < n - sh, jnp.maximum(x, pltpu.roll(x, n - sh, 0)), x)
        else:
            x = jnp.where(row >= sh, jnp.maximum(x, pltpu.roll(x, sh, 0)), x)
        sh *= 2
    return x


def _mlstm_body(qk_ref, qkp_ref, qkn_ref, v_ref, gt_ref, cw_ref, cb_ref, sel_ref, gb_ref, o_ref, st_ref, m_ref, *,
                reverse, n_sub, n_ctx, n_blocks):
    c, gw = MLSTM_CHUNK, GROUP_WIDTH
    i = pl.program_id(1)

    @pl.when(i == 0)
    def _():
        st_ref[...] = jnp.zeros_like(st_ref)
        m_ref[...] = jnp.zeros_like(m_ref)

    blk = _rev_block(i, n_ctx, n_blocks) if reverse else i
    seg_start = jnp.logical_or(blk == 0, blk == n_ctx)
    seg_end = jnp.logical_or(blk == n_ctx - 1, blk == n_blocks - 1)
    qk = qk_ref[0]
    prev, nxt = _shifted_rows(qk, qkp_ref[0], qkn_ref[0], seg_start, seg_end)
    qk = prev * cw_ref[0:1, :] + qk * cw_ref[1:2, :] + nxt * cw_ref[2:3, :] + cb_ref[...]
    qk = qk * jax.nn.sigmoid(qk)
    q_all, k_all = qk[:, :gw], qk[:, gw:] * HEAD_DIM ** -0.5
    v_all = v_ref[0]
    gt = gt_ref[0]
    ig_all = _dot01_right(gt, sel_ref[0]) + gb_ref[0:1, :]
    fg_all = -_softplus(-(_dot01_right(gt, sel_ref[1]) + gb_ref[1:2, :]))

    tt = lax.broadcasted_iota(jnp.int32, (c, c), 0)
    ss = lax.broadcasted_iota(jnp.int32, (c, c), 1)
    tri = jnp.where((ss >= tt) if reverse else (ss <= tt), 1.0, 0.0).astype(F32)
    row = lax.broadcasted_iota(jnp.int32, (c, gw), 0)
    s_of_lane = lax.broadcasted_iota(jnp.int32, (c, gw), 1) % c
    causal = (s_of_lane >= row) if reverse else (s_of_lane <= row)
    diag = s_of_lane == row
    block = _head_block()
    block2 = jnp.concatenate([block, block], axis=1)
    ones8 = jnp.ones((SUBLANES, c), F32)
    nt = (((1,), (1,)), ((), ()))
    last = 0 if reverse else c - 1
    for u in (range(n_sub - 1, -1, -1) if reverse else range(n_sub)):
        sl = slice(u * c, (u + 1) * c)
        q, k, v, ig, fg = q_all[sl], k_all[sl], v_all[sl], ig_all[sl], fg_all[sl]
        b = _dot01_left(tri, fg)
        g = ig - b
        m_prev = m_ref[0:1, :]
        m_t = b + jnp.maximum(m_prev, _prefix_max(g, reverse))
        w_inter = jnp.exp(b + m_prev - m_t)
        g_row = _dot01_left(ones8, jnp.where(diag, g, 0.0))[0:1]
        qb = q.astype(BF16)
        k_bd = (jnp.concatenate([k] * N_HEADS, axis=0) * block).astype(BF16)
        scores = lax.dot_general(qb, k_bd, nt, preferred_element_type=F32)
        w = jnp.where(causal, jnp.exp((b - m_t) + g_row), 0.0) * scores
        v_bd = jnp.concatenate([jnp.concatenate([v] * N_HEADS, axis=0) * block, block], axis=1).astype(BF16)
        intra = jnp.dot(w.astype(BF16), v_bd, preferred_element_type=F32)
        inter = jnp.dot(qb, st_ref[...].astype(BF16), preferred_element_type=F32)
        num = w_inter * inter[:, :gw] + intra[:, :gw]
        den = w_inter * inter[:, gw:] + intra[:, gw:]
        floor = jnp.exp(jnp.minimum(-m_t, MAX_NEG_LOG_STAB))
        o_ref[0, pl.ds(u * c, c), :] = num / jnp.maximum(jnp.abs(den), floor)
        m_new = m_t[last:last + 1]
        b_end = b[last:last + 1]
        kw = k * jnp.exp(b_end - b + ig - m_new)
        decay = jnp.exp(b_end + m_prev - m_new)
        v_one = jnp.concatenate([v, jnp.ones_like(v)], axis=1).astype(BF16)
        upd = jnp.dot(kw.T.astype(BF16), v_one, preferred_element_type=F32)
        st_ref[...] = st_ref[...] * jnp.concatenate([decay, decay], axis=1) + upd * block2
        m_ref[0:1, :] = m_new


def _mlstm_chunked(p, conv_w, conv_b, gate_b, reverse, tc):
    b, t, _ = p.shape
    gw = GROUP_WIDTH
    n_blocks, n_ctx = t // ROW_TILE, tc // ROW_TILE
    sub, n_sub8 = ROW_TILE // SUBLANES, t // SUBLANES
    blk = (lambda i: _rev_block(i, n_ctx, n_blocks)) if reverse else (lambda i: i)
    d = 1 if reverse else 0
    lane_head = np.arange(gw) // HEAD_DIM
    sel = np.zeros((2, LANES, gw), np.float32)
    for j, base in enumerate((d * N_HEADS, (2 + d) * N_HEADS)):
        sel[j, base + lane_head, np.arange(gw)] = 1.0
    gate_bias = jnp.stack([jnp.repeat(gate_b[d], HEAD_DIM), jnp.repeat(gate_b[2 + d], HEAD_DIM)], axis=0)
    full = lambda a: pl.BlockSpec(a.shape, lambda bi, i: (0,) * a.ndim)
    cb = conv_b.reshape(1, 2 * gw)
    return pl.pallas_call(
        functools.partial(_mlstm_body, reverse=reverse, n_sub=ROW_TILE // MLSTM_CHUNK, n_ctx=n_ctx,
                          n_blocks=n_blocks),
        grid=(b, n_blocks),
        in_specs=[pl.BlockSpec((1, ROW_TILE, 2 * gw), lambda bi, i: (bi, blk(i), 0)),
                  pl.BlockSpec((1, SUBLANES, 2 * gw), lambda bi, i: (bi, jnp.maximum(blk(i) * sub - 1, 0), 0)),
                  pl.BlockSpec((1, SUBLANES, 2 * gw), lambda bi, i: (bi, jnp.minimum((blk(i) + 1) * sub, n_sub8 - 1), 0)),
                  pl.BlockSpec((1, ROW_TILE, gw), lambda bi, i: (bi, blk(i), 2)),
                  pl.BlockSpec((1, ROW_TILE, LANES), lambda bi, i: (bi, blk(i), 4 * gw // LANES)),
                  full(conv_w), full(cb), full(sel), full(gate_bias)],
        out_specs=pl.BlockSpec((1, ROW_TILE, gw), lambda bi, i: (bi, blk(i), 0)),
        out_shape=jax.ShapeDtypeStruct((b, t, gw), F32),
        scratch_shapes=[pltpu.VMEM((gw, 2 * gw), F32), pltpu.VMEM((SUBLANES, gw), F32)],
        compiler_params=_cparams(("parallel", "arbitrary")),
        name="mlstm_chunk",
    )(p, p, p, p, p, conv_w, cb, jnp.asarray(sel), gate_bias)


def _attn_body(q_ref, k_ref, v_ref, lam_ref, g_ref, o_ref, *, out_scale, ctx_tiles, n_ctx_keys):
    def attend(n_keys):
        q = q_ref[0]
        k = k_ref[0, :n_keys, :]
        v = v_ref[0, :n_keys, :].astype(BF16)
        lane = lax.broadcasted_iota(jnp.int32, q.shape, 1)
        comp = lane // DIFF_QK_DIM
        nt = (((1,), (1,)), ((), ()))
        n_maps = LANES // DIFF_QK_DIM
        ss = [lax.dot_general(jnp.where(comp == c, q, 0.0).astype(BF16), k, nt, preferred_element_type=F32)
              for c in range(n_maps)]
        es = [jnp.exp2(s - jnp.max(s, axis=-1, keepdims=True)) for s in ss]
        ls = [jnp.sum(e, axis=-1, keepdims=True) for e in es]
        ys = []
        for hh in range(LANES // HEAD_DIM):
            e0, e1, l0, l1 = es[2 * hh], es[2 * hh + 1], ls[2 * hh], ls[2 * hh + 1]
            pr = (e0 - e1 * (lam_ref[0:1, 0:1] * l0 / l1)).astype(BF16)
            ys.append(jnp.dot(pr, v, preferred_element_type=F32) * (1.0 / l0))
        first = lane < HEAD_DIM
        y = jnp.where(first, ys[0], ys[1])
        ysq = y * y
        ms = jnp.where(first, jnp.sum(jnp.where(first, ysq, 0.0), axis=-1, keepdims=True),
                       jnp.sum(jnp.where(first, 0.0, ysq), axis=-1, keepdims=True)) * (1.0 / HEAD_DIM)
        o_ref[0] = y * lax.rsqrt(ms + LN_EPS) * g_ref[...] * out_scale

    if ctx_tiles == 0:
        attend(k_ref.shape[1])
    else:
        is_ctx = pl.program_id(2) < ctx_tiles
        pl.when(is_ctx)(lambda: attend(n_ctx_keys))
        pl.when(jnp.logical_not(is_ctx))(lambda: attend(k_ref.shape[1]))


def _attention(q, k, v, v_col0, lam, norm_g, out_scale, q_tile0, tc):
    b, t, gw = q.shape
    n_tiles = t // ATTN_TILE - q_tile0
    ctx_tiles = max(tc // ATTN_TILE - q_tile0, 0)
    qspec = pl.BlockSpec((1, ATTN_TILE, LANES), lambda bi, pi, i: (bi, q_tile0 + i, pi))
    kspec = pl.BlockSpec((1, t, LANES), lambda bi, pi, i: (bi, 0, pi))
    vspec = pl.BlockSpec((1, t, LANES), lambda bi, pi, i: (bi, 0, v_col0 + pi))
    vec = pl.BlockSpec((1, LANES), lambda bi, pi, i: (0, 0))
    return pl.pallas_call(
        functools.partial(_attn_body, out_scale=out_scale, ctx_tiles=ctx_tiles, n_ctx_keys=tc),
        grid=(b, gw // LANES, n_tiles),
        in_specs=[qspec, kspec, vspec, vec, vec],
        out_specs=pl.BlockSpec((1, ATTN_TILE, LANES), lambda bi, pi, i: (bi, i, pi)),
        out_shape=jax.ShapeDtypeStruct((b, n_tiles * ATTN_TILE, gw), F32),
        compiler_params=_cparams(("parallel", "parallel", "parallel")),
        name="diff_attn",
    )(q, k, v, jnp.full((1, LANES), lam, F32), jnp.tile(norm_g.reshape(1, HEAD_DIM), (1, LANES // HEAD_DIM)))


def _layernorm(z, g, b):
    z = z - jnp.mean(z, axis=-1, keepdims=True)
    return z * lax.rsqrt(jnp.mean(z * z, axis=-1, keepdims=True) + LN_EPS) * g + b


def _outproj_body(ya_ref, yb_ref, yc_ref, hf_ref, hb_ref, og_ref, ng_ref, h_ref, gl_ref, gc_ref, w_ref, g_ref,
                  b_ref, o_ref, *, ctx_tiles, alpha):
    is_ctx = pl.program_id(1) < ctx_tiles
    gate = jnp.where(is_ctx, gc_ref[0], gl_ref[0])
    mean_block = _head_block(1.0 / HEAD_DIM).astype(BF16)
    yd = hf_ref[0] + hb_ref[0]
    yd = yd - _head_sum(yd, mean_block)
    yd = yd * lax.rsqrt(_head_sum(yd * yd, mean_block) + LN_EPS) * ng_ref[...] * jax.nn.sigmoid(og_ref[0])
    y = None
    for m, ym in enumerate((ya_ref[0], yb_ref[0], yc_ref[0], yd)):
        part = jnp.dot(ym.astype(BF16), w_ref[m * GROUP_WIDTH:(m + 1) * GROUP_WIDTH, :],
                       preferred_element_type=F32)
        y = part if y is None else y + part
    o_ref[0] = _layernorm(alpha * h_ref[0] + gate * y, g_ref[...], b_ref[...])


def _outproj(ya, yb, yc, hf, hb, p_mlstm, mlstm_norm_g, h, gate_l, gate_c, w, ln_g, ln_b, tile0, ctx_tiles, alpha):
    b, t, d = h.shape
    gw = GROUP_WIDTH
    n_tiles = t // ROW_TILE - tile0
    yrow = lambda c: pl.BlockSpec((1, ROW_TILE, gw), lambda bi, i: (bi, tile0 + i, c))
    yb_row = pl.BlockSpec((1, ROW_TILE, gw), lambda bi, i: (bi, i, 0))
    hrow = pl.BlockSpec((1, ROW_TILE, d), lambda bi, i: (bi, tile0 + i, 0))
    lat = pl.BlockSpec((1, 1, d), lambda bi, i: (bi, 0, 0))
    ctx = pl.BlockSpec((1, 1, d), lambda bi, i: (0, 0, 0))
    vec = lambda n: pl.BlockSpec((1, n), lambda bi, i: (0, 0))
    return pl.pallas_call(
        functools.partial(_outproj_body, ctx_tiles=ctx_tiles, alpha=alpha),
        grid=(b, n_tiles),
        in_specs=[yrow(0), yb_row, yrow(0), yrow(0), yrow(0), yrow(3), vec(gw), hrow, lat, ctx,
                  pl.BlockSpec(w.shape, lambda bi, i: (0, 0)), vec(d), vec(d)],
        out_specs=pl.BlockSpec((1, ROW_TILE, d), lambda bi, i: (bi, i, 0)),
        out_shape=jax.ShapeDtypeStruct((b, n_tiles * ROW_TILE, d), F32),
        compiler_params=_cparams(("parallel", "parallel")),
        name="outproj_ln",
    )(ya, yb, yc, hf, hb, p_mlstm, mlstm_norm_g.reshape(1, gw), h, gate_l, gate_c, w, ln_g.reshape(1, d),
      ln_b.reshape(1, d))


FFN_CHUNKS = 1


def _ffn_body(x_ref, xp_ref, xn_ref, sl_ref, cl_ref, gl_ref, sc_ref, cc_ref, gc_ref,
              wu_ref, cw_ref, cb_ref, wd_ref, g_ref, b_ref, o_ref, *, ctx_tiles, n_tiles, alpha, d_ff):
    i = pl.program_id(1)
    is_ctx = i < ctx_tiles
    shift = jnp.where(is_ctx, sc_ref[0], sl_ref[0])
    scale = jnp.where(is_ctx, cc_ref[0], cl_ref[0])
    gate_mod = jnp.where(is_ctx, gc_ref[0], gl_ref[0])
    seg_start = jnp.logical_or(i == 0, i == ctx_tiles)
    seg_end = jnp.logical_or(i == ctx_tiles - 1, i == n_tiles - 1)

    x = x_ref[0]
    xm = (x * (1.0 + scale) + shift).astype(BF16)
    xp = (xp_ref[0] * (1.0 + scale) + shift).astype(BF16)
    xn = (xn_ref[0] * (1.0 + scale) + shift).astype(BF16)
    rows = x.shape[0]
    row_id = lax.broadcasted_iota(jnp.int32, (rows, 1), 0)
    ch = d_ff // FFN_CHUNKS
    f = jnp.zeros(x.shape, F32)
    for c in range(FFN_CHUNKS):
        wg = wu_ref[:, c * ch:(c + 1) * ch]
        wv = wu_ref[:, d_ff + c * ch:d_ff + (c + 1) * ch]
        gate = jnp.dot(xm, wg, preferred_element_type=F32)
        val = jnp.dot(xm, wv, preferred_element_type=F32)
        gp = jnp.dot(xp, wg, preferred_element_type=F32)[SUBLANES - 1:SUBLANES, :]
        gn = jnp.dot(xn, wg, preferred_element_type=F32)[0:1, :]
        gp = jnp.where(seg_start, 0.0, gp)
        gn = jnp.where(seg_end, 0.0, gn)
        prev = jnp.where(row_id == 0, gp, pltpu.roll(gate, 1, 0))
        nxt = jnp.where(row_id == rows - 1, gn, pltpu.roll(gate, rows - 1, 0))
        cw = cw_ref[:, c * ch:(c + 1) * ch]
        conv = prev * cw[0:1, :] + gate * cw[1:2, :] + nxt * cw[2:3, :] + cb_ref[:, c * ch:(c + 1) * ch]
        act = (jax.nn.gelu(conv) * val).astype(BF16)
        f = f + jnp.dot(act, wd_ref[c * ch:(c + 1) * ch, :], preferred_element_type=F32)
    o_ref[0] = _layernorm(alpha * x + gate_mod * f, g_ref[...], b_ref[...])


def _ffn(h, mods_l, mods_c, w_up, conv_w, conv_b, w_down, ln_g, ln_b, ctx_tiles, alpha):
    b, t, d = h.shape
    d_ff = w_down.shape[0]
    tile = 2 * ROW_TILE if ctx_tiles == 0 and t % (2 * ROW_TILE) == 0 else ROW_TILE
    n_tiles = t // tile
    sub = tile // SUBLANES
    n_sub = t // SUBLANES
    row = pl.BlockSpec((1, tile, d), lambda bi, i: (bi, i, 0))
    prev = pl.BlockSpec((1, SUBLANES, d), lambda bi, i: (bi, jnp.maximum(i * sub - 1, 0), 0))
    nxt = pl.BlockSpec((1, SUBLANES, d), lambda bi, i: (bi, jnp.minimum((i + 1) * sub, n_sub - 1), 0))
    lat = pl.BlockSpec((1, 1, d), lambda bi, i: (bi, 0, 0))
    ctx = pl.BlockSpec((1, 1, d), lambda bi, i: (0, 0, 0))
    vec = pl.BlockSpec((1, d), lambda bi, i: (0, 0))
    full = lambda a: pl.BlockSpec(a.shape, lambda bi, i: (0,) * a.ndim)
    cb = conv_b.reshape(1, d_ff)
    return pl.pallas_call(
        functools.partial(_ffn_body, ctx_tiles=ctx_tiles, n_tiles=n_tiles, alpha=alpha, d_ff=d_ff),
        grid=(b, n_tiles),
        in_specs=[row, prev, nxt, lat, lat, lat, ctx, ctx, ctx,
                  full(w_up), full(conv_w), full(cb), full(w_down), vec, vec],
        out_specs=row,
        out_shape=jax.ShapeDtypeStruct((b, t, d), F32),
        compiler_params=_cparams(("parallel", "parallel")),
        name="ffn",
    )(h, h, h, *mods_l, *mods_c, w_up, conv_w, cb, w_down, ln_g.reshape(1, d), ln_b.reshape(1, d))


def _rwkv7(f, tc, lnx_g, lnx_b):
    b = f.shape[0]
    col = RWKV_FEATS.index
    pairs = [("r", "r"), ("w0", "w1"), ("k0", "k1"), ("a", "a"), ("b0", "b1")]
    y = _scan("rwkv", _to_scan(f, [(col(p), col(q)) for p, q in pairs], "k", tc),
              _to_scan(f, [(col("v"), col("v"))], "v", tc))
    return _from_scan(y, b, tc, "rwkv", f, (col("bonus"), col("g")), (lnx_g, lnx_b))


def _rope_tables(n_ctx, n_lat):
    rows = n_lat // GRID_W
    row = jnp.repeat(jnp.arange(rows, dtype=F32), GRID_W)
    col = jnp.tile(jnp.arange(GRID_W, dtype=F32), rows)
    n_freq = DIFF_QK_DIM // 4
    inv_freq = ROPE_BASE ** (-jnp.arange(n_freq, dtype=F32) / n_freq)
    ang = jnp.concatenate([row[:, None] * inv_freq, col[:, None] * inv_freq], axis=-1)
    ang = jnp.concatenate([jnp.zeros((n_ctx, ang.shape[1]), F32), ang], axis=0)
    half = DIFF_QK_DIM // 2
    lane = np.arange(GROUP_WIDTH)
    sign = np.where(lane % DIFF_QK_DIM < half, -1.0, 1.0).astype(np.float32)
    return jnp.cos(ang)[:, lane % half], jnp.sin(ang)[:, lane % half] * sign


def _diff_attn(q, k, v, tc, lam_vecs, norm_g, layer, need_ctx):
    lam_init = 0.8 - 0.6 * math.exp(-0.3 * layer)
    lv = lam_vecs.astype(F32)
    lam = jnp.exp(jnp.sum(lv[0] * lv[1])) - jnp.exp(jnp.sum(lv[2] * lv[3])) + lam_init
    q_tile0 = 0 if need_ctx else tc // ATTN_TILE
    return _attention(q, k, v, 0, lam, norm_g, 1.0 - lam_init, q_tile0, tc)


def _hgrn2(f, tc, norm_g):
    b = f.shape[0]
    col = HGRN_FEATS.index
    pairs = [("q", "q"), ("w0", "w1"), ("k0", "k1")]
    o = _scan("gla", _to_scan(f, [(col(p), col(q)) for p, q in pairs], "k", tc),
              _to_scan(f, [(col("v"), col("v"))], "v", tc))
    return _from_scan(o, b, tc, "gla", f, (col("gate"),), (jnp.tile(norm_g, N_HEADS),))


RWKV_COLS = 3 * GROUP_WIDTH + 2 * RWKV_DECAY_LORA + 2 * RWKV_ICLR_LORA + RWKV_GATE_LORA
DIFF_COLS = 3 * GROUP_WIDTH
HGRN_COLS = 5 * GROUP_WIDTH
MLSTM_COLS = 4 * GROUP_WIDTH + 4 * N_HEADS
SLAB_COLS = (RWKV_COLS, DIFF_COLS, HGRN_COLS, MLSTM_COLS)
SLAB_PAD = (5 * GROUP_WIDTH, DIFF_COLS, HGRN_COLS, 4 * GROUP_WIDTH + LANES)


def _split_cols(w):
    parts, off = [], 0
    for n, n_pad in zip(SLAB_COLS, SLAB_PAD):
        parts.append(jnp.pad(w[:, off:off + n], ((0, 0), (0, n_pad - n))).astype(BF16))
        off += n
    return parts


def kernel(x, c, ctx, c_ctx, ada_w, ada_b, w_in, rwkv_mu, rwkv_w0, rwkv_w2, rwkv_a0, rwkv_a2, rwkv_g2, rwkv_k_k, rwkv_k_a, rwkv_r_k, rwkv_lnx_g, rwkv_lnx_b, diff_lambda, diff_norm_g, hgrn_lb_logits, hgrn_norm_g, mlstm_conv_w, mlstm_conv_b, mlstm_gate_b, mlstm_norm_g, w_out, ffn_w_up, ffn_conv_w, ffn_conv_b, ffn_w_down, ln_g, ln_b):
    depth = w_in.shape[0]
    b, seq, d = x.shape
    tc = ctx.shape[1]
    assert tc == ROW_TILE == ATTN_TILE and seq % ROW_TILE == 0 and 2 * b * N_HEADS <= LANES
    alpha = (2.0 * depth) ** 0.25

    lb_w = jax.nn.softmax(hgrn_lb_logits.astype(F32), axis=0)
    lower_bounds = jnp.cumsum(lb_w, axis=0) - lb_w[0]

    rope = _rope_tables(tc, seq)
    cond = jnp.zeros((8, d), F32).at[:b].set(c).at[b].set(c_ctx)
    h = jnp.concatenate([ctx, x], axis=1)
    for layer in range(depth):
        last = layer == depth - 1
        mod = _adaln(cond, ada_w, ada_b, layer).reshape(8, 6, d)
        mods_l = [mod[:b, j][:, None, :] for j in range(6)]
        mods_c = [mod[b:b + 1, j][:, None, :] for j in range(6)]

        rwkv_prm = (rwkv_mu[layer], rwkv_w0[layer], rwkv_w2[layer], rwkv_a0[layer], rwkv_a2[layer], rwkv_g2[layer],
                    rwkv_k_k[layer], rwkv_k_a[layer], rwkv_r_k[layer])
        f_rwkv, q, k, v, f_hgrn, p_mlstm = _inproj(h, mods_l[0], mods_l[1], mods_c[0], mods_c[1], rope,
                                                   _split_cols(w_in[layer]), rwkv_prm, lower_bounds[layer],
                                                   tc // ROW_TILE)
        y_a = _rwkv7(f_rwkv, tc, rwkv_lnx_g[layer], rwkv_lnx_b[layer])
        y_b = _diff_attn(q, k, v, tc, diff_lambda[layer], diff_norm_g[layer], layer, not last)
        y_c = _hgrn2(f_hgrn, tc, hgrn_norm_g[layer])
        h_f, h_b = (_mlstm_chunked(p_mlstm, mlstm_conv_w[layer], mlstm_conv_b[layer], mlstm_gate_b[layer], rev, tc)
                    for rev in (False, True))

        tile0, ctx_tiles = (tc // ROW_TILE, 0) if last else (0, tc // ROW_TILE)
        h = _outproj(y_a, y_b, y_c, h_f, h_b, p_mlstm, mlstm_norm_g[layer], h, mods_l[2], mods_c[2],
                     w_out[layer].astype(BF16), ln_g[layer, 0], ln_b[layer, 0], tile0, ctx_tiles, alpha)
        h = _ffn(h, mods_l[3:6], mods_c[3:6], ffn_w_up[layer].astype(BF16), ffn_conv_w[layer], ffn_conv_b[layer],
                 ffn_w_down[layer].astype(BF16), ln_g[layer, 1], ln_b[layer, 1], ctx_tiles, alpha)
    return h
```

```python
import functools
import math

import jax
import jax.numpy as jnp
import numpy as np
from jax import lax
from jax.experimental import pallas as pl
from jax.experimental.pallas import tpu as pltpu

F32 = jnp.float32
BF16 = jnp.bfloat16

HEAD_DIM = 64
N_HEADS = 4
GROUP_WIDTH = N_HEADS * HEAD_DIM
DIFF_QK_DIM = HEAD_DIM // 2
GRID_W = 64
ROPE_BASE = 10000.0
RWKV_DECAY_LORA = 64
RWKV_ICLR_LORA = 64
RWKV_GATE_LORA = 160
RWKV_LNX_EPS = 64e-5
LN_EPS = 1e-5
LB_FLOOR = 1e-30
MAX_NEG_LOG_STAB = 60.0

LANES = 128
SUBLANES = 8
ADALN_TILE = 1536
ROW_TILE = 256
ATTN_TILE = 256
TIME_BLOCK = 256
SCAN_TB = 128
VMEM_LIMIT = 56 * 1024 * 1024


def _cparams(sem):
    return pltpu.CompilerParams(dimension_semantics=sem, vmem_limit_bytes=VMEM_LIMIT)


def _rev_block(i, n_ctx, n_all):
    return jnp.where(i < n_ctx, n_ctx - 1 - i, n_ctx + n_all - 1 - i)


def _split3(x):
    hi = x.astype(BF16)
    r1 = x - hi.astype(F32)
    mid = r1.astype(BF16)
    return hi, mid, (r1 - mid.astype(F32)).astype(BF16)


def _dot01_left(a01, x):
    a = a01.astype(BF16)
    p = [jnp.dot(a, part, preferred_element_type=F32) for part in _split3(x)]
    return (p[0] + p[1]) + p[2]


def _dot01_right(x, b01):
    b = b01.astype(BF16)
    p = [jnp.dot(part, b, preferred_element_type=F32) for part in _split3(x)]
    return (p[0] + p[1]) + p[2]


def _flip_rows(x):
    n = x.shape[0]
    r = lax.broadcasted_iota(jnp.int32, (n, n), 0)
    c = lax.broadcasted_iota(jnp.int32, (n, n), 1)
    return _dot01_left(jnp.where(r + c == n - 1, 1.0, 0.0).astype(F32), x)


def _adaln_body(c_ref, w_ref, b_ref, o_ref):
    x = c_ref[...]
    x = (x * jax.nn.sigmoid(x)).astype(BF16)
    o_ref[...] = jnp.dot(x, w_ref[0].astype(BF16), preferred_element_type=F32) + b_ref[0]


def _adaln(cond, w, b, layer):
    m, d = cond.shape
    n = w.shape[2]
    tn = ADALN_TILE
    return pl.pallas_call(
        _adaln_body,
        grid=(n // tn,),
        in_specs=[pl.BlockSpec((m, d), lambda j: (0, 0)),
                  pl.BlockSpec((1, d, tn), lambda j: (layer, 0, j)),
                  pl.BlockSpec((1, 1, tn), lambda j: (layer, 0, j))],
        out_specs=pl.BlockSpec((m, tn), lambda j: (0, j)),
        out_shape=jax.ShapeDtypeStruct((m, n), F32),
        compiler_params=_cparams(("arbitrary",)),
        name="adaln",
    )(cond, w, b.reshape(b.shape[0], 1, n))


def _softplus(x):
    return jnp.maximum(x, 0.0) + jnp.log(1.0 + jnp.exp(-jnp.abs(x)))


def _head_block(scale=1.0):
    r = lax.broadcasted_iota(jnp.int32, (GROUP_WIDTH, GROUP_WIDTH), 0) // HEAD_DIM
    c = lax.broadcasted_iota(jnp.int32, (GROUP_WIDTH, GROUP_WIDTH), 1) // HEAD_DIM
    return jnp.where(r == c, scale, 0.0).astype(F32)


def _head_sum(x, block_bf16):
    hi = x.astype(BF16)
    lo = (x - hi.astype(F32)).astype(BF16)
    return (jnp.dot(hi, block_bf16, preferred_element_type=F32)
            + jnp.dot(lo, block_bf16, preferred_element_type=F32))


def _shifted_rows(x, prev8, next8, seg_start, seg_end):
    rows = x.shape[0]
    row_id = lax.broadcasted_iota(jnp.int32, (rows, 1), 0)
    prev_row = jnp.where(seg_start, 0.0, prev8[SUBLANES - 1:SUBLANES, :])
    next_row = jnp.where(seg_end, 0.0, next8[0:1, :])
    prev = jnp.where(row_id == 0, prev_row, pltpu.roll(x, 1, 0))
    nxt = jnp.where(row_id == rows - 1, next_row, pltpu.roll(x, rows - 1, 0))
    return prev, nxt


RWKV_FEATS = ("r", "v", "w0", "w1", "k0", "k1", "a", "b0", "b1", "g", "bonus")


def _rwkv_feats(p, prev8, next8, seg_start, seg_end, mu_ref, w0_ref, w2_ref, a0_ref, a2_ref, g2_ref, kk_ref,
                ka_ref, rk_ref):
    gw = GROUP_WIDTH
    prev, nxt = _shifted_rows(p, prev8, next8, seg_start, seg_end)
    p = p + (0.5 * (prev + nxt) - p) * mu_ref[...]
    r, k, v = p[:, :gw], p[:, gw:2 * gw], p[:, 2 * gw:3 * gw]
    wd, ad, gd = p[:, 3 * gw:3 * gw + LANES], p[:, 3 * gw + LANES:4 * gw], p[:, 4 * gw:5 * gw]
    lora_w = jnp.dot(jnp.tanh(wd).astype(BF16), w2_ref[...], preferred_element_type=F32)
    lora_a = jnp.dot(ad.astype(BF16), a2_ref[...], preferred_element_type=F32)
    decay = jnp.exp(-jnp.exp(-_softplus(-(w0_ref[...] + lora_w)) - 0.5))
    a = jax.nn.sigmoid(a0_ref[...] + lora_a)
    g = jnp.dot(jax.nn.sigmoid(gd).astype(BF16), g2_ref[...], preferred_element_type=F32)
    block = _head_block().astype(BF16)
    kk = k * kk_ref[...]
    kk = kk / jnp.maximum(jnp.sqrt(_head_sum(kk * kk, block)), 1e-12)
    k0 = k * (1.0 + (a[:, :gw] - 1.0) * ka_ref[...])
    k1 = k * (1.0 + (a[:, gw:] - 1.0) * ka_ref[...])
    bonus = _head_sum(r * (k0 + k1) * rk_ref[...], block) * v
    return dict(r=r, v=v, w0=decay[:, :gw], w1=decay[:, gw:], k0=k0, k1=k1, a=-kk, b0=kk * a[:, :gw],
                b1=kk * a[:, gw:], g=g, bonus=bonus)


def _block_diag2(w):
    z = jnp.zeros_like(w[0])
    return jnp.concatenate([jnp.concatenate([w[0], z], axis=1), jnp.concatenate([z, w[1]], axis=1)], axis=0)


HGRN_FEATS = ("q", "w0", "w1", "k0", "k1", "gate", "v")


def _hgrn_feats(p, llb_ref, l1m_ref, oml_ref):
    gw = GROUP_WIDTH
    q, f_f, f_b, v, g = (p[:, j * gw:(j + 1) * gw] for j in range(5))
    feats = dict(q=q * jax.nn.sigmoid(q), gate=g * jax.nn.sigmoid(g), v=v)
    for d, f in enumerate((f_f, f_b)):
        x = llb_ref[...]
        y = l1m_ref[...] - _softplus(-f)
        log_f = jnp.maximum(x, y) + jnp.log(1.0 + jnp.exp(-jnp.abs(x - y)))
        feats["w%d" % d] = jnp.exp(log_f)
        feats["k%d" % d] = oml_ref[...] * jax.nn.sigmoid(-f)
    return feats


N_RWKV_PRM, N_HGRN_PRM = 9, 3


def _rope(x, cos, sin_signed):
    half = DIFF_QK_DIM // 2
    lane = lax.broadcasted_iota(jnp.int32, x.shape, 1)
    n = x.shape[1]
    partner = jnp.where(lane % DIFF_QK_DIM < half, pltpu.roll(x, n - half, 1), pltpu.roll(x, half, 1))
    return x * cos + partner * sin_signed


def _inproj_body(x_ref, xp_ref, xn_ref, sl_ref, cl_ref, sc_ref, cc_ref, cos_ref, sin_ref, wr_ref, wd_ref, wh_ref,
                 wm_ref, *refs, ctx_tiles, n_tiles):
    rwkv_prm, hgrn_prm = refs[:N_RWKV_PRM], refs[N_RWKV_PRM:N_RWKV_PRM + N_HGRN_PRM]
    fr_ref, q_ref, k_ref, v_ref, fh_ref, pm_ref = refs[N_RWKV_PRM + N_HGRN_PRM:]
    i = pl.program_id(1)
    is_ctx = i < ctx_tiles
    seg_start = jnp.logical_or(i == 0, i == ctx_tiles)
    seg_end = jnp.logical_or(i == ctx_tiles - 1, i == n_tiles - 1)
    shift = jnp.where(is_ctx, sc_ref[0], sl_ref[0])
    scale = jnp.where(is_ctx, cc_ref[0], cl_ref[0])
    mod = lambda x: (x * (1.0 + scale) + shift).astype(BF16)
    proj = lambda x, w_ref: jnp.dot(x, w_ref[...], preferred_element_type=F32)
    gw = GROUP_WIDTH
    xm = mod(x_ref[0])
    feats = _rwkv_feats(proj(xm, wr_ref), proj(mod(xp_ref[0]), wr_ref), proj(mod(xn_ref[0]), wr_ref),
                        seg_start, seg_end, *rwkv_prm)
    for j, name in enumerate(RWKV_FEATS):
        fr_ref[0, :, j * gw:(j + 1) * gw] = feats[name]
    pd = proj(xm, wd_ref)
    q_ref[0] = (_rope(pd[:, :gw], cos_ref[...], sin_ref[...])
                * (DIFF_QK_DIM ** -0.5 * math.log2(math.e))).astype(BF16)
    k_ref[0] = _rope(pd[:, gw:2 * gw], cos_ref[...], sin_ref[...]).astype(BF16)
    v_ref[0] = pd[:, 2 * gw:].astype(BF16)
    feats = _hgrn_feats(proj(xm, wh_ref), *hgrn_prm)
    for j, name in enumerate(HGRN_FEATS):
        fh_ref[0, :, j * gw:(j + 1) * gw] = feats[name]
    pm_ref[0] = proj(xm, wm_ref)


def _inproj(h, shift_l, scale_l, shift_c, scale_c, rope, ws, rwkv_prm, lb, ctx_tiles):
    b, t, d = h.shape
    gw = GROUP_WIDTH
    n_tiles, sub, n_sub = t // ROW_TILE, ROW_TILE // SUBLANES, t // SUBLANES
    row = lambda n: pl.BlockSpec((1, ROW_TILE, n), lambda bi, i: (bi, i, 0))
    prev = pl.BlockSpec((1, SUBLANES, d), lambda bi, i: (bi, jnp.maximum(i * sub - 1, 0), 0))
    nxt = pl.BlockSpec((1, SUBLANES, d), lambda bi, i: (bi, jnp.minimum((i + 1) * sub, n_sub - 1), 0))
    lat = pl.BlockSpec((1, 1, d), lambda bi, i: (bi, 0, 0))
    ctx = pl.BlockSpec((1, 1, d), lambda bi, i: (0, 0, 0))
    full = lambda a: pl.BlockSpec(a.shape, lambda bi, i: (0,) * a.ndim)
    vec = lambda a: a.reshape(1, -1).astype(F32)
    mu, w0, w2, a0, a2, g2, k_k, k_a, r_k = rwkv_prm
    cols = ws[0].shape[1]
    prm = [jnp.pad(vec(mu), ((0, 0), (0, cols - mu.shape[0]))), vec(w0), _block_diag2(w2).astype(BF16), vec(a0),
           _block_diag2(a2).astype(BF16), jnp.pad(g2, ((0, gw - g2.shape[0]), (0, 0))).astype(BF16), vec(k_k),
           vec(k_a), vec(r_k),
           vec(jnp.log(jnp.maximum(lb, LB_FLOOR))), vec(jnp.log1p(-lb)), vec(1.0 - lb)]
    assert len(prm) == N_RWKV_PRM + N_HGRN_PRM
    outs = [(len(RWKV_FEATS) * gw, F32), (gw, BF16), (gw, BF16), (gw, BF16), (len(HGRN_FEATS) * gw, F32),
            (ws[3].shape[1], F32)]
    tab = pl.BlockSpec((ROW_TILE, gw), lambda bi, i: (i, 0))
    return pl.pallas_call(
        functools.partial(_inproj_body, ctx_tiles=ctx_tiles, n_tiles=n_tiles),
        grid=(b, n_tiles),
        in_specs=[row(d), prev, nxt, lat, lat, ctx, ctx, tab, tab] + [full(a) for a in list(ws) + prm],
        out_specs=[row(n) for n, _ in outs],
        out_shape=[jax.ShapeDtypeStruct((b, t, n), dt) for n, dt in outs],
        compiler_params=_cparams(("parallel", "parallel")),
        name="inproj",
    )(h, h, h, shift_l, scale_l, shift_c, scale_c, *rope, *ws, *prm)


def _to_scan_body(x0_ref, x1_ref, o_ref, r_ref, *, kind, nb):
    n_scan = 2 * nb * N_HEADS
    rep = LANES // n_scan
    n_piece = x0_ref.shape[1] // LANES
    for b in range(nb):
        for d, xt in enumerate((x0_ref[b].T, _flip_rows(x1_ref[b]).T)):
            for pc in range(n_piece):
                r_ref[pc, pl.ds((d * nb + b) * GROUP_WIDTH, GROUP_WIDTH), :] = xt[:, pc * LANES:(pc + 1) * LANES]
    for pc in range(n_piece):
        t_sl = slice(pc * LANES, (pc + 1) * LANES)
        if kind == "k":
            for k in range(HEAD_DIM):
                rows = r_ref[pc, pl.ds(k, n_scan, stride=HEAD_DIM), :]
                o_ref[0, k, t_sl, :] = jnp.concatenate([rows] * rep, axis=0).T
        else:
            for vh in range(HEAD_DIM // rep):
                rows = [r_ref[pc, pl.ds(vh * rep + vl, n_scan, stride=HEAD_DIM), :] for vl in range(rep)]
                o_ref[0, t_sl, vh, :] = jnp.concatenate(rows, axis=0).T


def _pick(a, values):
    out = values[0]
    for j, v in enumerate(values[1:], 1):
        out = jnp.where(a == j, v, out)
    return out


def _to_scan(x, cols, kind, tc):
    nb, t, _ = x.shape
    gw = GROUP_WIDTH
    blk = TIME_BLOCK
    n_all, n_ctx = t // blk, tc // blk
    n_scan = 2 * nb * N_HEADS
    rep = LANES // n_scan
    col0, col1 = [c[0] for c in cols], [c[1] for c in cols]
    fwd = pl.BlockSpec((nb, blk, gw), lambda a, i: (0, i, _pick(a, col0)))
    bwd = pl.BlockSpec((nb, blk, gw), lambda a, i: (0, _rev_block(i, n_ctx, n_all), _pick(a, col1)))
    if kind == "k":
        out_shape = (len(cols), HEAD_DIM, t, LANES)
        out_spec = pl.BlockSpec((1, HEAD_DIM, blk, LANES), lambda a, i: (a, 0, i, 0))
    else:
        out_shape = (len(cols), t, HEAD_DIM // rep, LANES)
        out_spec = pl.BlockSpec((1, blk, HEAD_DIM // rep, LANES), lambda a, i: (a, i, 0, 0))
    return pl.pallas_call(
        functools.partial(_to_scan_body, kind=kind, nb=nb),
        grid=(len(cols), n_all),
        in_specs=[fwd, bwd],
        out_specs=out_spec,
        out_shape=jax.ShapeDtypeStruct(out_shape, F32),
        scratch_shapes=[pltpu.VMEM((blk // LANES, n_scan * HEAD_DIM, LANES), F32)],
        compiler_params=_cparams(("parallel", "parallel")),
        name="to_scan_" + kind,
    )(x, x)


def _from_scan_body(yf_ref, yb_ref, *rest, nb, readout):
    if readout == "rwkv":
        bonus_ref, g_ref, ng_ref, nb_ref, o_ref, r_ref = rest
    else:
        gate_ref, ng_ref, o_ref, r_ref = rest
    n_scan = 2 * nb * N_HEADS
    rep = LANES // n_scan
    n_piece = yf_ref.shape[1] // LANES
    for vh in range(HEAD_DIM // rep):
        for d, yt in enumerate((yf_ref[vh].T, yb_ref[vh].T)):
            for pc in range(n_piece):
                r_ref[d, pc, pl.ds(vh * LANES, LANES), :] = yt[:, pc * LANES:(pc + 1) * LANES]
    mean_block = _head_block(1.0 / HEAD_DIM).astype(BF16)
    for b in range(nb):
        slab = lambda d: jnp.concatenate(
            [jnp.concatenate([r_ref[d, pc, pl.ds((d * nb + b) * N_HEADS + h, HEAD_DIM, stride=n_scan), :]
                              for h in range(N_HEADS)], axis=0).T for pc in range(n_piece)], axis=0)
        y = slab(0) + _flip_rows(slab(1))
        if readout == "rwkv":
            y = y - _head_sum(y, mean_block)
            y = y * lax.rsqrt(_head_sum(y * y, mean_block) + RWKV_LNX_EPS) * ng_ref[...] + nb_ref[...]
            o_ref[b] = (y + bonus_ref[b]) * g_ref[b]
        else:
            y = y * lax.rsqrt(_head_sum(y * y, mean_block) + LN_EPS) * ng_ref[...]
            o_ref[b] = y * gate_ref[b]


def _from_scan(y, nb, tc, readout, feats, cols, vecs):
    n_vh, t, _ = y.shape
    gw = GROUP_WIDTH
    n_all, n_ctx = t // TIME_BLOCK, tc // TIME_BLOCK
    fwd = pl.BlockSpec((n_vh, TIME_BLOCK, LANES), lambda i: (0, i, 0))
    bwd = pl.BlockSpec((n_vh, TIME_BLOCK, LANES), lambda i: (0, _rev_block(i, n_ctx, n_all), 0))
    tok = lambda c: pl.BlockSpec((nb, TIME_BLOCK, gw), lambda i: (0, i, c))
    vec = pl.BlockSpec((1, gw), lambda i: (0, 0))
    return pl.pallas_call(
        functools.partial(_from_scan_body, nb=nb, readout=readout),
        grid=(n_all,),
        in_specs=[fwd, bwd] + [tok(c) for c in cols] + [vec] * len(vecs),
        out_specs=tok(0),
        out_shape=jax.ShapeDtypeStruct((nb, t, gw), F32),
        scratch_shapes=[pltpu.VMEM((2, TIME_BLOCK // LANES, n_vh * LANES, LANES), F32)],
        compiler_params=_cparams(("parallel",)),
        name="from_scan_" + readout,
    )(y, y, *([feats] * len(cols)), *[v.reshape(1, gw) for v in vecs])


N_ACC = 4


def _acc_add(acc, i, x):
    acc[i % N_ACC] = x if acc[i % N_ACC] is None else acc[i % N_ACC] + x


def _acc_total(acc):
    return (acc[0] + acc[1]) + (acc[2] + acc[3])


def _scan_body(*refs, mode, tb, n_vh):
    if mode == "rwkv":
        r_ref, w_ref, k_ref, a_ref, b_ref, v_ref, y_ref, s_ref = refs
    else:
        r_ref, w_ref, k_ref, v_ref, y_ref, s_ref = refs

    @pl.when(pl.program_id(0) == 0)
    def _():
        s_ref[...] = jnp.zeros_like(s_ref)

    row = lambda ref, k, j: ref[k, pl.ds(j, 1), :]
    wide = lambda x: jnp.broadcast_to(x, (n_vh, LANES))

    def store_y(j, y):
        for vh in range(n_vh):
            y_ref[vh, pl.ds(j, 1), :] = y[vh:vh + 1, :]

    def rwkv_step(j, sa):
        v = v_ref[j]
        j_next = jnp.minimum(j + 1, tb - 1)
        ys, sas = [None] * N_ACC, [None] * N_ACC
        for k in range(HEAD_DIM):
            s = s_ref[k] * wide(row(w_ref, k, j)) + sa * wide(row(b_ref, k, j)) + v * wide(row(k_ref, k, j))
            s_ref[k] = s
            _acc_add(ys, k, s * wide(row(r_ref, k, j)))
            _acc_add(sas, k, s * wide(row(a_ref, k, j_next)))
        store_y(j, _acc_total(ys))
        return _acc_total(sas)

    def gla_step(j, carry):
        v = v_ref[j]
        ys = [None] * N_ACC
        for k in range(HEAD_DIM):
            s = s_ref[k] * wide(row(w_ref, k, j)) + v * wide(row(k_ref, k, j))
            s_ref[k] = s
            _acc_add(ys, k, s * wide(row(r_ref, k, j)))
        store_y(j, _acc_total(ys))
        return carry

    if mode == "rwkv":
        sa0 = [None] * N_ACC
        for k in range(HEAD_DIM):
            _acc_add(sa0, k, s_ref[k] * wide(row(a_ref, k, 0)))
        lax.fori_loop(0, tb, rwkv_step, _acc_total(sa0), unroll=8)
    else:
        lax.fori_loop(0, tb, gla_step, 0, unroll=8)


def _scan(mode, k_in, v_in):
    _, t, n_vh, _ = v_in.shape
    tb = SCAN_TB
    big = lambda a: pl.BlockSpec((None, HEAD_DIM, tb, LANES), lambda i: (a, 0, i, 0))
    small = pl.BlockSpec((None, tb, n_vh, LANES), lambda i: (0, i, 0, 0))
    n = k_in.shape[0]
    return pl.pallas_call(
        functools.partial(_scan_body, mode=mode, tb=tb, n_vh=n_vh),
        grid=(t // tb,),
        in_specs=[big(a) for a in range(n)] + [small],
        out_specs=pl.BlockSpec((n_vh, tb, LANES), lambda i: (0, i, 0)),
        out_shape=jax.ShapeDtypeStruct((n_vh, t, LANES), F32),
        scratch_shapes=[pltpu.VMEM((HEAD_DIM, n_vh, LANES), F32)],
        compiler_params=_cparams(("arbitrary",)),
        name="scan_" + mode,
    )(*([k_in] * n), v_in)


MLSTM_CHUNK = 64


def _prefix_max(x, reverse):
    n = x.shape[0]
    row = lax.broadcasted_iota(jnp.int32, x.shape, 0)
    sh = 1
    while sh < n:
        if reverse:
            x = jnp.where(row < n - sh, jnp.maximum(x, pltpu.roll(x, n - sh, 0)), x)
        else:
            x = jnp.where(row >= sh, jnp.maximum(x, pltpu.roll(x, sh, 0)), x)
        sh *= 2
    return x


def _mlstm_body(qk_ref, qkp_ref, qkn_ref, v_ref, gt_ref, cw_ref, cb_ref, sel_ref, gb_ref, o_ref, st_ref, m_ref, *,
                reverse, n_sub, n_ctx, n_blocks):
    c, gw = MLSTM_CHUNK, GROUP_WIDTH
    i = pl.program_id(1)

    @pl.when(i == 0)
    def _():
        st_ref[...] = jnp.zeros_like(st_ref)
        m_ref[...] = jnp.zeros_like(m_ref)

    blk = _rev_block(i, n_ctx, n_blocks) if reverse else i
    seg_start = jnp.logical_or(blk == 0, blk == n_ctx)
    seg_end = jnp.logical_or(blk == n_ctx - 1, blk == n_blocks - 1)
    qk = qk_ref[0]
    prev, nxt = _shifted_rows(qk, qkp_ref[0], qkn_ref[0], seg_start, seg_end)
    qk = prev * cw_ref[0:1, :] + qk * cw_ref[1:2, :] + nxt * cw_ref[2:3, :] + cb_ref[...]
    qk = qk * jax.nn.sigmoid(qk)
    q_all, k_all = qk[:, :gw], qk[:, gw:] * HEAD_DIM ** -0.5
    v_all = v_ref[0]
    gt = gt_ref[0]
    ig_all = _dot01_right(gt, sel_ref[0]) + gb_ref[0:1, :]
    fg_all = -_softplus(-(_dot01_right(gt, sel_ref[1]) + gb_ref[1:2, :]))

    tt = lax.broadcasted_iota(jnp.int32, (c, c), 0)
    ss = lax.broadcasted_iota(jnp.int32, (c, c), 1)
    tri = jnp.where((ss >= tt) if reverse else (ss <= tt), 1.0, 0.0).astype(F32)
    row = lax.broadcasted_iota(jnp.int32, (c, gw), 0)
    s_of_lane = lax.broadcasted_iota(jnp.int32, (c, gw), 1) % c
    causal = (s_of_lane >= row) if reverse else (s_of_lane <= row)
    diag = s_of_lane == row
    block = _head_block()
    block2 = jnp.concatenate([block, block], axis=1)
    ones8 = jnp.ones((SUBLANES, c), F32)
    nt = (((1,), (1,)), ((), ()))
    last = 0 if reverse else c - 1
    for u in (range(n_sub - 1, -1, -1) if reverse else range(n_sub)):
        sl = slice(u * c, (u + 1) * c)
        q, k, v, ig, fg = q_all[sl], k_all[sl], v_all[sl], ig_all[sl], fg_all[sl]
        b = _dot01_left(tri, fg)
        g = ig - b
        m_prev = m_ref[0:1, :]
        m_t = b + jnp.maximum(m_prev, _prefix_max(g, reverse))
        w_inter = jnp.exp(b + m_prev - m_t)
        g_row = _dot01_left(ones8, jnp.where(diag, g, 0.0))[0:1]
        qb = q.astype(BF16)
        k_bd = (jnp.concatenate([k] * N_HEADS, axis=0) * block).astype(BF16)
        scores = lax.dot_general(qb, k_bd, nt, preferred_element_type=F32)
        w = jnp.where(causal, jnp.exp((b - m_t) + g_row), 0.0) * scores
        v_bd = jnp.concatenate([jnp.concatenate([v] * N_HEADS, axis=0) * block, block], axis=1).astype(BF16)
        intra = jnp.dot(w.astype(BF16), v_bd, preferred_element_type=F32)
        inter = jnp.dot(qb, st_ref[...].astype(BF16), preferred_element_type=F32)
        num = w_inter * inter[:, :gw] + intra[:, :gw]
        den = w_inter * inter[:, gw:] + intra[:, gw:]
        floor = jnp.exp(jnp.minimum(-m_t, MAX_NEG_LOG_STAB))
        o_ref[0, pl.ds(u * c, c), :] = num / jnp.maximum(jnp.abs(den), floor)
        m_new = m_t[last:last + 1]
        b_end = b[last:last + 1]
        kw = k * jnp.exp(b_end - b + ig - m_new)
        decay = jnp.exp(b_end + m_prev - m_new)
        v_one = jnp.concatenate([v, jnp.ones_like(v)], axis=1).astype(BF16)
        upd = jnp.dot(kw.T.astype(BF16), v_one, preferred_element_type=F32)
        st_ref[...] = st_ref[...] * jnp.concatenate([decay, decay], axis=1) + upd * block2
        m_ref[0:1, :] = m_new


def _mlstm_chunked(p, conv_w, conv_b, gate_b, reverse, tc):
    b, t, _ = p.shape
    gw = GROUP_WIDTH
    n_blocks, n_ctx = t // ROW_TILE, tc // ROW_TILE
    sub, n_sub8 = ROW_TILE // SUBLANES, t // SUBLANES
    blk = (lambda i: _rev_block(i, n_ctx, n_blocks)) if reverse else (lambda i: i)
    d = 1 if reverse else 0
    lane_head = np.arange(gw) // HEAD_DIM
    sel = np.zeros((2, LANES, gw), np.float32)
    for j, base in enumerate((d * N_HEADS, (2 + d) * N_HEADS)):
        sel[j, base + lane_head, np.arange(gw)] = 1.0
    gate_bias = jnp.stack([jnp.repeat(gate_b[d], HEAD_DIM), jnp.repeat(gate_b[2 + d], HEAD_DIM)], axis=0)
    full = lambda a: pl.BlockSpec(a.shape, lambda bi, i: (0,) * a.ndim)
    cb = conv_b.reshape(1, 2 * gw)
    return pl.pallas_call(
        functools.partial(_mlstm_body, reverse=reverse, n_sub=ROW_TILE // MLSTM_CHUNK, n_ctx=n_ctx,
                          n_blocks=n_blocks),
        grid=(b, n_blocks),
        in_specs=[pl.BlockSpec((1, ROW_TILE, 2 * gw), lambda bi, i: (bi, blk(i), 0)),
                  pl.BlockSpec((1, SUBLANES, 2 * gw), lambda bi, i: (bi, jnp.maximum(blk(i) * sub - 1, 0), 0)),
                  pl.BlockSpec((1, SUBLANES, 2 * gw), lambda bi, i: (bi, jnp.minimum((blk(i) + 1) * sub, n_sub8 - 1), 0)),
                  pl.BlockSpec((1, ROW_TILE, gw), lambda bi, i: (bi, blk(i), 2)),
                  pl.BlockSpec((1, ROW_TILE, LANES), lambda bi, i: (bi, blk(i), 4 * gw // LANES)),
                  full(conv_w), full(cb), full(sel), full(gate_bias)],
        out_specs=pl.BlockSpec((1, ROW_TILE, gw), lambda bi, i: (bi, blk(i), 0)),
        out_shape=jax.ShapeDtypeStruct((b, t, gw), F32),
        scratch_shapes=[pltpu.VMEM((gw, 2 * gw), F32), pltpu.VMEM((SUBLANES, gw), F32)],
        compiler_params=_cparams(("parallel", "arbitrary")),
        name="mlstm_chunk",
    )(p, p, p, p, p, conv_w, cb, jnp.asarray(sel), gate_bias)


def _attn_body(q_ref, k_ref, v_ref, lam_ref, g_ref, o_ref, *, out_scale, ctx_tiles, n_ctx_keys):
    def attend(n_keys):
        q = q_ref[0]
        k = k_ref[0, :n_keys, :]
        v = v_ref[0, :n_keys, :]
        lane = lax.broadcasted_iota(jnp.int32, q.shape, 1)
        comp = lane // DIFF_QK_DIM
        nt = (((1,), (1,)), ((), ()))
        n_maps = LANES // DIFF_QK_DIM
        ss = [lax.dot_general(jnp.where(comp == c, q, 0.0).astype(BF16), k, nt, preferred_element_type=F32)
              for c in range(n_maps)]
        es = [jnp.exp2(s - jnp.max(s, axis=-1, keepdims=True)) for s in ss]
        ls = [jnp.sum(e, axis=-1, keepdims=True) for e in es]
        ys = []
        for hh in range(LANES // HEAD_DIM):
            e0, e1, l0, l1 = es[2 * hh], es[2 * hh + 1], ls[2 * hh], ls[2 * hh + 1]
            pr = (e0 - e1 * (lam_ref[0:1, 0:1] * l0 / l1)).astype(BF16)
            ys.append(jnp.dot(pr, v, preferred_element_type=F32) * (1.0 / l0))
        first = lane < HEAD_DIM
        y = jnp.where(first, ys[0], ys[1])
        ysq = y * y
        ms = jnp.where(first, jnp.sum(jnp.where(first, ysq, 0.0), axis=-1, keepdims=True),
                       jnp.sum(jnp.where(first, 0.0, ysq), axis=-1, keepdims=True)) * (1.0 / HEAD_DIM)
        o_ref[0] = y * lax.rsqrt(ms + LN_EPS) * g_ref[...] * out_scale

    if ctx_tiles == 0:
        attend(k_ref.shape[1])
    else:
        is_ctx = pl.program_id(2) < ctx_tiles
        pl.when(is_ctx)(lambda: attend(n_ctx_keys))
        pl.when(jnp.logical_not(is_ctx))(lambda: attend(k_ref.shape[1]))


def _attention(q, k, v, v_col0, lam, norm_g, out_scale, q_tile0, tc):
    b, t, gw = q.shape
    n_tiles = t // ATTN_TILE - q_tile0
    ctx_tiles = max(tc // ATTN_TILE - q_tile0, 0)
    qspec = pl.BlockSpec((1, ATTN_TILE, LANES), lambda bi, pi, i: (bi, q_tile0 + i, pi))
    kspec = pl.BlockSpec((1, t, LANES), lambda bi, pi, i: (bi, 0, pi))
    vspec = pl.BlockSpec((1, t, LANES), lambda bi, pi, i: (bi, 0, v_col0 + pi))
    vec = pl.BlockSpec((1, LANES), lambda bi, pi, i: (0, 0))
    return pl.pallas_call(
        functools.partial(_attn_body, out_scale=out_scale, ctx_tiles=ctx_tiles, n_ctx_keys=tc),
        grid=(b, gw // LANES, n_tiles),
        in_specs=[qspec, kspec, vspec, vec, vec],
        out_specs=pl.BlockSpec((1, ATTN_TILE, LANES), lambda bi, pi, i: (bi, i, pi)),
        out_shape=jax.ShapeDtypeStruct((b, n_tiles * ATTN_TILE, gw), F32),
        compiler_params=_cparams(("parallel", "parallel", "parallel")),
        name="diff_attn",
    )(q, k, v, jnp.full((1, LANES), lam, F32), jnp.tile(norm_g.reshape(1, HEAD_DIM), (1, LANES // HEAD_DIM)))


def _layernorm(z, g, b):
    z = z - jnp.mean(z, axis=-1, keepdims=True)
    return z * lax.rsqrt(jnp.mean(z * z, axis=-1, keepdims=True) + LN_EPS) * g + b


def _outproj_body(ya_ref, yb_ref, yc_ref, hf_ref, hb_ref, og_ref, ng_ref, h_ref, gl_ref, gc_ref, w_ref, g_ref,
                  b_ref, o_ref, *, ctx_tiles, alpha):
    is_ctx = pl.program_id(1) < ctx_tiles
    gate = jnp.where(is_ctx, gc_ref[0], gl_ref[0])
    mean_block = _head_block(1.0 / HEAD_DIM).astype(BF16)
    yd = hf_ref[0] + hb_ref[0]
    yd = yd - _head_sum(yd, mean_block)
    yd = yd * lax.rsqrt(_head_sum(yd * yd, mean_block) + LN_EPS) * ng_ref[...] * jax.nn.sigmoid(og_ref[0])
    y = None
    for m, ym in enumerate((ya_ref[0], yb_ref[0], yc_ref[0], yd)):
        part = jnp.dot(ym.astype(BF16), w_ref[m * GROUP_WIDTH:(m + 1) * GROUP_WIDTH, :],
                       preferred_element_type=F32)
        y = part if y is None else y + part
    o_ref[0] = _layernorm(alpha * h_ref[0] + gate * y, g_ref[...], b_ref[...])


def _outproj(ya, yb, yc, hf, hb, p_mlstm, mlstm_norm_g, h, gate_l, gate_c, w, ln_g, ln_b, tile0, ctx_tiles, alpha):
    b, t, d = h.shape
    gw = GROUP_WIDTH
    n_tiles = t // ROW_TILE - tile0
    yrow = lambda c: pl.BlockSpec((1, ROW_TILE, gw), lambda bi, i: (bi, tile0 + i, c))
    yb_row = pl.BlockSpec((1, ROW_TILE, gw), lambda bi, i: (bi, i, 0))
    hrow = pl.BlockSpec((1, ROW_TILE, d), lambda bi, i: (bi, tile0 + i, 0))
    lat = pl.BlockSpec((1, 1, d), lambda bi, i: (bi, 0, 0))
    ctx = pl.BlockSpec((1, 1, d), lambda bi, i: (0, 0, 0))
    vec = lambda n: pl.BlockSpec((1, n), lambda bi, i: (0, 0))
    return pl.pallas_call(
        functools.partial(_outproj_body, ctx_tiles=ctx_tiles, alpha=alpha),
        grid=(b, n_tiles),
        in_specs=[yrow(0), yb_row, yrow(0), yrow(0), yrow(0), yrow(3), vec(gw), hrow, lat, ctx,
                  pl.BlockSpec(w.shape, lambda bi, i: (0, 0)), vec(d), vec(d)],
        out_specs=pl.BlockSpec((1, ROW_TILE, d), lambda bi, i: (bi, i, 0)),
        out_shape=jax.ShapeDtypeStruct((b, n_tiles * ROW_TILE, d), F32),
        compiler_params=_cparams(("parallel", "parallel")),
        name="outproj_ln",
    )(ya, yb, yc, hf, hb, p_mlstm, mlstm_norm_g.reshape(1, gw), h, gate_l, gate_c, w, ln_g.reshape(1, d),
      ln_b.reshape(1, d))


FFN_CHUNKS = 1


def _ffn_body(x_ref, xp_ref, xn_ref, sl_ref, cl_ref, gl_ref, sc_ref, cc_ref, gc_ref,
              wu_ref, cw_ref, cb_ref, wd_ref, g_ref, b_ref, o_ref, *, ctx_tiles, n_tiles, alpha, d_ff):
    i = pl.program_id(1)
    is_ctx = i < ctx_tiles
    shift = jnp.where(is_ctx, sc_ref[0], sl_ref[0])
    scale = jnp.where(is_ctx, cc_ref[0], cl_ref[0])
    gate_mod = jnp.where(is_ctx, gc_ref[0], gl_ref[0])
    seg_start = jnp.logical_or(i == 0, i == ctx_tiles)
    seg_end = jnp.logical_or(i == ctx_tiles - 1, i == n_tiles - 1)

    x = x_ref[0]
    xm = (x * (1.0 + scale) + shift).astype(BF16)
    xp = (xp_ref[0] * (1.0 + scale) + shift).astype(BF16)
    xn = (xn_ref[0] * (1.0 + scale) + shift).astype(BF16)
    rows = x.shape[0]
    row_id = lax.broadcasted_iota(jnp.int32, (rows, 1), 0)
    ch = d_ff // FFN_CHUNKS
    f = jnp.zeros(x.shape, F32)
    for c in range(FFN_CHUNKS):
        wg = wu_ref[:, c * ch:(c + 1) * ch]
        wv = wu_ref[:, d_ff + c * ch:d_ff + (c + 1) * ch]
        gate = jnp.dot(xm, wg, preferred_element_type=F32)
        val = jnp.dot(xm, wv, preferred_element_type=F32)
        gp = jnp.dot(xp, wg, preferred_element_type=F32)[SUBLANES - 1:SUBLANES, :]
        gn = jnp.dot(xn, wg, preferred_element_type=F32)[0:1, :]
        gp = jnp.where(seg_start, 0.0, gp)
        gn = jnp.where(seg_end, 0.0, gn)
        prev = jnp.where(row_id == 0, gp, pltpu.roll(gate, 1, 0))
        nxt = jnp.where(row_id == rows - 1, gn, pltpu.roll(gate, rows - 1, 0))
        cw = cw_ref[:, c * ch:(c + 1) * ch]
        conv = prev * cw[0:1, :] + gate * cw[1:2, :] + nxt * cw[2:3, :] + cb_ref[:, c * ch:(c + 1) * ch]
        act = (jax.nn.gelu(conv) * val).astype(BF16)
        f = f + jnp.dot(act, wd_ref[c * ch:(c + 1) * ch, :], preferred_element_type=F32)
    o_ref[0] = _layernorm(alpha * x + gate_mod * f, g_ref[...], b_ref[...])


def _ffn(h, mods_l, mods_c, w_up, conv_w, conv_b, w_down, ln_g, ln_b, ctx_tiles, alpha):
    b, t, d = h.shape
    d_ff = w_down.shape[0]
    tile = 2 * ROW_TILE if ctx_tiles == 0 and t % (2 * ROW_TILE) == 0 else ROW_TILE
    n_tiles = t // tile
    sub = tile // SUBLANES
    n_sub = t // SUBLANES
    row = pl.BlockSpec((1, tile, d), lambda bi, i: (bi, i, 0))
    prev = pl.BlockSpec((1, SUBLANES, d), lambda bi, i: (bi, jnp.maximum(i * sub - 1, 0), 0))
    nxt = pl.BlockSpec((1, SUBLANES, d), lambda bi, i: (bi, jnp.minimum((i + 1) * sub, n_sub - 1), 0))
    lat = pl.BlockSpec((1, 1, d), lambda bi, i: (bi, 0, 0))
    ctx = pl.BlockSpec((1, 1, d), lambda bi, i: (0, 0, 0))
    vec = pl.BlockSpec((1, d), lambda bi, i: (0, 0))
    full = lambda a: pl.BlockSpec(a.shape, lambda bi, i: (0,) * a.ndim)
    cb = conv_b.reshape(1, d_ff)
    return pl.pallas_call(
        functools.partial(_ffn_body, ctx_tiles=ctx_tiles, n_tiles=n_tiles, alpha=alpha, d_ff=d_ff),
        grid=(b, n_tiles),
        in_specs=[row, prev, nxt, lat, lat, lat, ctx, ctx, ctx,
                  full(w_up), full(conv_w), full(cb), full(w_down), vec, vec],
        out_specs=row,
        out_shape=jax.ShapeDtypeStruct((b, t, d), F32),
        compiler_params=_cparams(("parallel", "parallel")),
        name="ffn",
    )(h, h, h, *mods_l, *mods_c, w_up, conv_w, cb, w_down, ln_g.reshape(1, d), ln_b.reshape(1, d))


def _rwkv7(f, tc, lnx_g, lnx_b):
    b = f.shape[0]
    col = RWKV_FEATS.index
    pairs = [("r", "r"), ("w0", "w1"), ("k0", "k1"), ("a", "a"), ("b0", "b1")]
    y = _scan("rwkv", _to_scan(f, [(col(p), col(q)) for p, q in pairs], "k", tc),
              _to_scan(f, [(col("v"), col("v"))], "v", tc))
    return _from_scan(y, b, tc, "rwkv", f, (col("bonus"), col("g")), (lnx_g, lnx_b))


def _rope_tables(n_ctx, n_lat):
    rows = n_lat // GRID_W
    row = jnp.repeat(jnp.arange(rows, dtype=F32), GRID_W)
    col = jnp.tile(jnp.arange(GRID_W, dtype=F32), rows)
    n_freq = DIFF_QK_DIM // 4
    inv_freq = ROPE_BASE ** (-jnp.arange(n_freq, dtype=F32) / n_freq)
    ang = jnp.concatenate([row[:, None] * inv_freq, col[:, None] * inv_freq], axis=-1)
    ang = jnp.concatenate([jnp.zeros((n_ctx, ang.shape[1]), F32), ang], axis=0)
    half = DIFF_QK_DIM // 2
    lane = np.arange(GROUP_WIDTH)
    sign = np.where(lane % DIFF_QK_DIM < half, -1.0, 1.0).astype(np.float32)
    return jnp.cos(ang)[:, lane % half], jnp.sin(ang)[:, lane % half] * sign


def _diff_attn(q, k, v, tc, lam_vecs, norm_g, layer, need_ctx):
    lam_init = 0.8 - 0.6 * math.exp(-0.3 * layer)
    lv = lam_vecs.astype(F32)
    lam = jnp.exp(jnp.sum(lv[0] * lv[1])) - jnp.exp(jnp.sum(lv[2] * lv[3])) + lam_init
    q_tile0 = 0 if need_ctx else tc // ATTN_TILE
    return _attention(q, k, v, 0, lam, norm_g, 1.0 - lam_init, q_tile0, tc)


def _hgrn2(f, tc, norm_g):
    b = f.shape[0]
    col = HGRN_FEATS.index
    pairs = [("q", "q"), ("w0", "w1"), ("k0", "k1")]
    o = _scan("gla", _to_scan(f, [(col(p), col(q)) for p, q in pairs], "k", tc),
              _to_scan(f, [(col("v"), col("v"))], "v", tc))
    return _from_scan(o, b, tc, "gla", f, (col("gate"),), (jnp.tile(norm_g, N_HEADS),))


RWKV_COLS = 3 * GROUP_WIDTH + 2 * RWKV_DECAY_LORA + 2 * RWKV_ICLR_LORA + RWKV_GATE_LORA
DIFF_COLS = 3 * GROUP_WIDTH
HGRN_COLS = 5 * GROUP_WIDTH
MLSTM_COLS = 4 * GROUP_WIDTH + 4 * N_HEADS
SLAB_COLS = (RWKV_COLS, DIFF_COLS, HGRN_COLS, MLSTM_COLS)
SLAB_PAD = (5 * GROUP_WIDTH, DIFF_COLS, HGRN_COLS, 4 * GROUP_WIDTH + LANES)


def _split_cols(w):
    parts, off = [], 0
    for n, n_pad in zip(SLAB_COLS, SLAB_PAD):
        parts.append(jnp.pad(w[:, off:off + n], ((0, 0), (0, n_pad - n))).astype(BF16))
        off += n
    return parts


def kernel(x, c, ctx, c_ctx, ada_w, ada_b, w_in, rwkv_mu, rwkv_w0, rwkv_w2, rwkv_a0, rwkv_a2, rwkv_g2, rwkv_k_k, rwkv_k_a, rwkv_r_k, rwkv_lnx_g, rwkv_lnx_b, diff_lambda, diff_norm_g, hgrn_lb_logits, hgrn_norm_g, mlstm_conv_w, mlstm_conv_b, mlstm_gate_b, mlstm_norm_g, w_out, ffn_w_up, ffn_conv_w, ffn_conv_b, ffn_w_down, ln_g, ln_b):
    depth = w_in.shape[0]
    b, seq, d = x.shape
    tc = ctx.shape[1]
    assert tc == ROW_TILE == ATTN_TILE and seq % ROW_TILE == 0 and 2 * b * N_HEADS <= LANES
    alpha = (2.0 * depth) ** 0.25

    lb_w = jax.nn.softmax(hgrn_lb_logits.astype(F32), axis=0)
    lower_bounds = jnp.cumsum(lb_w, axis=0) - lb_w[0]

    rope = _rope_tables(tc, seq)
    cond = jnp.zeros((8, d), F32).at[:b].set(c).at[b].set(c_ctx)
    h = jnp.concatenate([ctx, x], axis=1)
    for layer in range(depth):
        last = layer == depth - 1
        mod = _adaln(cond, ada_w, ada_b, layer).reshape(8, 6, d)
        mods_l = [mod[:b, j][:, None, :] for j in range(6)]
        mods_c = [mod[b:b + 1, j][:, None, :] for j in range(6)]

        rwkv_prm = (rwkv_mu[layer], rwkv_w0[layer], rwkv_w2[layer], rwkv_a0[layer], rwkv_a2[layer], rwkv_g2[layer],
                    rwkv_k_k[layer], rwkv_k_a[layer], rwkv_r_k[layer])
        f_rwkv, q, k, v, f_hgrn, p_mlstm = _inproj(h, mods_l[0], mods_l[1], mods_c[0], mods_c[1], rope,
                                                   _split_cols(w_in[layer]), rwkv_prm, lower_bounds[layer],
                                                   tc // ROW_TILE)
        y_a = _rwkv7(f_rwkv, tc, rwkv_lnx_g[layer], rwkv_lnx_b[layer])
        y_b = _diff_attn(q, k, v, tc, diff_lambda[layer], diff_norm_g[layer], layer, not last)
        y_c = _hgrn2(f_hgrn, tc, hgrn_norm_g[layer])
        h_f, h_b = (_mlstm_chunked(p_mlstm, mlstm_conv_w[layer], mlstm_conv_b[layer], mlstm_gate_b[layer], rev, tc)
                    for rev in (False, True))

        tile0, ctx_tiles = (tc // ROW_TILE, 0) if last else (0, tc // ROW_TILE)
        h = _outproj(y_a, y_b, y_c, h_f, h_b, p_mlstm, mlstm_norm_g[layer], h, mods_l[2], mods_c[2],
                     w_out[layer].astype(BF16), ln_g[layer, 0], ln_b[layer, 0], tile0, ctx_tiles, alpha)
        h = _ffn(h, mods_l[3:6], mods_c[3:6], ffn_w_up[layer].astype(BF16), ffn_conv_w[layer], ffn_conv_b[layer],
                 ffn_w_down[layer].astype(BF16), ln_g[layer, 1], ln_b[layer, 1], ctx_tiles, alpha)
    return h
```
